```python
import math
import jax, jax.numpy as jnp
from jax import lax
import numpy as np

D_MODEL = 1024
BATCH = 8
SEQ = 8192
DEPTH = 1

D_MIX = D_MODEL
RWKV_W = D_MIX // 2
RWKV_N = 64
RWKV_H = RWKV_W // RWKV_N
R_DECAY = 32
R_AAA = 32
R_GATE = 96
DIFF_W = D_MIX - RWKV_W
DIFF_H = 4
DIFF_D = DIFF_W // DIFF_H // 2
N_SHIFT = 3 * RWKV_W + R_DECAY + R_AAA + R_GATE
N_IN = N_SHIFT + 3 * DIFF_W
MEM_LEN = 256
CROSS_H = 4
CROSS_D = D_MODEL // CROSS_H
D_FF = 2816
CONV_W = 3
ROPE_THETA = 10000.0
Q_BLOCK = 128
NORM_EPS = 1e-6
LNX_EPS = 64e-5
SUBLN_EPS = 1e-5

kernel_name = "hybrid_rwkv7_diffattn_memxattn_convffn"

F32 = jnp.float32


def rms_norm(x, g, eps=NORM_EPS):
    xf = x.astype(F32)
    y = xf * lax.rsqrt(jnp.mean(xf * xf, axis=-1, keepdims=True) + eps)
    return (y * g.astype(F32)).astype(x.dtype)


def rope(x, positions):
    d = x.shape[-1]
    inv = ROPE_THETA ** (-jnp.arange(0, d, 2, dtype=F32) / d)
    ang = positions.astype(F32)[..., None] * inv
    ang = ang.reshape(ang.shape[:2] + (1,) * (x.ndim - 3) + ang.shape[-1:])
    cos, sin = jnp.cos(ang), jnp.sin(ang)
    xf = x.astype(F32)
    x1, x2 = xf[..., : d // 2], xf[..., d // 2:]
    return jnp.concatenate([x1 * cos - x2 * sin, x2 * cos + x1 * sin], axis=-1).astype(x.dtype)


def token_shift(z):
    return jnp.pad(z, ((0, 0), (1, 0), (0, 0)))[:, :-1]


def rwkv7_mix(z, shift_mix, w0, w_lora_up, a0, a_lora_up, g_lora_up,
              k_k, k_a, r_k, lnx_gain, lnx_bias):
    B, T, _ = z.shape
    z = z + (token_shift(z) - z) * shift_mix
    r, k, v, zw, za, zg = jnp.split(
        z, [RWKV_W, 2 * RWKV_W, 3 * RWKV_W, 3 * RWKV_W + R_DECAY,
            3 * RWKV_W + R_DECAY + R_AAA], axis=-1)
    w_log = -jax.nn.softplus(-(w0 + jnp.tanh(zw) @ w_lora_up)) - 0.5
    decay = jnp.exp(-jnp.exp(w_log.astype(F32)))
    a = jax.nn.sigmoid(a0 + za @ a_lora_up)
    g = jax.nn.sigmoid(zg) @ g_lora_up
    heads = lambda t: t.reshape(B, T, RWKV_H, RWKV_N).astype(F32)
    kk = heads(k * k_k)
    kk = kk / jnp.maximum(jnp.sqrt(jnp.sum(kk * kk, -1, keepdims=True)), 1e-12)
    k = heads(k * (1.0 + (a - 1.0) * k_a))
    r_h, v_h, a_h, w_h = heads(r), heads(v), heads(a), heads(decay)

    def step(S, inp):
        r_t, w_t, k_t, v_t, kk_t, a_t = inp
        sa = jnp.einsum('bhvk,bhk->bhv', S, kk_t)
        S = (S * w_t[:, :, None, :]
             - sa[..., None] * (kk_t * a_t)[:, :, None, :]
             + v_t[..., None] * k_t[:, :, None, :])
        return S, jnp.einsum('bhvk,bhk->bhv', S, r_t)

    tm = lambda t: jnp.moveaxis(t, 1, 0)
    S0 = jnp.zeros((B, RWKV_H, RWKV_N, RWKV_N), F32)
    _, y = lax.scan(step, S0, (tm(r_h), tm(w_h), tm(k), tm(v_h), tm(kk), tm(a_h)))
    y = jnp.moveaxis(y, 0, 1)
    mu = jnp.mean(y, -1, keepdims=True)
    var = jnp.mean(jnp.square(y - mu), -1, keepdims=True)
    y = ((y - mu) * lax.rsqrt(var + LNX_EPS)).reshape(B, T, RWKV_W)
    y = y * lnx_gain.astype(F32) + lnx_bias.astype(F32)
    bonus = jnp.sum(r_h * k * r_k.astype(F32), -1, keepdims=True) * v_h
    out = (y + bonus.reshape(B, T, RWKV_W)) * g.astype(F32)
    return out.astype(z.dtype)


def diff_attn(z, positions, lam_q1, lam_k1, lam_q2, lam_k2, subln_gain, lambda_init):
    B, T, _ = z.shape
    q, k, v = jnp.split(z, 3, axis=-1)
    q = rope(q.reshape(B, T, DIFF_H, 2, DIFF_D), positions)
    k = rope(k.reshape(B, T, DIFF_H, 2, DIFF_D), positions)
    v = v.reshape(B, T, DIFF_H, 2 * DIFF_D)
    lam = (jnp.exp(jnp.sum(lam_q1.astype(F32) * lam_k1.astype(F32)))
           - jnp.exp(jnp.sum(lam_q2.astype(F32) * lam_k2.astype(F32))) + lambda_init)
    nb = T // Q_BLOCK
    qb_all = (q * (DIFF_D ** -0.5)).reshape(B, nb, Q_BLOCK, DIFF_H, 2, DIFF_D)
    qb_all = jnp.moveaxis(qb_all, 1, 0)
    key_pos = jnp.arange(T)

    def block(args):
        qb, bi = args
        s = jnp.einsum('bqhcd,bkhcd->bhcqk', qb, k, preferred_element_type=F32)
        q_pos = bi * Q_BLOCK + jnp.arange(Q_BLOCK)
        s = jnp.where((key_pos[None, :] <= q_pos[:, None]), s, -jnp.inf)
        p = jax.nn.softmax(s, axis=-1)
        attn = p[:, :, 0] - lam * p[:, :, 1]
        return jnp.einsum('bhqk,bkhe->bqhe', attn.astype(v.dtype), v)

    o = lax.map(block, (qb_all, jnp.arange(nb)))
    o = jnp.moveaxis(o, 0, 1).reshape(B, T, DIFF_H, 2 * DIFF_D)
    o = rms_norm(o, subln_gain, SUBLN_EPS) * (1.0 - lambda_init)
    return o.reshape(B, T, DIFF_W)


def memory_cross_attn(h, mem_n, wq, wkv, wo):
    B, T, _ = h.shape
    q = (h @ wq).reshape(B, T, CROSS_H, CROSS_D)
    k, v = jnp.split(mem_n @ wkv, 2, axis=-1)
    k = k.reshape(B, -1, CROSS_H, CROSS_D)
    v = v.reshape(B, -1, CROSS_H, CROSS_D)
    s = jnp.einsum('bthd,bmhd->bhtm', q, k, preferred_element_type=F32) * (CROSS_D ** -0.5)
    p = jax.nn.softmax(s, axis=-1)
    o = jnp.einsum('bhtm,bmhd->bthd', p.astype(v.dtype), v).reshape(B, T, D_MODEL)
    return o @ wo


def conv_ffn(h, w_up, conv_w, conv_b, w_down):
    T = h.shape[1]
    gate, val = jnp.split(h @ w_up, 2, axis=-1)
    gp = jnp.pad(gate, ((0, 0), (CONV_W - 1, 0), (0, 0)))
    c = sum(conv_w[j] * gp[:, j:j + T] for j in range(CONV_W)) + conv_b
    return (jax.nn.silu(c) * val) @ w_down


def setup_inputs(seed: int = 0) -> dict:
    key = jax.random.key(seed)
    ks = iter(jax.random.split(key, 40))
    nrm = lambda shape, scale: scale * jax.random.normal(next(ks), shape, F32)
    near1 = lambda shape: 1.0 + 0.02 * jax.random.normal(next(ks), shape, F32)
    L = DEPTH
    x = jax.random.normal(next(ks), (BATCH, SEQ, D_MODEL), F32)
    mem = jax.random.normal(next(ks), (BATCH, MEM_LEN, D_MODEL), F32)
    offset = jax.random.randint(next(ks), (BATCH, 1), 0, 4096, jnp.int32)
    positions = (offset + jnp.arange(SEQ, dtype=jnp.int32)[None, :]).astype(jnp.int32)
    return {
        "x": x, "mem": mem, "positions": positions,
        "norm_mix": near1((L, D_MODEL)),
        "w_in": nrm((L, D_MODEL, N_IN), D_MODEL ** -0.5),
        "shift_mix": jax.random.uniform(next(ks), (L, N_SHIFT), F32),
        "w0": jax.random.uniform(next(ks), (L, RWKV_W), F32, -6.0, 1.0),
        "w_lora_up": nrm((L, R_DECAY, RWKV_W), R_DECAY ** -0.5),
        "a0": nrm((L, RWKV_W), 0.5),
        "a_lora_up": nrm((L, R_AAA, RWKV_W), R_AAA ** -0.5),
        "g_lora_up": nrm((L, R_GATE, RWKV_W), R_GATE ** -0.5),
        "k_k": 0.85 + nrm((L, RWKV_W), 0.1),
        "k_a": 1.0 + nrm((L, RWKV_W), 0.1),
        "r_k": nrm((L, RWKV_H, RWKV_N), 0.1),
        "lnx_gain": near1((L, RWKV_W)),
        "lnx_bias": nrm((L, RWKV_W), 0.02),
        "lam_q1": nrm((L, DIFF_D), 0.1),
        "lam_k1": nrm((L, DIFF_D), 0.1),
        "lam_q2": nrm((L, DIFF_D), 0.1),
        "lam_k2": nrm((L, DIFF_D), 0.1),
        "subln_gain": near1((L, 2 * DIFF_D)),
        "w_out": nrm((L, D_MIX, D_MODEL), D_MIX ** -0.5),
        "norm_cross": near1((L, D_MODEL)),
        "norm_mem": near1((L, D_MODEL)),
        "wq_c": nrm((L, D_MODEL, D_MODEL), D_MODEL ** -0.5),
        "wkv_c": nrm((L, D_MODEL, 2 * D_MODEL), D_MODEL ** -0.5),
        "wo_c": nrm((L, D_MODEL, D_MODEL), D_MODEL ** -0.5),
        "norm_ffn": near1((L, D_MODEL)),
        "w_up": nrm((L, D_MODEL, 2 * D_FF), D_MODEL ** -0.5),
        "conv_w": nrm((L, CONV_W, D_FF), CONV_W ** -0.5),
        "conv_b": nrm((L, D_FF), 0.02),
        "w_down": nrm((L, D_FF, D_MODEL), D_FF ** -0.5),
        "norm_final": near1((D_MODEL,)),
    }


def reference(x, mem, positions, norm_mix, w_in, shift_mix, w0, w_lora_up, a0, a_lora_up,
              g_lora_up, k_k, k_a, r_k, lnx_gain, lnx_bias, lam_q1, lam_k1, lam_q2, lam_k2,
              subln_gain, w_out, norm_cross, norm_mem, wq_c, wkv_c, wo_c, norm_ffn, w_up,
              conv_w, conv_b, w_down, norm_final):
    for l in range(DEPTH):
        lambda_init = 0.8 - 0.6 * math.exp(-0.3 * l)
        z = rms_norm(x, norm_mix[l]) @ w_in[l]
        y_a = rwkv7_mix(z[..., :N_SHIFT], shift_mix[l], w0[l], w_lora_up[l], a0[l],
                        a_lora_up[l], g_lora_up[l], k_k[l], k_a[l], r_k[l],
                        lnx_gain[l], lnx_bias[l])
        y_b = diff_attn(z[..., N_SHIFT:], positions, lam_q1[l], lam_k1[l], lam_q2[l],
                        lam_k2[l], subln_gain[l], lambda_init)
        x = x + jnp.concatenate([y_a, y_b], axis=-1) @ w_out[l]
        x = x + memory_cross_attn(rms_norm(x, norm_cross[l]), rms_norm(mem, norm_mem[l]),
                                  wq_c[l], wkv_c[l], wo_c[l])
        x = x + conv_ffn(rms_norm(x, norm_ffn[l]), w_up[l], conv_w[l], conv_b[l], w_down[l])
    return rms_norm(x, norm_final)
```

```python
import functools
import math

import numpy as np
import jax
import jax.numpy as jnp
from jax import lax
from jax.experimental import pallas as pl
from jax.experimental.pallas import tpu as pltpu

F32 = jnp.float32
BF16 = jnp.bfloat16

D_MODEL = 1024
RWKV_W = 512
RWKV_N = 64
R_DECAY = 32
R_AAA = 32
R_GATE = 96
DIFF_W = 512
DIFF_H = 4
DIFF_D = 64
N_SHIFT = 3 * RWKV_W + R_DECAY + R_AAA + R_GATE
MEM_LEN = 256
CROSS_H = 4
CROSS_D = D_MODEL // CROSS_H
D_FF = 2816
ROPE_THETA = 10000.0
NORM_EPS = 1e-6
LNX_EPS = 64e-5
SUBLN_EPS = 1e-5

LANES = 128
LORA_PAD = 256
ZR_W = 3 * RWKV_W + LORA_PAD
VMEM_LIMIT = 56 * 1024 * 1024

ROW_TILE = 512
CHUNK = 64
ATT_TQ = 512
ATT_TK = 512
FF_TILE = 256


def _mm(a, b):
    return jnp.dot(a.astype(BF16), b.astype(BF16), preferred_element_type=F32)


def _mm_nt(a, b):
    return lax.dot_general(a.astype(BF16), b.astype(BF16), (((1,), (1,)), ((), ())),
                           preferred_element_type=F32)


def _mm_tn(a, b):
    return lax.dot_general(a.astype(BF16), b.astype(BF16), (((0,), (0,)), ((), ())),
                           preferred_element_type=F32)


def _sigmoid(x):
    return 1.0 / (1.0 + jnp.exp(-x))


def _rms(x, g, eps):
    return x * lax.rsqrt(jnp.mean(x * x, axis=-1, keepdims=True) + eps) * g


def _split_dot(x, ones_b):
    hi = x.astype(BF16)
    lo = (x - hi.astype(F32)).astype(BF16)
    return (jnp.dot(hi, ones_b, preferred_element_type=F32)
            + jnp.dot(lo, ones_b, preferred_element_type=F32))


def _params(*sem):
    return pltpu.CompilerParams(dimension_semantics=sem, vmem_limit_bytes=VMEM_LIMIT)


def _inproj_kernel(x_ref, pos_ref, g_ref, wr_ref, wd_ref, zr_ref, q_ref, k_ref, v_ref):
    h = _rms(x_ref[...], g_ref[...], NORM_EPS).astype(BF16)
    for j in range(ZR_W // 256):
        sl = slice(j * 256, (j + 1) * 256)
        zr_ref[:, sl] = jnp.dot(h, wr_ref[:, sl], preferred_element_type=F32)

    lane = lax.broadcasted_iota(jnp.int32, (1, LANES), 1)
    first_half = (lane % DIFF_D) < (DIFF_D // 2)
    freq = (lane % (DIFF_D // 2)).astype(F32)
    inv = jnp.exp(freq * (-2.0 / DIFF_D * math.log(ROPE_THETA)))
    ang = pos_ref[...].astype(F32) * inv
    cos = jnp.cos(ang)
    sin = jnp.where(first_half, -jnp.sin(ang), jnp.sin(ang))

    def rope(xb):
        rot = jnp.where(first_half, pltpu.roll(xb, LANES - DIFF_D // 2, axis=1),
                        pltpu.roll(xb, DIFF_D // 2, axis=1))
        return xb * cos + rot * sin

    scale = DIFF_D ** -0.5
    for j in range(DIFF_W // 256):
        zq = jnp.dot(h, wd_ref[:, j * 256:(j + 1) * 256], preferred_element_type=F32)
        zk = jnp.dot(h, wd_ref[:, DIFF_W + j * 256:DIFF_W + (j + 1) * 256],
                     preferred_element_type=F32)
        zv = jnp.dot(h, wd_ref[:, 2 * DIFF_W + j * 256:2 * DIFF_W + (j + 1) * 256],
                     preferred_element_type=F32)
        for u in range(2):
            c0 = j * 256 + u * LANES
            q_ref[:, c0:c0 + LANES] = (rope(zq[:, u * LANES:(u + 1) * LANES]) * scale).astype(BF16)
            k_ref[:, c0:c0 + LANES] = rope(zk[:, u * LANES:(u + 1) * LANES]).astype(BF16)
        v_ref[:, j * 256:(j + 1) * 256] = zv.astype(BF16)


def _inproj(x2d, pos2d, g, w_r, w_d):
    m = x2d.shape[0]
    tm = ROW_TILE
    row = lambda i: (i, 0)
    const = lambda i: (0, 0)
    return pl.pallas_call(
        _inproj_kernel,
        grid=(m // tm,),
        in_specs=[
            pl.BlockSpec((tm, D_MODEL), row),
            pl.BlockSpec((tm, 1), row),
            pl.BlockSpec((1, D_MODEL), const),
            pl.BlockSpec((D_MODEL, ZR_W), const),
            pl.BlockSpec((D_MODEL, 3 * DIFF_W), const),
        ],
        out_specs=[
            pl.BlockSpec((tm, ZR_W), row),
            pl.BlockSpec((tm, DIFF_W), row),
            pl.BlockSpec((tm, DIFF_W), row),
            pl.BlockSpec((tm, DIFF_W), row),
        ],
        out_shape=[
            jax.ShapeDtypeStruct((m, ZR_W), F32),
            jax.ShapeDtypeStruct((m, DIFF_W), BF16),
            jax.ShapeDtypeStruct((m, DIFF_W), BF16),
            jax.ShapeDtypeStruct((m, DIFF_W), BF16),
        ],
        compiler_params=_params("parallel"),
        name="inproj",
    )(x2d, pos2d, g, w_r, w_d)


def _rwkv_kernel(z_ref, mix_ref, wl_ref, w0_ref, a0_ref, kk_ref, ka_ref, rk_ref, gain_ref,
                 bias_ref, y_ref, carry_s, st_s, r_s, lw_s, k_s, v_s, al_s, be_s, g_s, bo_s, y_s):
    tb = z_ref.shape[0]
    n_pair = RWKV_W // LANES

    @pl.when(pl.program_id(1) == 0)
    def _():
        carry_s[...] = jnp.zeros_like(carry_s)
        st_s[...] = jnp.zeros_like(st_s)

    row = lax.broadcasted_iota(jnp.int32, (tb, 1), 0)

    def shifted(c0, c1):
        zc = z_ref[:, c0:c1]
        zp = jnp.where(row == 0, carry_s[7:8, c0:c1], pltpu.roll(zc, 1, axis=0))
        return zc + (zp - zc) * mix_ref[:, c0:c1]

    ri = lax.broadcasted_iota(jnp.int32, (LANES, LANES), 0)
    ci = lax.broadcasted_iota(jnp.int32, (LANES, LANES), 1)
    same_head = (ri // RWKV_N) == (ci // RWKV_N)
    ones_head = jnp.where(same_head, 1.0, 0.0).astype(BF16)

    def headsum(x):
        return jnp.concatenate(
            [_split_dot(x[:, p * LANES:(p + 1) * LANES], ones_head) for p in range(n_pair)], axis=1)

    zl = shifted(3 * RWKV_W, ZR_W)
    ll = lax.broadcasted_iota(jnp.int32, (1, LORA_PAD), 1)
    act = jnp.where(ll < R_DECAY, jnp.tanh(zl),
                    jnp.where(ll < R_DECAY + R_AAA, zl, _sigmoid(zl)))
    lo = jnp.dot(act.astype(BF16), wl_ref[...], preferred_element_type=F32)
    lw_s[...] = -math.exp(-0.5) * _sigmoid(w0_ref[...] + lo[:, 0:RWKV_W])
    a = _sigmoid(a0_ref[...] + lo[:, RWKV_W:2 * RWKV_W])
    g_s[...] = lo[:, 2 * RWKV_W:3 * RWKV_W]

    r = shifted(0, RWKV_W)
    k = shifted(RWKV_W, 2 * RWKV_W)
    v = shifted(2 * RWKV_W, 3 * RWKV_W)
    carry_s[...] = z_ref[tb - 8:tb, :]
    r_s[...] = r
    v_s[...] = v
    kk = k * kk_ref[...]
    ss = headsum(kk * kk)
    alpha = kk * lax.rsqrt(jnp.maximum(ss, 1e-24))
    al_s[...] = alpha
    be_s[...] = alpha * a
    k2 = k * (1.0 + (a - 1.0) * ka_ref[...])
    k_s[...] = k2
    bo_s[...] = headsum(r * k2 * rk_ref[...]) * v

    c_len = CHUNK
    tri = jnp.where(lax.broadcasted_iota(jnp.int32, (c_len, c_len), 0)
                    >= lax.broadcasted_iota(jnp.int32, (c_len, c_len), 1), 1.0, 0.0).astype(BF16)
    lane = lax.broadcasted_iota(jnp.int32, (1, LANES), 1)
    head0 = lane < RWKV_N
    strict = same_head & (ri > ci)
    incl = same_head & (ri >= ci)
    b16 = (ri // 16) == (ci // 16)
    b32 = (ri // 32) == (ci // 32)
    eye = jnp.where(ri == ci, 1.0, 0.0).astype(F32)

    def stack2(x):
        return jnp.concatenate([jnp.where(head0, x, 0.0), jnp.where(head0, 0.0, x)], axis=0)

    def fold(x):
        return x[:c_len] + x[c_len:]

    def chunk_body(c, carry):
        r0 = pl.multiple_of(c * c_len, c_len)
        for p in range(n_pair):
            sl = (pl.ds(r0, c_len), slice(p * LANES, (p + 1) * LANES))
            r_c, lw, k_c, v_c, al, be = r_s[sl], lw_s[sl], k_s[sl], v_s[sl], al_s[sl], be_s[sl]
            hi = lw.astype(BF16)
            r1 = lw - hi.astype(F32)
            mid = r1.astype(BF16)
            low = (r1 - mid.astype(F32)).astype(BF16)
            cum = (jnp.dot(tri, hi, preferred_element_type=F32)
                   + jnp.dot(tri, mid, preferred_element_type=F32)
                   + jnp.dot(tri, low, preferred_element_type=F32))
            tot = cum[c_len - 1:c_len, :]
            e_neg = jnp.exp(-cum)
            e_end = jnp.exp(tot - cum)
            a_t = al * jnp.exp(cum - lw)
            r_t = r_c * jnp.exp(cum)
            a_st, r_st, v_st = stack2(a_t), stack2(r_t), stack2(v_c)
            b_n, k_n = be * e_neg, k_c * e_neg
            gram = _mm_nt(jnp.concatenate([a_st, r_st], axis=0),
                          jnp.concatenate([b_n, b_n, k_n, k_n], axis=0))
            a_ab = jnp.where(strict, gram[:LANES, :LANES], 0.0)
            a_ak = jnp.where(strict, gram[:LANES, LANES:], 0.0)
            a_rb = jnp.where(incl, gram[LANES:, :LANES], 0.0)
            a_rk = jnp.where(incl, gram[LANES:, LANES:], 0.0)
            a0 = jnp.where(b16, a_ab, 0.0)
            e1 = jnp.where(b32 & jnp.logical_not(b16), a_ab, 0.0)
            e2 = jnp.where(b32, 0.0, a_ab)
            p2 = _mm(a0, a0)
            p4 = _mm(p2, p2)
            p8 = _mm(p4, p4)
            t = eye - a0
            t = t + _mm(t, p2)
            t = t + _mm(t, p4)
            t = t + _mm(t, p8)
            t = t - _mm(_mm(t, e1), t)
            t = t - _mm(_mm(t, e2), t)
            akv = _mm(a_ak, v_st)
            w = _mm(t, jnp.concatenate([a_st, akv], axis=1))
            rbw = _mm(a_rb, w)
            rkv = _mm(a_rk, v_st)
            r_hat = fold(r_st - rbw[:, :LANES])
            y0 = fold(rkv - rbw[:, LANES:])
            ta_tv = jnp.concatenate([fold(w[:, :LANES]), fold(w[:, LANES:])], axis=1)
            bw = _mm_tn(be * e_end, ta_tv)
            kv = _mm_tn(k_c * e_end, v_c)
            m_p = jnp.where(same_head, bw[:, :LANES], 0.0)
            n_p = jnp.where(same_head, kv - bw[:, LANES:], 0.0)
            decay_col = jnp.exp(jnp.broadcast_to(tot, (LANES, LANES)).T)
            st = st_s[p]
            zz = _mm(jnp.concatenate([m_p, r_hat], axis=0), st)
            y_s[sl] = zz[LANES:] + y0
            st_s[p] = decay_col * st - zz[:LANES] + n_p
        return carry

    lax.fori_loop(0, tb // c_len, chunk_body, 0)

    y = y_s[...]
    mu = headsum(y) * (1.0 / RWKV_N)
    d = y - mu
    var = headsum(d * d) * (1.0 / RWKV_N)
    yn = d * lax.rsqrt(var + LNX_EPS) * gain_ref[...] + bias_ref[...]
    y_ref[...] = ((yn + bo_s[...]) * g_s[...]).astype(y_ref.dtype)


def _rwkv(zr, bsz, seq, mix, w_lora, w0, a0, k_k, k_a, r_k, gain, bias):
    tb = ROW_TILE
    nt = seq // tb
    row = lambda b, t: (b * nt + t, 0)
    const = lambda b, t: (0, 0)
    vec = pl.BlockSpec((1, RWKV_W), const)
    big = pltpu.VMEM((tb, RWKV_W), F32)
    return pl.pallas_call(
        _rwkv_kernel,
        grid=(bsz, nt),
        in_specs=[
            pl.BlockSpec((tb, ZR_W), row),
            pl.BlockSpec((1, ZR_W), const),
            pl.BlockSpec((LORA_PAD, 3 * RWKV_W), const),
            vec, vec, vec, vec, vec, vec, vec,
        ],
        out_specs=pl.BlockSpec((tb, RWKV_W), row),
        out_shape=jax.ShapeDtypeStruct((bsz * seq, RWKV_W), BF16),
        scratch_shapes=[
            pltpu.VMEM((8, ZR_W), F32),
            pltpu.VMEM((RWKV_W // LANES, LANES, LANES), F32),
            big, big, big, big, big, big, big, big, big,
        ],
        compiler_params=_params("parallel", "arbitrary"),
        name="rwkv7",
    )(zr, mix, w_lora, w0, a0, k_k, k_a, r_k, gain, bias)


def _attn_kernel(qi_tab, ki_tab, q_ref, k_ref, v_ref, lq1_ref, lk1_ref, lq2_ref, lk2_ref, sg_ref,
                 o_ref, q_s, m_s, l_s, acc_s, *, lambda_init, ratio):
    tq, tk = q_ref.shape[0], k_ref.shape[0]
    pidx = pl.program_id(2)
    qi = qi_tab[pidx]
    ki = ki_tab[pidx]
    lane = lax.broadcasted_iota(jnp.int32, (1, LANES), 1)
    map0 = lane < DIFF_D

    @pl.when(ki == 0)
    def _():
        q = q_ref[...]
        zero = jnp.zeros_like(q)
        q_s[0] = jnp.where(map0, q, zero)
        q_s[1] = jnp.where(map0, zero, q)
        m_s[...] = jnp.full_like(m_s, -jnp.inf)
        l_s[...] = jnp.zeros_like(l_s)
        acc_s[...] = jnp.zeros_like(acc_s)

    def step(masked):
        k = k_ref[...]
        v = v_ref[...]
        if masked:
            rowp = qi * tq + lax.broadcasted_iota(jnp.int32, (tq, tk), 0)
            colp = ki * tk + lax.broadcasted_iota(jnp.int32, (tq, tk), 1)
            keep = colp <= rowp
        for c in range(2):
            s = lax.dot_general(q_s[c], k, (((1,), (1,)), ((), ())), preferred_element_type=F32)
            if masked:
                s = jnp.where(keep, s, -jnp.inf)
            m_old = m_s[c]
            m_new = jnp.maximum(m_old, jnp.max(s, axis=-1, keepdims=True))
            corr = jnp.exp(m_old - m_new)
            p = jnp.exp(s - m_new)
            l_s[c] = corr * l_s[c] + jnp.sum(p, axis=-1, keepdims=True)
            acc_s[c] = corr * acc_s[c] + jnp.dot(p.astype(BF16), v, preferred_element_type=F32)
            m_s[c] = m_new

    on_diag = ki >= qi * ratio

    @pl.when(on_diag)
    def _():
        step(True)

    @pl.when(jnp.logical_not(on_diag))
    def _():
        step(False)

    @pl.when(ki == (qi + 1) * ratio - 1)
    def _():
        lam = (jnp.exp(jnp.sum(lq1_ref[...] * lk1_ref[...], axis=-1, keepdims=True))
               - jnp.exp(jnp.sum(lq2_ref[...] * lk2_ref[...], axis=-1, keepdims=True))
               + lambda_init)
        o = acc_s[0] / l_s[0] - lam * (acc_s[1] / l_s[1])
        o = _rms(o, sg_ref[...], SUBLN_EPS) * (1.0 - lambda_init)
        o_ref[...] = o.astype(o_ref.dtype)


def _diff_attn(q, k, v, bsz, seq, lq1, lk1, lq2, lk2, sg, lambda_init):
    tq, tk = ATT_TQ, ATT_TK
    ratio = tq // tk
    nq, nk = seq // tq, seq // tk
    pairs = [(a, b) for a in range(nq) for b in range((a + 1) * ratio)]
    qi_tab = jnp.asarray(np.array([a for a, _ in pairs], np.int32))
    ki_tab = jnp.asarray(np.array([b for _, b in pairs], np.int32))
    qmap = lambda b, h, p, qt, kt: (b * nq + qt[p], h)
    kmap = lambda b, h, p, qt, kt: (b * nk + kt[p], h)
    const = lambda b, h, p, qt, kt: (0, 0)
    lam_spec = pl.BlockSpec((1, DIFF_D), const)
    grid_spec = pltpu.PrefetchScalarGridSpec(
        num_scalar_prefetch=2,
        grid=(bsz, DIFF_H, len(pairs)),
        in_specs=[
            pl.BlockSpec((tq, LANES), qmap),
            pl.BlockSpec((tk, LANES), kmap),
            pl.BlockSpec((tk, LANES), kmap),
            lam_spec, lam_spec, lam_spec, lam_spec,
            pl.BlockSpec((1, 2 * DIFF_D), const),
        ],
        out_specs=pl.BlockSpec((tq, LANES), qmap),
        scratch_shapes=[
            pltpu.VMEM((2, tq, LANES), BF16),
            pltpu.VMEM((2, tq, 1), F32),
            pltpu.VMEM((2, tq, 1), F32),
            pltpu.VMEM((2, tq, LANES), F32),
        ],
    )
    return pl.pallas_call(
        functools.partial(_attn_kernel, lambda_init=lambda_init, ratio=ratio),
        grid_spec=grid_spec,
        out_shape=jax.ShapeDtypeStruct((bsz * seq, DIFF_W), BF16),
        compiler_params=_params("parallel", "parallel", "arbitrary"),
        name="diff_attn",
    )(qi_tab, ki_tab, q, k, v, lq1, lk1, lq2, lk2, sg)


def _memkv_kernel(m_ref, g_ref, w_ref, o_ref):
    h = _rms(m_ref[...], g_ref[...], NORM_EPS).astype(BF16)
    for j in range(o_ref.shape[1] // 256):
        sl = slice(j * 256, (j + 1) * 256)
        o_ref[:, sl] = jnp.dot(h, w_ref[:, sl], preferred_element_type=F32).astype(o_ref.dtype)


def _memkv(mem2d, g, wkv):
    m = mem2d.shape[0]
    tm = MEM_LEN
    return pl.pallas_call(
        _memkv_kernel,
        grid=(m // tm,),
        in_specs=[
            pl.BlockSpec((tm, D_MODEL), lambda i: (i, 0)),
            pl.BlockSpec((1, D_MODEL), lambda i: (0, 0)),
            pl.BlockSpec((D_MODEL, 2 * D_MODEL), lambda i: (0, 0)),
        ],
        out_specs=pl.BlockSpec((tm, 2 * D_MODEL), lambda i: (i, 0)),
        out_shape=jax.ShapeDtypeStruct((m, 2 * D_MODEL), BF16),
        compiler_params=_params("parallel"),
        name="memkv",
    )(mem2d, g, wkv)


def _cross_kernel(x_ref, ya_ref, yb_ref, kv_ref, wout_ref, g_ref, wq_ref, wo_ref, o_ref, att_s):
    x1 = x_ref[...] + (jnp.dot(ya_ref[...], wout_ref[0:RWKV_W, :], preferred_element_type=F32)
                       + jnp.dot(yb_ref[...], wout_ref[RWKV_W:, :], preferred_element_type=F32))
    hc = _rms(x1, g_ref[...], NORM_EPS).astype(BF16)
    for h in range(CROSS_H):
        sl = slice(h * CROSS_D, (h + 1) * CROSS_D)
        q = jnp.dot(hc, wq_ref[:, sl], preferred_element_type=F32) * (CROSS_D ** -0.5)
        kh = kv_ref[:, sl]
        vh = kv_ref[:, D_MODEL + h * CROSS_D:D_MODEL + (h + 1) * CROSS_D]
        s = lax.dot_general(q.astype(BF16), kh, (((1,), (1,)), ((), ())), preferred_element_type=F32)
        s = s - jnp.max(s, axis=-1, keepdims=True)
        p = jnp.exp(s)
        p = p / jnp.sum(p, axis=-1, keepdims=True)
        att_s[:, sl] = jnp.dot(p.astype(BF16), vh, preferred_element_type=F32).astype(BF16)
    o_ref[...] = x1 + jnp.dot(att_s[...], wo_ref[...], preferred_element_type=F32)


def _cross(x2d, ya, yb, kv, bsz, seq, w_out, g, wq, wo):
    tm = ROW_TILE
    nt = seq // tm
    row = lambda b, t: (b * nt + t, 0)
    const = lambda b, t: (0, 0)
    return pl.pallas_call(
        _cross_kernel,
        grid=(bsz, nt),
        in_specs=[
            pl.BlockSpec((tm, D_MODEL), row),
            pl.BlockSpec((tm, RWKV_W), row),
            pl.BlockSpec((tm, DIFF_W), row),
            pl.BlockSpec((MEM_LEN, 2 * D_MODEL), lambda b, t: (b, 0)),
            pl.BlockSpec((D_MODEL, D_MODEL), const),
            pl.BlockSpec((1, D_MODEL), const),
            pl.BlockSpec((D_MODEL, D_MODEL), const),
            pl.BlockSpec((D_MODEL, D_MODEL), const),
        ],
        out_specs=pl.BlockSpec((tm, D_MODEL), row),
        out_shape=jax.ShapeDtypeStruct((bsz * seq, D_MODEL), F32),
        scratch_shapes=[pltpu.VMEM((tm, D_MODEL), BF16)],
        compiler_params=_params("parallel", "parallel"),
        name="outproj_cross",
    )(x2d, ya, yb, kv, w_out, g, wq, wo)


def _ffn_kernel(x_ref, g_ref, wg_ref, wv_ref, cw_ref, cb_ref, wd_ref, gf_ref, o_ref, carry_s, *, final):
    tm = x_ref.shape[0]
    n_ff = wg_ref.shape[0]

    @pl.when(pl.program_id(1) == 0)
    def _():
        carry_s[...] = jnp.zeros_like(carry_s)

    x = x_ref[...]
    h = _rms(x, g_ref[...], NORM_EPS).astype(BF16)
    row = lax.broadcasted_iota(jnp.int32, (tm, 1), 0)
    acc = None
    for j in range(n_ff):
        gate = jnp.dot(h, wg_ref[j], preferred_element_type=F32)
        val = jnp.dot(h, wv_ref[j], preferred_element_type=F32)
        prev = carry_s[j]
        g1 = jnp.where(row == 0, prev[7:8, :], pltpu.roll(gate, 1, axis=0))
        g2 = jnp.where(row == 0, prev[6:7, :],
                       jnp.where(row == 1, prev[7:8, :], pltpu.roll(gate, 2, axis=0)))
        carry_s[j] = gate[tm - 8:tm, :]
        cw = cw_ref[j]
        c = cw[0:1, :] * g2 + cw[1:2, :] * g1 + cw[2:3, :] * gate + cb_ref[j]
        act = c * _sigmoid(c) * val
        down = jnp.dot(act.astype(BF16), wd_ref[j], preferred_element_type=F32)
        acc = down if acc is None else acc + down
    out = x + acc
    o_ref[...] = _rms(out, gf_ref[...], NORM_EPS) if final else out


def _ffn(x2d, bsz, seq, g, wg, wv, cw, cb, wd, gf, final):
    tm = ROW_TILE
    nt = seq // tm
    n_ff, _, tf = wg.shape
    row = lambda b, t: (b * nt + t, 0)
    c2 = lambda b, t: (0, 0)
    c3 = lambda b, t: (0, 0, 0)
    once = pl.Buffered(1)
    return pl.pallas_call(
        functools.partial(_ffn_kernel, final=final),
        grid=(bsz, nt),
        in_specs=[
            pl.BlockSpec((tm, D_MODEL), row),
            pl.BlockSpec((1, D_MODEL), c2),
            pl.BlockSpec((n_ff, D_MODEL, tf), c3, pipeline_mode=once),
            pl.BlockSpec((n_ff, D_MODEL, tf), c3, pipeline_mode=once),
            pl.BlockSpec((n_ff, 3, tf), c3),
            pl.BlockSpec((n_ff, 1, tf), c3),
            pl.BlockSpec((n_ff, tf, D_MODEL), c3, pipeline_mode=once),
            pl.BlockSpec((1, D_MODEL), c2),
        ],
        out_specs=pl.BlockSpec((tm, D_MODEL), row),
        out_shape=jax.ShapeDtypeStruct((bsz * seq, D_MODEL), F32),
        scratch_shapes=[pltpu.VMEM((n_ff, 8, tf), F32)],
        compiler_params=_params("parallel", "arbitrary"),
        name="conv_ffn",
    )(x2d, g, wg, wv, cw, cb, wd, gf)


def _layer(x2d, pos2d, mem2d, bsz, seq, lambda_init, p):
    w_in = p["w_in"]
    pad = jnp.zeros((D_MODEL, LORA_PAD - (N_SHIFT - 3 * RWKV_W)), F32)
    w_r = jnp.concatenate([w_in[:, :N_SHIFT], pad], axis=1).astype(BF16)
    w_d = w_in[:, N_SHIFT:].astype(BF16)
    mix = jnp.concatenate([p["shift_mix"], jnp.zeros((LORA_PAD - (N_SHIFT - 3 * RWKV_W),), F32)])[None, :]
    w_lora = jnp.zeros((LORA_PAD, 3 * RWKV_W), F32)
    w_lora = w_lora.at[0:R_DECAY, 0:RWKV_W].set(p["w_lora_up"])
    w_lora = w_lora.at[R_DECAY:R_DECAY + R_AAA, RWKV_W:2 * RWKV_W].set(p["a_lora_up"])
    w_lora = w_lora.at[R_DECAY + R_AAA:R_DECAY + R_AAA + R_GATE, 2 * RWKV_W:].set(p["g_lora_up"])
    w_lora = w_lora.astype(BF16)
    v512 = lambda a: a.reshape(1, RWKV_W)

    zr, q, k, v = _inproj(x2d, pos2d, p["norm_mix"][None, :], w_r, w_d)
    ya = _rwkv(zr, bsz, seq, mix, w_lora, v512(p["w0"]), v512(p["a0"]), v512(p["k_k"]),
               v512(p["k_a"]), v512(p["r_k"]), v512(p["lnx_gain"]), v512(p["lnx_bias"]))
    yb = _diff_attn(q, k, v, bsz, seq, p["lam_q1"][None, :], p["lam_k1"][None, :],
                    p["lam_q2"][None, :], p["lam_k2"][None, :], p["subln_gain"][None, :], lambda_init)
    kv = _memkv(mem2d, p["norm_mem"][None, :], p["wkv_c"].astype(BF16))
    x2 = _cross(x2d, ya, yb, kv, bsz, seq, p["w_out"].astype(BF16), p["norm_cross"][None, :],
                p["wq_c"].astype(BF16), p["wo_c"].astype(BF16))
    n_ff = D_FF // FF_TILE
    w_up = p["w_up"]
    wg = w_up[:, :D_FF].reshape(D_MODEL, n_ff, FF_TILE).transpose(1, 0, 2).astype(BF16)
    wv = w_up[:, D_FF:].reshape(D_MODEL, n_ff, FF_TILE).transpose(1, 0, 2).astype(BF16)
    cw = p["conv_w"].reshape(3, n_ff, FF_TILE).transpose(1, 0, 2)
    cb = p["conv_b"].reshape(n_ff, 1, FF_TILE)
    wd = p["w_down"].reshape(n_ff, FF_TILE, D_MODEL).astype(BF16)
    return x2, (p["norm_ffn"][None, :], wg, wv, cw, cb, wd)


def kernel(x, mem, positions, norm_mix, w_in, shift_mix, w0, w_lora_up, a0, a_lora_up, g_lora_up, k_k, k_a, r_k, lnx_gain, lnx_bias, lam_q1, lam_k1, lam_q2, lam_k2, subln_gain, w_out, norm_cross, norm_mem, wq_c, wkv_c, wo_c, norm_ffn, w_up, conv_w, conv_b, w_down, norm_final):
    bsz, seq, _ = x.shape
    depth = norm_mix.shape[0]
    x2d = x.reshape(bsz * seq, D_MODEL)
    pos2d = positions.reshape(bsz * seq, 1)
    mem2d = mem.reshape(bsz * mem.shape[1], D_MODEL)
    stacked = dict(norm_mix=norm_mix, w_in=w_in, shift_mix=shift_mix, w0=w0, w_lora_up=w_lora_up,
                   a0=a0, a_lora_up=a_lora_up, g_lora_up=g_lora_up, k_k=k_k, k_a=k_a,
                   r_k=r_k.reshape(depth, RWKV_W), lnx_gain=lnx_gain, lnx_bias=lnx_bias,
                   lam_q1=lam_q1, lam_k1=lam_k1, lam_q2=lam_q2, lam_k2=lam_k2,
                   subln_gain=subln_gain, w_out=w_out, norm_cross=norm_cross, norm_mem=norm_mem,
                   wq_c=wq_c, wkv_c=wkv_c, wo_c=wo_c, norm_ffn=norm_ffn, w_up=w_up,
                   conv_w=conv_w, conv_b=conv_b, w_down=w_down)
    for l in range(depth):
        p = {name: a[l] for name, a in stacked.items()}
        lambda_init = 0.8 - 0.6 * math.exp(-0.3 * l)
        x2, (gn, wg, wv, cw, cb, wd) = _layer(x2d, pos2d, mem2d, bsz, seq, lambda_init, p)
        x2d = _ffn(x2, bsz, seq, gn, wg, wv, cw, cb, wd, norm_final[None, :], l == depth - 1)
    return x2d.reshape(bsz, seq, D_MODEL)
```

```python
import functools
import math

import numpy as np
import jax
import jax.numpy as jnp
from jax import lax
from jax.experimental import pallas as pl
from jax.experimental.pallas import tpu as pltpu

F32 = jnp.float32
BF16 = jnp.bfloat16

D_MODEL = 1024
RWKV_W = 512
RWKV_N = 64
R_DECAY = 32
R_AAA = 32
R_GATE = 96
DIFF_W = 512
DIFF_H = 4
DIFF_D = 64
N_SHIFT = 3 * RWKV_W + R_DECAY + R_AAA + R_GATE
MEM_LEN = 256
CROSS_H = 4
CROSS_D = D_MODEL // CROSS_H
D_FF = 2816
ROPE_THETA = 10000.0
NORM_EPS = 1e-6
LNX_EPS = 64e-5
SUBLN_EPS = 1e-5

LANES = 128
LORA_PAD = 256
ZR_W = 3 * RWKV_W + LORA_PAD
VMEM_LIMIT = 56 * 1024 * 1024

ROW_TILE = 512
CHUNK = 64
ATT_TQ = 512
ATT_TK = 512
FF_TILE = 256


def _mm(a, b):
    return jnp.dot(a.astype(BF16), b.astype(BF16), preferred_element_type=F32)


def _mm_nt(a, b):
    return lax.dot_general(a.astype(BF16), b.astype(BF16), (((1,), (1,)), ((), ())),
                           preferred_element_type=F32)


def _mm_tn(a, b):
    return lax.dot_general(a.astype(BF16), b.astype(BF16), (((0,), (0,)), ((), ())),
                           preferred_element_type=F32)


def _sigmoid(x):
    return 1.0 / (1.0 + jnp.exp(-x))


def _rms(x, g, eps):
    return x * lax.rsqrt(jnp.mean(x * x, axis=-1, keepdims=True) + eps) * g


def _split_dot(x, ones_b):
    hi = x.astype(BF16)
    lo = (x - hi.astype(F32)).astype(BF16)
    return (jnp.dot(hi, ones_b, preferred_element_type=F32)
            + jnp.dot(lo, ones_b, preferred_element_type=F32))


def _params(*sem):
    return pltpu.CompilerParams(dimension_semantics=sem, vmem_limit_bytes=VMEM_LIMIT)


def _inproj_kernel(x_ref, pos_ref, g_ref, wr_ref, wd_ref, zr_ref, q_ref, k_ref, v_ref):
    h = _rms(x_ref[...], g_ref[...], NORM_EPS).astype(BF16)
    for j in range(ZR_W // 256):
        sl = slice(j * 256, (j + 1) * 256)
        zr_ref[:, sl] = jnp.dot(h, wr_ref[:, sl], preferred_element_type=F32)

    lane = lax.broadcasted_iota(jnp.int32, (1, LANES), 1)
    first_half = (lane % DIFF_D) < (DIFF_D // 2)
    freq = (lane % (DIFF_D // 2)).astype(F32)
    inv = jnp.exp(freq * (-2.0 / DIFF_D * math.log(ROPE_THETA)))
    ang = pos_ref[...].astype(F32) * inv
    cos = jnp.cos(ang)
    sin = jnp.where(first_half, -jnp.sin(ang), jnp.sin(ang))

    def rope(xb):
        rot = jnp.where(first_half, pltpu.roll(xb, LANES - DIFF_D // 2, axis=1),
                        pltpu.roll(xb, DIFF_D // 2, axis=1))
        return xb * cos + rot * sin

    scale = DIFF_D ** -0.5 * math.log2(math.e)
    for j in range(DIFF_W // 256):
        zq = jnp.dot(h, wd_ref[:, j * 256:(j + 1) * 256], preferred_element_type=F32)
        zk = jnp.dot(h, wd_ref[:, DIFF_W + j * 256:DIFF_W + (j + 1) * 256],
                     preferred_element_type=F32)
        zv = jnp.dot(h, wd_ref[:, 2 * DIFF_W + j * 256:2 * DIFF_W + (j + 1) * 256],
                     preferred_element_type=F32)
        for u in range(2):
            c0 = j * 256 + u * LANES
            q_ref[:, c0:c0 + LANES] = (rope(zq[:, u * LANES:(u + 1) * LANES]) * scale).astype(BF16)
            k_ref[:, c0:c0 + LANES] = rope(zk[:, u * LANES:(u + 1) * LANES]).astype(BF16)
        v_ref[:, j * 256:(j + 1) * 256] = zv.astype(BF16)


def _inproj(x2d, pos2d, g, w_r, w_d):
    m = x2d.shape[0]
    tm = ROW_TILE
    row = lambda i: (i, 0)
    const = lambda i: (0, 0)
    return pl.pallas_call(
        _inproj_kernel,
        grid=(m // tm,),
        in_specs=[
            pl.BlockSpec((tm, D_MODEL), row),
            pl.BlockSpec((tm, 1), row),
            pl.BlockSpec((1, D_MODEL), const),
            pl.BlockSpec((D_MODEL, ZR_W), const),
            pl.BlockSpec((D_MODEL, 3 * DIFF_W), const),
        ],
        out_specs=[
            pl.BlockSpec((tm, ZR_W), row),
            pl.BlockSpec((tm, DIFF_W), row),
            pl.BlockSpec((tm, DIFF_W), row),
            pl.BlockSpec((tm, DIFF_W), row),
        ],
        out_shape=[
            jax.ShapeDtypeStruct((m, ZR_W), F32),
            jax.ShapeDtypeStruct((m, DIFF_W), BF16),
            jax.ShapeDtypeStruct((m, DIFF_W), BF16),
            jax.ShapeDtypeStruct((m, DIFF_W), BF16),
        ],
        compiler_params=_params("parallel"),
        name="inproj",
    )(x2d, pos2d, g, w_r, w_d)


def _rwkv_kernel(z_ref, mix_ref, wl_ref, w0_ref, a0_ref, kk_ref, ka_ref, rk_ref, gain_ref,
                 bias_ref, y_ref, carry_s, st_s, r_s, lw_s, k_s, v_s, al_s, be_s, g_s, bo_s, y_s):
    tb = z_ref.shape[0]
    n_pair = RWKV_W // LANES

    @pl.when(pl.program_id(1) == 0)
    def _():
        carry_s[...] = jnp.zeros_like(carry_s)
        st_s[...] = jnp.zeros_like(st_s)

    row = lax.broadcasted_iota(jnp.int32, (tb, 1), 0)

    def shifted(c0, c1):
        zc = z_ref[:, c0:c1]
        zp = jnp.where(row == 0, carry_s[7:8, c0:c1], pltpu.roll(zc, 1, axis=0))
        return zc + (zp - zc) * mix_ref[:, c0:c1]

    ri = lax.broadcasted_iota(jnp.int32, (LANES, LANES), 0)
    ci = lax.broadcasted_iota(jnp.int32, (LANES, LANES), 1)
    same_head = (ri // RWKV_N) == (ci // RWKV_N)
    ones_head = jnp.where(same_head, 1.0, 0.0).astype(BF16)

    def headsum(x):
        return jnp.concatenate(
            [_split_dot(x[:, p * LANES:(p + 1) * LANES], ones_head) for p in range(n_pair)], axis=1)

    zl = shifted(3 * RWKV_W, ZR_W)
    ll = lax.broadcasted_iota(jnp.int32, (1, LORA_PAD), 1)
    act = jnp.where(ll < R_DECAY, jnp.tanh(zl),
                    jnp.where(ll < R_DECAY + R_AAA, zl, _sigmoid(zl)))
    lo = jnp.dot(act.astype(BF16), wl_ref[...], preferred_element_type=F32)
    lw_s[...] = -math.exp(-0.5) * _sigmoid(w0_ref[...] + lo[:, 0:RWKV_W])
    a = _sigmoid(a0_ref[...] + lo[:, RWKV_W:2 * RWKV_W])
    g_s[...] = lo[:, 2 * RWKV_W:3 * RWKV_W]

    r = shifted(0, RWKV_W)
    k = shifted(RWKV_W, 2 * RWKV_W)
    v = shifted(2 * RWKV_W, 3 * RWKV_W)
    carry_s[...] = z_ref[tb - 8:tb, :]
    r_s[...] = r
    v_s[...] = v
    kk = k * kk_ref[...]
    ss = headsum(kk * kk)
    alpha = kk * lax.rsqrt(jnp.maximum(ss, 1e-24))
    al_s[...] = alpha
    be_s[...] = alpha * a
    k2 = k * (1.0 + (a - 1.0) * ka_ref[...])
    k_s[...] = k2
    bo_s[...] = headsum(r * k2 * rk_ref[...]) * v

    c_len = CHUNK
    tri = jnp.where(lax.broadcasted_iota(jnp.int32, (c_len, c_len), 0)
                    >= lax.broadcasted_iota(jnp.int32, (c_len, c_len), 1), 1.0, 0.0).astype(BF16)
    lane = lax.broadcasted_iota(jnp.int32, (1, LANES), 1)
    head0 = lane < RWKV_N
    strict = same_head & (ri > ci)
    incl = same_head & (ri >= ci)
    b16 = (ri // 16) == (ci // 16)
    b32 = (ri // 32) == (ci // 32)
    eye = jnp.where(ri == ci, 1.0, 0.0).astype(F32)

    def stack2(x):
        return jnp.concatenate([jnp.where(head0, x, 0.0), jnp.where(head0, 0.0, x)], axis=0)

    def fold(x):
        return x[:c_len] + x[c_len:]

    def chunk_body(c, carry):
        r0 = pl.multiple_of(c * c_len, c_len)
        for p in range(n_pair):
            sl = (pl.ds(r0, c_len), slice(p * LANES, (p + 1) * LANES))
            r_c, lw, k_c, v_c, al, be = r_s[sl], lw_s[sl], k_s[sl], v_s[sl], al_s[sl], be_s[sl]
            hi = lw.astype(BF16)
            r1 = lw - hi.astype(F32)
            mid = r1.astype(BF16)
            low = (r1 - mid.astype(F32)).astype(BF16)
            cum = (jnp.dot(tri, hi, preferred_element_type=F32)
                   + jnp.dot(tri, mid, preferred_element_type=F32)
                   + jnp.dot(tri, low, preferred_element_type=F32))
            tot = cum[c_len - 1:c_len, :]
            e_neg = jnp.exp(-cum)
            e_end = jnp.exp(tot - cum)
            a_t = al * jnp.exp(cum - lw)
            r_t = r_c * jnp.exp(cum)
            a_st, r_st, v_st = stack2(a_t), stack2(r_t), stack2(v_c)
            b_n, k_n = be * e_neg, k_c * e_neg
            gram = _mm_nt(jnp.concatenate([a_st, r_st], axis=0),
                          jnp.concatenate([b_n, b_n, k_n, k_n], axis=0))
            a_ab = jnp.where(strict, gram[:LANES, :LANES], 0.0)
            a_ak = jnp.where(strict, gram[:LANES, LANES:], 0.0)
            a_rb = jnp.where(incl, gram[LANES:, :LANES], 0.0)
            a_rk = jnp.where(incl, gram[LANES:, LANES:], 0.0)
            a0 = jnp.where(b16, a_ab, 0.0)
            e1 = jnp.where(b32 & jnp.logical_not(b16), a_ab, 0.0)
            e2 = jnp.where(b32, 0.0, a_ab)
            p2 = _mm(a0, a0)
            p4 = _mm(p2, p2)
            p8 = _mm(p4, p4)
            t = eye - a0
            t = t + _mm(t, p2)
            t = t + _mm(t, p4)
            t = t + _mm(t, p8)
            t = t - _mm(_mm(t, e1), t)
            t = t - _mm(_mm(t, e2), t)
            akv = _mm(a_ak, v_st)
            w = _mm(t, jnp.concatenate([a_st, akv], axis=1))
            rbw = _mm(a_rb, w)
            rkv = _mm(a_rk, v_st)
            r_hat = fold(r_st - rbw[:, :LANES])
            y0 = fold(rkv - rbw[:, LANES:])
            ta_tv = jnp.concatenate([fold(w[:, :LANES]), fold(w[:, LANES:])], axis=1)
            bw = _mm_tn(be * e_end, ta_tv)
            kv = _mm_tn(k_c * e_end, v_c)
            m_p = jnp.where(same_head, bw[:, :LANES], 0.0)
            n_p = jnp.where(same_head, kv - bw[:, LANES:], 0.0)
            decay_col = jnp.exp(jnp.broadcast_to(tot, (LANES, LANES)).T)
            st = st_s[p]
            zz = _mm(jnp.concatenate([m_p, r_hat], axis=0), st)
            y_s[sl] = zz[LANES:] + y0
            st_s[p] = decay_col * st - zz[:LANES] + n_p
        return carry

    lax.fori_loop(0, tb // c_len, chunk_body, 0)

    y = y_s[...]
    mu = headsum(y) * (1.0 / RWKV_N)
    d = y - mu
    var = headsum(d * d) * (1.0 / RWKV_N)
    yn = d * lax.rsqrt(var + LNX_EPS) * gain_ref[...] + bias_ref[...]
    y_ref[...] = ((yn + bo_s[...]) * g_s[...]).astype(y_ref.dtype)


def _rwkv(zr, bsz, seq, mix, w_lora, w0, a0, k_k, k_a, r_k, gain, bias):
    tb = ROW_TILE
    nt = seq // tb
    row = lambda b, t: (b * nt + t, 0)
    const = lambda b, t: (0, 0)
    vec = pl.BlockSpec((1, RWKV_W), const)
    big = pltpu.VMEM((tb, RWKV_W), F32)
    return pl.pallas_call(
        _rwkv_kernel,
        grid=(bsz, nt),
        in_specs=[
            pl.BlockSpec((tb, ZR_W), row),
            pl.BlockSpec((1, ZR_W), const),
            pl.BlockSpec((LORA_PAD, 3 * RWKV_W), const),
            vec, vec, vec, vec, vec, vec, vec,
        ],
        out_specs=pl.BlockSpec((tb, RWKV_W), row),
        out_shape=jax.ShapeDtypeStruct((bsz * seq, RWKV_W), BF16),
        scratch_shapes=[
            pltpu.VMEM((8, ZR_W), F32),
            pltpu.VMEM((RWKV_W // LANES, LANES, LANES), F32),
            big, big, big, big, big, big, big, big, big,
        ],
        compiler_params=_params("parallel", "arbitrary"),
        name="rwkv7",
    )(zr, mix, w_lora, w0, a0, k_k, k_a, r_k, gain, bias)


def _attn_kernel(qi_tab, ki_tab, q_ref, k_ref, v_ref, lq1_ref, lk1_ref, lq2_ref, lk2_ref, sg_ref,
                 o_ref, q_s, m_s, acc_s, *, lambda_init, ratio):
    tq, tk = q_ref.shape[0], k_ref.shape[0]
    n_col = tk // LANES
    pidx = pl.program_id(1)
    qi = qi_tab[pidx]
    ki = ki_tab[pidx]
    nt = (((1,), (1,)), ((), ()))

    @pl.when(ki == 0)
    def _():
        map0 = lax.broadcasted_iota(jnp.int32, (1, LANES), 1) < DIFF_D
        for h in range(DIFF_H):
            q = q_ref[:, h * LANES:(h + 1) * LANES]
            zero = jnp.zeros_like(q)
            q_s[2 * h] = jnp.where(map0, q, zero)
            q_s[2 * h + 1] = jnp.where(map0, zero, q)
        m_s[...] = jnp.full_like(m_s, -jnp.inf)
        acc_s[...] = jnp.zeros_like(acc_s)

    def step(masked):
        if masked:
            rowp = qi * tq + lax.broadcasted_iota(jnp.int32, (tq, LANES), 0)
            colp = ki * tk + lax.broadcasted_iota(jnp.int32, (tq, LANES), 1)
            keep = [colp + j * LANES <= rowp for j in range(n_col)]
        ones = jnp.ones((tk, LANES), BF16)
        for h in range(DIFF_H):
            k = k_ref[:, h * LANES:(h + 1) * LANES]
            v_ext = jnp.concatenate([v_ref[:, h * LANES:(h + 1) * LANES], ones], axis=1)
            for c in range(2):
                i = 2 * h + c
                s = lax.dot_general(q_s[i], k, nt, preferred_element_type=F32)
                cols = [s[:, j * LANES:(j + 1) * LANES] for j in range(n_col)]
                if masked:
                    cols = [jnp.where(keep[j], cols[j], -jnp.inf) for j in range(n_col)]
                mx = cols[0]
                for j in range(1, n_col):
                    mx = jnp.maximum(mx, cols[j])
                m_old = m_s[i]
                m_new = jnp.maximum(m_old, jnp.max(mx, axis=-1, keepdims=True))
                corr = jnp.exp2(m_old - m_new)
                p = jnp.concatenate([jnp.exp2(cj - m_new).astype(BF16) for cj in cols], axis=1)
                pv = jnp.dot(p, v_ext, preferred_element_type=F32)
                acc_s[i] = jnp.concatenate([corr, corr], axis=1) * acc_s[i] + pv
                m_s[i] = m_new

    on_diag = ki >= qi * ratio

    @pl.when(on_diag)
    def _():
        step(True)

    @pl.when(jnp.logical_not(on_diag))
    def _():
        step(False)

    @pl.when(ki == (qi + 1) * ratio - 1)
    def _():
        lam = (jnp.exp(jnp.sum(lq1_ref[...] * lk1_ref[...], axis=-1, keepdims=True))
               - jnp.exp(jnp.sum(lq2_ref[...] * lk2_ref[...], axis=-1, keepdims=True))
               + lambda_init)
        for h in range(DIFF_H):
            a1 = acc_s[2 * h]
            a2 = acc_s[2 * h + 1]
            o = a1[:, :LANES] / a1[:, LANES:] - lam * (a2[:, :LANES] / a2[:, LANES:])
            o = _rms(o, sg_ref[...], SUBLN_EPS) * (1.0 - lambda_init)
            o_ref[:, h * LANES:(h + 1) * LANES] = o.astype(o_ref.dtype)


def _diff_attn(q, k, v, bsz, seq, lq1, lk1, lq2, lk2, sg, lambda_init):
    tq, tk = ATT_TQ, ATT_TK
    ratio = tq // tk
    nq, nk = seq // tq, seq // tk
    pairs = [(a, b) for a in range(nq) for b in range((a + 1) * ratio)]
    qi_tab = jnp.asarray(np.array([a for a, _ in pairs], np.int32))
    ki_tab = jnp.asarray(np.array([b for _, b in pairs], np.int32))
    qmap = lambda b, p, qt, kt: (b * nq + qt[p], 0)
    kmap = lambda b, p, qt, kt: (b * nk + kt[p], 0)
    const = lambda b, p, qt, kt: (0, 0)
    lam_spec = pl.BlockSpec((1, DIFF_D), const)
    grid_spec = pltpu.PrefetchScalarGridSpec(
        num_scalar_prefetch=2,
        grid=(bsz, len(pairs)),
        in_specs=[
            pl.BlockSpec((tq, DIFF_W), qmap),
            pl.BlockSpec((tk, DIFF_W), kmap),
            pl.BlockSpec((tk, DIFF_W), kmap),
            lam_spec, lam_spec, lam_spec, lam_spec,
            pl.BlockSpec((1, 2 * DIFF_D), const),
        ],
        out_specs=pl.BlockSpec((tq, DIFF_W), qmap),
        scratch_shapes=[
            pltpu.VMEM((2 * DIFF_H, tq, LANES), BF16),
            pltpu.VMEM((2 * DIFF_H, tq, LANES), F32),
            pltpu.VMEM((2 * DIFF_H, tq, 2 * LANES), F32),
        ],
    )
    return pl.pallas_call(
        functools.partial(_attn_kernel, lambda_init=lambda_init, ratio=ratio),
        grid_spec=grid_spec,
        out_shape=jax.ShapeDtypeStruct((bsz * seq, DIFF_W), BF16),
        compiler_params=_params("parallel", "arbitrary"),
        name="diff_attn",
    )(qi_tab, ki_tab, q, k, v, lq1, lk1, lq2, lk2, sg)


def _memkv_kernel(m_ref, g_ref, w_ref, o_ref):
    h = _rms(m_ref[...], g_ref[...], NORM_EPS).astype(BF16)
    for j in range(o_ref.shape[1] // 256):
        sl = slice(j * 256, (j + 1) * 256)
        o_ref[:, sl] = jnp.dot(h, w_ref[:, sl], preferred_element_type=F32).astype(o_ref.dtype)


def _memkv(mem2d, g, wkv):
    m = mem2d.shape[0]
    tm = MEM_LEN
    return pl.pallas_call(
        _memkv_kernel,
        grid=(m // tm,),
        in_specs=[
            pl.BlockSpec((tm, D_MODEL), lambda i: (i, 0)),
            pl.BlockSpec((1, D_MODEL), lambda i: (0, 0)),
            pl.BlockSpec((D_MODEL, 2 * D_MODEL), lambda i: (0, 0)),
        ],
        out_specs=pl.BlockSpec((tm, 2 * D_MODEL), lambda i: (i, 0)),
        out_shape=jax.ShapeDtypeStruct((m, 2 * D_MODEL), BF16),
        compiler_params=_params("parallel"),
        name="memkv",
    )(mem2d, g, wkv)


def _cross_kernel(x_ref, ya_ref, yb_ref, kv_ref, wout_ref, g_ref, wq_ref, wo_ref, o_ref, att_s):
    x1 = x_ref[...] + (jnp.dot(ya_ref[...], wout_ref[0:RWKV_W, :], preferred_element_type=F32)
                       + jnp.dot(yb_ref[...], wout_ref[RWKV_W:, :], preferred_element_type=F32))
    hc = _rms(x1, g_ref[...], NORM_EPS).astype(BF16)
    for h in range(CROSS_H):
        sl = slice(h * CROSS_D, (h + 1) * CROSS_D)
        q = jnp.dot(hc, wq_ref[:, sl], preferred_element_type=F32) * (CROSS_D ** -0.5)
        kh = kv_ref[:, sl]
        vh = kv_ref[:, D_MODEL + h * CROSS_D:D_MODEL + (h + 1) * CROSS_D]
        s = lax.dot_general(q.astype(BF16), kh, (((1,), (1,)), ((), ())), preferred_element_type=F32)
        s = s - jnp.max(s, axis=-1, keepdims=True)
        p = jnp.exp(s)
        p = p / jnp.sum(p, axis=-1, keepdims=True)
        att_s[:, sl] = jnp.dot(p.astype(BF16), vh, preferred_element_type=F32).astype(BF16)
    o_ref[...] = x1 + jnp.dot(att_s[...], wo_ref[...], preferred_element_type=F32)


def _cross(x2d, ya, yb, kv, bsz, seq, w_out, g, wq, wo):
    tm = ROW_TILE
    nt = seq // tm
    row = lambda b, t: (b * nt + t, 0)
    const = lambda b, t: (0, 0)
    return pl.pallas_call(
        _cross_kernel,
        grid=(bsz, nt),
        in_specs=[
            pl.BlockSpec((tm, D_MODEL), row),
            pl.BlockSpec((tm, RWKV_W), row),
            pl.BlockSpec((tm, DIFF_W), row),
            pl.BlockSpec((MEM_LEN, 2 * D_MODEL), lambda b, t: (b, 0)),
            pl.BlockSpec((D_MODEL, D_MODEL), const),
            pl.BlockSpec((1, D_MODEL), const),
            pl.BlockSpec((D_MODEL, D_MODEL), const),
            pl.BlockSpec((D_MODEL, D_MODEL), const),
        ],
        out_specs=pl.BlockSpec((tm, D_MODEL), row),
        out_shape=jax.ShapeDtypeStruct((bsz * seq, D_MODEL), F32),
        scratch_shapes=[pltpu.VMEM((tm, D_MODEL), BF16)],
        compiler_params=_params("parallel", "parallel"),
        name="outproj_cross",
    )(x2d, ya, yb, kv, w_out, g, wq, wo)


def _ffn_kernel(x_ref, g_ref, wg_ref, wv_ref, cw_ref, cb_ref, wd_ref, gf_ref, o_ref, carry_s, *, final):
    tm = x_ref.shape[0]
    n_ff = wg_ref.shape[0]

    @pl.when(pl.program_id(1) == 0)
    def _():
        carry_s[...] = jnp.zeros_like(carry_s)

    x = x_ref[...]
    h = _rms(x, g_ref[...], NORM_EPS).astype(BF16)
    row = lax.broadcasted_iota(jnp.int32, (tm, 1), 0)
    acc = None
    for j in range(n_ff):
        gate = jnp.dot(h, wg_ref[j], preferred_element_type=F32)
        val = jnp.dot(h, wv_ref[j], preferred_element_type=F32)
        prev = carry_s[j]
        g1 = jnp.where(row == 0, prev[7:8, :], pltpu.roll(gate, 1, axis=0))
        g2 = jnp.where(row == 0, prev[6:7, :],
                       jnp.where(row == 1, prev[7:8, :], pltpu.roll(gate, 2, axis=0)))
        carry_s[j] = gate[tm - 8:tm, :]
        cw = cw_ref[j]
        c = cw[0:1, :] * g2 + cw[1:2, :] * g1 + cw[2:3, :] * gate + cb_ref[j]
        act = c * _sigmoid(c) * val
        down = jnp.dot(act.astype(BF16), wd_ref[j], preferred_element_type=F32)
        acc = down if acc is None else acc + down
    out = x + acc
    o_ref[...] = _rms(out, gf_ref[...], NORM_EPS) if final else out


def _ffn(x2d, bsz, seq, g, wg, wv, cw, cb, wd, gf, final):
    tm = ROW_TILE
    nt = seq // tm
    n_ff, _, tf = wg.shape
    row = lambda b, t: (b * nt + t, 0)
    c2 = lambda b, t: (0, 0)
    c3 = lambda b, t: (0, 0, 0)
    once = pl.Buffered(1)
    return pl.pallas_call(
        functools.partial(_ffn_kernel, final=final),
        grid=(bsz, nt),
        in_specs=[
            pl.BlockSpec((tm, D_MODEL), row),
            pl.BlockSpec((1, D_MODEL), c2),
            pl.BlockSpec((n_ff, D_MODEL, tf), c3, pipeline_mode=once),
            pl.BlockSpec((n_ff, D_MODEL, tf), c3, pipeline_mode=once),
            pl.BlockSpec((n_ff, 3, tf), c3),
            pl.BlockSpec((n_ff, 1, tf), c3),
            pl.BlockSpec((n_ff, tf, D_MODEL), c3, pipeline_mode=once),
            pl.BlockSpec((1, D_MODEL), c2),
        ],
        out_specs=pl.BlockSpec((tm, D_MODEL), row),
        out_shape=jax.ShapeDtypeStruct((bsz * seq, D_MODEL), F32),
        scratch_shapes=[pltpu.VMEM((n_ff, 8, tf), F32)],
        compiler_params=_params("parallel", "arbitrary"),
        name="conv_ffn",
    )(x2d, g, wg, wv, cw, cb, wd, gf)


def _layer(x2d, pos2d, mem2d, bsz, seq, lambda_init, p):
    w_in = p["w_in"]
    pad = jnp.zeros((D_MODEL, LORA_PAD - (N_SHIFT - 3 * RWKV_W)), F32)
    w_r = jnp.concatenate([w_in[:, :N_SHIFT], pad], axis=1).astype(BF16)
    w_d = w_in[:, N_SHIFT:].astype(BF16)
    mix = jnp.concatenate([p["shift_mix"], jnp.zeros((LORA_PAD - (N_SHIFT - 3 * RWKV_W),), F32)])[None, :]
    w_lora = jnp.zeros((LORA_PAD, 3 * RWKV_W), F32)
    w_lora = w_lora.at[0:R_DECAY, 0:RWKV_W].set(p["w_lora_up"])
    w_lora = w_lora.at[R_DECAY:R_DECAY + R_AAA, RWKV_W:2 * RWKV_W].set(p["a_lora_up"])
    w_lora = w_lora.at[R_DECAY + R_AAA:R_DECAY + R_AAA + R_GATE, 2 * RWKV_W:].set(p["g_lora_up"])
    w_lora = w_lora.astype(BF16)
    v512 = lambda a: a.reshape(1, RWKV_W)

    zr, q, k, v = _inproj(x2d, pos2d, p["norm_mix"][None, :], w_r, w_d)
    ya = _rwkv(zr, bsz, seq, mix, w_lora, v512(p["w0"]), v512(p["a0"]), v512(p["k_k"]),
               v512(p["k_a"]), v512(p["r_k"]), v512(p["lnx_gain"]), v512(p["lnx_bias"]))
    yb = _diff_attn(q, k, v, bsz, seq, p["lam_q1"][None, :], p["lam_k1"][None, :],
                    p["lam_q2"][None, :], p["lam_k2"][None, :], p["subln_gain"][None, :], lambda_init)
    kv = _memkv(mem2d, p["norm_mem"][None, :], p["wkv_c"].astype(BF16))
    x2 = _cross(x2d, ya, yb, kv, bsz, seq, p["w_out"].astype(BF16), p["norm_cross"][None, :],
                p["wq_c"].astype(BF16), p["wo_c"].astype(BF16))
    n_ff = D_FF // FF_TILE
    w_up = p["w_up"]
    wg = w_up[:, :D_FF].reshape(D_MODEL, n_ff, FF_TILE).transpose(1, 0, 2).astype(BF16)
    wv = w_up[:, D_FF:].reshape(D_MODEL, n_ff, FF_TILE).transpose(1, 0, 2).astype(BF16)
    cw = p["conv_w"].reshape(3, n_ff, FF_TILE).transpose(1, 0, 2)
    cb = p["conv_b"].reshape(n_ff, 1, FF_TILE)
    wd = p["w_down"].reshape(n_ff, FF_TILE, D_MODEL).astype(BF16)
    return x2, (p["norm_ffn"][None, :], wg, wv, cw, cb, wd)


def kernel(x, mem, positions, norm_mix, w_in, shift_mix, w0, w_lora_up, a0, a_lora_up, g_lora_up, k_k, k_a, r_k, lnx_gain, lnx_bias, lam_q1, lam_k1, lam_q2, lam_k2, subln_gain, w_out, norm_cross, norm_mem, wq_c, wkv_c, wo_c, norm_ffn, w_up, conv_w, conv_b, w_down, norm_final):
    bsz, seq, _ = x.shape
    depth = norm_mix.shape[0]
    x2d = x.reshape(bsz * seq, D_MODEL)
    pos2d = positions.reshape(bsz * seq, 1)
    mem2d = mem.reshape(bsz * mem.shape[1], D_MODEL)
    stacked = dict(norm_mix=norm_mix, w_in=w_in, shift_mix=shift_mix, w0=w0, w_lora_up=w_lora_up,
                   a0=a0, a_lora_up=a_lora_up, g_lora_up=g_lora_up, k_k=k_k, k_a=k_a,
                   r_k=r_k.reshape(depth, RWKV_W), lnx_gain=lnx_gain, lnx_bias=lnx_bias,
                   lam_q1=lam_q1, lam_k1=lam_k1, lam_q2=lam_q2, lam_k2=lam_k2,
                   subln_gain=subln_gain, w_out=w_out, norm_cross=norm_cross, norm_mem=norm_mem,
                   wq_c=wq_c, wkv_c=wkv_c, wo_c=wo_c, norm_ffn=norm_ffn, w_up=w_up,
                   conv_w=conv_w, conv_b=conv_b, w_down=w_down)
    for l in range(depth):
        p = {name: a[l] for name, a in stacked.items()}
        lambda_init = 0.8 - 0.6 * math.exp(-0.3 * l)
        x2, (gn, wg, wv, cw, cb, wd) = _layer(x2d, pos2d, mem2d, bsz, seq, lambda_init, p)
        x2d = _ffn(x2, bsz, seq, gn, wg, wv, cw, cb, wd, norm_final[None, :], l == depth - 1)
    return x2d.reshape(bsz, seq, D_MODEL)
```

```python
import functools
import math

import numpy as np
import jax
import jax.numpy as jnp
from jax import lax
from jax.experimental import pallas as pl
from jax.experimental.pallas import tpu as pltpu

F32 = jnp.float32
BF16 = jnp.bfloat16

D_MODEL = 1024
RWKV_W = 512
RWKV_N = 64
R_DECAY = 32
R_AAA = 32
R_GATE = 96
DIFF_W = 512
DIFF_H = 4
DIFF_D = 64
N_SHIFT = 3 * RWKV_W + R_DECAY + R_AAA + R_GATE
MEM_LEN = 256
CROSS_H = 4
CROSS_D = D_MODEL // CROSS_H
D_FF = 2816
ROPE_THETA = 10000.0
NORM_EPS = 1e-6
LNX_EPS = 64e-5
SUBLN_EPS = 1e-5

LANES = 128
LORA_PAD = 256
ZR_W = 3 * RWKV_W + LORA_PAD
VMEM_LIMIT = 56 * 1024 * 1024

ROW_TILE = 512
CHUNK = 64
ATT_TQ = 512
ATT_TK = 512
FF_TILE = 256


def _mm(a, b):
    return jnp.dot(a.astype(BF16), b.astype(BF16), preferred_element_type=F32)


def _mm_nt(a, b):
    return lax.dot_general(a.astype(BF16), b.astype(BF16), (((1,), (1,)), ((), ())),
                           preferred_element_type=F32)


def _mm_tn(a, b):
    return lax.dot_general(a.astype(BF16), b.astype(BF16), (((0,), (0,)), ((), ())),
                           preferred_element_type=F32)


def _sigmoid(x):
    return 1.0 / (1.0 + jnp.exp(-x))


def _rms(x, g, eps):
    return x * lax.rsqrt(jnp.mean(x * x, axis=-1, keepdims=True) + eps) * g


def _split_dot(x, ones_b):
    hi = x.astype(BF16)
    lo = (x - hi.astype(F32)).astype(BF16)
    return (jnp.dot(hi, ones_b, preferred_element_type=F32)
            + jnp.dot(lo, ones_b, preferred_element_type=F32))


def _params(*sem):
    return pltpu.CompilerParams(dimension_semantics=sem, vmem_limit_bytes=VMEM_LIMIT)


def _inproj_kernel(x_ref, pos_ref, g_ref, wr_ref, wd_ref, zr_ref, q_ref, k_ref, v_ref):
    h = _rms(x_ref[...], g_ref[...], NORM_EPS).astype(BF16)
    for j in range(ZR_W // 256):
        sl = slice(j * 256, (j + 1) * 256)
        zr_ref[:, sl] = jnp.dot(h, wr_ref[:, sl], preferred_element_type=F32)

    lane = lax.broadcasted_iota(jnp.int32, (1, LANES), 1)
    first_half = (lane % DIFF_D) < (DIFF_D // 2)
    freq = (lane % (DIFF_D // 2)).astype(F32)
    inv = jnp.exp(freq * (-2.0 / DIFF_D * math.log(ROPE_THETA)))
    ang = pos_ref[...].astype(F32) * inv
    cos = jnp.cos(ang)
    sin = jnp.where(first_half, -jnp.sin(ang), jnp.sin(ang))

    def rope(xb):
        rot = jnp.where(first_half, pltpu.roll(xb, LANES - DIFF_D // 2, axis=1),
                        pltpu.roll(xb, DIFF_D // 2, axis=1))
        return xb * cos + rot * sin

    scale = DIFF_D ** -0.5 * math.log2(math.e)
    for j in range(DIFF_W // 256):
        zq = jnp.dot(h, wd_ref[:, j * 256:(j + 1) * 256], preferred_element_type=F32)
        zk = jnp.dot(h, wd_ref[:, DIFF_W + j * 256:DIFF_W + (j + 1) * 256],
                     preferred_element_type=F32)
        zv = jnp.dot(h, wd_ref[:, 2 * DIFF_W + j * 256:2 * DIFF_W + (j + 1) * 256],
                     preferred_element_type=F32)
        for u in range(2):
            c0 = j * 256 + u * LANES
            q_ref[:, c0:c0 + LANES] = (rope(zq[:, u * LANES:(u + 1) * LANES]) * scale).astype(BF16)
            k_ref[:, c0:c0 + LANES] = rope(zk[:, u * LANES:(u + 1) * LANES]).astype(BF16)
        v_ref[:, j * 256:(j + 1) * 256] = zv.astype(BF16)


def _inproj(x2d, pos2d, g, w_r, w_d):
    m = x2d.shape[0]
    tm = ROW_TILE
    row = lambda i: (i, 0)
    const = lambda i: (0, 0)
    return pl.pallas_call(
        _inproj_kernel,
        grid=(m // tm,),
        in_specs=[
            pl.BlockSpec((tm, D_MODEL), row),
            pl.BlockSpec((tm, 1), row),
            pl.BlockSpec((1, D_MODEL), const),
            pl.BlockSpec((D_MODEL, ZR_W), const),
            pl.BlockSpec((D_MODEL, 3 * DIFF_W), const),
        ],
        out_specs=[
            pl.BlockSpec((tm, ZR_W), row),
            pl.BlockSpec((tm, DIFF_W), row),
            pl.BlockSpec((tm, DIFF_W), row),
            pl.BlockSpec((tm, DIFF_W), row),
        ],
        out_shape=[
            jax.ShapeDtypeStruct((m, ZR_W), F32),
            jax.ShapeDtypeStruct((m, DIFF_W), BF16),
            jax.ShapeDtypeStruct((m, DIFF_W), BF16),
            jax.ShapeDtypeStruct((m, DIFF_W), BF16),
        ],
        compiler_params=_params("parallel"),
        name="inproj",
    )(x2d, pos2d, g, w_r, w_d)


def _rwkv_kernel(z_ref, mix_ref, wl_ref, w0_ref, a0_ref, kk_ref, ka_ref, rk_ref, gain_ref,
                 bias_ref, y_ref, carry_s, st_s, r_s, lw_s, k_s, v_s, al_s, be_s, g_s, bo_s, y_s):
    tb = z_ref.shape[0]
    n_pair = RWKV_W // LANES

    @pl.when(pl.program_id(1) == 0)
    def _():
        carry_s[...] = jnp.zeros_like(carry_s)
        st_s[...] = jnp.zeros_like(st_s)

    row = lax.broadcasted_iota(jnp.int32, (tb, 1), 0)

    def shifted(c0, c1):
        zc = z_ref[:, c0:c1]
        zp = jnp.where(row == 0, carry_s[7:8, c0:c1], pltpu.roll(zc, 1, axis=0))
        return zc + (zp - zc) * mix_ref[:, c0:c1]

    ri = lax.broadcasted_iota(jnp.int32, (LANES, LANES), 0)
    ci = lax.broadcasted_iota(jnp.int32, (LANES, LANES), 1)
    same_head = (ri // RWKV_N) == (ci // RWKV_N)
    ones_head = jnp.where(same_head, 1.0, 0.0).astype(BF16)

    def headsum(x):
        return jnp.concatenate(
            [_split_dot(x[:, p * LANES:(p + 1) * LANES], ones_head) for p in range(n_pair)], axis=1)

    zl = shifted(3 * RWKV_W, ZR_W)
    ll = lax.broadcasted_iota(jnp.int32, (1, LORA_PAD), 1)
    act = jnp.where(ll < R_DECAY, jnp.tanh(zl),
                    jnp.where(ll < R_DECAY + R_AAA, zl, _sigmoid(zl)))
    lo = jnp.dot(act.astype(BF16), wl_ref[...], preferred_element_type=F32)
    lw_s[...] = -math.exp(-0.5) * _sigmoid(w0_ref[...] + lo[:, 0:RWKV_W])
    a = _sigmoid(a0_ref[...] + lo[:, RWKV_W:2 * RWKV_W])
    g_s[...] = lo[:, 2 * RWKV_W:3 * RWKV_W]

    r = shifted(0, RWKV_W)
    k = shifted(RWKV_W, 2 * RWKV_W)
    v = shifted(2 * RWKV_W, 3 * RWKV_W)
    carry_s[...] = z_ref[tb - 8:tb, :]
    r_s[...] = r
    v_s[...] = v
    kk = k * kk_ref[...]
    ss = headsum(kk * kk)
    alpha = kk * lax.rsqrt(jnp.maximum(ss, 1e-24))
    al_s[...] = alpha
    be_s[...] = alpha * a
    k2 = k * (1.0 + (a - 1.0) * ka_ref[...])
    k_s[...] = k2
    bo_s[...] = headsum(r * k2 * rk_ref[...]) * v

    c_len = CHUNK
    tri = jnp.where(lax.broadcasted_iota(jnp.int32, (c_len, c_len), 0)
                    >= lax.broadcasted_iota(jnp.int32, (c_len, c_len), 1), 1.0, 0.0).astype(BF16)
    lane = lax.broadcasted_iota(jnp.int32, (1, LANES), 1)
    head0 = lane < RWKV_N
    strict = same_head & (ri > ci)
    incl = same_head & (ri >= ci)
    b16 = (ri // 16) == (ci // 16)
    b32 = (ri // 32) == (ci // 32)
    eye = jnp.where(ri == ci, 1.0, 0.0).astype(F32)

    def stack2(x):
        return jnp.concatenate([jnp.where(head0, x, 0.0), jnp.where(head0, 0.0, x)], axis=0)

    def fold(x):
        return x[:c_len] + x[c_len:]

    def chunk_body(c, carry):
        r0 = pl.multiple_of(c * c_len, c_len)
        for p in range(n_pair):
            sl = (pl.ds(r0, c_len), slice(p * LANES, (p + 1) * LANES))
            r_c, lw, k_c, v_c, al, be = r_s[sl], lw_s[sl], k_s[sl], v_s[sl], al_s[sl], be_s[sl]
            hi = lw.astype(BF16)
            r1 = lw - hi.astype(F32)
            mid = r1.astype(BF16)
            low = (r1 - mid.astype(F32)).astype(BF16)
            cum = (jnp.dot(tri, hi, preferred_element_type=F32)
                   + jnp.dot(tri, mid, preferred_element_type=F32)
                   + jnp.dot(tri, low, preferred_element_type=F32))
            tot = cum[c_len - 1:c_len, :]
            e_neg = jnp.exp(-cum)
            e_end = jnp.exp(tot - cum)
            a_t = al * jnp.exp(cum - lw)
            r_t = r_c * jnp.exp(cum)
            a_st, r_st, v_st = stack2(a_t), stack2(r_t), stack2(v_c)
            b_n, k_n = be * e_neg, k_c * e_neg
            gram = _mm_nt(jnp.concatenate([a_st, r_st], axis=0),
                          jnp.concatenate([b_n, b_n, k_n, k_n], axis=0))
            a_ab = jnp.where(strict, gram[:LANES, :LANES], 0.0)
            a_ak = jnp.where(strict, gram[:LANES, LANES:], 0.0)
            a_rb = jnp.where(incl, gram[LANES:, :LANES], 0.0)
            a_rk = jnp.where(incl, gram[LANES:, LANES:], 0.0)
            a0 = jnp.where(b16, a_ab, 0.0)
            e1 = jnp.where(b32 & jnp.logical_not(b16), a_ab, 0.0)
            e2 = jnp.where(b32, 0.0, a_ab)
            p2 = _mm(a0, a0)
            p4 = _mm(p2, p2)
            p8 = _mm(p4, p4)
            t = eye - a0
            t = t + _mm(t, p2)
            t = t + _mm(t, p4)
            t = t + _mm(t, p8)
            t = t - _mm(_mm(t, e1), t)
            t = t - _mm(_mm(t, e2), t)
            akv = _mm(a_ak, v_st)
            w = _mm(t, jnp.concatenate([a_st, akv], axis=1))
            rbw = _mm(a_rb, w)
            rkv = _mm(a_rk, v_st)
            r_hat = fold(r_st - rbw[:, :LANES])
            y0 = fold(rkv - rbw[:, LANES:])
            ta_tv = jnp.concatenate([fold(w[:, :LANES]), fold(w[:, LANES:])], axis=1)
            bw = _mm_tn(be * e_end, ta_tv)
            kv = _mm_tn(k_c * e_end, v_c)
            m_p = jnp.where(same_head, bw[:, :LANES], 0.0)
            n_p = jnp.where(same_head, kv - bw[:, LANES:], 0.0)
            decay_col = jnp.exp(jnp.broadcast_to(tot, (LANES, LANES)).T)
            st = st_s[p]
            zz = _mm(jnp.concatenate([m_p, r_hat], axis=0), st)
            y_s[sl] = zz[LANES:] + y0
            st_s[p] = decay_col * st - zz[:LANES] + n_p
        return carry

    lax.fori_loop(0, tb // c_len, chunk_body, 0)

    y = y_s[...]
    mu = headsum(y) * (1.0 / RWKV_N)
    d = y - mu
    var = headsum(d * d) * (1.0 / RWKV_N)
    yn = d * lax.rsqrt(var + LNX_EPS) * gain_ref[...] + bias_ref[...]
    y_ref[...] = ((yn + bo_s[...]) * g_s[...]).astype(y_ref.dtype)


def _rwkv(zr, bsz, seq, mix, w_lora, w0, a0, k_k, k_a, r_k, gain, bias):
    tb = ROW_TILE
    nt = seq // tb
    row = lambda b, t: (b * nt + t, 0)
    const = lambda b, t: (0, 0)
    vec = pl.BlockSpec((1, RWKV_W), const)
    big = pltpu.VMEM((tb, RWKV_W), F32)
    return pl.pallas_call(
        _rwkv_kernel,
        grid=(bsz, nt),
        in_specs=[
            pl.BlockSpec((tb, ZR_W), row),
            pl.BlockSpec((1, ZR_W), const),
            pl.BlockSpec((LORA_PAD, 3 * RWKV_W), const),
            vec, vec, vec, vec, vec, vec, vec,
        ],
        out_specs=pl.BlockSpec((tb, RWKV_W), row),
        out_shape=jax.ShapeDtypeStruct((bsz * seq, RWKV_W), BF16),
        scratch_shapes=[
            pltpu.VMEM((8, ZR_W), F32),
            pltpu.VMEM((RWKV_W // LANES, LANES, LANES), F32),
            big, big, big, big, big, big, big, big, big,
        ],
        compiler_params=_params("parallel", "arbitrary"),
        name="rwkv7",
    )(zr, mix, w_lora, w0, a0, k_k, k_a, r_k, gain, bias)


def _rwkv2_kernel(z_ref, mix_ref, wl_ref, w0_ref, a0_ref, kk_ref, ka_ref, rk_ref, gain_ref,
                  bias_ref, y_ref, carry_s, st_s, g_s, bo_s, y_s, rt_s, dec_s,
                  ah0_s, ah1_s, rh0_s, rh1_s, vh0_s, vh1_s, bn_s, kn_s, be_s, ke_s, v_s,
                  mr_s, n_s, dc_s):
    tb = z_ref.shape[0]
    n_pair = RWKV_W // LANES
    c_len = CHUNK
    n_chunk = tb // c_len

    @pl.when(pl.program_id(1) == 0)
    def _():
        carry_s[...] = jnp.zeros_like(carry_s)
        st_s[...] = jnp.zeros_like(st_s)

    row = lax.broadcasted_iota(jnp.int32, (tb, 1), 0)

    def shifted(c0, c1):
        zc = z_ref[:, c0:c1]
        zp = jnp.where(row == 0, carry_s[7:8, c0:c1], pltpu.roll(zc, 1, axis=0))
        return zc + (zp - zc) * mix_ref[:, c0:c1]

    ri = lax.broadcasted_iota(jnp.int32, (LANES, LANES), 0)
    ci = lax.broadcasted_iota(jnp.int32, (LANES, LANES), 1)
    same_head = (ri // RWKV_N) == (ci // RWKV_N)
    ones_head = jnp.where(same_head, 1.0, 0.0).astype(BF16)

    def headsum(x):
        return jnp.concatenate(
            [_split_dot(x[:, p * LANES:(p + 1) * LANES], ones_head) for p in range(n_pair)], axis=1)

    zl = shifted(3 * RWKV_W, ZR_W)
    ll = lax.broadcasted_iota(jnp.int32, (1, LORA_PAD), 1)
    act = jnp.where(ll < R_DECAY, jnp.tanh(zl),
                    jnp.where(ll < R_DECAY + R_AAA, zl, _sigmoid(zl)))
    lo = jnp.dot(act.astype(BF16), wl_ref[...], preferred_element_type=F32)
    lw = -math.exp(-0.5) * _sigmoid(w0_ref[...] + lo[:, 0:RWKV_W])
    a = _sigmoid(a0_ref[...] + lo[:, RWKV_W:2 * RWKV_W])
    g_s[...] = lo[:, 2 * RWKV_W:3 * RWKV_W]

    r = shifted(0, RWKV_W)
    k = shifted(RWKV_W, 2 * RWKV_W)
    v = shifted(2 * RWKV_W, 3 * RWKV_W)
    carry_s[...] = z_ref[tb - 8:tb, :]
    kk = k * kk_ref[...]
    ss = headsum(kk * kk)
    alpha = kk * lax.rsqrt(jnp.maximum(ss, 1e-24))
    beta = alpha * a
    k2 = k * (1.0 + (a - 1.0) * ka_ref[...])
    bo_s[...] = headsum(r * k2 * rk_ref[...]) * v

    rin = row % c_len
    cum = lw
    for s in (1, 2, 4, 8, 16, 32):
        cum = cum + jnp.where(rin >= s, pltpu.roll(cum, s, axis=0), 0.0)
    tot = jnp.broadcast_to(cum.reshape(n_chunk, c_len, RWKV_W)[:, c_len - 1:c_len, :],
                           (n_chunk, c_len, RWKV_W)).reshape(tb, RWKV_W)
    head0 = (lax.broadcasted_iota(jnp.int32, (1, RWKV_W), 1) % LANES) < RWKV_N
    e_neg = jnp.exp(-cum)
    e_end = jnp.exp(tot - cum)
    a_t = alpha * jnp.exp(cum - lw)
    r_t = r * jnp.exp(cum)
    rt_s[...] = r_t
    dec_s[...] = jnp.exp(tot)
    ah0_s[...] = jnp.where(head0, a_t, 0.0).astype(BF16)
    ah1_s[...] = jnp.where(head0, 0.0, a_t).astype(BF16)
    rh0_s[...] = jnp.where(head0, r_t, 0.0).astype(BF16)
    rh1_s[...] = jnp.where(head0, 0.0, r_t).astype(BF16)
    vh0_s[...] = jnp.where(head0, v, 0.0).astype(BF16)
    vh1_s[...] = jnp.where(head0, 0.0, v).astype(BF16)
    v_s[...] = v.astype(BF16)
    bn_s[...] = (beta * e_neg).astype(BF16)
    kn_s[...] = (k2 * e_neg).astype(BF16)
    be_s[...] = (beta * e_end).astype(BF16)
    ke_s[...] = (k2 * e_end).astype(BF16)

    strict = same_head & (ri > ci)
    incl = same_head & (ri >= ci)
    b16 = (ri // 16) == (ci // 16)
    b32 = (ri // 32) == (ci // 32)
    eye = jnp.where(ri == ci, 1.0, 0.0).astype(F32)
    group = 2 * n_pair
    nb = range(group)

    def fold(x):
        return x[:c_len] + x[c_len:]

    def mmb(xs, ys):
        return [_mm(x, y) for x, y in zip(xs, ys)]

    def prep_body(it, carry):
        idx = []
        for u in range(group):
            c = it * (group // n_pair) + u // n_pair
            p = u % n_pair
            idx.append((c, c * n_pair + p,
                        (pl.ds(pl.multiple_of(c * c_len, c_len), c_len), slice(p * LANES, (p + 1) * LANES))))
        sls = [sl for _, _, sl in idx]
        lhs = [jnp.concatenate([ah0_s[sl], ah1_s[sl], rh0_s[sl], rh1_s[sl]], axis=0) for sl in sls]
        rhs = [jnp.concatenate([bn_s[sl], bn_s[sl], kn_s[sl], kn_s[sl]], axis=0) for sl in sls]
        gram = [lax.dot_general(x, y, (((1,), (1,)), ((), ())), preferred_element_type=F32)
                for x, y in zip(lhs, rhs)]
        a_ab = [jnp.where(strict, gm[:LANES, :LANES], 0.0) for gm in gram]
        a_kr = [jnp.concatenate([jnp.where(strict, gm[:LANES, LANES:], 0.0),
                                 jnp.where(incl, gm[LANES:, LANES:], 0.0)], axis=0).astype(BF16)
                for gm in gram]
        a_rb = [jnp.where(incl, gm[LANES:, :LANES], 0.0).astype(BF16) for gm in gram]
        a0 = [jnp.where(b16, x, 0.0).astype(BF16) for x in a_ab]
        e1 = [jnp.where(b32 & jnp.logical_not(b16), x, 0.0).astype(BF16) for x in a_ab]
        e2 = [jnp.where(b32, 0.0, x).astype(BF16) for x in a_ab]
        p2 = mmb(a0, a0)
        p4 = mmb(p2, p2)
        p8 = mmb(p4, p4)
        t = [eye - x.astype(F32) for x in a0]
        t = [x + y for x, y in zip(t, mmb(t, p2))]
        t = [x + y for x, y in zip(t, mmb(t, p4))]
        t = [x + y for x, y in zip(t, mmb(t, p8))]
        t = [x - y for x, y in zip(t, mmb(mmb(t, e1), t))]
        t = [x - y for x, y in zip(t, mmb(mmb(t, e2), t))]
        v_st = [jnp.concatenate([vh0_s[sl], vh1_s[sl]], axis=0) for sl in sls]
        av = mmb(a_kr, v_st)
        w = mmb(t, [jnp.concatenate([x[:LANES], y[:LANES].astype(BF16)], axis=1)
                    for x, y in zip(lhs, av)])
        rbw = mmb(a_rb, w)
        ta_tv = [jnp.concatenate([fold(x[:, :LANES]), fold(x[:, LANES:])], axis=1) for x in w]
        bw = [_mm_tn(be_s[sl], x) for sl, x in zip(sls, ta_tv)]
        kv = [_mm_tn(ke_s[sl], v_s[sl]) for sl in sls]
        for u in nb:
            c, j, sl = idx[u]
            r_hat = rt_s[sl] - fold(rbw[u][:, :LANES])
            m_p = jnp.where(same_head, bw[u][:, :LANES], 0.0)
            mr_s[j] = jnp.concatenate([m_p, r_hat], axis=0).astype(BF16)
            y_s[sl] = fold(av[u][LANES:] - rbw[u][:, LANES:])
            n_s[j] = jnp.where(same_head, kv[u] - bw[u][:, LANES:], 0.0)
            d = dec_s[sl]
            dc_s[j] = jnp.concatenate([d, d], axis=0).T
        return carry

    lax.fori_loop(0, n_chunk * n_pair // group, prep_body, 0)

    def scan_body(c, carry):
        rows = pl.ds(pl.multiple_of(c * c_len, c_len), c_len)
        sts = [st_s[p] for p in range(n_pair)]
        zz = [jnp.dot(mr_s[c * n_pair + p], sts[p].astype(BF16), preferred_element_type=F32)
              for p in range(n_pair)]
        for p in range(n_pair):
            j = c * n_pair + p
            sl = (rows, slice(p * LANES, (p + 1) * LANES))
            y_s[sl] = y_s[sl] + zz[p][LANES:]
            st_s[p] = dc_s[j] * sts[p] - zz[p][:LANES] + n_s[j]
        return carry

    lax.fori_loop(0, n_chunk, scan_body, 0)

    y = y_s[...]
    mu = headsum(y) * (1.0 / RWKV_N)
    d = y - mu
    var = headsum(d * d) * (1.0 / RWKV_N)
    yn = d * lax.rsqrt(var + LNX_EPS) * gain_ref[...] + bias_ref[...]
    y_ref[...] = ((yn + bo_s[...]) * g_s[...]).astype(y_ref.dtype)


def _rwkv2(zr, bsz, seq, mix, w_lora, w0, a0, k_k, k_a, r_k, gain, bias):
    tb = ROW_TILE
    nt = seq // tb
    n_prob = (tb // CHUNK) * (RWKV_W // LANES)
    row = lambda b, t: (b * nt + t, 0)
    const = lambda b, t: (0, 0)
    vec = pl.BlockSpec((1, RWKV_W), const)
    big = pltpu.VMEM((tb, RWKV_W), F32)
    half = pltpu.VMEM((tb, RWKV_W), BF16)
    return pl.pallas_call(
        _rwkv2_kernel,
        grid=(bsz, nt),
        in_specs=[
            pl.BlockSpec((tb, ZR_W), row),
            pl.BlockSpec((1, ZR_W), const),
            pl.BlockSpec((LORA_PAD, 3 * RWKV_W), const),
            vec, vec, vec, vec, vec, vec, vec,
        ],
        out_specs=pl.BlockSpec((tb, RWKV_W), row),
        out_shape=jax.ShapeDtypeStruct((bsz * seq, RWKV_W), BF16),
        scratch_shapes=[
            pltpu.VMEM((8, ZR_W), F32),
            pltpu.VMEM((RWKV_W // LANES, LANES, LANES), F32),
            big, big, big, big, big,
            half, half, half, half, half, half, half, half, half, half, half,
            pltpu.VMEM((n_prob, LANES + CHUNK, LANES), BF16),
            pltpu.VMEM((n_prob, LANES, LANES), F32),
            pltpu.VMEM((n_prob, LANES, LANES), F32),
        ],
        compiler_params=_params("parallel", "arbitrary"),
        name="rwkv7",
    )(zr, mix, w_lora, w0, a0, k_k, k_a, r_k, gain, bias)


def _attn_kernel(qi_tab, ki_tab, q_ref, k_ref, v_ref, lq1_ref, lk1_ref, lq2_ref, lk2_ref, sg_ref,
                 o_ref, q_s, m_s, acc_s, *, lambda_init, ratio):
    tq, tk = q_ref.shape[0], k_ref.shape[0]
    n_col = tk // LANES
    pidx = pl.program_id(1)
    qi = qi_tab[pidx]
    ki = ki_tab[pidx]
    nt = (((1,), (1,)), ((), ()))

    @pl.when(ki == 0)
    def _():
        map0 = lax.broadcasted_iota(jnp.int32, (1, LANES), 1) < DIFF_D
        for h in range(DIFF_H):
            q = q_ref[:, h * LANES:(h + 1) * LANES]
            zero = jnp.zeros_like(q)
            q_s[2 * h] = jnp.where(map0, q, zero)
            q_s[2 * h + 1] = jnp.where(map0, zero, q)
        m_s[...] = jnp.full_like(m_s, -jnp.inf)
        acc_s[...] = jnp.zeros_like(acc_s)

    def step(masked):
        if masked:
            rowp = qi * tq + lax.broadcasted_iota(jnp.int32, (tq, LANES), 0)
            colp = ki * tk + lax.broadcasted_iota(jnp.int32, (tq, LANES), 1)
            keep = [colp + j * LANES <= rowp for j in range(n_col)]
        ones = jnp.ones((tk, LANES), BF16)
        for h in range(DIFF_H):
            k = k_ref[:, h * LANES:(h + 1) * LANES]
            v_ext = jnp.concatenate([v_ref[:, h * LANES:(h + 1) * LANES], ones], axis=1)
            for c in range(2):
                i = 2 * h + c
                s = lax.dot_general(q_s[i], k, nt, preferred_element_type=F32)
                cols = [s[:, j * LANES:(j + 1) * LANES] for j in range(n_col)]
                if masked:
                    cols = [jnp.where(keep[j], cols[j], -jnp.inf) for j in range(n_col)]
                mx = cols[0]
                for j in range(1, n_col):
                    mx = jnp.maximum(mx, cols[j])
                m_old = m_s[i]
                m_new = jnp.maximum(m_old, jnp.max(mx, axis=-1, keepdims=True))
                corr = jnp.exp2(m_old - m_new)
                p = jnp.concatenate([jnp.exp2(cj - m_new).astype(BF16) for cj in cols], axis=1)
                pv = jnp.dot(p, v_ext, preferred_element_type=F32)
                acc_s[i] = jnp.concatenate([corr, corr], axis=1) * acc_s[i] + pv
                m_s[i] = m_new

    on_diag = ki >= qi * ratio

    @pl.when(on_diag)
    def _():
        step(True)

    @pl.when(jnp.logical_not(on_diag))
    def _():
        step(False)

    @pl.when(ki == (qi + 1) * ratio - 1)
    def _():
        lam = (jnp.exp(jnp.sum(lq1_ref[...] * lk1_ref[...], axis=-1, keepdims=True))
               - jnp.exp(jnp.sum(lq2_ref[...] * lk2_ref[...], axis=-1, keepdims=True))
               + lambda_init)
        for h in range(DIFF_H):
            a1 = acc_s[2 * h]
            a2 = acc_s[2 * h + 1]
            o = a1[:, :LANES] / a1[:, LANES:] - lam * (a2[:, :LANES] / a2[:, LANES:])
            o = _rms(o, sg_ref[...], SUBLN_EPS) * (1.0 - lambda_init)
            o_ref[:, h * LANES:(h + 1) * LANES] = o.astype(o_ref.dtype)


def _diff_attn(q, k, v, bsz, seq, lq1, lk1, lq2, lk2, sg, lambda_init):
    tq, tk = ATT_TQ, ATT_TK
    ratio = tq // tk
    nq, nk = seq // tq, seq // tk
    pairs = [(a, b) for a in range(nq) for b in range((a + 1) * ratio)]
    qi_tab = jnp.asarray(np.array([a for a, _ in pairs], np.int32))
    ki_tab = jnp.asarray(np.array([b for _, b in pairs], np.int32))
    qmap = lambda b, p, qt, kt: (b * nq + qt[p], 0)
    kmap = lambda b, p, qt, kt: (b * nk + kt[p], 0)
    const = lambda b, p, qt, kt: (0, 0)
    lam_spec = pl.BlockSpec((1, DIFF_D), const)
    grid_spec = pltpu.PrefetchScalarGridSpec(
        num_scalar_prefetch=2,
        grid=(bsz, len(pairs)),
        in_specs=[
            pl.BlockSpec((tq, DIFF_W), qmap),
            pl.BlockSpec((tk, DIFF_W), kmap),
            pl.BlockSpec((tk, DIFF_W), kmap),
            lam_spec, lam_spec, lam_spec, lam_spec,
            pl.BlockSpec((1, 2 * DIFF_D), const),
        ],
        out_specs=pl.BlockSpec((tq, DIFF_W), qmap),
        scratch_shapes=[
            pltpu.VMEM((2 * DIFF_H, tq, LANES), BF16),
            pltpu.VMEM((2 * DIFF_H, tq, LANES), F32),
            pltpu.VMEM((2 * DIFF_H, tq, 2 * LANES), F32),
        ],
    )
    return pl.pallas_call(
        functools.partial(_attn_kernel, lambda_init=lambda_init, ratio=ratio),
        grid_spec=grid_spec,
        out_shape=jax.ShapeDtypeStruct((bsz * seq, DIFF_W), BF16),
        compiler_params=_params("parallel", "arbitrary"),
        name="diff_attn",
    )(qi_tab, ki_tab, q, k, v, lq1, lk1, lq2, lk2, sg)


def _memkv_kernel(m_ref, g_ref, w_ref, o_ref):
    h = _rms(m_ref[...], g_ref[...], NORM_EPS).astype(BF16)
    for j in range(o_ref.shape[1] // 256):
        sl = slice(j * 256, (j + 1) * 256)
        o_ref[:, sl] = jnp.dot(h, w_ref[:, sl], preferred_element_type=F32).astype(o_ref.dtype)


def _memkv(mem2d, g, wkv):
    m = mem2d.shape[0]
    tm = MEM_LEN
    return pl.pallas_call(
        _memkv_kernel,
        grid=(m // tm,),
        in_specs=[
            pl.BlockSpec((tm, D_MODEL), lambda i: (i, 0)),
            pl.BlockSpec((1, D_MODEL), lambda i: (0, 0)),
            pl.BlockSpec((D_MODEL, 2 * D_MODEL), lambda i: (0, 0)),
        ],
        out_specs=pl.BlockSpec((tm, 2 * D_MODEL), lambda i: (i, 0)),
        out_shape=jax.ShapeDtypeStruct((m, 2 * D_MODEL), BF16),
        compiler_params=_params("parallel"),
        name="memkv",
    )(mem2d, g, wkv)


def _cross_kernel(x_ref, ya_ref, yb_ref, kv_ref, wout_ref, g_ref, wq_ref, wo_ref, o_ref, att_s):
    x1 = x_ref[...] + (jnp.dot(ya_ref[...], wout_ref[0:RWKV_W, :], preferred_element_type=F32)
                       + jnp.dot(yb_ref[...], wout_ref[RWKV_W:, :], preferred_element_type=F32))
    hc = _rms(x1, g_ref[...], NORM_EPS).astype(BF16)
    for h in range(CROSS_H):
        sl = slice(h * CROSS_D, (h + 1) * CROSS_D)
        q = jnp.dot(hc, wq_ref[:, sl], preferred_element_type=F32) * (CROSS_D ** -0.5)
        kh = kv_ref[:, sl]
        vh = kv_ref[:, D_MODEL + h * CROSS_D:D_MODEL + (h + 1) * CROSS_D]
        s = lax.dot_general(q.astype(BF16), kh, (((1,), (1,)), ((), ())), preferred_element_type=F32)
        s = s - jnp.max(s, axis=-1, keepdims=True)
        p = jnp.exp(s)
        p = p / jnp.sum(p, axis=-1, keepdims=True)
        att_s[:, sl] = jnp.dot(p.astype(BF16), vh, preferred_element_type=F32).astype(BF16)
    o_ref[...] = x1 + jnp.dot(att_s[...], wo_ref[...], preferred_element_type=F32)


def _cross(x2d, ya, yb, kv, bsz, seq, w_out, g, wq, wo):
    tm = ROW_TILE
    nt = seq // tm
    row = lambda b, t: (b * nt + t, 0)
    const = lambda b, t: (0, 0)
    return pl.pallas_call(
        _cross_kernel,
        grid=(bsz, nt),
        in_specs=[
            pl.BlockSpec((tm, D_MODEL), row),
            pl.BlockSpec((tm, RWKV_W), row),
            pl.BlockSpec((tm, DIFF_W), row),
            pl.BlockSpec((MEM_LEN, 2 * D_MODEL), lambda b, t: (b, 0)),
            pl.BlockSpec((D_MODEL, D_MODEL), const),
            pl.BlockSpec((1, D_MODEL), const),
            pl.BlockSpec((D_MODEL, D_MODEL), const),
            pl.BlockSpec((D_MODEL, D_MODEL), const),
        ],
        out_specs=pl.BlockSpec((tm, D_MODEL), row),
        out_shape=jax.ShapeDtypeStruct((bsz * seq, D_MODEL), F32),
        scratch_shapes=[pltpu.VMEM((tm, D_MODEL), BF16)],
        compiler_params=_params("parallel", "parallel"),
        name="outproj_cross",
    )(x2d, ya, yb, kv, w_out, g, wq, wo)


def _ffn_kernel(x_ref, g_ref, wg_ref, wv_ref, cw_ref, cb_ref, wd_ref, gf_ref, o_ref, carry_s, *, final):
    tm = x_ref.shape[0]
    n_ff = wg_ref.shape[0]

    @pl.when(pl.program_id(1) == 0)
    def _():
        carry_s[...] = jnp.zeros_like(carry_s)

    x = x_ref[...]
    h = _rms(x, g_ref[...], NORM_EPS).astype(BF16)
    row = lax.broadcasted_iota(jnp.int32, (tm, 1), 0)
    acc = None
    for j in range(n_ff):
        gate = jnp.dot(h, wg_ref[j], preferred_element_type=F32)
        val = jnp.dot(h, wv_ref[j], preferred_element_type=F32)
        prev = carry_s[j]
        g1 = jnp.where(row == 0, prev[7:8, :], pltpu.roll(gate, 1, axis=0))
        g2 = jnp.where(row == 0, prev[6:7, :],
                       jnp.where(row == 1, prev[7:8, :], pltpu.roll(gate, 2, axis=0)))
        carry_s[j] = gate[tm - 8:tm, :]
        cw = cw_ref[j]
        c = cw[0:1, :] * g2 + cw[1:2, :] * g1 + cw[2:3, :] * gate + cb_ref[j]
        act = c * _sigmoid(c) * val
        down = jnp.dot(act.astype(BF16), wd_ref[j], preferred_element_type=F32)
        acc = down if acc is None else acc + down
    out = x + acc
    o_ref[...] = _rms(out, gf_ref[...], NORM_EPS) if final else out


def _ffn(x2d, bsz, seq, g, wg, wv, cw, cb, wd, gf, final):
    tm = ROW_TILE
    nt = seq // tm
    n_ff, _, tf = wg.shape
    row = lambda b, t: (b * nt + t, 0)
    c2 = lambda b, t: (0, 0)
    c3 = lambda b, t: (0, 0, 0)
    once = pl.Buffered(1)
    return pl.pallas_call(
        functools.partial(_ffn_kernel, final=final),
        grid=(bsz, nt),
        in_specs=[
            pl.BlockSpec((tm, D_MODEL), row),
            pl.BlockSpec((1, D_MODEL), c2),
            pl.BlockSpec((n_ff, D_MODEL, tf), c3, pipeline_mode=once),
            pl.BlockSpec((n_ff, D_MODEL, tf), c3, pipeline_mode=once),
            pl.BlockSpec((n_ff, 3, tf), c3),
            pl.BlockSpec((n_ff, 1, tf), c3),
            pl.BlockSpec((n_ff, tf, D_MODEL), c3, pipeline_mode=once),
            pl.BlockSpec((1, D_MODEL), c2),
        ],
        out_specs=pl.BlockSpec((tm, D_MODEL), row),
        out_shape=jax.ShapeDtypeStruct((bsz * seq, D_MODEL), F32),
        scratch_shapes=[pltpu.VMEM((n_ff, 8, tf), F32)],
        compiler_params=_params("parallel", "arbitrary"),
        name="conv_ffn",
    )(x2d, g, wg, wv, cw, cb, wd, gf)


def _layer(x2d, pos2d, mem2d, bsz, seq, lambda_init, p):
    w_in = p["w_in"]
    pad = jnp.zeros((D_MODEL, LORA_PAD - (N_SHIFT - 3 * RWKV_W)), F32)
    w_r = jnp.concatenate([w_in[:, :N_SHIFT], pad], axis=1).astype(BF16)
    w_d = w_in[:, N_SHIFT:].astype(BF16)
    mix = jnp.concatenate([p["shift_mix"], jnp.zeros((LORA_PAD - (N_SHIFT - 3 * RWKV_W),), F32)])[None, :]
    w_lora = jnp.zeros((LORA_PAD, 3 * RWKV_W), F32)
    w_lora = w_lora.at[0:R_DECAY, 0:RWKV_W].set(p["w_lora_up"])
    w_lora = w_lora.at[R_DECAY:R_DECAY + R_AAA, RWKV_W:2 * RWKV_W].set(p["a_lora_up"])
    w_lora = w_lora.at[R_DECAY + R_AAA:R_DECAY + R_AAA + R_GATE, 2 * RWKV_W:].set(p["g_lora_up"])
    w_lora = w_lora.astype(BF16)
    v512 = lambda a: a.reshape(1, RWKV_W)

    zr, q, k, v = _inproj(x2d, pos2d, p["norm_mix"][None, :], w_r, w_d)
    ya = _rwkv2(zr, bsz, seq, mix, w_lora, v512(p["w0"]), v512(p["a0"]), v512(p["k_k"]),
               v512(p["k_a"]), v512(p["r_k"]), v512(p["lnx_gain"]), v512(p["lnx_bias"]))
    yb = _diff_attn(q, k, v, bsz, seq, p["lam_q1"][None, :], p["lam_k1"][None, :],
                    p["lam_q2"][None, :], p["lam_k2"][None, :], p["subln_gain"][None, :], lambda_init)
    kv = _memkv(mem2d, p["norm_mem"][None, :], p["wkv_c"].astype(BF16))
    x2 = _cross(x2d, ya, yb, kv, bsz, seq, p["w_out"].astype(BF16), p["norm_cross"][None, :],
                p["wq_c"].astype(BF16), p["wo_c"].astype(BF16))
    n_ff = D_FF // FF_TILE
    w_up = p["w_up"]
    wg = w_up[:, :D_FF].reshape(D_MODEL, n_ff, FF_TILE).transpose(1, 0, 2).astype(BF16)
    wv = w_up[:, D_FF:].reshape(D_MODEL, n_ff, FF_TILE).transpose(1, 0, 2).astype(BF16)
    cw = p["conv_w"].reshape(3, n_ff, FF_TILE).transpose(1, 0, 2)
    cb = p["conv_b"].reshape(n_ff, 1, FF_TILE)
    wd = p["w_down"].reshape(n_ff, FF_TILE, D_MODEL).astype(BF16)
    return x2, (p["norm_ffn"][None, :], wg, wv, cw, cb, wd)


def kernel(x, mem, positions, norm_mix, w_in, shift_mix, w0, w_lora_up, a0, a_lora_up, g_lora_up, k_k, k_a, r_k, lnx_gain, lnx_bias, lam_q1, lam_k1, lam_q2, lam_k2, subln_gain, w_out, norm_cross, norm_mem, wq_c, wkv_c, wo_c, norm_ffn, w_up, conv_w, conv_b, w_down, norm_final):
    bsz, seq, _ = x.shape
    depth = norm_mix.shape[0]
    x2d = x.reshape(bsz * seq, D_MODEL)
    pos2d = positions.reshape(bsz * seq, 1)
    mem2d = mem.reshape(bsz * mem.shape[1], D_MODEL)
    stacked = dict(norm_mix=norm_mix, w_in=w_in, shift_mix=shift_mix, w0=w0, w_lora_up=w_lora_up,
                   a0=a0, a_lora_up=a_lora_up, g_lora_up=g_lora_up, k_k=k_k, k_a=k_a,
                   r_k=r_k.reshape(depth, RWKV_W), lnx_gain=lnx_gain, lnx_bias=lnx_bias,
                   lam_q1=lam_q1, lam_k1=lam_k1, lam_q2=lam_q2, lam_k2=lam_k2,
                   subln_gain=subln_gain, w_out=w_out, norm_cross=norm_cross, norm_mem=norm_mem,
                   wq_c=wq_c, wkv_c=wkv_c, wo_c=wo_c, norm_ffn=norm_ffn, w_up=w_up,
                   conv_w=conv_w, conv_b=conv_b, w_down=w_down)
    for l in range(depth):
        p = {name: a[l] for name, a in stacked.items()}
        lambda_init = 0.8 - 0.6 * math.exp(-0.3 * l)
        x2, (gn, wg, wv, cw, cb, wd) = _layer(x2d, pos2d, mem2d, bsz, seq, lambda_init, p)
        x2d = _ffn(x2, bsz, seq, gn, wg, wv, cw, cb, wd, norm_final[None, :], l == depth - 1)
    return x2d.reshape(bsz, seq, D_MODEL)
```

```python
import functools
import math

import numpy as np
import jax
import jax.numpy as jnp
from jax import lax
from jax.experimental import pallas as pl
from jax.experimental.pallas import tpu as pltpu

F32 = jnp.float32
BF16 = jnp.bfloat16

D_MODEL = 1024
RWKV_W = 512
RWKV_N = 64
R_DECAY = 32
R_AAA = 32
R_GATE = 96
DIFF_W = 512
DIFF_H = 4
DIFF_D = 64
N_SHIFT = 3 * RWKV_W + R_DECAY + R_AAA + R_GATE
MEM_LEN = 256
CROSS_H = 4
CROSS_D = D_MODEL // CROSS_H
D_FF = 2816
ROPE_THETA = 10000.0
NORM_EPS = 1e-6
LNX_EPS = 64e-5
SUBLN_EPS = 1e-5

LANES = 128
LORA_PAD = 256
ZR_W = 3 * RWKV_W + LORA_PAD
VMEM_LIMIT = 56 * 1024 * 1024

ROW_TILE = 512
PROJ_TILE = 1024
CHUNK = 64
ATT_TQ = 2048
ATT_TK = 512
FF_TILE = 256


def _mm(a, b):
    return jnp.dot(a.astype(BF16), b.astype(BF16), preferred_element_type=F32)


def _mm_nt(a, b):
    return lax.dot_general(a.astype(BF16), b.astype(BF16), (((1,), (1,)), ((), ())),
                           preferred_element_type=F32)


def _mm_tn(a, b):
    return lax.dot_general(a.astype(BF16), b.astype(BF16), (((0,), (0,)), ((), ())),
                           preferred_element_type=F32)


def _sigmoid(x):
    return 1.0 / (1.0 + jnp.exp(-x))


def _rms(x, g, eps):
    return x * lax.rsqrt(jnp.mean(x * x, axis=-1, keepdims=True) + eps) * g


def _split_dot(x, ones_b):
    hi = x.astype(BF16)
    lo = (x - hi.astype(F32)).astype(BF16)
    return (jnp.dot(hi, ones_b, preferred_element_type=F32)
            + jnp.dot(lo, ones_b, preferred_element_type=F32))


def _params(*sem):
    return pltpu.CompilerParams(dimension_semantics=sem, vmem_limit_bytes=VMEM_LIMIT)


def _inproj_kernel(x_ref, pos_ref, g_ref, wr_ref, wd_ref, zr_ref, q_ref, k_ref, v_ref):
    h = _rms(x_ref[...], g_ref[...], NORM_EPS).astype(BF16)
    for j in range(ZR_W // 256):
        sl = slice(j * 256, (j + 1) * 256)
        zr_ref[:, sl] = jnp.dot(h, wr_ref[:, sl], preferred_element_type=F32)

    lane = lax.broadcasted_iota(jnp.int32, (1, LANES), 1)
    first_half = (lane % DIFF_D) < (DIFF_D // 2)
    freq = (lane % (DIFF_D // 2)).astype(F32)
    inv = jnp.exp(freq * (-2.0 / DIFF_D * math.log(ROPE_THETA)))
    ang = pos_ref[...].astype(F32) * inv
    cos = jnp.cos(ang)
    sin = jnp.where(first_half, -jnp.sin(ang), jnp.sin(ang))

    def rope(xb):
        rot = jnp.where(first_half, pltpu.roll(xb, LANES - DIFF_D // 2, axis=1),
                        pltpu.roll(xb, DIFF_D // 2, axis=1))
        return xb * cos + rot * sin

    scale = DIFF_D ** -0.5 * math.log2(math.e)
    for j in range(DIFF_W // 256):
        zq = jnp.dot(h, wd_ref[:, j * 256:(j + 1) * 256], preferred_element_type=F32)
        zk = jnp.dot(h, wd_ref[:, DIFF_W + j * 256:DIFF_W + (j + 1) * 256],
                     preferred_element_type=F32)
        zv = jnp.dot(h, wd_ref[:, 2 * DIFF_W + j * 256:2 * DIFF_W + (j + 1) * 256],
                     preferred_element_type=F32)
        for u in range(2):
            c0 = j * 256 + u * LANES
            q_ref[:, c0:c0 + LANES] = (rope(zq[:, u * LANES:(u + 1) * LANES]) * scale).astype(BF16)
            k_ref[:, c0:c0 + LANES] = rope(zk[:, u * LANES:(u + 1) * LANES]).astype(BF16)
        v_ref[:, j * 256:(j + 1) * 256] = zv.astype(BF16)


def _inproj(x2d, pos2d, g, w_r, w_d):
    m = x2d.shape[0]
    tm = PROJ_TILE
    row = lambda i: (i, 0)
    const = lambda i: (0, 0)
    return pl.pallas_call(
        _inproj_kernel,
        grid=(m // tm,),
        in_specs=[
            pl.BlockSpec((tm, D_MODEL), row),
            pl.BlockSpec((tm, 1), row),
            pl.BlockSpec((1, D_MODEL), const),
            pl.BlockSpec((D_MODEL, ZR_W), const, pipeline_mode=pl.Buffered(1)),
            pl.BlockSpec((D_MODEL, 3 * DIFF_W), const, pipeline_mode=pl.Buffered(1)),
        ],
        out_specs=[
            pl.BlockSpec((tm, ZR_W), row),
            pl.BlockSpec((tm, DIFF_W), row),
            pl.BlockSpec((tm, DIFF_W), row),
            pl.BlockSpec((tm, DIFF_W), row),
        ],
        out_shape=[
            jax.ShapeDtypeStruct((m, ZR_W), F32),
            jax.ShapeDtypeStruct((m, DIFF_W), BF16),
            jax.ShapeDtypeStruct((m, DIFF_W), BF16),
            jax.ShapeDtypeStruct((m, DIFF_W), BF16),
        ],
        compiler_params=_params("parallel"),
        name="inproj",
    )(x2d, pos2d, g, w_r, w_d)


def _rwkv_kernel(z_ref, mix_ref, wl_ref, w0_ref, a0_ref, kk_ref, ka_ref, rk_ref, gain_ref,
                 bias_ref, y_ref, carry_s, st_s, r_s, lw_s, k_s, v_s, al_s, be_s, g_s, bo_s, y_s):
    tb = z_ref.shape[0]
    n_pair = RWKV_W // LANES

    @pl.when(pl.program_id(1) == 0)
    def _():
        carry_s[...] = jnp.zeros_like(carry_s)
        st_s[...] = jnp.zeros_like(st_s)

    row = lax.broadcasted_iota(jnp.int32, (tb, 1), 0)

    def shifted(c0, c1):
        zc = z_ref[:, c0:c1]
        zp = jnp.where(row == 0, carry_s[7:8, c0:c1], pltpu.roll(zc, 1, axis=0))
        return zc + (zp - zc) * mix_ref[:, c0:c1]

    ri = lax.broadcasted_iota(jnp.int32, (LANES, LANES), 0)
    ci = lax.broadcasted_iota(jnp.int32, (LANES, LANES), 1)
    same_head = (ri // RWKV_N) == (ci // RWKV_N)
    ones_head = jnp.where(same_head, 1.0, 0.0).astype(BF16)

    def headsum(x):
        return jnp.concatenate(
            [_split_dot(x[:, p * LANES:(p + 1) * LANES], ones_head) for p in range(n_pair)], axis=1)

    zl = shifted(3 * RWKV_W, ZR_W)
    ll = lax.broadcasted_iota(jnp.int32, (1, LORA_PAD), 1)
    act = jnp.where(ll < R_DECAY, jnp.tanh(zl),
                    jnp.where(ll < R_DECAY + R_AAA, zl, _sigmoid(zl)))
    lo = jnp.dot(act.astype(BF16), wl_ref[...], preferred_element_type=F32)
    lw_s[...] = -math.exp(-0.5) * _sigmoid(w0_ref[...] + lo[:, 0:RWKV_W])
    a = _sigmoid(a0_ref[...] + lo[:, RWKV_W:2 * RWKV_W])
    g_s[...] = lo[:, 2 * RWKV_W:3 * RWKV_W]

    r = shifted(0, RWKV_W)
    k = shifted(RWKV_W, 2 * RWKV_W)
    v = shifted(2 * RWKV_W, 3 * RWKV_W)
    carry_s[...] = z_ref[tb - 8:tb, :]
    r_s[...] = r
    v_s[...] = v
    kk = k * kk_ref[...]
    ss = headsum(kk * kk)
    alpha = kk * lax.rsqrt(jnp.maximum(ss, 1e-24))
    al_s[...] = alpha
    be_s[...] = alpha * a
    k2 = k * (1.0 + (a - 1.0) * ka_ref[...])
    k_s[...] = k2
    bo_s[...] = headsum(r * k2 * rk_ref[...]) * v

    c_len = CHUNK
    tri = jnp.where(lax.broadcasted_iota(jnp.int32, (c_len, c_len), 0)
                    >= lax.broadcasted_iota(jnp.int32, (c_len, c_len), 1), 1.0, 0.0).astype(BF16)
    lane = lax.broadcasted_iota(jnp.int32, (1, LANES), 1)
    head0 = lane < RWKV_N
    strict = same_head & (ri > ci)
    incl = same_head & (ri >= ci)
    b16 = (ri // 16) == (ci // 16)
    b32 = (ri // 32) == (ci // 32)
    eye = jnp.where(ri == ci, 1.0, 0.0).astype(F32)

    def stack2(x):
        return jnp.concatenate([jnp.where(head0, x, 0.0), jnp.where(head0, 0.0, x)], axis=0)

    def fold(x):
        return x[:c_len] + x[c_len:]

    def chunk_body(c, carry):
        r0 = pl.multiple_of(c * c_len, c_len)
        for p in range(n_pair):
            sl = (pl.ds(r0, c_len), slice(p * LANES, (p + 1) * LANES))
            r_c, lw, k_c, v_c, al, be = r_s[sl], lw_s[sl], k_s[sl], v_s[sl], al_s[sl], be_s[sl]
            hi = lw.astype(BF16)
            r1 = lw - hi.astype(F32)
            mid = r1.astype(BF16)
            low = (r1 - mid.astype(F32)).astype(BF16)
            cum = (jnp.dot(tri, hi, preferred_element_type=F32)
                   + jnp.dot(tri, mid, preferred_element_type=F32)
                   + jnp.dot(tri, low, preferred_element_type=F32))
            tot = cum[c_len - 1:c_len, :]
            e_neg = jnp.exp(-cum)
            e_end = jnp.exp(tot - cum)
            a_t = al * jnp.exp(cum - lw)
            r_t = r_c * jnp.exp(cum)
            a_st, r_st, v_st = stack2(a_t), stack2(r_t), stack2(v_c)
            b_n, k_n = be * e_neg, k_c * e_neg
            gram = _mm_nt(jnp.concatenate([a_st, r_st], axis=0),
                          jnp.concatenate([b_n, b_n, k_n, k_n], axis=0))
            a_ab = jnp.where(strict, gram[:LANES, :LANES], 0.0)
            a_ak = jnp.where(strict, gram[:LANES, LANES:], 0.0)
            a_rb = jnp.where(incl, gram[LANES:, :LANES], 0.0)
            a_rk = jnp.where(incl, gram[LANES:, LANES:], 0.0)
            a0 = jnp.where(b16, a_ab, 0.0)
            e1 = jnp.where(b32 & jnp.logical_not(b16), a_ab, 0.0)
            e2 = jnp.where(b32, 0.0, a_ab)
            p2 = _mm(a0, a0)
            p4 = _mm(p2, p2)
            p8 = _mm(p4, p4)
            t = eye - a0
            t = t + _mm(t, p2)
            t = t + _mm(t, p4)
            t = t + _mm(t, p8)
            t = t - _mm(_mm(t, e1), t)
            t = t - _mm(_mm(t, e2), t)
            akv = _mm(a_ak, v_st)
            w = _mm(t, jnp.concatenate([a_st, akv], axis=1))
            rbw = _mm(a_rb, w)
            rkv = _mm(a_rk, v_st)
            r_hat = fold(r_st - rbw[:, :LANES])
            y0 = fold(rkv - rbw[:, LANES:])
            ta_tv = jnp.concatenate([fold(w[:, :LANES]), fold(w[:, LANES:])], axis=1)
            bw = _mm_tn(be * e_end, ta_tv)
            kv = _mm_tn(k_c * e_end, v_c)
            m_p = jnp.where(same_head, bw[:, :LANES], 0.0)
            n_p = jnp.where(same_head, kv - bw[:, LANES:], 0.0)
            decay_col = jnp.exp(jnp.broadcast_to(tot, (LANES, LANES)).T)
            st = st_s[p]
            zz = _mm(jnp.concatenate([m_p, r_hat], axis=0), st)
            y_s[sl] = zz[LANES:] + y0
            st_s[p] = decay_col * st - zz[:LANES] + n_p
        return carry

    lax.fori_loop(0, tb // c_len, chunk_body, 0)

    y = y_s[...]
    mu = headsum(y) * (1.0 / RWKV_N)
    d = y - mu
    var = headsum(d * d) * (1.0 / RWKV_N)
    yn = d * lax.rsqrt(var + LNX_EPS) * gain_ref[...] + bias_ref[...]
    y_ref[...] = ((yn + bo_s[...]) * g_s[...]).astype(y_ref.dtype)


def _rwkv(zr, bsz, seq, mix, w_lora, w0, a0, k_k, k_a, r_k, gain, bias):
    tb = ROW_TILE
    nt = seq // tb
    row = lambda b, t: (b * nt + t, 0)
    const = lambda b, t: (0, 0)
    vec = pl.BlockSpec((1, RWKV_W), const)
    big = pltpu.VMEM((tb, RWKV_W), F32)
    return pl.pallas_call(
        _rwkv_kernel,
        grid=(bsz, nt),
        in_specs=[
            pl.BlockSpec((tb, ZR_W), row),
            pl.BlockSpec((1, ZR_W), const),
            pl.BlockSpec((LORA_PAD, 3 * RWKV_W), const),
            vec, vec, vec, vec, vec, vec, vec,
        ],
        out_specs=pl.BlockSpec((tb, RWKV_W), row),
        out_shape=jax.ShapeDtypeStruct((bsz * seq, RWKV_W), BF16),
        scratch_shapes=[
            pltpu.VMEM((8, ZR_W), F32),
            pltpu.VMEM((RWKV_W // LANES, LANES, LANES), F32),
            big, big, big, big, big, big, big, big, big,
        ],
        compiler_params=_params("parallel", "arbitrary"),
        name="rwkv7",
    )(zr, mix, w_lora, w0, a0, k_k, k_a, r_k, gain, bias)


def _rwkv2_kernel(z_ref, mix_ref, wl_ref, w0_ref, a0_ref, kk_ref, ka_ref, rk_ref, gain_ref,
                  bias_ref, y_ref, carry_s, st_s, g_s, bo_s, y_s, rt_s, dec_s,
                  ah0_s, ah1_s, rh0_s, rh1_s, vh0_s, vh1_s, bn_s, kn_s, be_s, ke_s, v_s,
                  mr_s, n_s, dc_s):
    tb = z_ref.shape[0]
    n_pair = RWKV_W // LANES
    c_len = CHUNK
    n_chunk = tb // c_len

    @pl.when(pl.program_id(1) == 0)
    def _():
        carry_s[...] = jnp.zeros_like(carry_s)
        st_s[...] = jnp.zeros_like(st_s)

    row = lax.broadcasted_iota(jnp.int32, (tb, 1), 0)

    def shifted(c0, c1):
        zc = z_ref[:, c0:c1]
        zp = jnp.where(row == 0, carry_s[7:8, c0:c1], pltpu.roll(zc, 1, axis=0))
        return zc + (zp - zc) * mix_ref[:, c0:c1]

    ri = lax.broadcasted_iota(jnp.int32, (LANES, LANES), 0)
    ci = lax.broadcasted_iota(jnp.int32, (LANES, LANES), 1)
    same_head = (ri // RWKV_N) == (ci // RWKV_N)
    ones_head = jnp.where(same_head, 1.0, 0.0).astype(BF16)

    def headsum(x):
        return jnp.concatenate(
            [_split_dot(x[:, p * LANES:(p + 1) * LANES], ones_head) for p in range(n_pair)], axis=1)

    zl = shifted(3 * RWKV_W, ZR_W)
    ll = lax.broadcasted_iota(jnp.int32, (1, LORA_PAD), 1)
    act = jnp.where(ll < R_DECAY, jnp.tanh(zl),
                    jnp.where(ll < R_DECAY + R_AAA, zl, _sigmoid(zl)))
    lo = jnp.dot(act.astype(BF16), wl_ref[...], preferred_element_type=F32)
    lw = -math.exp(-0.5) * _sigmoid(w0_ref[...] + lo[:, 0:RWKV_W])
    a = _sigmoid(a0_ref[...] + lo[:, RWKV_W:2 * RWKV_W])
    g_s[...] = lo[:, 2 * RWKV_W:3 * RWKV_W]

    r = shifted(0, RWKV_W)
    k = shifted(RWKV_W, 2 * RWKV_W)
    v = shifted(2 * RWKV_W, 3 * RWKV_W)
    carry_s[...] = z_ref[tb - 8:tb, :]
    kk = k * kk_ref[...]
    ss = headsum(kk * kk)
    alpha = kk * lax.rsqrt(jnp.maximum(ss, 1e-24))
    beta = alpha * a
    k2 = k * (1.0 + (a - 1.0) * ka_ref[...])
    bo_s[...] = headsum(r * k2 * rk_ref[...]) * v

    rin = row % c_len
    cum = lw
    for s in (1, 2, 4, 8, 16, 32):
        cum = cum + jnp.where(rin >= s, pltpu.roll(cum, s, axis=0), 0.0)
    tot = jnp.broadcast_to(cum.reshape(n_chunk, c_len, RWKV_W)[:, c_len - 1:c_len, :],
                           (n_chunk, c_len, RWKV_W)).reshape(tb, RWKV_W)
    head0 = (lax.broadcasted_iota(jnp.int32, (1, RWKV_W), 1) % LANES) < RWKV_N
    e_neg = jnp.exp(-cum)
    e_end = jnp.exp(tot - cum)
    a_t = alpha * jnp.exp(cum - lw)
    r_t = r * jnp.exp(cum)
    rt_s[...] = r_t
    dec_s[...] = jnp.exp(tot)
    ah0_s[...] = jnp.where(head0, a_t, 0.0).astype(BF16)
    ah1_s[...] = jnp.where(head0, 0.0, a_t).astype(BF16)
    rh0_s[...] = jnp.where(head0, r_t, 0.0).astype(BF16)
    rh1_s[...] = jnp.where(head0, 0.0, r_t).astype(BF16)
    vh0_s[...] = jnp.where(head0, v, 0.0).astype(BF16)
    vh1_s[...] = jnp.where(head0, 0.0, v).astype(BF16)
    v_s[...] = v.astype(BF16)
    bn_s[...] = (beta * e_neg).astype(BF16)
    kn_s[...] = (k2 * e_neg).astype(BF16)
    be_s[...] = (beta * e_end).astype(BF16)
    ke_s[...] = (k2 * e_end).astype(BF16)

    strict = same_head & (ri > ci)
    incl = same_head & (ri >= ci)
    b16 = (ri // 16) == (ci // 16)
    b32 = (ri // 32) == (ci // 32)
    eye = jnp.where(ri == ci, 1.0, 0.0).astype(F32)
    group = 2 * n_pair
    nb = range(group)

    def fold(x):
        return x[:c_len] + x[c_len:]

    def mmb(xs, ys):
        return [_mm(x, y) for x, y in zip(xs, ys)]

    def prep_body(it, carry):
        idx = []
        for u in range(group):
            c = it * (group // n_pair) + u // n_pair
            p = u % n_pair
            idx.append((c, c * n_pair + p,
                        (pl.ds(pl.multiple_of(c * c_len, c_len), c_len), slice(p * LANES, (p + 1) * LANES))))
        sls = [sl for _, _, sl in idx]
        lhs = [jnp.concatenate([ah0_s[sl], ah1_s[sl], rh0_s[sl], rh1_s[sl]], axis=0) for sl in sls]
        rhs = [jnp.concatenate([bn_s[sl], bn_s[sl], kn_s[sl], kn_s[sl]], axis=0) for sl in sls]
        gram = [lax.dot_general(x, y, (((1,), (1,)), ((), ())), preferred_element_type=F32)
                for x, y in zip(lhs, rhs)]
        a_ab = [jnp.where(strict, gm[:LANES, :LANES], 0.0) for gm in gram]
        a_kr = [jnp.concatenate([jnp.where(strict, gm[:LANES, LANES:], 0.0),
                                 jnp.where(incl, gm[LANES:, LANES:], 0.0)], axis=0).astype(BF16)
                for gm in gram]
        a_rb = [jnp.where(incl, gm[LANES:, :LANES], 0.0).astype(BF16) for gm in gram]
        a0 = [jnp.where(b16, x, 0.0).astype(BF16) for x in a_ab]
        e1 = [jnp.where(b32 & jnp.logical_not(b16), x, 0.0).astype(BF16) for x in a_ab]
        e2 = [jnp.where(b32, 0.0, x).astype(BF16) for x in a_ab]
        p2 = mmb(a0, a0)
        p4 = mmb(p2, p2)
        p8 = mmb(p4, p4)
        t = [eye - x.astype(F32) for x in a0]
        t = [x + y for x, y in zip(t, mmb(t, p2))]
        t = [x + y for x, y in zip(t, mmb(t, p4))]
        t = [x + y for x, y in zip(t, mmb(t, p8))]
        t = [x - y for x, y in zip(t, mmb(mmb(t, e1), t))]
        t = [x - y for x, y in zip(t, mmb(mmb(t, e2), t))]
        v_st = [jnp.concatenate([vh0_s[sl], vh1_s[sl]], axis=0) for sl in sls]
        av = mmb(a_kr, v_st)
        w = mmb(t, [jnp.concatenate([x[:LANES], y[:LANES].astype(BF16)], axis=1)
                    for x, y in zip(lhs, av)])
        rbw = mmb(a_rb, w)
        ta_tv = [jnp.concatenate([fold(x[:, :LANES]), fold(x[:, LANES:])], axis=1) for x in w]
        bw = [_mm_tn(be_s[sl], x) for sl, x in zip(sls, ta_tv)]
        kv = [_mm_tn(ke_s[sl], v_s[sl]) for sl in sls]
        for u in nb:
            c, j, sl = idx[u]
            r_hat = rt_s[sl] - fold(rbw[u][:, :LANES])
            m_p = jnp.where(same_head, bw[u][:, :LANES], 0.0)
            mr_s[j] = jnp.concatenate([m_p, r_hat], axis=0).astype(BF16)
            y_s[sl] = fold(av[u][LANES:] - rbw[u][:, LANES:])
            n_s[j] = jnp.where(same_head, kv[u] - bw[u][:, LANES:], 0.0)
            d = dec_s[sl]
            dc_s[j] = jnp.concatenate([d, d], axis=0).T
        return carry

    lax.fori_loop(0, n_chunk * n_pair // group, prep_body, 0)

    def scan_body(c, carry):
        rows = pl.ds(pl.multiple_of(c * c_len, c_len), c_len)
        sts = [st_s[p] for p in range(n_pair)]
        zz = [jnp.dot(mr_s[c * n_pair + p], sts[p].astype(BF16), preferred_element_type=F32)
              for p in range(n_pair)]
        for p in range(n_pair):
            j = c * n_pair + p
            sl = (rows, slice(p * LANES, (p + 1) * LANES))
            y_s[sl] = y_s[sl] + zz[p][LANES:]
            st_s[p] = dc_s[j] * sts[p] - zz[p][:LANES] + n_s[j]
        return carry

    lax.fori_loop(0, n_chunk, scan_body, 0)

    y = y_s[...]
    mu = headsum(y) * (1.0 / RWKV_N)
    d = y - mu
    var = headsum(d * d) * (1.0 / RWKV_N)
    yn = d * lax.rsqrt(var + LNX_EPS) * gain_ref[...] + bias_ref[...]
    y_ref[...] = ((yn + bo_s[...]) * g_s[...]).astype(y_ref.dtype)


def _rwkv2(zr, bsz, seq, mix, w_lora, w0, a0, k_k, k_a, r_k, gain, bias):
    tb = ROW_TILE
    nt = seq // tb
    n_prob = (tb // CHUNK) * (RWKV_W // LANES)
    row = lambda b, t: (b * nt + t, 0)
    const = lambda b, t: (0, 0)
    vec = pl.BlockSpec((1, RWKV_W), const)
    big = pltpu.VMEM((tb, RWKV_W), F32)
    half = pltpu.VMEM((tb, RWKV_W), BF16)
    return pl.pallas_call(
        _rwkv2_kernel,
        grid=(bsz, nt),
        in_specs=[
            pl.BlockSpec((tb, ZR_W), row),
            pl.BlockSpec((1, ZR_W), const),
            pl.BlockSpec((LORA_PAD, 3 * RWKV_W), const),
            vec, vec, vec, vec, vec, vec, vec,
        ],
        out_specs=pl.BlockSpec((tb, RWKV_W), row),
        out_shape=jax.ShapeDtypeStruct((bsz * seq, RWKV_W), BF16),
        scratch_shapes=[
            pltpu.VMEM((8, ZR_W), F32),
            pltpu.VMEM((RWKV_W // LANES, LANES, LANES), F32),
            big, big, big, big, big,
            half, half, half, half, half, half, half, half, half, half, half,
            pltpu.VMEM((n_prob, LANES + CHUNK, LANES), BF16),
            pltpu.VMEM((n_prob, LANES, LANES), F32),
            pltpu.VMEM((n_prob, LANES, LANES), F32),
        ],
        compiler_params=_params("parallel", "arbitrary"),
        name="rwkv7",
    )(zr, mix, w_lora, w0, a0, k_k, k_a, r_k, gain, bias)


def _attn_kernel(qi_tab, ki_tab, q_ref, k_ref, v_ref, lq1_ref, lk1_ref, lq2_ref, lk2_ref, sg_ref,
                 o_ref, q_s, m_s, acc_s, *, lambda_init, ratio):
    tq, tk = q_ref.shape[0], k_ref.shape[0]
    n_col = tk // LANES
    pidx = pl.program_id(1)
    qi = qi_tab[pidx]
    ki = ki_tab[pidx]
    nt = (((1,), (1,)), ((), ()))

    @pl.when(ki == 0)
    def _():
        map0 = lax.broadcasted_iota(jnp.int32, (1, LANES), 1) < DIFF_D
        for h in range(DIFF_H):
            q = q_ref[:, h * LANES:(h + 1) * LANES]
            zero = jnp.zeros_like(q)
            q_s[2 * h] = jnp.where(map0, q, zero)
            q_s[2 * h + 1] = jnp.where(map0, zero, q)
        m_s[...] = jnp.full_like(m_s, -jnp.inf)
        acc_s[...] = jnp.zeros_like(acc_s)

    def step(masked):
        if masked:
            rowp = qi * tq + lax.broadcasted_iota(jnp.int32, (tq, LANES), 0)
            colp = ki * tk + lax.broadcasted_iota(jnp.int32, (tq, LANES), 1)
            keep = [colp + j * LANES <= rowp for j in range(n_col)]
        ones = jnp.ones((tk, LANES), BF16)
        for h in range(DIFF_H):
            k = k_ref[:, h * LANES:(h + 1) * LANES]
            v_ext = jnp.concatenate([v_ref[:, h * LANES:(h + 1) * LANES], ones], axis=1)
            for c in range(2):
                i = 2 * h + c
                s = lax.dot_general(q_s[i], k, nt, preferred_element_type=F32)
                cols = [s[:, j * LANES:(j + 1) * LANES] for j in range(n_col)]
                if masked:
                    cols = [jnp.where(keep[j], cols[j], -jnp.inf) for j in range(n_col)]
                mx = cols[0]
                for j in range(1, n_col):
                    mx = jnp.maximum(mx, cols[j])
                m_old = m_s[i]
                m_new = jnp.maximum(m_old, jnp.max(mx, axis=-1, keepdims=True))
                corr = jnp.exp2(m_old - m_new)
                p = jnp.concatenate([jnp.exp2(cj - m_new).astype(BF16) for cj in cols], axis=1)
                pv = jnp.dot(p, v_ext, preferred_element_type=F32)
                acc_s[i] = jnp.concatenate([corr, corr], axis=1) * acc_s[i] + pv
                m_s[i] = m_new

    on_diag = ki >= qi * ratio

    @pl.when(on_diag)
    def _():
        step(True)

    @pl.when(jnp.logical_not(on_diag))
    def _():
        step(False)

    @pl.when(ki == (qi + 1) * ratio - 1)
    def _():
        lam = (jnp.exp(jnp.sum(lq1_ref[...] * lk1_ref[...], axis=-1, keepdims=True))
               - jnp.exp(jnp.sum(lq2_ref[...] * lk2_ref[...], axis=-1, keepdims=True))
               + lambda_init)
        for h in range(DIFF_H):
            a1 = acc_s[2 * h]
            a2 = acc_s[2 * h + 1]
            o = a1[:, :LANES] / a1[:, LANES:] - lam * (a2[:, :LANES] / a2[:, LANES:])
            o = _rms(o, sg_ref[...], SUBLN_EPS) * (1.0 - lambda_init)
            o_ref[:, h * LANES:(h + 1) * LANES] = o.astype(o_ref.dtype)


def _diff_attn(q, k, v, bsz, seq, lq1, lk1, lq2, lk2, sg, lambda_init):
    tq, tk = ATT_TQ, ATT_TK
    ratio = tq // tk
    nq, nk = seq // tq, seq // tk
    pairs = [(a, b) for a in range(nq) for b in range((a + 1) * ratio)]
    qi_tab = jnp.asarray(np.array([a for a, _ in pairs], np.int32))
    ki_tab = jnp.asarray(np.array([b for _, b in pairs], np.int32))
    qmap = lambda b, p, qt, kt: (b * nq + qt[p], 0)
    kmap = lambda b, p, qt, kt: (b * nk + kt[p], 0)
    const = lambda b, p, qt, kt: (0, 0)
    lam_spec = pl.BlockSpec((1, DIFF_D), const)
    grid_spec = pltpu.PrefetchScalarGridSpec(
        num_scalar_prefetch=2,
        grid=(bsz, len(pairs)),
        in_specs=[
            pl.BlockSpec((tq, DIFF_W), qmap),
            pl.BlockSpec((tk, DIFF_W), kmap),
            pl.BlockSpec((tk, DIFF_W), kmap),
            lam_spec, lam_spec, lam_spec, lam_spec,
            pl.BlockSpec((1, 2 * DIFF_D), const),
        ],
        out_specs=pl.BlockSpec((tq, DIFF_W), qmap),
        scratch_shapes=[
            pltpu.VMEM((2 * DIFF_H, tq, LANES), BF16),
            pltpu.VMEM((2 * DIFF_H, tq, LANES), F32),
            pltpu.VMEM((2 * DIFF_H, tq, 2 * LANES), F32),
        ],
    )
    return pl.pallas_call(
        functools.partial(_attn_kernel, lambda_init=lambda_init, ratio=ratio),
        grid_spec=grid_spec,
        out_shape=jax.ShapeDtypeStruct((bsz * seq, DIFF_W), BF16),
        compiler_params=_params("parallel", "arbitrary"),
        name="diff_attn",
    )(qi_tab, ki_tab, q, k, v, lq1, lk1, lq2, lk2, sg)


def _attn2_kernel(q_ref, k_ref, v_ref, lq1_ref, lk1_ref, lq2_ref, lk2_ref, sg_ref, o_ref,
                  q_s, m_s, acc_s, *, lambda_init, tk):
    tq = q_ref.shape[0]
    n_col = tk // LANES
    n_diag = tq // tk
    qi = pl.program_id(2)
    nt = (((1,), (1,)), ((), ()))

    map0 = lax.broadcasted_iota(jnp.int32, (1, LANES), 1) < DIFF_D
    q = q_ref[...]
    zero = jnp.zeros_like(q)
    q_s[0] = jnp.where(map0, q, zero)
    q_s[1] = jnp.where(map0, zero, q)
    m_s[...] = jnp.full_like(m_s, -jnp.inf)
    acc_s[...] = jnp.zeros_like(acc_s)
    ones = jnp.ones((tk, LANES), BF16)
    tri_keep = [lax.broadcasted_iota(jnp.int32, (tk, LANES), 1) + j * LANES
                <= lax.broadcasted_iota(jnp.int32, (tk, LANES), 0) for j in range(n_col)]

    def kv_step(kv_rows, r0, r1, masked):
        k = k_ref[kv_rows, :]
        v_ext = jnp.concatenate([v_ref[kv_rows, :], ones], axis=1)
        for c in range(2):
            s = lax.dot_general(q_s[c, r0:r1, :], k, nt, preferred_element_type=F32)
            cols = [s[:, j * LANES:(j + 1) * LANES] for j in range(n_col)]
            if masked:
                cols = [jnp.where(tri_keep[j], cols[j], -jnp.inf) for j in range(n_col)]
            mx = cols[0]
            for j in range(1, n_col):
                mx = jnp.maximum(mx, cols[j])
            m_old = m_s[c, r0:r1, :]
            m_new = jnp.maximum(m_old, jnp.max(mx, axis=-1, keepdims=True))
            corr = jnp.exp2(m_old - m_new)
            p = jnp.concatenate([jnp.exp2(cj - m_new).astype(BF16) for cj in cols], axis=1)
            pv = jnp.dot(p, v_ext, preferred_element_type=F32)
            acc_s[c, r0:r1, :] = jnp.concatenate([corr, corr], axis=1) * acc_s[c, r0:r1, :] + pv
            m_s[c, r0:r1, :] = m_new

    def full_body(j, carry):
        for u in range(n_diag):
            kv_step(pl.ds(pl.multiple_of((j * n_diag + u) * tk, tk), tk), 0, tq, False)
        return carry

    lax.fori_loop(0, qi, full_body, 0)
    for d in range(n_diag):
        kv_rows = pl.ds(pl.multiple_of((qi * n_diag + d) * tk, tk), tk)
        kv_step(kv_rows, d * tk, (d + 1) * tk, True)
        if d + 1 < n_diag:
            kv_step(kv_rows, (d + 1) * tk, tq, False)

    lam = (jnp.exp(jnp.sum(lq1_ref[...] * lk1_ref[...], axis=-1, keepdims=True))
           - jnp.exp(jnp.sum(lq2_ref[...] * lk2_ref[...], axis=-1, keepdims=True))
           + lambda_init)
    a1 = acc_s[0]
    a2 = acc_s[1]
    o = a1[:, :LANES] / a1[:, LANES:] - lam * (a2[:, :LANES] / a2[:, LANES:])
    o_ref[...] = (_rms(o, sg_ref[...], SUBLN_EPS) * (1.0 - lambda_init)).astype(o_ref.dtype)


def _diff_attn2(q, k, v, bsz, seq, lq1, lk1, lq2, lk2, sg, lambda_init):
    tq = min(ATT_TQ, seq)
    tk = min(ATT_TK, tq)
    nq = seq // tq
    qmap = lambda b, h, i: (b * nq + i, h)
    kmap = lambda b, h, i: (b, h)
    const = lambda b, h, i: (0, 0)
    lam_spec = pl.BlockSpec((1, DIFF_D), const)
    return pl.pallas_call(
        functools.partial(_attn2_kernel, lambda_init=lambda_init, tk=tk),
        grid=(bsz, DIFF_H, nq),
        in_specs=[
            pl.BlockSpec((tq, LANES), qmap),
            pl.BlockSpec((seq, LANES), kmap),
            pl.BlockSpec((seq, LANES), kmap),
            lam_spec, lam_spec, lam_spec, lam_spec,
            pl.BlockSpec((1, 2 * DIFF_D), const),
        ],
        out_specs=pl.BlockSpec((tq, LANES), qmap),
        out_shape=jax.ShapeDtypeStruct((bsz * seq, DIFF_W), BF16),
        scratch_shapes=[
            pltpu.VMEM((2, tq, LANES), BF16),
            pltpu.VMEM((2, tq, LANES), F32),
            pltpu.VMEM((2, tq, 2 * LANES), F32),
        ],
        compiler_params=_params("parallel", "parallel", "arbitrary"),
        name="diff_attn",
    )(q, k, v, lq1, lk1, lq2, lk2, sg)


def _memkv_kernel(m_ref, g_ref, w_ref, o_ref):
    h = _rms(m_ref[...], g_ref[...], NORM_EPS).astype(BF16)
    for j in range(o_ref.shape[1] // 256):
        sl = slice(j * 256, (j + 1) * 256)
        o_ref[:, sl] = jnp.dot(h, w_ref[:, sl], preferred_element_type=F32).astype(o_ref.dtype)


def _memkv(mem2d, g, wkv):
    m = mem2d.shape[0]
    tm = MEM_LEN
    return pl.pallas_call(
        _memkv_kernel,
        grid=(m // tm,),
        in_specs=[
            pl.BlockSpec((tm, D_MODEL), lambda i: (i, 0)),
            pl.BlockSpec((1, D_MODEL), lambda i: (0, 0)),
            pl.BlockSpec((D_MODEL, 2 * D_MODEL), lambda i: (0, 0)),
        ],
        out_specs=pl.BlockSpec((tm, 2 * D_MODEL), lambda i: (i, 0)),
        out_shape=jax.ShapeDtypeStruct((m, 2 * D_MODEL), BF16),
        compiler_params=_params("parallel"),
        name="memkv",
    )(mem2d, g, wkv)


def _cross_kernel(x_ref, ya_ref, yb_ref, kv_ref, wout_ref, g_ref, wq_ref, wo_ref, o_ref, att_s):
    x1 = x_ref[...] + (jnp.dot(ya_ref[...], wout_ref[0:RWKV_W, :], preferred_element_type=F32)
                       + jnp.dot(yb_ref[...], wout_ref[RWKV_W:, :], preferred_element_type=F32))
    hc = _rms(x1, g_ref[...], NORM_EPS).astype(BF16)
    for h in range(CROSS_H):
        sl = slice(h * CROSS_D, (h + 1) * CROSS_D)
        q = jnp.dot(hc, wq_ref[:, sl], preferred_element_type=F32) * (CROSS_D ** -0.5)
        kh = kv_ref[:, sl]
        vh = kv_ref[:, D_MODEL + h * CROSS_D:D_MODEL + (h + 1) * CROSS_D]
        s = lax.dot_general(q.astype(BF16), kh, (((1,), (1,)), ((), ())), preferred_element_type=F32)
        s = s - jnp.max(s, axis=-1, keepdims=True)
        p = jnp.exp(s)
        p = p / jnp.sum(p, axis=-1, keepdims=True)
        att_s[:, sl] = jnp.dot(p.astype(BF16), vh, preferred_element_type=F32).astype(BF16)
    o_ref[...] = x1 + jnp.dot(att_s[...], wo_ref[...], preferred_element_type=F32)


def _cross(x2d, ya, yb, kv, bsz, seq, w_out, g, wq, wo):
    tm = PROJ_TILE
    nt = seq // tm
    row = lambda b, t: (b * nt + t, 0)
    const = lambda b, t: (0, 0)
    return pl.pallas_call(
        _cross_kernel,
        grid=(bsz, nt),
        in_specs=[
            pl.BlockSpec((tm, D_MODEL), row),
            pl.BlockSpec((tm, RWKV_W), row),
            pl.BlockSpec((tm, DIFF_W), row),
            pl.BlockSpec((MEM_LEN, 2 * D_MODEL), lambda b, t: (b, 0)),
            pl.BlockSpec((D_MODEL, D_MODEL), const, pipeline_mode=pl.Buffered(1)),
            pl.BlockSpec((1, D_MODEL), const),
            pl.BlockSpec((D_MODEL, D_MODEL), const, pipeline_mode=pl.Buffered(1)),
            pl.BlockSpec((D_MODEL, D_MODEL), const, pipeline_mode=pl.Buffered(1)),
        ],
        out_specs=pl.BlockSpec((tm, D_MODEL), row),
        out_shape=jax.ShapeDtypeStruct((bsz * seq, D_MODEL), F32),
        scratch_shapes=[pltpu.VMEM((tm, D_MODEL), BF16)],
        compiler_params=_params("parallel", "parallel"),
        name="outproj_cross",
    )(x2d, ya, yb, kv, w_out, g, wq, wo)


def _ffn_kernel(x_ref, g_ref, wg_ref, wv_ref, cw_ref, cb_ref, wd_ref, gf_ref, o_ref, carry_s, *, final):
    tm = x_ref.shape[0]
    n_ff = wg_ref.shape[0]

    @pl.when(pl.program_id(1) == 0)
    def _():
        carry_s[...] = jnp.zeros_like(carry_s)

    x = x_ref[...]
    h = _rms(x, g_ref[...], NORM_EPS).astype(BF16)
    row = lax.broadcasted_iota(jnp.int32, (tm, 1), 0)
    acc = None
    for j in range(n_ff):
        gate = jnp.dot(h, wg_ref[j], preferred_element_type=F32)
        val = jnp.dot(h, wv_ref[j], preferred_element_type=F32)
        prev = carry_s[j]
        g1 = jnp.where(row == 0, prev[7:8, :], pltpu.roll(gate, 1, axis=0))
        g2 = jnp.where(row == 0, prev[6:7, :],
                       jnp.where(row == 1, prev[7:8, :], pltpu.roll(gate, 2, axis=0)))
        carry_s[j] = gate[tm - 8:tm, :]
        cw = cw_ref[j]
        c = cw[0:1, :] * g2 + cw[1:2, :] * g1 + cw[2:3, :] * gate + cb_ref[j]
        act = c * _sigmoid(c) * val
        down = jnp.dot(act.astype(BF16), wd_ref[j], preferred_element_type=F32)
        acc = down if acc is None else acc + down
    out = x + acc
    o_ref[...] = _rms(out, gf_ref[...], NORM_EPS) if final else out


def _ffn(x2d, bsz, seq, g, wg, wv, cw, cb, wd, gf, final):
    tm = PROJ_TILE
    nt = seq // tm
    n_ff, _, tf = wg.shape
    row = lambda b, t: (b * nt + t, 0)
    c2 = lambda b, t: (0, 0)
    c3 = lambda b, t: (0, 0, 0)
    once = pl.Buffered(1)
    return pl.pallas_call(
        functools.partial(_ffn_kernel, final=final),
        grid=(bsz, nt),
        in_specs=[
            pl.BlockSpec((tm, D_MODEL), row),
            pl.BlockSpec((1, D_MODEL), c2),
            pl.BlockSpec((n_ff, D_MODEL, tf), c3, pipeline_mode=once),
            pl.BlockSpec((n_ff, D_MODEL, tf), c3, pipeline_mode=once),
            pl.BlockSpec((n_ff, 3, tf), c3),
            pl.BlockSpec((n_ff, 1, tf), c3),
            pl.BlockSpec((n_ff, tf, D_MODEL), c3, pipeline_mode=once),
            pl.BlockSpec((1, D_MODEL), c2),
        ],
        out_specs=pl.BlockSpec((tm, D_MODEL), row),
        out_shape=jax.ShapeDtypeStruct((bsz * seq, D_MODEL), F32),
        scratch_shapes=[pltpu.VMEM((n_ff, 8, tf), F32)],
        compiler_params=_params("parallel", "arbitrary"),
        name="conv_ffn",
    )(x2d, g, wg, wv, cw, cb, wd, gf)


def _layer(x2d, pos2d, mem2d, bsz, seq, lambda_init, p):
    w_in = p["w_in"]
    pad = jnp.zeros((D_MODEL, LORA_PAD - (N_SHIFT - 3 * RWKV_W)), F32)
    w_r = jnp.concatenate([w_in[:, :N_SHIFT], pad], axis=1).astype(BF16)
    w_d = w_in[:, N_SHIFT:].astype(BF16)
    mix = jnp.concatenate([p["shift_mix"], jnp.zeros((LORA_PAD - (N_SHIFT - 3 * RWKV_W),), F32)])[None, :]
    w_lora = jnp.zeros((LORA_PAD, 3 * RWKV_W), F32)
    w_lora = w_lora.at[0:R_DECAY, 0:RWKV_W].set(p["w_lora_up"])
    w_lora = w_lora.at[R_DECAY:R_DECAY + R_AAA, RWKV_W:2 * RWKV_W].set(p["a_lora_up"])
    w_lora = w_lora.at[R_DECAY + R_AAA:R_DECAY + R_AAA + R_GATE, 2 * RWKV_W:].set(p["g_lora_up"])
    w_lora = w_lora.astype(BF16)
    v512 = lambda a: a.reshape(1, RWKV_W)

    zr, q, k, v = _inproj(x2d, pos2d, p["norm_mix"][None, :], w_r, w_d)
    ya = _rwkv2(zr, bsz, seq, mix, w_lora, v512(p["w0"]), v512(p["a0"]), v512(p["k_k"]),
               v512(p["k_a"]), v512(p["r_k"]), v512(p["lnx_gain"]), v512(p["lnx_bias"]))
    yb = _diff_attn2(q, k, v, bsz, seq, p["lam_q1"][None, :], p["lam_k1"][None, :],
                    p["lam_q2"][None, :], p["lam_k2"][None, :], p["subln_gain"][None, :], lambda_init)
    kv = _memkv(mem2d, p["norm_mem"][None, :], p["wkv_c"].astype(BF16))
    x2 = _cross(x2d, ya, yb, kv, bsz, seq, p["w_out"].astype(BF16), p["norm_cross"][None, :],
                p["wq_c"].astype(BF16), p["wo_c"].astype(BF16))
    n_ff = D_FF // FF_TILE
    w_up = p["w_up"]
    wg = w_up[:, :D_FF].reshape(D_MODEL, n_ff, FF_TILE).transpose(1, 0, 2).astype(BF16)
    wv = w_up[:, D_FF:].reshape(D_MODEL, n_ff, FF_TILE).transpose(1, 0, 2).astype(BF16)
    cw = p["conv_w"].reshape(3, n_ff, FF_TILE).transpose(1, 0, 2)
    cb = p["conv_b"].reshape(n_ff, 1, FF_TILE)
    wd = p["w_down"].reshape(n_ff, FF_TILE, D_MODEL).astype(BF16)
    return x2, (p["norm_ffn"][None, :], wg, wv, cw, cb, wd)


def kernel(x, mem, positions, norm_mix, w_in, shift_mix, w0, w_lora_up, a0, a_lora_up, g_lora_up, k_k, k_a, r_k, lnx_gain, lnx_bias, lam_q1, lam_k1, lam_q2, lam_k2, subln_gain, w_out, norm_cross, norm_mem, wq_c, wkv_c, wo_c, norm_ffn, w_up, conv_w, conv_b, w_down, norm_final):
    bsz, seq, _ = x.shape
    depth = norm_mix.shape[0]
    x2d = x.reshape(bsz * seq, D_MODEL)
    pos2d = positions.reshape(bsz * seq, 1)
    mem2d = mem.reshape(bsz * mem.shape[1], D_MODEL)
    stacked = dict(norm_mix=norm_mix, w_in=w_in, shift_mix=shift_mix, w0=w0, w_lora_up=w_lora_up,
                   a0=a0, a_lora_up=a_lora_up, g_lora_up=g_lora_up, k_k=k_k, k_a=k_a,
                   r_k=r_k.reshape(depth, RWKV_W), lnx_gain=lnx_gain, lnx_bias=lnx_bias,
                   lam_q1=lam_q1, lam_k1=lam_k1, lam_q2=lam_q2, lam_k2=lam_k2,
                   subln_gain=subln_gain, w_out=w_out, norm_cross=norm_cross, norm_mem=norm_mem,
                   wq_c=wq_c, wkv_c=wkv_c, wo_c=wo_c, norm_ffn=norm_ffn, w_up=w_up,
                   conv_w=conv_w, conv_b=conv_b, w_down=w_down)
    for l in range(depth):
        p = {name: a[l] for name, a in stacked.items()}
        lambda_init = 0.8 - 0.6 * math.exp(-0.3 * l)
        x2, (gn, wg, wv, cw, cb, wd) = _layer(x2d, pos2d, mem2d, bsz, seq, lambda_init, p)
        x2d = _ffn(x2, bsz, seq, gn, wg, wv, cw, cb, wd, norm_final[None, :], l == depth - 1)
    return x2d.reshape(bsz, seq, D_MODEL)
```

```python
import functools
import math

import numpy as np
import jax
import jax.numpy as jnp
from jax import lax
from jax.experimental import pallas as pl
from jax.experimental.pallas import tpu as pltpu

F32 = jnp.float32
BF16 = jnp.bfloat16

D_MODEL = 1024
RWKV_W = 512
RWKV_N = 64
R_DECAY = 32
R_AAA = 32
R_GATE = 96
DIFF_W = 512
DIFF_H = 4
DIFF_D = 64
N_SHIFT = 3 * RWKV_W + R_DECAY + R_AAA + R_GATE
MEM_LEN = 256
CROSS_H = 4
CROSS_D = D_MODEL // CROSS_H
D_FF = 2816
ROPE_THETA = 10000.0
NORM_EPS = 1e-6
LNX_EPS = 64e-5
SUBLN_EPS = 1e-5

LANES = 128
LORA_PAD = 256
ZR_W = 3 * RWKV_W + LORA_PAD
VMEM_LIMIT = 56 * 1024 * 1024

ROW_TILE = 512
PROJ_TILE = 1024
CHUNK = 64
ATT_TQ = 2048
ATT_TK = 512
FF_TILE = 256


def _mm(a, b):
    return jnp.dot(a.astype(BF16), b.astype(BF16), preferred_element_type=F32)


def _mm_nt(a, b):
    return lax.dot_general(a.astype(BF16), b.astype(BF16), (((1,), (1,)), ((), ())),
                           preferred_element_type=F32)


def _mm_tn(a, b):
    return lax.dot_general(a.astype(BF16), b.astype(BF16), (((0,), (0,)), ((), ())),
                           preferred_element_type=F32)


def _sigmoid(x):
    return 1.0 / (1.0 + jnp.exp(-x))


def _rms(x, g, eps):
    return x * lax.rsqrt(jnp.mean(x * x, axis=-1, keepdims=True) + eps) * g


def _split_dot(x, ones_b):
    hi = x.astype(BF16)
    lo = (x - hi.astype(F32)).astype(BF16)
    return (jnp.dot(hi, ones_b, preferred_element_type=F32)
            + jnp.dot(lo, ones_b, preferred_element_type=F32))


def _params(*sem):
    return pltpu.CompilerParams(dimension_semantics=sem, vmem_limit_bytes=VMEM_LIMIT)


def _inproj_kernel(x_ref, pos_ref, g_ref, wr_ref, wd_ref, zr_ref, q_ref, k_ref, v_ref):
    lane = lax.broadcasted_iota(jnp.int32, (1, LANES), 1)
    freq = (lane % (DIFF_D // 2)).astype(F32)
    inv = jnp.exp(freq * (-2.0 / DIFF_D * math.log(ROPE_THETA)))
    ang = pos_ref[...].astype(F32) * inv
    cos = jnp.cos(ang)
    sin = jnp.where(lane < LANES // 2, -jnp.sin(ang), jnp.sin(ang))
    scale = DIFF_D ** -0.5 * math.log2(math.e)
    cos_q = cos * scale
    sin_q = sin * scale

    def rope(xb, c, s):
        return xb * c + pltpu.roll(xb, LANES // 2, axis=1) * s

    h = _rms(x_ref[...], g_ref[...], NORM_EPS).astype(BF16)
    for j in range(ZR_W // 256):
        sl = slice(j * 256, (j + 1) * 256)
        zr_ref[:, sl] = jnp.dot(h, wr_ref[:, sl], preferred_element_type=F32)

    for j in range(DIFF_W // 256):
        zq = jnp.dot(h, wd_ref[:, j * 256:(j + 1) * 256], preferred_element_type=F32)
        zk = jnp.dot(h, wd_ref[:, DIFF_W + j * 256:DIFF_W + (j + 1) * 256],
                     preferred_element_type=F32)
        zv = jnp.dot(h, wd_ref[:, 2 * DIFF_W + j * 256:2 * DIFF_W + (j + 1) * 256],
                     preferred_element_type=F32)
        for u in range(2):
            c0 = j * 256 + u * LANES
            q_ref[:, c0:c0 + LANES] = rope(zq[:, u * LANES:(u + 1) * LANES], cos_q, sin_q).astype(BF16)
            k_ref[:, c0:c0 + LANES] = rope(zk[:, u * LANES:(u + 1) * LANES], cos, sin).astype(BF16)
        v_ref[:, j * 256:(j + 1) * 256] = zv.astype(BF16)


def _inproj(x2d, pos2d, g, w_r, w_d):
    m = x2d.shape[0]
    tm = PROJ_TILE
    row = lambda i: (i, 0)
    const = lambda i: (0, 0)
    return pl.pallas_call(
        _inproj_kernel,
        grid=(m // tm,),
        in_specs=[
            pl.BlockSpec((tm, D_MODEL), row),
            pl.BlockSpec((tm, 1), row),
            pl.BlockSpec((1, D_MODEL), const),
            pl.BlockSpec((D_MODEL, ZR_W), const, pipeline_mode=pl.Buffered(1)),
            pl.BlockSpec((D_MODEL, 3 * DIFF_W), const, pipeline_mode=pl.Buffered(1)),
        ],
        out_specs=[
            pl.BlockSpec((tm, ZR_W), row),
            pl.BlockSpec((tm, DIFF_W), row),
            pl.BlockSpec((tm, DIFF_W), row),
            pl.BlockSpec((tm, DIFF_W), row),
        ],
        out_shape=[
            jax.ShapeDtypeStruct((m, ZR_W), F32),
            jax.ShapeDtypeStruct((m, DIFF_W), BF16),
            jax.ShapeDtypeStruct((m, DIFF_W), BF16),
            jax.ShapeDtypeStruct((m, DIFF_W), BF16),
        ],
        compiler_params=_params("parallel"),
        name="inproj",
    )(x2d, pos2d, g, w_r, w_d)


def _rwkv_kernel(z_ref, mix_ref, wl_ref, w0_ref, a0_ref, kk_ref, ka_ref, rk_ref, gain_ref,
                 bias_ref, y_ref, carry_s, st_s, r_s, lw_s, k_s, v_s, al_s, be_s, g_s, bo_s, y_s):
    tb = z_ref.shape[0]
    n_pair = RWKV_W // LANES

    @pl.when(pl.program_id(1) == 0)
    def _():
        carry_s[...] = jnp.zeros_like(carry_s)
        st_s[...] = jnp.zeros_like(st_s)

    row = lax.broadcasted_iota(jnp.int32, (tb, 1), 0)

    def shifted(c0, c1):
        zc = z_ref[:, c0:c1]
        zp = jnp.where(row == 0, carry_s[7:8, c0:c1], pltpu.roll(zc, 1, axis=0))
        return zc + (zp - zc) * mix_ref[:, c0:c1]

    ri = lax.broadcasted_iota(jnp.int32, (LANES, LANES), 0)
    ci = lax.broadcasted_iota(jnp.int32, (LANES, LANES), 1)
    same_head = (ri // RWKV_N) == (ci // RWKV_N)
    ones_head = jnp.where(same_head, 1.0, 0.0).astype(BF16)

    def headsum(x):
        return jnp.concatenate(
            [_split_dot(x[:, p * LANES:(p + 1) * LANES], ones_head) for p in range(n_pair)], axis=1)

    zl = shifted(3 * RWKV_W, ZR_W)
    ll = lax.broadcasted_iota(jnp.int32, (1, LORA_PAD), 1)
    act = jnp.where(ll < R_DECAY, jnp.tanh(zl),
                    jnp.where(ll < R_DECAY + R_AAA, zl, _sigmoid(zl)))
    lo = jnp.dot(act.astype(BF16), wl_ref[...], preferred_element_type=F32)
    lw_s[...] = -math.exp(-0.5) * _sigmoid(w0_ref[...] + lo[:, 0:RWKV_W])
    a = _sigmoid(a0_ref[...] + lo[:, RWKV_W:2 * RWKV_W])
    g_s[...] = lo[:, 2 * RWKV_W:3 * RWKV_W]

    r = shifted(0, RWKV_W)
    k = shifted(RWKV_W, 2 * RWKV_W)
    v = shifted(2 * RWKV_W, 3 * RWKV_W)
    carry_s[...] = z_ref[tb - 8:tb, :]
    r_s[...] = r
    v_s[...] = v
    kk = k * kk_ref[...]
    ss = headsum(kk * kk)
    alpha = kk * lax.rsqrt(jnp.maximum(ss, 1e-24))
    al_s[...] = alpha
    be_s[...] = alpha * a
    k2 = k * (1.0 + (a - 1.0) * ka_ref[...])
    k_s[...] = k2
    bo_s[...] = headsum(r * k2 * rk_ref[...]) * v

    c_len = CHUNK
    tri = jnp.where(lax.broadcasted_iota(jnp.int32, (c_len, c_len), 0)
                    >= lax.broadcasted_iota(jnp.int32, (c_len, c_len), 1), 1.0, 0.0).astype(BF16)
    lane = lax.broadcasted_iota(jnp.int32, (1, LANES), 1)
    head0 = lane < RWKV_N
    strict = same_head & (ri > ci)
    incl = same_head & (ri >= ci)
    b16 = (ri // 16) == (ci // 16)
    b32 = (ri // 32) == (ci // 32)
    eye = jnp.where(ri == ci, 1.0, 0.0).astype(F32)

    def stack2(x):
        return jnp.concatenate([jnp.where(head0, x, 0.0), jnp.where(head0, 0.0, x)], axis=0)

    def fold(x):
        return x[:c_len] + x[c_len:]

    def chunk_body(c, carry):
        r0 = pl.multiple_of(c * c_len, c_len)
        for p in range(n_pair):
            sl = (pl.ds(r0, c_len), slice(p * LANES, (p + 1) * LANES))
            r_c, lw, k_c, v_c, al, be = r_s[sl], lw_s[sl], k_s[sl], v_s[sl], al_s[sl], be_s[sl]
            hi = lw.astype(BF16)
            r1 = lw - hi.astype(F32)
            mid = r1.astype(BF16)
            low = (r1 - mid.astype(F32)).astype(BF16)
            cum = (jnp.dot(tri, hi, preferred_element_type=F32)
                   + jnp.dot(tri, mid, preferred_element_type=F32)
                   + jnp.dot(tri, low, preferred_element_type=F32))
            tot = cum[c_len - 1:c_len, :]
            e_neg = jnp.exp(-cum)
            e_end = jnp.exp(tot - cum)
            a_t = al * jnp.exp(cum - lw)
            r_t = r_c * jnp.exp(cum)
            a_st, r_st, v_st = stack2(a_t), stack2(r_t), stack2(v_c)
            b_n, k_n = be * e_neg, k_c * e_neg
            gram = _mm_nt(jnp.concatenate([a_st, r_st], axis=0),
                          jnp.concatenate([b_n, b_n, k_n, k_n], axis=0))
            a_ab = jnp.where(strict, gram[:LANES, :LANES], 0.0)
            a_ak = jnp.where(strict, gram[:LANES, LANES:], 0.0)
            a_rb = jnp.where(incl, gram[LANES:, :LANES], 0.0)
            a_rk = jnp.where(incl, gram[LANES:, LANES:], 0.0)
            a0 = jnp.where(b16, a_ab, 0.0)
            e1 = jnp.where(b32 & jnp.logical_not(b16), a_ab, 0.0)
            e2 = jnp.where(b32, 0.0, a_ab)
            p2 = _mm(a0, a0)
            p4 = _mm(p2, p2)
            p8 = _mm(p4, p4)
            t = eye - a0
            t = t + _mm(t, p2)
            t = t + _mm(t, p4)
            t = t + _mm(t, p8)
            t = t - _mm(_mm(t, e1), t)
            t = t - _mm(_mm(t, e2), t)
            akv = _mm(a_ak, v_st)
            w = _mm(t, jnp.concatenate([a_st, akv], axis=1))
            rbw = _mm(a_rb, w)
            rkv = _mm(a_rk, v_st)
            r_hat = fold(r_st - rbw[:, :LANES])
            y0 = fold(rkv - rbw[:, LANES:])
            ta_tv = jnp.concatenate([fold(w[:, :LANES]), fold(w[:, LANES:])], axis=1)
            bw = _mm_tn(be * e_end, ta_tv)
            kv = _mm_tn(k_c * e_end, v_c)
            m_p = jnp.where(same_head, bw[:, :LANES], 0.0)
            n_p = jnp.where(same_head, kv - bw[:, LANES:], 0.0)
            decay_col = jnp.exp(jnp.broadcast_to(tot, (LANES, LANES)).T)
            st = st_s[p]
            zz = _mm(jnp.concatenate([m_p, r_hat], axis=0), st)
            y_s[sl] = zz[LANES:] + y0
            st_s[p] = decay_col * st - zz[:LANES] + n_p
        return carry

    lax.fori_loop(0, tb // c_len, chunk_body, 0)

    y = y_s[...]
    mu = headsum(y) * (1.0 / RWKV_N)
    d = y - mu
    var = headsum(d * d) * (1.0 / RWKV_N)
    yn = d * lax.rsqrt(var + LNX_EPS) * gain_ref[...] + bias_ref[...]
    y_ref[...] = ((yn + bo_s[...]) * g_s[...]).astype(y_ref.dtype)


def _rwkv(zr, bsz, seq, mix, w_lora, w0, a0, k_k, k_a, r_k, gain, bias):
    tb = ROW_TILE
    nt = seq // tb
    row = lambda b, t: (b * nt + t, 0)
    const = lambda b, t: (0, 0)
    vec = pl.BlockSpec((1, RWKV_W), const)
    big = pltpu.VMEM((tb, RWKV_W), F32)
    return pl.pallas_call(
        _rwkv_kernel,
        grid=(bsz, nt),
        in_specs=[
            pl.BlockSpec((tb, ZR_W), row),
            pl.BlockSpec((1, ZR_W), const),
            pl.BlockSpec((LORA_PAD, 3 * RWKV_W), const),
            vec, vec, vec, vec, vec, vec, vec,
        ],
        out_specs=pl.BlockSpec((tb, RWKV_W), row),
        out_shape=jax.ShapeDtypeStruct((bsz * seq, RWKV_W), BF16),
        scratch_shapes=[
            pltpu.VMEM((8, ZR_W), F32),
            pltpu.VMEM((RWKV_W // LANES, LANES, LANES), F32),
            big, big, big, big, big, big, big, big, big,
        ],
        compiler_params=_params("parallel", "arbitrary"),
        name="rwkv7",
    )(zr, mix, w_lora, w0, a0, k_k, k_a, r_k, gain, bias)


def _rwkv2_kernel(z_ref, mix_ref, wl_ref, w0_ref, a0_ref, kk_ref, ka_ref, rk_ref, gain_ref,
                  bias_ref, y_ref, carry_s, st_s, g_s, bo_s, y_s, rt_s, dec_s,
                  ah0_s, ah1_s, rh0_s, rh1_s, vh0_s, vh1_s, bn_s, kn_s, be_s, ke_s, v_s,
                  mr_s, n_s, dc_s):
    tb = z_ref.shape[0]
    n_pair = RWKV_W // LANES
    c_len = CHUNK
    n_chunk = tb // c_len

    @pl.when(pl.program_id(1) == 0)
    def _():
        carry_s[...] = jnp.zeros_like(carry_s)
        st_s[...] = jnp.zeros_like(st_s)

    row = lax.broadcasted_iota(jnp.int32, (tb, 1), 0)

    def shifted(c0, c1):
        zc = z_ref[:, c0:c1]
        zp = jnp.where(row == 0, carry_s[7:8, c0:c1], pltpu.roll(zc, 1, axis=0))
        return zc + (zp - zc) * mix_ref[:, c0:c1]

    ri = lax.broadcasted_iota(jnp.int32, (LANES, LANES), 0)
    ci = lax.broadcasted_iota(jnp.int32, (LANES, LANES), 1)
    same_head = (ri // RWKV_N) == (ci // RWKV_N)
    ones_head = jnp.where(same_head, 1.0, 0.0).astype(BF16)

    def headsum(x):
        return jnp.concatenate(
            [_split_dot(x[:, p * LANES:(p + 1) * LANES], ones_head) for p in range(n_pair)], axis=1)

    zl = shifted(3 * RWKV_W, ZR_W)
    ll = lax.broadcasted_iota(jnp.int32, (1, LORA_PAD), 1)
    act = jnp.where(ll < R_DECAY, jnp.tanh(zl),
                    jnp.where(ll < R_DECAY + R_AAA, zl, _sigmoid(zl)))
    lo = jnp.dot(act.astype(BF16), wl_ref[...], preferred_element_type=F32)
    lw = -math.exp(-0.5) * _sigmoid(w0_ref[...] + lo[:, 0:RWKV_W])
    a = _sigmoid(a0_ref[...] + lo[:, RWKV_W:2 * RWKV_W])
    g_s[...] = lo[:, 2 * RWKV_W:3 * RWKV_W]

    r = shifted(0, RWKV_W)
    k = shifted(RWKV_W, 2 * RWKV_W)
    v = shifted(2 * RWKV_W, 3 * RWKV_W)
    carry_s[...] = z_ref[tb - 8:tb, :]
    kk = k * kk_ref[...]
    ss = headsum(kk * kk)
    alpha = kk * lax.rsqrt(jnp.maximum(ss, 1e-24))
    beta = alpha * a
    k2 = k * (1.0 + (a - 1.0) * ka_ref[...])
    bo_s[...] = headsum(r * k2 * rk_ref[...]) * v

    rin = row % c_len
    cum = lw
    for s in (1, 2, 4, 8, 16, 32):
        cum = cum + jnp.where(rin >= s, pltpu.roll(cum, s, axis=0), 0.0)
    tot = jnp.broadcast_to(cum.reshape(n_chunk, c_len, RWKV_W)[:, c_len - 1:c_len, :],
                           (n_chunk, c_len, RWKV_W)).reshape(tb, RWKV_W)
    head0 = (lax.broadcasted_iota(jnp.int32, (1, RWKV_W), 1) % LANES) < RWKV_N
    e_neg = jnp.exp(-cum)
    e_end = jnp.exp(tot - cum)
    a_t = alpha * jnp.exp(cum - lw)
    r_t = r * jnp.exp(cum)
    rt_s[...] = r_t
    dec_s[...] = jnp.exp(tot)
    ah0_s[...] = jnp.where(head0, a_t, 0.0).astype(BF16)
    ah1_s[...] = jnp.where(head0, 0.0, a_t).astype(BF16)
    rh0_s[...] = jnp.where(head0, r_t, 0.0).astype(BF16)
    rh1_s[...] = jnp.where(head0, 0.0, r_t).astype(BF16)
    vh0_s[...] = jnp.where(head0, v, 0.0).astype(BF16)
    vh1_s[...] = jnp.where(head0, 0.0, v).astype(BF16)
    v_s[...] = v.astype(BF16)
    bn_s[...] = (beta * e_neg).astype(BF16)
    kn_s[...] = (k2 * e_neg).astype(BF16)
    be_s[...] = (beta * e_end).astype(BF16)
    ke_s[...] = (k2 * e_end).astype(BF16)

    strict = same_head & (ri > ci)
    incl = same_head & (ri >= ci)
    b16 = (ri // 16) == (ci // 16)
    b32 = (ri // 32) == (ci // 32)
    eye = jnp.where(ri == ci, 1.0, 0.0).astype(F32)
    group = 2 * n_pair
    nb = range(group)

    def fold(x):
        return x[:c_len] + x[c_len:]

    def mmb(xs, ys):
        return [_mm(x, y) for x, y in zip(xs, ys)]

    def prep_body(it, carry):
        idx = []
        for u in range(group):
            c = it * (group // n_pair) + u // n_pair
            p = u % n_pair
            idx.append((c, c * n_pair + p,
                        (pl.ds(pl.multiple_of(c * c_len, c_len), c_len), slice(p * LANES, (p + 1) * LANES))))
        sls = [sl for _, _, sl in idx]
        lhs = [jnp.concatenate([ah0_s[sl], ah1_s[sl], rh0_s[sl], rh1_s[sl]], axis=0) for sl in sls]
        rhs = [jnp.concatenate([bn_s[sl], bn_s[sl], kn_s[sl], kn_s[sl]], axis=0) for sl in sls]
        gram = [lax.dot_general(x, y, (((1,), (1,)), ((), ())), preferred_element_type=F32)
                for x, y in zip(lhs, rhs)]
        a_ab = [jnp.where(strict, gm[:LANES, :LANES], 0.0) for gm in gram]
        a_kr = [jnp.concatenate([jnp.where(strict, gm[:LANES, LANES:], 0.0),
                                 jnp.where(incl, gm[LANES:, LANES:], 0.0)], axis=0).astype(BF16)
                for gm in gram]
        a_rb = [jnp.where(incl, gm[LANES:, :LANES], 0.0).astype(BF16) for gm in gram]
        a0 = [jnp.where(b16, x, 0.0).astype(BF16) for x in a_ab]
        e1 = [jnp.where(b32 & jnp.logical_not(b16), x, 0.0).astype(BF16) for x in a_ab]
        e2 = [jnp.where(b32, 0.0, x).astype(BF16) for x in a_ab]
        p2 = mmb(a0, a0)
        p4 = mmb(p2, p2)
        p8 = mmb(p4, p4)
        t = [eye - x.astype(F32) for x in a0]
        t = [x + y for x, y in zip(t, mmb(t, p2))]
        t = [x + y for x, y in zip(t, mmb(t, p4))]
        t = [x + y for x, y in zip(t, mmb(t, p8))]
        t = [x - y for x, y in zip(t, mmb(mmb(t, e1), t))]
        t = [x - y for x, y in zip(t, mmb(mmb(t, e2), t))]
        v_st = [jnp.concatenate([vh0_s[sl], vh1_s[sl]], axis=0) for sl in sls]
        av = mmb(a_kr, v_st)
        w = mmb(t, [jnp.concatenate([x[:LANES], y[:LANES].astype(BF16)], axis=1)
                    for x, y in zip(lhs, av)])
        rbw = mmb(a_rb, w)
        ta_tv = [jnp.concatenate([fold(x[:, :LANES]), fold(x[:, LANES:])], axis=1) for x in w]
        bw = [_mm_tn(be_s[sl], x) for sl, x in zip(sls, ta_tv)]
        kv = [_mm_tn(ke_s[sl], v_s[sl]) for sl in sls]
        for u in nb:
            c, j, sl = idx[u]
            r_hat = rt_s[sl] - fold(rbw[u][:, :LANES])
            m_p = jnp.where(same_head, bw[u][:, :LANES], 0.0)
            mr_s[j] = jnp.concatenate([m_p, r_hat], axis=0).astype(BF16)
            y_s[sl] = fold(av[u][LANES:] - rbw[u][:, LANES:])
            n_s[j] = jnp.where(same_head, kv[u] - bw[u][:, LANES:], 0.0)
            d = dec_s[sl]
            dc_s[j] = jnp.concatenate([d, d], axis=0).T
        return carry

    lax.fori_loop(0, n_chunk * n_pair // group, prep_body, 0)

    def scan_body(c, carry):
        rows = pl.ds(pl.multiple_of(c * c_len, c_len), c_len)
        sts = [st_s[p] for p in range(n_pair)]
        zz = [jnp.dot(mr_s[c * n_pair + p], sts[p].astype(BF16), preferred_element_type=F32)
              for p in range(n_pair)]
        for p in range(n_pair):
            j = c * n_pair + p
            sl = (rows, slice(p * LANES, (p + 1) * LANES))
            y_s[sl] = y_s[sl] + zz[p][LANES:]
            st_s[p] = dc_s[j] * sts[p] - zz[p][:LANES] + n_s[j]
        return carry

    lax.fori_loop(0, n_chunk, scan_body, 0)

    y = y_s[...]
    mu = headsum(y) * (1.0 / RWKV_N)
    d = y - mu
    var = headsum(d * d) * (1.0 / RWKV_N)
    yn = d * lax.rsqrt(var + LNX_EPS) * gain_ref[...] + bias_ref[...]
    y_ref[...] = ((yn + bo_s[...]) * g_s[...]).astype(y_ref.dtype)


def _rwkv2(zr, bsz, seq, mix, w_lora, w0, a0, k_k, k_a, r_k, gain, bias):
    tb = ROW_TILE
    nt = seq // tb
    n_prob = (tb // CHUNK) * (RWKV_W // LANES)
    row = lambda b, t: (b * nt + t, 0)
    const = lambda b, t: (0, 0)
    vec = pl.BlockSpec((1, RWKV_W), const)
    big = pltpu.VMEM((tb, RWKV_W), F32)
    half = pltpu.VMEM((tb, RWKV_W), BF16)
    return pl.pallas_call(
        _rwkv2_kernel,
        grid=(bsz, nt),
        in_specs=[
            pl.BlockSpec((tb, ZR_W), row),
            pl.BlockSpec((1, ZR_W), const),
            pl.BlockSpec((LORA_PAD, 3 * RWKV_W), const),
            vec, vec, vec, vec, vec, vec, vec,
        ],
        out_specs=pl.BlockSpec((tb, RWKV_W), row),
        out_shape=jax.ShapeDtypeStruct((bsz * seq, RWKV_W), BF16),
        scratch_shapes=[
            pltpu.VMEM((8, ZR_W), F32),
            pltpu.VMEM((RWKV_W // LANES, LANES, LANES), F32),
            big, big, big, big, big,
            half, half, half, half, half, half, half, half, half, half, half,
            pltpu.VMEM((n_prob, LANES + CHUNK, LANES), BF16),
            pltpu.VMEM((n_prob, LANES, LANES), F32),
            pltpu.VMEM((n_prob, LANES, LANES), F32),
        ],
        compiler_params=_params("parallel", "arbitrary"),
        name="rwkv7",
    )(zr, mix, w_lora, w0, a0, k_k, k_a, r_k, gain, bias)


def _attn_kernel(qi_tab, ki_tab, q_ref, k_ref, v_ref, lq1_ref, lk1_ref, lq2_ref, lk2_ref, sg_ref,
                 o_ref, q_s, m_s, acc_s, *, lambda_init, ratio):
    tq, tk = q_ref.shape[0], k_ref.shape[0]
    n_col = tk // LANES
    pidx = pl.program_id(1)
    qi = qi_tab[pidx]
    ki = ki_tab[pidx]
    nt = (((1,), (1,)), ((), ()))

    @pl.when(ki == 0)
    def _():
        map0 = lax.broadcasted_iota(jnp.int32, (1, LANES), 1) < DIFF_D
        for h in range(DIFF_H):
            q = q_ref[:, h * LANES:(h + 1) * LANES]
            zero = jnp.zeros_like(q)
            q_s[2 * h] = jnp.where(map0, q, zero)
            q_s[2 * h + 1] = jnp.where(map0, zero, q)
        m_s[...] = jnp.full_like(m_s, -jnp.inf)
        acc_s[...] = jnp.zeros_like(acc_s)

    def step(masked):
        if masked:
            rowp = qi * tq + lax.broadcasted_iota(jnp.int32, (tq, LANES), 0)
            colp = ki * tk + lax.broadcasted_iota(jnp.int32, (tq, LANES), 1)
            keep = [colp + j * LANES <= rowp for j in range(n_col)]
        ones = jnp.ones((tk, LANES), BF16)
        for h in range(DIFF_H):
            k = k_ref[:, h * LANES:(h + 1) * LANES]
            v_ext = jnp.concatenate([v_ref[:, h * LANES:(h + 1) * LANES], ones], axis=1)
            for c in range(2):
                i = 2 * h + c
                s = lax.dot_general(q_s[i], k, nt, preferred_element_type=F32)
                cols = [s[:, j * LANES:(j + 1) * LANES] for j in range(n_col)]
                if masked:
                    cols = [jnp.where(keep[j], cols[j], -jnp.inf) for j in range(n_col)]
                mx = cols[0]
                for j in range(1, n_col):
                    mx = jnp.maximum(mx, cols[j])
                m_old = m_s[i]
                m_new = jnp.maximum(m_old, jnp.max(mx, axis=-1, keepdims=True))
                corr = jnp.exp2(m_old - m_new)
                p = jnp.concatenate([jnp.exp2(cj - m_new).astype(BF16) for cj in cols], axis=1)
                pv = jnp.dot(p, v_ext, preferred_element_type=F32)
                acc_s[i] = jnp.concatenate([corr, corr], axis=1) * acc_s[i] + pv
                m_s[i] = m_new

    on_diag = ki >= qi * ratio

    @pl.when(on_diag)
    def _():
        step(True)

    @pl.when(jnp.logical_not(on_diag))
    def _():
        step(False)

    @pl.when(ki == (qi + 1) * ratio - 1)
    def _():
        lam = (jnp.exp(jnp.sum(lq1_ref[...] * lk1_ref[...], axis=-1, keepdims=True))
               - jnp.exp(jnp.sum(lq2_ref[...] * lk2_ref[...], axis=-1, keepdims=True))
               + lambda_init)
        for h in range(DIFF_H):
            a1 = acc_s[2 * h]
            a2 = acc_s[2 * h + 1]
            o = a1[:, :LANES] / a1[:, LANES:] - lam * (a2[:, :LANES] / a2[:, LANES:])
            o = _rms(o, sg_ref[...], SUBLN_EPS) * (1.0 - lambda_init)
            o_ref[:, h * LANES:(h + 1) * LANES] = o.astype(o_ref.dtype)


def _diff_attn(q, k, v, bsz, seq, lq1, lk1, lq2, lk2, sg, lambda_init):
    tq, tk = ATT_TQ, ATT_TK
    ratio = tq // tk
    nq, nk = seq // tq, seq // tk
    pairs = [(a, b) for a in range(nq) for b in range((a + 1) * ratio)]
    qi_tab = jnp.asarray(np.array([a for a, _ in pairs], np.int32))
    ki_tab = jnp.asarray(np.array([b for _, b in pairs], np.int32))
    qmap = lambda b, p, qt, kt: (b * nq + qt[p], 0)
    kmap = lambda b, p, qt, kt: (b * nk + kt[p], 0)
    const = lambda b, p, qt, kt: (0, 0)
    lam_spec = pl.BlockSpec((1, DIFF_D), const)
    grid_spec = pltpu.PrefetchScalarGridSpec(
        num_scalar_prefetch=2,
        grid=(bsz, len(pairs)),
        in_specs=[
            pl.BlockSpec((tq, DIFF_W), qmap),
            pl.BlockSpec((tk, DIFF_W), kmap),
            pl.BlockSpec((tk, DIFF_W), kmap),
            lam_spec, lam_spec, lam_spec, lam_spec,
            pl.BlockSpec((1, 2 * DIFF_D), const),
        ],
        out_specs=pl.BlockSpec((tq, DIFF_W), qmap),
        scratch_shapes=[
            pltpu.VMEM((2 * DIFF_H, tq, LANES), BF16),
            pltpu.VMEM((2 * DIFF_H, tq, LANES), F32),
            pltpu.VMEM((2 * DIFF_H, tq, 2 * LANES), F32),
        ],
    )
    return pl.pallas_call(
        functools.partial(_attn_kernel, lambda_init=lambda_init, ratio=ratio),
        grid_spec=grid_spec,
        out_shape=jax.ShapeDtypeStruct((bsz * seq, DIFF_W), BF16),
        compiler_params=_params("parallel", "arbitrary"),
        name="diff_attn",
    )(qi_tab, ki_tab, q, k, v, lq1, lk1, lq2, lk2, sg)


def _attn2_kernel(q_ref, k_ref, v_ref, lq1_ref, lk1_ref, lq2_ref, lk2_ref, sg_ref, o_ref,
                  q_s, m_s, acc_s, *, lambda_init, tk):
    tq = q_ref.shape[0]
    n_col = tk // LANES
    n_diag = tq // tk
    qi = pl.program_id(2)
    nt = (((1,), (1,)), ((), ()))

    map0 = (lax.broadcasted_iota(jnp.int32, (1, LANES), 1) % DIFF_D) < DIFF_D // 2
    q = q_ref[...]
    zero = jnp.zeros_like(q)
    q_s[0] = jnp.where(map0, q, zero)
    q_s[1] = jnp.where(map0, zero, q)
    m_s[...] = jnp.full_like(m_s, -jnp.inf)
    acc_s[...] = jnp.zeros_like(acc_s)
    ones = jnp.ones((tk, LANES), BF16)
    tri_keep = [lax.broadcasted_iota(jnp.int32, (tk, LANES), 1) + j * LANES
                <= lax.broadcasted_iota(jnp.int32, (tk, LANES), 0) for j in range(n_col)]

    def kv_step(kv_rows, r0, r1, masked):
        k = k_ref[kv_rows, :]
        v_ext = jnp.concatenate([v_ref[kv_rows, :], ones], axis=1)
        for c in range(2):
            s = lax.dot_general(q_s[c, r0:r1, :], k, nt, preferred_element_type=F32)
            cols = [s[:, j * LANES:(j + 1) * LANES] for j in range(n_col)]
            if masked:
                cols = [jnp.where(tri_keep[j], cols[j], -jnp.inf) for j in range(n_col)]
            mx = cols[0]
            for j in range(1, n_col):
                mx = jnp.maximum(mx, cols[j])
            m_old = m_s[c, r0:r1, :]
            m_new = jnp.maximum(m_old, jnp.max(mx, axis=-1, keepdims=True))
            corr = jnp.exp2(m_old - m_new)
            p = jnp.concatenate([jnp.exp2(cj - m_new).astype(BF16) for cj in cols], axis=1)
            pv = jnp.dot(p, v_ext, preferred_element_type=F32)
            acc_s[c, r0:r1, :] = jnp.concatenate([corr, corr], axis=1) * acc_s[c, r0:r1, :] + pv
            m_s[c, r0:r1, :] = m_new

    def full_body(j, carry):
        for u in range(n_diag):
            kv_step(pl.ds(pl.multiple_of((j * n_diag + u) * tk, tk), tk), 0, tq, False)
        return carry

    lax.fori_loop(0, qi, full_body, 0)
    for d in range(n_diag):
        kv_rows = pl.ds(pl.multiple_of((qi * n_diag + d) * tk, tk), tk)
        kv_step(kv_rows, d * tk, (d + 1) * tk, True)
        if d + 1 < n_diag:
            kv_step(kv_rows, (d + 1) * tk, tq, False)

    lam = (jnp.exp(jnp.sum(lq1_ref[...] * lk1_ref[...], axis=-1, keepdims=True))
           - jnp.exp(jnp.sum(lq2_ref[...] * lk2_ref[...], axis=-1, keepdims=True))
           + lambda_init)
    a1 = acc_s[0]
    a2 = acc_s[1]
    o = a1[:, :LANES] / a1[:, LANES:] - lam * (a2[:, :LANES] / a2[:, LANES:])
    o_ref[...] = (_rms(o, sg_ref[...], SUBLN_EPS) * (1.0 - lambda_init)).astype(o_ref.dtype)


def _diff_attn2(q, k, v, bsz, seq, lq1, lk1, lq2, lk2, sg, lambda_init):
    tq = min(ATT_TQ, seq)
    tk = min(ATT_TK, tq)
    nq = seq // tq
    qmap = lambda b, h, i: (b * nq + i, h)
    kmap = lambda b, h, i: (b, h)
    const = lambda b, h, i: (0, 0)
    lam_spec = pl.BlockSpec((1, DIFF_D), const)
    return pl.pallas_call(
        functools.partial(_attn2_kernel, lambda_init=lambda_init, tk=tk),
        grid=(bsz, DIFF_H, nq),
        in_specs=[
            pl.BlockSpec((tq, LANES), qmap),
            pl.BlockSpec((seq, LANES), kmap),
            pl.BlockSpec((seq, LANES), kmap),
            lam_spec, lam_spec, lam_spec, lam_spec,
            pl.BlockSpec((1, 2 * DIFF_D), const),
        ],
        out_specs=pl.BlockSpec((tq, LANES), qmap),
        out_shape=jax.ShapeDtypeStruct((bsz * seq, DIFF_W), BF16),
        scratch_shapes=[
            pltpu.VMEM((2, tq, LANES), BF16),
            pltpu.VMEM((2, tq, LANES), F32),
            pltpu.VMEM((2, tq, 2 * LANES), F32),
        ],
        compiler_params=_params("parallel", "parallel", "arbitrary"),
        name="diff_attn",
    )(q, k, v, lq1, lk1, lq2, lk2, sg)


def _memkv_kernel(m_ref, g_ref, w_ref, o_ref):
    h = _rms(m_ref[...], g_ref[...], NORM_EPS).astype(BF16)
    for j in range(o_ref.shape[1] // 256):
        sl = slice(j * 256, (j + 1) * 256)
        o_ref[:, sl] = jnp.dot(h, w_ref[:, sl], preferred_element_type=F32).astype(o_ref.dtype)


def _memkv(mem2d, g, wkv):
    m = mem2d.shape[0]
    tm = MEM_LEN
    return pl.pallas_call(
        _memkv_kernel,
        grid=(m // tm,),
        in_specs=[
            pl.BlockSpec((tm, D_MODEL), lambda i: (i, 0)),
            pl.BlockSpec((1, D_MODEL), lambda i: (0, 0)),
            pl.BlockSpec((D_MODEL, 2 * D_MODEL), lambda i: (0, 0)),
        ],
        out_specs=pl.BlockSpec((tm, 2 * D_MODEL), lambda i: (i, 0)),
        out_shape=jax.ShapeDtypeStruct((m, 2 * D_MODEL), BF16),
        compiler_params=_params("parallel"),
        name="memkv",
    )(mem2d, g, wkv)


def _cross_kernel(x_ref, ya_ref, yb_ref, kv_ref, wout_ref, g_ref, wq_ref, wo_ref, o_ref, att_s):
    tm = x_ref.shape[0]
    halves = [slice(0, tm // 2), slice(tm // 2, tm)]
    nt = (((1,), (1,)), ((), ()))
    x1 = [x_ref[r, :] + (jnp.dot(ya_ref[r, :], wout_ref[0:RWKV_W, :], preferred_element_type=F32)
                         + jnp.dot(yb_ref[r, :], wout_ref[RWKV_W:, :], preferred_element_type=F32))
          for r in halves]
    hc = [_rms(x, g_ref[...], NORM_EPS).astype(BF16) for x in x1]
    scale = CROSS_D ** -0.5 * math.log2(math.e)
    q = [(jnp.dot(x, wq_ref[...], preferred_element_type=F32) * scale).astype(BF16) for x in hc]
    for h in range(CROSS_H):
        sl = slice(h * CROSS_D, (h + 1) * CROSS_D)
        kh = kv_ref[:, sl]
        vh = kv_ref[:, D_MODEL + h * CROSS_D:D_MODEL + (h + 1) * CROSS_D]
        s = [lax.dot_general(x[:, sl], kh, nt, preferred_element_type=F32) for x in q]
        p = [jnp.exp2(x - jnp.max(x, axis=-1, keepdims=True)) for x in s]
        o = [jnp.dot(x.astype(BF16), vh, preferred_element_type=F32) / jnp.sum(x, axis=-1, keepdims=True)
             for x in p]
        for r, x in zip(halves, o):
            att_s[r, sl] = x.astype(BF16)
    for r, x in zip(halves, x1):
        o_ref[r, :] = x + jnp.dot(att_s[r, :], wo_ref[...], preferred_element_type=F32)


def _cross(x2d, ya, yb, kv, bsz, seq, w_out, g, wq, wo):
    tm = PROJ_TILE
    nt = seq // tm
    row = lambda b, t: (b * nt + t, 0)
    const = lambda b, t: (0, 0)
    return pl.pallas_call(
        _cross_kernel,
        grid=(bsz, nt),
        in_specs=[
            pl.BlockSpec((tm, D_MODEL), row),
            pl.BlockSpec((tm, RWKV_W), row),
            pl.BlockSpec((tm, DIFF_W), row),
            pl.BlockSpec((MEM_LEN, 2 * D_MODEL), lambda b, t: (b, 0)),
            pl.BlockSpec((D_MODEL, D_MODEL), const, pipeline_mode=pl.Buffered(1)),
            pl.BlockSpec((1, D_MODEL), const),
            pl.BlockSpec((D_MODEL, D_MODEL), const, pipeline_mode=pl.Buffered(1)),
            pl.BlockSpec((D_MODEL, D_MODEL), const, pipeline_mode=pl.Buffered(1)),
        ],
        out_specs=pl.BlockSpec((tm, D_MODEL), row),
        out_shape=jax.ShapeDtypeStruct((bsz * seq, D_MODEL), F32),
        scratch_shapes=[pltpu.VMEM((tm, D_MODEL), BF16)],
        compiler_params=_params("parallel", "parallel"),
        name="outproj_cross",
    )(x2d, ya, yb, kv, w_out, g, wq, wo)


def _ffn_kernel(x_ref, g_ref, wg_ref, wv_ref, cw_ref, cb_ref, wd_ref, gf_ref, o_ref, carry_s, *, final):
    tm = x_ref.shape[0]
    n_ff = wg_ref.shape[0]

    @pl.when(pl.program_id(1) == 0)
    def _():
        carry_s[...] = jnp.zeros_like(carry_s)

    x = x_ref[...]
    h = _rms(x, g_ref[...], NORM_EPS).astype(BF16)

    def up(j):
        return (jnp.dot(h, wg_ref[j], preferred_element_type=F32),
                jnp.dot(h, wv_ref[j], preferred_element_type=F32))

    nxt = up(0)
    acc = None
    for j in range(n_ff):
        gate, val = nxt
        if j + 1 < n_ff:
            nxt = up(j + 1)
        ext = jnp.concatenate([carry_s[j], gate], axis=0)
        g1 = ext[7:tm + 7, :]
        g2 = ext[6:tm + 6, :]
        carry_s[j] = gate[tm - 8:tm, :]
        cwh = 0.5 * cw_ref[j]
        hc = cwh[0:1, :] * g2 + cwh[1:2, :] * g1 + cwh[2:3, :] * gate + 0.5 * cb_ref[j]
        act = (hc + hc * jnp.tanh(hc)) * val
        down = jnp.dot(act.astype(BF16), wd_ref[j], preferred_element_type=F32)
        acc = down if acc is None else acc + down
    out = x + acc
    o_ref[...] = _rms(out, gf_ref[...], NORM_EPS) if final else out


def _ffn(x2d, bsz, seq, g, wg, wv, cw, cb, wd, gf, final):
    tm = PROJ_TILE
    nt = seq // tm
    n_ff, _, tf = wg.shape
    row = lambda b, t: (b * nt + t, 0)
    c2 = lambda b, t: (0, 0)
    c3 = lambda b, t: (0, 0, 0)
    once = pl.Buffered(1)
    return pl.pallas_call(
        functools.partial(_ffn_kernel, final=final),
        grid=(bsz, nt),
        in_specs=[
            pl.BlockSpec((tm, D_MODEL), row),
            pl.BlockSpec((1, D_MODEL), c2),
            pl.BlockSpec((n_ff, D_MODEL, tf), c3, pipeline_mode=once),
            pl.BlockSpec((n_ff, D_MODEL, tf), c3, pipeline_mode=once),
            pl.BlockSpec((n_ff, 3, tf), c3),
            pl.BlockSpec((n_ff, 1, tf), c3),
            pl.BlockSpec((n_ff, tf, D_MODEL), c3, pipeline_mode=once),
            pl.BlockSpec((1, D_MODEL), c2),
        ],
        out_specs=pl.BlockSpec((tm, D_MODEL), row),
        out_shape=jax.ShapeDtypeStruct((bsz * seq, D_MODEL), F32),
        scratch_shapes=[pltpu.VMEM((n_ff, 8, tf), F32)],
        compiler_params=_params("parallel", "arbitrary"),
        name="conv_ffn",
    )(x2d, g, wg, wv, cw, cb, wd, gf)


def _rope_perm():
    idx = np.empty((DIFF_W,), np.int32)
    half = DIFF_D // 2
    for h in range(DIFF_H):
        for c in range(2):
            for d in range(DIFF_D):
                idx[h * LANES + (d // half) * DIFF_D + c * half + d % half] = h * LANES + c * DIFF_D + d
    return idx


def _layer(x2d, pos2d, mem2d, bsz, seq, lambda_init, p):
    w_in = p["w_in"]
    pad = jnp.zeros((D_MODEL, LORA_PAD - (N_SHIFT - 3 * RWKV_W)), F32)
    w_r = jnp.concatenate([w_in[:, :N_SHIFT], pad], axis=1).astype(BF16)
    perm = _rope_perm()
    w_d = jnp.concatenate([w_in[:, N_SHIFT:N_SHIFT + DIFF_W][:, perm],
                           w_in[:, N_SHIFT + DIFF_W:N_SHIFT + 2 * DIFF_W][:, perm],
                           w_in[:, N_SHIFT + 2 * DIFF_W:]], axis=1).astype(BF16)
    mix = jnp.concatenate([p["shift_mix"], jnp.zeros((LORA_PAD - (N_SHIFT - 3 * RWKV_W),), F32)])[None, :]
    w_lora = jnp.zeros((LORA_PAD, 3 * RWKV_W), F32)
    w_lora = w_lora.at[0:R_DECAY, 0:RWKV_W].set(p["w_lora_up"])
    w_lora = w_lora.at[R_DECAY:R_DECAY + R_AAA, RWKV_W:2 * RWKV_W].set(p["a_lora_up"])
    w_lora = w_lora.at[R_DECAY + R_AAA:R_DECAY + R_AAA + R_GATE, 2 * RWKV_W:].set(p["g_lora_up"])
    w_lora = w_lora.astype(BF16)
    v512 = lambda a: a.reshape(1, RWKV_W)

    zr, q, k, v = _inproj(x2d, pos2d, p["norm_mix"][None, :], w_r, w_d)
    ya = _rwkv2(zr, bsz, seq, mix, w_lora, v512(p["w0"]), v512(p["a0"]), v512(p["k_k"]),
               v512(p["k_a"]), v512(p["r_k"]), v512(p["lnx_gain"]), v512(p["lnx_bias"]))
    yb = _diff_attn2(q, k, v, bsz, seq, p["lam_q1"][None, :], p["lam_k1"][None, :],
                    p["lam_q2"][None, :], p["lam_k2"][None, :], p["subln_gain"][None, :], lambda_init)
    kv = _memkv(mem2d, p["norm_mem"][None, :], p["wkv_c"].astype(BF16))
    x2 = _cross(x2d, ya, yb, kv, bsz, seq, p["w_out"].astype(BF16), p["norm_cross"][None, :],
                p["wq_c"].astype(BF16), p["wo_c"].astype(BF16))
    n_ff = D_FF // FF_TILE
    w_up = p["w_up"]
    wg = w_up[:, :D_FF].reshape(D_MODEL, n_ff, FF_TILE).transpose(1, 0, 2).astype(BF16)
    wv = w_up[:, D_FF:].reshape(D_MODEL, n_ff, FF_TILE).transpose(1, 0, 2).astype(BF16)
    cw = p["conv_w"].reshape(3, n_ff, FF_TILE).transpose(1, 0, 2)
    cb = p["conv_b"].reshape(n_ff, 1, FF_TILE)
    wd = p["w_down"].reshape(n_ff, FF_TILE, D_MODEL).astype(BF16)
    return x2, (p["norm_ffn"][None, :], wg, wv, cw, cb, wd)


def kernel(x, mem, positions, norm_mix, w_in, shift_mix, w0, w_lora_up, a0, a_lora_up, g_lora_up, k_k, k_a, r_k, lnx_gain, lnx_bias, lam_q1, lam_k1, lam_q2, lam_k2, subln_gain, w_out, norm_cross, norm_mem, wq_c, wkv_c, wo_c, norm_ffn, w_up, conv_w, conv_b, w_down, norm_final):
    bsz, seq, _ = x.shape
    depth = norm_mix.shape[0]
    x2d = x.reshape(bsz * seq, D_MODEL)
    pos2d = positions.reshape(bsz * seq, 1)
    mem2d = mem.reshape(bsz * mem.shape[1], D_MODEL)
    stacked = dict(norm_mix=norm_mix, w_in=w_in, shift_mix=shift_mix, w0=w0, w_lora_up=w_lora_up,
                   a0=a0, a_lora_up=a_lora_up, g_lora_up=g_lora_up, k_k=k_k, k_a=k_a,
                   r_k=r_k.reshape(depth, RWKV_W), lnx_gain=lnx_gain, lnx_bias=lnx_bias,
                   lam_q1=lam_q1, lam_k1=lam_k1, lam_q2=lam_q2, lam_k2=lam_k2,
                   subln_gain=subln_gain, w_out=w_out, norm_cross=norm_cross, norm_mem=norm_mem,
                   wq_c=wq_c, wkv_c=wkv_c, wo_c=wo_c, norm_ffn=norm_ffn, w_up=w_up,
                   conv_w=conv_w, conv_b=conv_b, w_down=w_down)
    for l in range(depth):
        p = {name: a[l] for name, a in stacked.items()}
        lambda_init = 0.8 - 0.6 * math.exp(-0.3 * l)
        x2, (gn, wg, wv, cw, cb, wd) = _layer(x2d, pos2d, mem2d, bsz, seq, lambda_init, p)
        x2d = _ffn(x2, bsz, seq, gn, wg, wv, cw, cb, wd, norm_final[None, :], l == depth - 1)
    return x2d.reshape(bsz, seq, D_MODEL)
```

```python
import functools
import math

import numpy as np
import jax
import jax.numpy as jnp
from jax import lax
from jax.experimental import pallas as pl
from jax.experimental.pallas import tpu as pltpu

F32 = jnp.float32
BF16 = jnp.bfloat16

D_MODEL = 1024
RWKV_W = 512
RWKV_N = 64
R_DECAY = 32
R_AAA = 32
R_GATE = 96
DIFF_W = 512
DIFF_H = 4
DIFF_D = 64
N_SHIFT = 3 * RWKV_W + R_DECAY + R_AAA + R_GATE
MEM_LEN = 256
CROSS_H = 4
CROSS_D = D_MODEL // CROSS_H
D_FF = 2816
ROPE_THETA = 10000.0
NORM_EPS = 1e-6
LNX_EPS = 64e-5
SUBLN_EPS = 1e-5

LANES = 128
LORA_PAD = 256
ZR_W = 3 * RWKV_W + LORA_PAD
VMEM_LIMIT = 56 * 1024 * 1024

ROW_TILE = 1024
PROJ_TILE = 1024
CHUNK = 64
ATT_TQ = 2048
ATT_TK = 512
FF_TILE = 256


def _mm(a, b):
    return jnp.dot(a.astype(BF16), b.astype(BF16), preferred_element_type=F32)


def _mm_nt(a, b):
    return lax.dot_general(a.astype(BF16), b.astype(BF16), (((1,), (1,)), ((), ())),
                           preferred_element_type=F32)


def _mm_tn(a, b):
    return lax.dot_general(a.astype(BF16), b.astype(BF16), (((0,), (0,)), ((), ())),
                           preferred_element_type=F32)


def _sigmoid(x):
    return 1.0 / (1.0 + jnp.exp(-x))


def _rms(x, g, eps):
    return x * lax.rsqrt(jnp.mean(x * x, axis=-1, keepdims=True) + eps) * g


def _split_dot(x, ones_b):
    hi = x.astype(BF16)
    lo = (x - hi.astype(F32)).astype(BF16)
    return (jnp.dot(hi, ones_b, preferred_element_type=F32)
            + jnp.dot(lo, ones_b, preferred_element_type=F32))


def _run_interleaved(*stage_gens):
    live = list(stage_gens)
    while live:
        for g in list(live):
            try:
                next(g)
            except StopIteration:
                live.remove(g)


def _params(*sem):
    return pltpu.CompilerParams(dimension_semantics=sem, vmem_limit_bytes=VMEM_LIMIT)


def _inproj_kernel(x_ref, pos_ref, g_ref, wr_ref, wd_ref, zr_ref, q_ref, k_ref, v_ref):
    lane = lax.broadcasted_iota(jnp.int32, (1, LANES), 1)
    freq = (lane % (DIFF_D // 2)).astype(F32)
    inv = jnp.exp(freq * (-2.0 / DIFF_D * math.log(ROPE_THETA)))
    ang = pos_ref[...].astype(F32) * inv
    cos = jnp.cos(ang)
    sin = jnp.where(lane < LANES // 2, -jnp.sin(ang), jnp.sin(ang))
    scale = DIFF_D ** -0.5 * math.log2(math.e)
    cos_q = cos * scale
    sin_q = sin * scale

    def rope(xb, c, s):
        return xb * c + pltpu.roll(xb, LANES // 2, axis=1) * s

    h = _rms(x_ref[...], g_ref[...], NORM_EPS).astype(BF16)
    for j in range(ZR_W // 256):
        sl = slice(j * 256, (j + 1) * 256)
        zr_ref[:, sl] = jnp.dot(h, wr_ref[:, sl], preferred_element_type=F32)

    for j in range(DIFF_W // 256):
        zq = jnp.dot(h, wd_ref[:, j * 256:(j + 1) * 256], preferred_element_type=F32)
        zk = jnp.dot(h, wd_ref[:, DIFF_W + j * 256:DIFF_W + (j + 1) * 256],
                     preferred_element_type=F32)
        zv = jnp.dot(h, wd_ref[:, 2 * DIFF_W + j * 256:2 * DIFF_W + (j + 1) * 256],
                     preferred_element_type=F32)
        for u in range(2):
            c0 = j * 256 + u * LANES
            q_ref[:, c0:c0 + LANES] = rope(zq[:, u * LANES:(u + 1) * LANES], cos_q, sin_q).astype(BF16)
            k_ref[:, c0:c0 + LANES] = rope(zk[:, u * LANES:(u + 1) * LANES], cos, sin).astype(BF16)
        v_ref[:, j * 256:(j + 1) * 256] = zv.astype(BF16)


def _inproj(x2d, pos2d, g, w_r, w_d):
    m = x2d.shape[0]
    tm = PROJ_TILE
    row = lambda i: (i, 0)
    const = lambda i: (0, 0)
    return pl.pallas_call(
        _inproj_kernel,
        grid=(m // tm,),
        in_specs=[
            pl.BlockSpec((tm, D_MODEL), row),
            pl.BlockSpec((tm, 1), row),
            pl.BlockSpec((1, D_MODEL), const),
            pl.BlockSpec((D_MODEL, ZR_W), const, pipeline_mode=pl.Buffered(1)),
            pl.BlockSpec((D_MODEL, 3 * DIFF_W), const, pipeline_mode=pl.Buffered(1)),
        ],
        out_specs=[
            pl.BlockSpec((tm, ZR_W), row),
            pl.BlockSpec((tm, DIFF_W), row),
            pl.BlockSpec((tm, DIFF_W), row),
            pl.BlockSpec((tm, DIFF_W), row),
        ],
        out_shape=[
            jax.ShapeDtypeStruct((m, ZR_W), F32),
            jax.ShapeDtypeStruct((m, DIFF_W), BF16),
            jax.ShapeDtypeStruct((m, DIFF_W), BF16),
            jax.ShapeDtypeStruct((m, DIFF_W), BF16),
        ],
        compiler_params=_params("parallel"),
        name="inproj",
    )(x2d, pos2d, g, w_r, w_d)


def _rwkv_kernel(z_ref, mix_ref, wl_ref, w0_ref, a0_ref, kk_ref, ka_ref, rk_ref, gain_ref,
                 bias_ref, y_ref, carry_s, st_s, r_s, lw_s, k_s, v_s, al_s, be_s, g_s, bo_s, y_s):
    tb = z_ref.shape[0]
    n_pair = RWKV_W // LANES

    @pl.when(pl.program_id(1) == 0)
    def _():
        carry_s[...] = jnp.zeros_like(carry_s)
        st_s[...] = jnp.zeros_like(st_s)

    row = lax.broadcasted_iota(jnp.int32, (tb, 1), 0)

    def shifted(c0, c1):
        zc = z_ref[:, c0:c1]
        zp = jnp.where(row == 0, carry_s[7:8, c0:c1], pltpu.roll(zc, 1, axis=0))
        return zc + (zp - zc) * mix_ref[:, c0:c1]

    ri = lax.broadcasted_iota(jnp.int32, (LANES, LANES), 0)
    ci = lax.broadcasted_iota(jnp.int32, (LANES, LANES), 1)
    same_head = (ri // RWKV_N) == (ci // RWKV_N)
    ones_head = jnp.where(same_head, 1.0, 0.0).astype(BF16)

    def headsum(x):
        return jnp.concatenate(
            [_split_dot(x[:, p * LANES:(p + 1) * LANES], ones_head) for p in range(n_pair)], axis=1)

    zl = shifted(3 * RWKV_W, ZR_W)
    ll = lax.broadcasted_iota(jnp.int32, (1, LORA_PAD), 1)
    act = jnp.where(ll < R_DECAY, jnp.tanh(zl),
                    jnp.where(ll < R_DECAY + R_AAA, zl, _sigmoid(zl)))
    lo = jnp.dot(act.astype(BF16), wl_ref[...], preferred_element_type=F32)
    lw_s[...] = -math.exp(-0.5) * _sigmoid(w0_ref[...] + lo[:, 0:RWKV_W])
    a = _sigmoid(a0_ref[...] + lo[:, RWKV_W:2 * RWKV_W])
    g_s[...] = lo[:, 2 * RWKV_W:3 * RWKV_W]

    r = shifted(0, RWKV_W)
    k = shifted(RWKV_W, 2 * RWKV_W)
    v = shifted(2 * RWKV_W, 3 * RWKV_W)
    carry_s[...] = z_ref[tb - 8:tb, :]
    r_s[...] = r
    v_s[...] = v
    kk = k * kk_ref[...]
    ss = headsum(kk * kk)
    alpha = kk * lax.rsqrt(jnp.maximum(ss, 1e-24))
    al_s[...] = alpha
    be_s[...] = alpha * a
    k2 = k * (1.0 + (a - 1.0) * ka_ref[...])
    k_s[...] = k2
    bo_s[...] = headsum(r * k2 * rk_ref[...]) * v

    c_len = CHUNK
    tri = jnp.where(lax.broadcasted_iota(jnp.int32, (c_len, c_len), 0)
                    >= lax.broadcasted_iota(jnp.int32, (c_len, c_len), 1), 1.0, 0.0).astype(BF16)
    lane = lax.broadcasted_iota(jnp.int32, (1, LANES), 1)
    head0 = lane < RWKV_N
    strict = same_head & (ri > ci)
    incl = same_head & (ri >= ci)
    b16 = (ri // 16) == (ci // 16)
    b32 = (ri // 32) == (ci // 32)
    eye = jnp.where(ri == ci, 1.0, 0.0).astype(F32)

    def stack2(x):
        return jnp.concatenate([jnp.where(head0, x, 0.0), jnp.where(head0, 0.0, x)], axis=0)

    def fold(x):
        return x[:c_len] + x[c_len:]

    def chunk_body(c, carry):
        r0 = pl.multiple_of(c * c_len, c_len)
        for p in range(n_pair):
            sl = (pl.ds(r0, c_len), slice(p * LANES, (p + 1) * LANES))
            r_c, lw, k_c, v_c, al, be = r_s[sl], lw_s[sl], k_s[sl], v_s[sl], al_s[sl], be_s[sl]
            hi = lw.astype(BF16)
            r1 = lw - hi.astype(F32)
            mid = r1.astype(BF16)
            low = (r1 - mid.astype(F32)).astype(BF16)
            cum = (jnp.dot(tri, hi, preferred_element_type=F32)
                   + jnp.dot(tri, mid, preferred_element_type=F32)
                   + jnp.dot(tri, low, preferred_element_type=F32))
            tot = cum[c_len - 1:c_len, :]
            e_neg = jnp.exp(-cum)
            e_end = jnp.exp(tot - cum)
            a_t = al * jnp.exp(cum - lw)
            r_t = r_c * jnp.exp(cum)
            a_st, r_st, v_st = stack2(a_t), stack2(r_t), stack2(v_c)
            b_n, k_n = be * e_neg, k_c * e_neg
            gram = _mm_nt(jnp.concatenate([a_st, r_st], axis=0),
                          jnp.concatenate([b_n, b_n, k_n, k_n], axis=0))
            a_ab = jnp.where(strict, gram[:LANES, :LANES], 0.0)
            a_ak = jnp.where(strict, gram[:LANES, LANES:], 0.0)
            a_rb = jnp.where(incl, gram[LANES:, :LANES], 0.0)
            a_rk = jnp.where(incl, gram[LANES:, LANES:], 0.0)
            a0 = jnp.where(b16, a_ab, 0.0)
            e1 = jnp.where(b32 & jnp.logical_not(b16), a_ab, 0.0)
            e2 = jnp.where(b32, 0.0, a_ab)
            p2 = _mm(a0, a0)
            p4 = _mm(p2, p2)
            p8 = _mm(p4, p4)
            t = eye - a0
            t = t + _mm(t, p2)
            t = t + _mm(t, p4)
            t = t + _mm(t, p8)
            t = t - _mm(_mm(t, e1), t)
            t = t - _mm(_mm(t, e2), t)
            akv = _mm(a_ak, v_st)
            w = _mm(t, jnp.concatenate([a_st, akv], axis=1))
            rbw = _mm(a_rb, w)
            rkv = _mm(a_rk, v_st)
            r_hat = fold(r_st - rbw[:, :LANES])
            y0 = fold(rkv - rbw[:, LANES:])
            ta_tv = jnp.concatenate([fold(w[:, :LANES]), fold(w[:, LANES:])], axis=1)
            bw = _mm_tn(be * e_end, ta_tv)
            kv = _mm_tn(k_c * e_end, v_c)
            m_p = jnp.where(same_head, bw[:, :LANES], 0.0)
            n_p = jnp.where(same_head, kv - bw[:, LANES:], 0.0)
            decay_col = jnp.exp(jnp.broadcast_to(tot, (LANES, LANES)).T)
            st = st_s[p]
            zz = _mm(jnp.concatenate([m_p, r_hat], axis=0), st)
            y_s[sl] = zz[LANES:] + y0
            st_s[p] = decay_col * st - zz[:LANES] + n_p
        return carry

    lax.fori_loop(0, tb // c_len, chunk_body, 0)

    y = y_s[...]
    mu = headsum(y) * (1.0 / RWKV_N)
    d = y - mu
    var = headsum(d * d) * (1.0 / RWKV_N)
    yn = d * lax.rsqrt(var + LNX_EPS) * gain_ref[...] + bias_ref[...]
    y_ref[...] = ((yn + bo_s[...]) * g_s[...]).astype(y_ref.dtype)


def _rwkv(zr, bsz, seq, mix, w_lora, w0, a0, k_k, k_a, r_k, gain, bias):
    tb = ROW_TILE
    nt = seq // tb
    row = lambda b, t: (b * nt + t, 0)
    const = lambda b, t: (0, 0)
    vec = pl.BlockSpec((1, RWKV_W), const)
    big = pltpu.VMEM((tb, RWKV_W), F32)
    return pl.pallas_call(
        _rwkv_kernel,
        grid=(bsz, nt),
        in_specs=[
            pl.BlockSpec((tb, ZR_W), row),
            pl.BlockSpec((1, ZR_W), const),
            pl.BlockSpec((LORA_PAD, 3 * RWKV_W), const),
            vec, vec, vec, vec, vec, vec, vec,
        ],
        out_specs=pl.BlockSpec((tb, RWKV_W), row),
        out_shape=jax.ShapeDtypeStruct((bsz * seq, RWKV_W), BF16),
        scratch_shapes=[
            pltpu.VMEM((8, ZR_W), F32),
            pltpu.VMEM((RWKV_W // LANES, LANES, LANES), F32),
            big, big, big, big, big, big, big, big, big,
        ],
        compiler_params=_params("parallel", "arbitrary"),
        name="rwkv7",
    )(zr, mix, w_lora, w0, a0, k_k, k_a, r_k, gain, bias)


def _rwkv2_kernel(z_ref, mix_ref, wl_ref, w0_ref, a0_ref, kk_ref, ka_ref, rk_ref, gain_ref,
                  bias_ref, y_ref, carry_s, st_s, g_s, bo_s, y_s, rt_s, dec_s,
                  ah0_s, ah1_s, rh0_s, rh1_s, vh0_s, vh1_s, bn_s, kn_s, be_s, ke_s,
                  mr_s, n_s):
    tb = z_ref.shape[0]
    n_pair = RWKV_W // LANES
    c_len = CHUNK
    n_chunk = tb // c_len

    @pl.when(pl.program_id(1) == 0)
    def _():
        carry_s[...] = jnp.zeros_like(carry_s)
        st_s[...] = jnp.zeros_like(st_s)

    group_chunks = 2
    group_rows = group_chunks * c_len
    n_piece = tb // group_rows
    rowq = lax.broadcasted_iota(jnp.int32, (group_rows, 1), 0)
    ri = lax.broadcasted_iota(jnp.int32, (LANES, LANES), 0)
    ci = lax.broadcasted_iota(jnp.int32, (LANES, LANES), 1)
    same_head = (ri // RWKV_N) == (ci // RWKV_N)
    ones_head = jnp.where(same_head, 1.0, 0.0).astype(BF16)
    head0 = (lax.broadcasted_iota(jnp.int32, (1, RWKV_W), 1) % LANES) < RWKV_N

    def headsum(x):
        return jnp.concatenate(
            [_split_dot(x[:, p * LANES:(p + 1) * LANES], ones_head) for p in range(n_pair)], axis=1)

    def prep(r0, top):
        rows = pl.ds(r0, group_rows)

        def shifted(c0, c1):
            zc = z_ref[rows, c0:c1]
            zp = jnp.where(rowq == 0, top[7:8, c0:c1], pltpu.roll(zc, 1, axis=0))
            return zc + (zp - zc) * mix_ref[:, c0:c1]

        zl = shifted(3 * RWKV_W, ZR_W)
        ll = lax.broadcasted_iota(jnp.int32, (1, LORA_PAD), 1)
        act = jnp.where(ll < R_DECAY, jnp.tanh(zl),
                        jnp.where(ll < R_DECAY + R_AAA, zl, _sigmoid(zl)))
        lo = jnp.dot(act.astype(BF16), wl_ref[...], preferred_element_type=F32)
        yield
        lw = -math.exp(-0.5) * _sigmoid(w0_ref[...] + lo[:, 0:RWKV_W])
        a = _sigmoid(a0_ref[...] + lo[:, RWKV_W:2 * RWKV_W])
        g_s[rows, :] = lo[:, 2 * RWKV_W:3 * RWKV_W]
        yield

        r = shifted(0, RWKV_W)
        yield
        k = shifted(RWKV_W, 2 * RWKV_W)
        yield
        v = shifted(2 * RWKV_W, 3 * RWKV_W)
        yield
        kk = k * kk_ref[...]
        ss = headsum(kk * kk)
        alpha = kk * lax.rsqrt(jnp.maximum(ss, 1e-24))
        yield
        beta = alpha * a
        k2 = k * (1.0 + (a - 1.0) * ka_ref[...])
        bo_s[rows, :] = headsum(r * k2 * rk_ref[...]) * v
        yield

        rin = rowq % c_len
        cum = lw
        for s in (1, 2, 4, 8, 16, 32):
            cum = cum + jnp.where(rin >= s, pltpu.roll(cum, s, axis=0), 0.0)
            yield
        tot = jnp.broadcast_to(cum.reshape(group_chunks, c_len, RWKV_W)[:, c_len - 1:c_len, :],
                               (group_chunks, c_len, RWKV_W)).reshape(group_rows, RWKV_W)
        e_neg = jnp.exp(-cum)
        e_end = jnp.exp(tot - cum)
        yield
        a_t = alpha * jnp.exp(cum - lw)
        r_t = r * jnp.exp(cum)
        rt_s[rows, :] = r_t
        dec_s[rows, :] = jnp.exp(tot)
        yield
        ah0_s[rows, :] = jnp.where(head0, a_t, 0.0).astype(BF16)
        ah1_s[rows, :] = jnp.where(head0, 0.0, a_t).astype(BF16)
        yield
        rh0_s[rows, :] = jnp.where(head0, r_t, 0.0).astype(BF16)
        rh1_s[rows, :] = jnp.where(head0, 0.0, r_t).astype(BF16)
        yield
        vh0_s[rows, :] = jnp.where(head0, v, 0.0).astype(BF16)
        vh1_s[rows, :] = jnp.where(head0, 0.0, v).astype(BF16)
        yield
        bn_s[rows, :] = (beta * e_neg).astype(BF16)
        kn_s[rows, :] = (k2 * e_neg).astype(BF16)
        yield
        be_s[rows, :] = (beta * e_end).astype(BF16)
        ke_s[rows, :] = (k2 * e_end).astype(BF16)

    strict = same_head & (ri > ci)
    incl = same_head & (ri >= ci)
    b16 = (ri // 16) == (ci // 16)
    b32 = (ri // 32) == (ci // 32)
    eye = jnp.where(ri == ci, 1.0, 0.0).astype(F32)
    group = group_chunks * n_pair
    nb = range(group)

    def fold(x):
        return x[:c_len] + x[c_len:]

    def mmb(xs, ys):
        return [_mm(x, y) for x, y in zip(xs, ys)]

    def solve(it):
        idx = []
        for u in range(group):
            c = it * group_chunks + u // n_pair
            p = u % n_pair
            idx.append((c, c * n_pair + p,
                        (pl.ds(pl.multiple_of(c * c_len, c_len), c_len), slice(p * LANES, (p + 1) * LANES))))
        sls = [sl for _, _, sl in idx]
        lhs = [jnp.concatenate([ah0_s[sl], ah1_s[sl], rh0_s[sl], rh1_s[sl]], axis=0) for sl in sls]
        rhs = [jnp.concatenate([bn_s[sl], bn_s[sl], kn_s[sl], kn_s[sl]], axis=0) for sl in sls]
        gram = [lax.dot_general(x, y, (((1,), (1,)), ((), ())), preferred_element_type=F32)
                for x, y in zip(lhs, rhs)]
        a_ab = [jnp.where(strict, gm[:LANES, :LANES], 0.0) for gm in gram]
        a_kr = [jnp.concatenate([jnp.where(strict, gm[:LANES, LANES:], 0.0),
                                 jnp.where(incl, gm[LANES:, LANES:], 0.0)], axis=0).astype(BF16)
                for gm in gram]
        a_rb = [jnp.where(incl, gm[LANES:, :LANES], 0.0).astype(BF16) for gm in gram]
        yield
        a0 = [jnp.where(b16, x, 0.0).astype(BF16) for x in a_ab]
        e1 = [jnp.where(b32 & jnp.logical_not(b16), x, 0.0).astype(BF16) for x in a_ab]
        e2 = [jnp.where(b32, 0.0, x).astype(BF16) for x in a_ab]
        p2 = mmb(a0, a0)
        yield
        p4 = mmb(p2, p2)
        yield
        p8 = mmb(p4, p4)
        t = [eye - x.astype(F32) for x in a0]
        t = [x + y for x, y in zip(t, mmb(t, p2))]
        yield
        t = [x + y for x, y in zip(t, mmb(t, p4))]
        yield
        t = [x + y for x, y in zip(t, mmb(t, p8))]
        yield
        te = mmb(t, e1)
        yield
        t = [x - y for x, y in zip(t, mmb(te, t))]
        yield
        te = mmb(t, e2)
        yield
        t = [x - y for x, y in zip(t, mmb(te, t))]
        yield
        v_st = [jnp.concatenate([vh0_s[sl], vh1_s[sl]], axis=0) for sl in sls]
        av = mmb(a_kr, v_st)
        yield
        w = mmb(t, [jnp.concatenate([x[:LANES], y[:LANES].astype(BF16)], axis=1)
                    for x, y in zip(lhs, av)])
        yield
        rbw = mmb(a_rb, w)
        yield
        ta_tv = [jnp.concatenate([fold(x[:, :LANES]), fold(x[:, LANES:])], axis=1) for x in w]
        bw = [_mm_tn(be_s[sl], x) for sl, x in zip(sls, ta_tv)]
        yield
        kv = [_mm_tn(ke_s[sl], vh0_s[sl] + vh1_s[sl]) for sl in sls]
        yield
        for u in nb:
            c, j, sl = idx[u]
            r_hat = rt_s[sl] - fold(rbw[u][:, :LANES])
            m_p = jnp.where(same_head, bw[u][:, :LANES], 0.0)
            mr_s[j] = jnp.concatenate([m_p, r_hat], axis=0).astype(BF16)
            y_s[sl] = fold(av[u][LANES:] - rbw[u][:, LANES:])
            n_s[j] = jnp.where(same_head, kv[u] - bw[u][:, LANES:], 0.0)

    _run_interleaved(prep(0, carry_s[...]))
    carry_s[...] = z_ref[tb - 8:tb, :]

    def piece_body(it, carry):
        r0 = pl.multiple_of((it + 1) * group_rows, group_rows)
        _run_interleaved(solve(it), prep(r0, z_ref[pl.ds(r0 - 8, 8), :]))
        return carry

    lax.fori_loop(0, n_piece - 1, piece_body, 0)
    _run_interleaved(solve(n_piece - 1))

    def scan_body(c, carry):
        rows = pl.ds(pl.multiple_of(c * c_len, c_len), c_len)
        sts = [st_s[p] for p in range(n_pair)]
        zz = [jnp.dot(mr_s[c * n_pair + p], sts[p].astype(BF16), preferred_element_type=F32)
              for p in range(n_pair)]
        for p in range(n_pair):
            j = c * n_pair + p
            sl = (rows, slice(p * LANES, (p + 1) * LANES))
            y_s[sl] = y_s[sl] + zz[p][LANES:]
            d = dec_s[sl]
            decay_col = jnp.concatenate([d, d], axis=0).T
            st_s[p] = decay_col * sts[p] - zz[p][:LANES] + n_s[j]
        return carry

    lax.fori_loop(0, n_chunk, scan_body, 0)

    y = y_s[...]
    mu = headsum(y) * (1.0 / RWKV_N)
    d = y - mu
    var = headsum(d * d) * (1.0 / RWKV_N)
    yn = d * lax.rsqrt(var + LNX_EPS) * gain_ref[...] + bias_ref[...]
    y_ref[...] = ((yn + bo_s[...]) * g_s[...]).astype(y_ref.dtype)


def _rwkv2(zr, bsz, seq, mix, w_lora, w0, a0, k_k, k_a, r_k, gain, bias):
    tb = ROW_TILE
    nt = seq // tb
    n_prob = (tb // CHUNK) * (RWKV_W // LANES)
    row = lambda b, t: (b * nt + t, 0)
    const = lambda b, t: (0, 0)
    vec = pl.BlockSpec((1, RWKV_W), const)
    big = pltpu.VMEM((tb, RWKV_W), F32)
    half = pltpu.VMEM((tb, RWKV_W), BF16)
    return pl.pallas_call(
        _rwkv2_kernel,
        grid=(bsz, nt),
        in_specs=[
            pl.BlockSpec((tb, ZR_W), row),
            pl.BlockSpec((1, ZR_W), const),
            pl.BlockSpec((LORA_PAD, 3 * RWKV_W), const),
            vec, vec, vec, vec, vec, vec, vec,
        ],
        out_specs=pl.BlockSpec((tb, RWKV_W), row),
        out_shape=jax.ShapeDtypeStruct((bsz * seq, RWKV_W), BF16),
        scratch_shapes=[
            pltpu.VMEM((8, ZR_W), F32),
            pltpu.VMEM((RWKV_W // LANES, LANES, LANES), F32),
            big, big, big, big, big,
            half, half, half, half, half, half, half, half, half, half,
            pltpu.VMEM((n_prob, LANES + CHUNK, LANES), BF16),
            pltpu.VMEM((n_prob, LANES, LANES), F32),
        ],
        compiler_params=_params("parallel", "arbitrary"),
        name="rwkv7",
    )(zr, mix, w_lora, w0, a0, k_k, k_a, r_k, gain, bias)


def _attn_kernel(qi_tab, ki_tab, q_ref, k_ref, v_ref, lq1_ref, lk1_ref, lq2_ref, lk2_ref, sg_ref,
                 o_ref, q_s, m_s, acc_s, *, lambda_init, ratio):
    tq, tk = q_ref.shape[0], k_ref.shape[0]
    n_col = tk // LANES
    pidx = pl.program_id(1)
    qi = qi_tab[pidx]
    ki = ki_tab[pidx]
    nt = (((1,), (1,)), ((), ()))

    @pl.when(ki == 0)
    def _():
        map0 = lax.broadcasted_iota(jnp.int32, (1, LANES), 1) < DIFF_D
        for h in range(DIFF_H):
            q = q_ref[:, h * LANES:(h + 1) * LANES]
            zero = jnp.zeros_like(q)
            q_s[2 * h] = jnp.where(map0, q, zero)
            q_s[2 * h + 1] = jnp.where(map0, zero, q)
        m_s[...] = jnp.full_like(m_s, -jnp.inf)
        acc_s[...] = jnp.zeros_like(acc_s)

    def step(masked):
        if masked:
            rowp = qi * tq + lax.broadcasted_iota(jnp.int32, (tq, LANES), 0)
            colp = ki * tk + lax.broadcasted_iota(jnp.int32, (tq, LANES), 1)
            keep = [colp + j * LANES <= rowp for j in range(n_col)]
        ones = jnp.ones((tk, LANES), BF16)
        for h in range(DIFF_H):
            k = k_ref[:, h * LANES:(h + 1) * LANES]
            v_ext = jnp.concatenate([v_ref[:, h * LANES:(h + 1) * LANES], ones], axis=1)
            for c in range(2):
                i = 2 * h + c
                s = lax.dot_general(q_s[i], k, nt, preferred_element_type=F32)
                cols = [s[:, j * LANES:(j + 1) * LANES] for j in range(n_col)]
                if masked:
                    cols = [jnp.where(keep[j], cols[j], -jnp.inf) for j in range(n_col)]
                mx = cols[0]
                for j in range(1, n_col):
                    mx = jnp.maximum(mx, cols[j])
                m_old = m_s[i]
                m_new = jnp.maximum(m_old, jnp.max(mx, axis=-1, keepdims=True))
                corr = jnp.exp2(m_old - m_new)
                p = jnp.concatenate([jnp.exp2(cj - m_new).astype(BF16) for cj in cols], axis=1)
                pv = jnp.dot(p, v_ext, preferred_element_type=F32)
                acc_s[i] = jnp.concatenate([corr, corr], axis=1) * acc_s[i] + pv
                m_s[i] = m_new

    on_diag = ki >= qi * ratio

    @pl.when(on_diag)
    def _():
        step(True)

    @pl.when(jnp.logical_not(on_diag))
    def _():
        step(False)

    @pl.when(ki == (qi + 1) * ratio - 1)
    def _():
        lam = (jnp.exp(jnp.sum(lq1_ref[...] * lk1_ref[...], axis=-1, keepdims=True))
               - jnp.exp(jnp.sum(lq2_ref[...] * lk2_ref[...], axis=-1, keepdims=True))
               + lambda_init)
        for h in range(DIFF_H):
            a1 = acc_s[2 * h]
            a2 = acc_s[2 * h + 1]
            o = a1[:, :LANES] / a1[:, LANES:] - lam * (a2[:, :LANES] / a2[:, LANES:])
            o = _rms(o, sg_ref[...], SUBLN_EPS) * (1.0 - lambda_init)
            o_ref[:, h * LANES:(h + 1) * LANES] = o.astype(o_ref.dtype)


def _diff_attn(q, k, v, bsz, seq, lq1, lk1, lq2, lk2, sg, lambda_init):
    tq, tk = ATT_TQ, ATT_TK
    ratio = tq // tk
    nq, nk = seq // tq, seq // tk
    pairs = [(a, b) for a in range(nq) for b in range((a + 1) * ratio)]
    qi_tab = jnp.asarray(np.array([a for a, _ in pairs], np.int32))
    ki_tab = jnp.asarray(np.array([b for _, b in pairs], np.int32))
    qmap = lambda b, p, qt, kt: (b * nq + qt[p], 0)
    kmap = lambda b, p, qt, kt: (b * nk + kt[p], 0)
    const = lambda b, p, qt, kt: (0, 0)
    lam_spec = pl.BlockSpec((1, DIFF_D), const)
    grid_spec = pltpu.PrefetchScalarGridSpec(
        num_scalar_prefetch=2,
        grid=(bsz, len(pairs)),
        in_specs=[
            pl.BlockSpec((tq, DIFF_W), qmap),
            pl.BlockSpec((tk, DIFF_W), kmap),
            pl.BlockSpec((tk, DIFF_W), kmap),
            lam_spec, lam_spec, lam_spec, lam_spec,
            pl.BlockSpec((1, 2 * DIFF_D), const),
        ],
        out_specs=pl.BlockSpec((tq, DIFF_W), qmap),
        scratch_shapes=[
            pltpu.VMEM((2 * DIFF_H, tq, LANES), BF16),
            pltpu.VMEM((2 * DIFF_H, tq, LANES), F32),
            pltpu.VMEM((2 * DIFF_H, tq, 2 * LANES), F32),
        ],
    )
    return pl.pallas_call(
        functools.partial(_attn_kernel, lambda_init=lambda_init, ratio=ratio),
        grid_spec=grid_spec,
        out_shape=jax.ShapeDtypeStruct((bsz * seq, DIFF_W), BF16),
        compiler_params=_params("parallel", "arbitrary"),
        name="diff_attn",
    )(qi_tab, ki_tab, q, k, v, lq1, lk1, lq2, lk2, sg)


def _attn2_kernel(q_ref, k_ref, v_ref, lq1_ref, lk1_ref, lq2_ref, lk2_ref, sg_ref, o_ref,
                  q_s, m_s, acc_s, *, lambda_init, tk):
    tq = q_ref.shape[0]
    n_col = tk // LANES
    n_diag = tq // tk
    qi = pl.program_id(2)
    nt = (((1,), (1,)), ((), ()))

    map0 = (lax.broadcasted_iota(jnp.int32, (1, LANES), 1) % DIFF_D) < DIFF_D // 2
    q = q_ref[...]
    zero = jnp.zeros_like(q)
    q_s[0] = jnp.where(map0, q, zero)
    q_s[1] = jnp.where(map0, zero, q)
    m_s[...] = jnp.full_like(m_s, -jnp.inf)
    acc_s[...] = jnp.zeros_like(acc_s)
    ones = jnp.ones((tk, LANES), BF16)
    tri_keep = [lax.broadcasted_iota(jnp.int32, (tk, LANES), 1) + j * LANES
                <= lax.broadcasted_iota(jnp.int32, (tk, LANES), 0) for j in range(n_col)]

    def kv_step(kv_rows, r0, r1, masked):
        k = k_ref[kv_rows, :]
        v_ext = jnp.concatenate([v_ref[kv_rows, :], ones], axis=1)
        for c in range(2):
            s = lax.dot_general(q_s[c, r0:r1, :], k, nt, preferred_element_type=F32)
            cols = [s[:, j * LANES:(j + 1) * LANES] for j in range(n_col)]
            if masked:
                cols = [jnp.where(tri_keep[j], cols[j], -jnp.inf) for j in range(n_col)]
            mx = cols[0]
            for j in range(1, n_col):
                mx = jnp.maximum(mx, cols[j])
            m_old = m_s[c, r0:r1, :]
            m_new = jnp.maximum(m_old, jnp.max(mx, axis=-1, keepdims=True))
            corr = jnp.exp2(m_old - m_new)
            p = jnp.concatenate([jnp.exp2(cj - m_new).astype(BF16) for cj in cols], axis=1)
            pv = jnp.dot(p, v_ext, preferred_element_type=F32)
            acc_s[c, r0:r1, :] = jnp.concatenate([corr, corr], axis=1) * acc_s[c, r0:r1, :] + pv
            m_s[c, r0:r1, :] = m_new

    def full_body(j, carry):
        for u in range(n_diag):
            kv_step(pl.ds(pl.multiple_of((j * n_diag + u) * tk, tk), tk), 0, tq, False)
        return carry

    lax.fori_loop(0, qi, full_body, 0)
    for d in range(n_diag):
        kv_rows = pl.ds(pl.multiple_of((qi * n_diag + d) * tk, tk), tk)
        kv_step(kv_rows, d * tk, (d + 1) * tk, True)
        if d + 1 < n_diag:
            kv_step(kv_rows, (d + 1) * tk, tq, False)

    lam = (jnp.exp(jnp.sum(lq1_ref[...] * lk1_ref[...], axis=-1, keepdims=True))
           - jnp.exp(jnp.sum(lq2_ref[...] * lk2_ref[...], axis=-1, keepdims=True))
           + lambda_init)
    a1 = acc_s[0]
    a2 = acc_s[1]
    o = a1[:, :LANES] / a1[:, LANES:] - lam * (a2[:, :LANES] / a2[:, LANES:])
    o_ref[...] = (_rms(o, sg_ref[...], SUBLN_EPS) * (1.0 - lambda_init)).astype(o_ref.dtype)


def _diff_attn2(q, k, v, bsz, seq, lq1, lk1, lq2, lk2, sg, lambda_init):
    tq = min(ATT_TQ, seq)
    tk = min(ATT_TK, tq)
    nq = seq // tq
    qmap = lambda b, h, i: (b * nq + i, h)
    kmap = lambda b, h, i: (b, h)
    const = lambda b, h, i: (0, 0)
    lam_spec = pl.BlockSpec((1, DIFF_D), const)
    return pl.pallas_call(
        functools.partial(_attn2_kernel, lambda_init=lambda_init, tk=tk),
        grid=(bsz, DIFF_H, nq),
        in_specs=[
            pl.BlockSpec((tq, LANES), qmap),
            pl.BlockSpec((seq, LANES), kmap),
            pl.BlockSpec((seq, LANES), kmap),
            lam_spec, lam_spec, lam_spec, lam_spec,
            pl.BlockSpec((1, 2 * DIFF_D), const),
        ],
        out_specs=pl.BlockSpec((tq, LANES), qmap),
        out_shape=jax.ShapeDtypeStruct((bsz * seq, DIFF_W), BF16),
        scratch_shapes=[
            pltpu.VMEM((2, tq, LANES), BF16),
            pltpu.VMEM((2, tq, LANES), F32),
            pltpu.VMEM((2, tq, 2 * LANES), F32),
        ],
        compiler_params=_params("parallel", "parallel", "arbitrary"),
        name="diff_attn",
    )(q, k, v, lq1, lk1, lq2, lk2, sg)


def _memkv_kernel(m_ref, g_ref, w_ref, o_ref):
    h = _rms(m_ref[...], g_ref[...], NORM_EPS).astype(BF16)
    for j in range(o_ref.shape[1] // 256):
        sl = slice(j * 256, (j + 1) * 256)
        o_ref[:, sl] = jnp.dot(h, w_ref[:, sl], preferred_element_type=F32).astype(o_ref.dtype)


def _memkv(mem2d, g, wkv):
    m = mem2d.shape[0]
    tm = MEM_LEN
    return pl.pallas_call(
        _memkv_kernel,
        grid=(m // tm,),
        in_specs=[
            pl.BlockSpec((tm, D_MODEL), lambda i: (i, 0)),
            pl.BlockSpec((1, D_MODEL), lambda i: (0, 0)),
            pl.BlockSpec((D_MODEL, 2 * D_MODEL), lambda i: (0, 0)),
        ],
        out_specs=pl.BlockSpec((tm, 2 * D_MODEL), lambda i: (i, 0)),
        out_shape=jax.ShapeDtypeStruct((m, 2 * D_MODEL), BF16),
        compiler_params=_params("parallel"),
        name="memkv",
    )(mem2d, g, wkv)


def _cross_kernel(x_ref, ya_ref, yb_ref, kv_ref, wout_ref, g_ref, wq_ref, wo_ref, o_ref, att_s):
    tm = x_ref.shape[0]
    halves = [slice(0, tm // 2), slice(tm // 2, tm)]
    nt = (((1,), (1,)), ((), ()))
    x1 = [x_ref[r, :] + (jnp.dot(ya_ref[r, :], wout_ref[0:RWKV_W, :], preferred_element_type=F32)
                         + jnp.dot(yb_ref[r, :], wout_ref[RWKV_W:, :], preferred_element_type=F32))
          for r in halves]
    hc = [_rms(x, g_ref[...], NORM_EPS).astype(BF16) for x in x1]
    scale = CROSS_D ** -0.5 * math.log2(math.e)
    q = [(jnp.dot(x, wq_ref[...], preferred_element_type=F32) * scale).astype(BF16) for x in hc]
    for h in range(CROSS_H):
        sl = slice(h * CROSS_D, (h + 1) * CROSS_D)
        kh = kv_ref[:, sl]
        vh = kv_ref[:, D_MODEL + h * CROSS_D:D_MODEL + (h + 1) * CROSS_D]
        s = [lax.dot_general(x[:, sl], kh, nt, preferred_element_type=F32) for x in q]
        p = [jnp.exp2(x - jnp.max(x, axis=-1, keepdims=True)) for x in s]
        o = [jnp.dot(x.astype(BF16), vh, preferred_element_type=F32) / jnp.sum(x, axis=-1, keepdims=True)
             for x in p]
        for r, x in zip(halves, o):
            att_s[r, sl] = x.astype(BF16)
    for r, x in zip(halves, x1):
        o_ref[r, :] = x + jnp.dot(att_s[r, :], wo_ref[...], preferred_element_type=F32)


def _cross(x2d, ya, yb, kv, bsz, seq, w_out, g, wq, wo):
    tm = PROJ_TILE
    nt = seq // tm
    row = lambda b, t: (b * nt + t, 0)
    const = lambda b, t: (0, 0)
    return pl.pallas_call(
        _cross_kernel,
        grid=(bsz, nt),
        in_specs=[
            pl.BlockSpec((tm, D_MODEL), row),
            pl.BlockSpec((tm, RWKV_W), row),
            pl.BlockSpec((tm, DIFF_W), row),
            pl.BlockSpec((MEM_LEN, 2 * D_MODEL), lambda b, t: (b, 0)),
            pl.BlockSpec((D_MODEL, D_MODEL), const, pipeline_mode=pl.Buffered(1)),
            pl.BlockSpec((1, D_MODEL), const),
            pl.BlockSpec((D_MODEL, D_MODEL), const, pipeline_mode=pl.Buffered(1)),
            pl.BlockSpec((D_MODEL, D_MODEL), const, pipeline_mode=pl.Buffered(1)),
        ],
        out_specs=pl.BlockSpec((tm, D_MODEL), row),
        out_shape=jax.ShapeDtypeStruct((bsz * seq, D_MODEL), F32),
        scratch_shapes=[pltpu.VMEM((tm, D_MODEL), BF16)],
        compiler_params=_params("parallel", "parallel"),
        name="outproj_cross",
    )(x2d, ya, yb, kv, w_out, g, wq, wo)


def _ffn_kernel(x_ref, g_ref, wg_ref, wv_ref, cw_ref, cb_ref, wd_ref, gf_ref, o_ref, carry_s, *, final):
    tm = x_ref.shape[0]
    n_ff = wg_ref.shape[0]

    @pl.when(pl.program_id(1) == 0)
    def _():
        carry_s[...] = jnp.zeros_like(carry_s)

    x = x_ref[...]
    h = _rms(x, g_ref[...], NORM_EPS).astype(BF16)

    def up(j):
        return (jnp.dot(h, wg_ref[j], preferred_element_type=F32),
                jnp.dot(h, wv_ref[j], preferred_element_type=F32))

    nxt = up(0)
    acc = None
    for j in range(n_ff):
        gate, val = nxt
        if j + 1 < n_ff:
            nxt = up(j + 1)
        ext = jnp.concatenate([carry_s[j], gate], axis=0)
        g1 = ext[7:tm + 7, :]
        g2 = ext[6:tm + 6, :]
        carry_s[j] = gate[tm - 8:tm, :]
        cwh = 0.5 * cw_ref[j]
        hc = cwh[0:1, :] * g2 + cwh[1:2, :] * g1 + cwh[2:3, :] * gate + 0.5 * cb_ref[j]
        act = (hc + hc * jnp.tanh(hc)) * val
        down = jnp.dot(act.astype(BF16), wd_ref[j], preferred_element_type=F32)
        acc = down if acc is None else acc + down
    out = x + acc
    o_ref[...] = _rms(out, gf_ref[...], NORM_EPS) if final else out


def _ffn(x2d, bsz, seq, g, wg, wv, cw, cb, wd, gf, final):
    tm = PROJ_TILE
    nt = seq // tm
    n_ff, _, tf = wg.shape
    row = lambda b, t: (b * nt + t, 0)
    c2 = lambda b, t: (0, 0)
    c3 = lambda b, t: (0, 0, 0)
    once = pl.Buffered(1)
    return pl.pallas_call(
        functools.partial(_ffn_kernel, final=final),
        grid=(bsz, nt),
        in_specs=[
            pl.BlockSpec((tm, D_MODEL), row),
            pl.BlockSpec((1, D_MODEL), c2),
            pl.BlockSpec((n_ff, D_MODEL, tf), c3, pipeline_mode=once),
            pl.BlockSpec((n_ff, D_MODEL, tf), c3, pipeline_mode=once),
            pl.BlockSpec((n_ff, 3, tf), c3),
            pl.BlockSpec((n_ff, 1, tf), c3),
            pl.BlockSpec((n_ff, tf, D_MODEL), c3, pipeline_mode=once),
            pl.BlockSpec((1, D_MODEL), c2),
        ],
        out_specs=pl.BlockSpec((tm, D_MODEL), row),
        out_shape=jax.ShapeDtypeStruct((bsz * seq, D_MODEL), F32),
        scratch_shapes=[pltpu.VMEM((n_ff, 8, tf), F32)],
        compiler_params=_params("parallel", "arbitrary"),
        name="conv_ffn",
    )(x2d, g, wg, wv, cw, cb, wd, gf)


def _rope_perm():
    idx = np.empty((DIFF_W,), np.int32)
    half = DIFF_D // 2
    for h in range(DIFF_H):
        for c in range(2):
            for d in range(DIFF_D):
                idx[h * LANES + (d // half) * DIFF_D + c * half + d % half] = h * LANES + c * DIFF_D + d
    return idx


def _layer(x2d, pos2d, mem2d, bsz, seq, lambda_init, p):
    w_in = p["w_in"]
    pad = jnp.zeros((D_MODEL, LORA_PAD - (N_SHIFT - 3 * RWKV_W)), F32)
    w_r = jnp.concatenate([w_in[:, :N_SHIFT], pad], axis=1).astype(BF16)
    perm = _rope_perm()
    w_d = jnp.concatenate([w_in[:, N_SHIFT:N_SHIFT + DIFF_W][:, perm],
                           w_in[:, N_SHIFT + DIFF_W:N_SHIFT + 2 * DIFF_W][:, perm],
                           w_in[:, N_SHIFT + 2 * DIFF_W:]], axis=1).astype(BF16)
    mix = jnp.concatenate([p["shift_mix"], jnp.zeros((LORA_PAD - (N_SHIFT - 3 * RWKV_W),), F32)])[None, :]
    w_lora = jnp.zeros((LORA_PAD, 3 * RWKV_W), F32)
    w_lora = w_lora.at[0:R_DECAY, 0:RWKV_W].set(p["w_lora_up"])
    w_lora = w_lora.at[R_DECAY:R_DECAY + R_AAA, RWKV_W:2 * RWKV_W].set(p["a_lora_up"])
    w_lora = w_lora.at[R_DECAY + R_AAA:R_DECAY + R_AAA + R_GATE, 2 * RWKV_W:].set(p["g_lora_up"])
    w_lora = w_lora.astype(BF16)
    v512 = lambda a: a.reshape(1, RWKV_W)

    zr, q, k, v = _inproj(x2d, pos2d, p["norm_mix"][None, :], w_r, w_d)
    ya = _rwkv2(zr, bsz, seq, mix, w_lora, v512(p["w0"]), v512(p["a0"]), v512(p["k_k"]),
               v512(p["k_a"]), v512(p["r_k"]), v512(p["lnx_gain"]), v512(p["lnx_bias"]))
    yb = _diff_attn2(q, k, v, bsz, seq, p["lam_q1"][None, :], p["lam_k1"][None, :],
                    p["lam_q2"][None, :], p["lam_k2"][None, :], p["subln_gain"][None, :], lambda_init)
    kv = _memkv(mem2d, p["norm_mem"][None, :], p["wkv_c"].astype(BF16))
    x2 = _cross(x2d, ya, yb, kv, bsz, seq, p["w_out"].astype(BF16), p["norm_cross"][None, :],
                p["wq_c"].astype(BF16), p["wo_c"].astype(BF16))
    n_ff = D_FF // FF_TILE
    w_up = p["w_up"]
    wg = w_up[:, :D_FF].reshape(D_MODEL, n_ff, FF_TILE).transpose(1, 0, 2).astype(BF16)
    wv = w_up[:, D_FF:].reshape(D_MODEL, n_ff, FF_TILE).transpose(1, 0, 2).astype(BF16)
    cw = p["conv_w"].reshape(3, n_ff, FF_TILE).transpose(1, 0, 2)
    cb = p["conv_b"].reshape(n_ff, 1, FF_TILE)
    wd = p["w_down"].reshape(n_ff, FF_TILE, D_MODEL).astype(BF16)
    return x2, (p["norm_ffn"][None, :], wg, wv, cw, cb, wd)


def kernel(x, mem, positions, norm_mix, w_in, shift_mix, w0, w_lora_up, a0, a_lora_up, g_lora_up, k_k, k_a, r_k, lnx_gain, lnx_bias, lam_q1, lam_k1, lam_q2, lam_k2, subln_gain, w_out, norm_cross, norm_mem, wq_c, wkv_c, wo_c, norm_ffn, w_up, conv_w, conv_b, w_down, norm_final):
    bsz, seq, _ = x.shape
    depth = norm_mix.shape[0]
    x2d = x.reshape(bsz * seq, D_MODEL)
    pos2d = positions.reshape(bsz * seq, 1)
    mem2d = mem.reshape(bsz * mem.shape[1], D_MODEL)
    stacked = dict(norm_mix=norm_mix, w_in=w_in, shift_mix=shift_mix, w0=w0, w_lora_up=w_lora_up,
                   a0=a0, a_lora_up=a_lora_up, g_lora_up=g_lora_up, k_k=k_k, k_a=k_a,
                   r_k=r_k.reshape(depth, RWKV_W), lnx_gain=lnx_gain, lnx_bias=lnx_bias,
                   lam_q1=lam_q1, lam_k1=lam_k1, lam_q2=lam_q2, lam_k2=lam_k2,
                   subln_gain=subln_gain, w_out=w_out, norm_cross=norm_cross, norm_mem=norm_mem,
                   wq_c=wq_c, wkv_c=wkv_c, wo_c=wo_c, norm_ffn=norm_ffn, w_up=w_up,
                   conv_w=conv_w, conv_b=conv_b, w_down=w_down)
    for l in range(depth):
        p = {name: a[l] for name, a in stacked.items()}
        lambda_init = 0.8 - 0.6 * math.exp(-0.3 * l)
        x2, (gn, wg, wv, cw, cb, wd) = _layer(x2d, pos2d, mem2d, bsz, seq, lambda_init, p)
        x2d = _ffn(x2, bsz, seq, gn, wg, wv, cw, cb, wd, norm_final[None, :], l == depth - 1)
    return x2d.reshape(bsz, seq, D_MODEL)
```

```python
import functools
import math

import numpy as np
import jax
import jax.numpy as jnp
from jax import lax
from jax.experimental import pallas as pl
from jax.experimental.pallas import tpu as pltpu

F32 = jnp.float32
BF16 = jnp.bfloat16

D_MODEL = 1024
RWKV_W = 512
RWKV_N = 64
R_DECAY = 32
R_AAA = 32
R_GATE = 96
DIFF_W = 512
DIFF_H = 4
DIFF_D = 64
N_SHIFT = 3 * RWKV_W + R_DECAY + R_AAA + R_GATE
MEM_LEN = 256
CROSS_H = 4
CROSS_D = D_MODEL // CROSS_H
D_FF = 2816
ROPE_THETA = 10000.0
NORM_EPS = 1e-6
LNX_EPS = 64e-5
SUBLN_EPS = 1e-5

LANES = 128
LORA_PAD = 256
ZR_W = 3 * RWKV_W + LORA_PAD
VMEM_LIMIT = 56 * 1024 * 1024

ROW_TILE = 1024
PROJ_TILE = 1024
CHUNK = 64
ATT_TQ = 2048
ATT_TK = 512
FF_TILE = 256


def _mm(a, b):
    return jnp.dot(a.astype(BF16), b.astype(BF16), preferred_element_type=F32)


def _mm_nt(a, b):
    return lax.dot_general(a.astype(BF16), b.astype(BF16), (((1,), (1,)), ((), ())),
                           preferred_element_type=F32)


def _mm_tn(a, b):
    return lax.dot_general(a.astype(BF16), b.astype(BF16), (((0,), (0,)), ((), ())),
                           preferred_element_type=F32)


def _sigmoid(x):
    return 1.0 / (1.0 + jnp.exp(-x))


def _rms(x, g, eps):
    return x * lax.rsqrt(jnp.mean(x * x, axis=-1, keepdims=True) + eps) * g


def _split_dot(x, ones_b):
    hi = x.astype(BF16)
    lo = (x - hi.astype(F32)).astype(BF16)
    return (jnp.dot(hi, ones_b, preferred_element_type=F32)
            + jnp.dot(lo, ones_b, preferred_element_type=F32))


def _run_interleaved(*stage_gens):
    live = list(stage_gens)
    while live:
        for g in list(live):
            try:
                next(g)
            except StopIteration:
                live.remove(g)


def _params(*sem):
    return pltpu.CompilerParams(dimension_semantics=sem, vmem_limit_bytes=VMEM_LIMIT)


def _inproj_kernel(x_ref, pos_ref, g_ref, wr_ref, wd_ref, zr_ref, q_ref, k_ref, v_ref):
    lane = lax.broadcasted_iota(jnp.int32, (1, LANES), 1)
    freq = (lane % (DIFF_D // 2)).astype(F32)
    inv = jnp.exp(freq * (-2.0 / DIFF_D * math.log(ROPE_THETA)))
    ang = pos_ref[...].astype(F32) * inv
    cos = jnp.cos(ang)
    sin = jnp.where(lane < LANES // 2, -jnp.sin(ang), jnp.sin(ang))
    scale = DIFF_D ** -0.5 * math.log2(math.e)
    cos_q = cos * scale
    sin_q = sin * scale

    def rope(xb, c, s):
        return xb * c + pltpu.roll(xb, LANES // 2, axis=1) * s

    h = _rms(x_ref[...], g_ref[...], NORM_EPS).astype(BF16)
    for j in range(ZR_W // 256):
        sl = slice(j * 256, (j + 1) * 256)
        zr_ref[:, sl] = jnp.dot(h, wr_ref[:, sl], preferred_element_type=F32)

    for j in range(DIFF_W // 256):
        zq = jnp.dot(h, wd_ref[:, j * 256:(j + 1) * 256], preferred_element_type=F32)
        zk = jnp.dot(h, wd_ref[:, DIFF_W + j * 256:DIFF_W + (j + 1) * 256],
                     preferred_element_type=F32)
        zv = jnp.dot(h, wd_ref[:, 2 * DIFF_W + j * 256:2 * DIFF_W + (j + 1) * 256],
                     preferred_element_type=F32)
        for u in range(2):
            c0 = j * 256 + u * LANES
            q_ref[:, c0:c0 + LANES] = rope(zq[:, u * LANES:(u + 1) * LANES], cos_q, sin_q).astype(BF16)
            k_ref[:, c0:c0 + LANES] = rope(zk[:, u * LANES:(u + 1) * LANES], cos, sin).astype(BF16)
        v_ref[:, j * 256:(j + 1) * 256] = zv.astype(BF16)


def _inproj(x2d, pos2d, g, w_r, w_d):
    m = x2d.shape[0]
    tm = PROJ_TILE
    row = lambda i: (i, 0)
    const = lambda i: (0, 0)
    return pl.pallas_call(
        _inproj_kernel,
        grid=(m // tm,),
        in_specs=[
            pl.BlockSpec((tm, D_MODEL), row),
            pl.BlockSpec((tm, 1), row),
            pl.BlockSpec((1, D_MODEL), const),
            pl.BlockSpec((D_MODEL, ZR_W), const, pipeline_mode=pl.Buffered(1)),
            pl.BlockSpec((D_MODEL, 3 * DIFF_W), const, pipeline_mode=pl.Buffered(1)),
        ],
        out_specs=[
            pl.BlockSpec((tm, ZR_W), row),
            pl.BlockSpec((tm, DIFF_W), row),
            pl.BlockSpec((tm, DIFF_W), row),
            pl.BlockSpec((tm, DIFF_W), row),
        ],
        out_shape=[
            jax.ShapeDtypeStruct((m, ZR_W), F32),
            jax.ShapeDtypeStruct((m, DIFF_W), BF16),
            jax.ShapeDtypeStruct((m, DIFF_W), BF16),
            jax.ShapeDtypeStruct((m, DIFF_W), BF16),
        ],
        compiler_params=_params("parallel"),
        name="inproj",
    )(x2d, pos2d, g, w_r, w_d)


def _rwkv_kernel(z_ref, mix_ref, wl_ref, w0_ref, a0_ref, kk_ref, ka_ref, rk_ref, gain_ref,
                 bias_ref, y_ref, carry_s, st_s, r_s, lw_s, k_s, v_s, al_s, be_s, g_s, bo_s, y_s):
    tb = z_ref.shape[0]
    n_pair = RWKV_W // LANES

    @pl.when(pl.program_id(1) == 0)
    def _():
        carry_s[...] = jnp.zeros_like(carry_s)
        st_s[...] = jnp.zeros_like(st_s)

    row = lax.broadcasted_iota(jnp.int32, (tb, 1), 0)

    def shifted(c0, c1):
        zc = z_ref[:, c0:c1]
        zp = jnp.where(row == 0, carry_s[7:8, c0:c1], pltpu.roll(zc, 1, axis=0))
        return zc + (zp - zc) * mix_ref[:, c0:c1]

    ri = lax.broadcasted_iota(jnp.int32, (LANES, LANES), 0)
    ci = lax.broadcasted_iota(jnp.int32, (LANES, LANES), 1)
    same_head = (ri // RWKV_N) == (ci // RWKV_N)
    ones_head = jnp.where(same_head, 1.0, 0.0).astype(BF16)

    def headsum(x):
        return jnp.concatenate(
            [_split_dot(x[:, p * LANES:(p + 1) * LANES], ones_head) for p in range(n_pair)], axis=1)

    zl = shifted(3 * RWKV_W, ZR_W)
    ll = lax.broadcasted_iota(jnp.int32, (1, LORA_PAD), 1)
    act = jnp.where(ll < R_DECAY, jnp.tanh(zl),
                    jnp.where(ll < R_DECAY + R_AAA, zl, _sigmoid(zl)))
    lo = jnp.dot(act.astype(BF16), wl_ref[...], preferred_element_type=F32)
    lw_s[...] = -math.exp(-0.5) * _sigmoid(w0_ref[...] + lo[:, 0:RWKV_W])
    a = _sigmoid(a0_ref[...] + lo[:, RWKV_W:2 * RWKV_W])
    g_s[...] = lo[:, 2 * RWKV_W:3 * RWKV_W]

    r = shifted(0, RWKV_W)
    k = shifted(RWKV_W, 2 * RWKV_W)
    v = shifted(2 * RWKV_W, 3 * RWKV_W)
    carry_s[...] = z_ref[tb - 8:tb, :]
    r_s[...] = r
    v_s[...] = v
    kk = k * kk_ref[...]
    ss = headsum(kk * kk)
    alpha = kk * lax.rsqrt(jnp.maximum(ss, 1e-24))
    al_s[...] = alpha
    be_s[...] = alpha * a
    k2 = k * (1.0 + (a - 1.0) * ka_ref[...])
    k_s[...] = k2
    bo_s[...] = headsum(r * k2 * rk_ref[...]) * v

    c_len = CHUNK
    tri = jnp.where(lax.broadcasted_iota(jnp.int32, (c_len, c_len), 0)
                    >= lax.broadcasted_iota(jnp.int32, (c_len, c_len), 1), 1.0, 0.0).astype(BF16)
    lane = lax.broadcasted_iota(jnp.int32, (1, LANES), 1)
    head0 = lane < RWKV_N
    strict = same_head & (ri > ci)
    incl = same_head & (ri >= ci)
    b16 = (ri // 16) == (ci // 16)
    b32 = (ri // 32) == (ci // 32)
    eye = jnp.where(ri == ci, 1.0, 0.0).astype(F32)

    def stack2(x):
        return jnp.concatenate([jnp.where(head0, x, 0.0), jnp.where(head0, 0.0, x)], axis=0)

    def fold(x):
        return x[:c_len] + x[c_len:]

    def chunk_body(c, carry):
        r0 = pl.multiple_of(c * c_len, c_len)
        for p in range(n_pair):
            sl = (pl.ds(r0, c_len), slice(p * LANES, (p + 1) * LANES))
            r_c, lw, k_c, v_c, al, be = r_s[sl], lw_s[sl], k_s[sl], v_s[sl], al_s[sl], be_s[sl]
            hi = lw.astype(BF16)
            r1 = lw - hi.astype(F32)
            mid = r1.astype(BF16)
            low = (r1 - mid.astype(F32)).astype(BF16)
            cum = (jnp.dot(tri, hi, preferred_element_type=F32)
                   + jnp.dot(tri, mid, preferred_element_type=F32)
                   + jnp.dot(tri, low, preferred_element_type=F32))
            tot = cum[c_len - 1:c_len, :]
            e_neg = jnp.exp(-cum)
            e_end = jnp.exp(tot - cum)
            a_t = al * jnp.exp(cum - lw)
            r_t = r_c * jnp.exp(cum)
            a_st, r_st, v_st = stack2(a_t), stack2(r_t), stack2(v_c)
            b_n, k_n = be * e_neg, k_c * e_neg
            gram = _mm_nt(jnp.concatenate([a_st, r_st], axis=0),
                          jnp.concatenate([b_n, b_n, k_n, k_n], axis=0))
            a_ab = jnp.where(strict, gram[:LANES, :LANES], 0.0)
            a_ak = jnp.where(strict, gram[:LANES, LANES:], 0.0)
            a_rb = jnp.where(incl, gram[LANES:, :LANES], 0.0)
            a_rk = jnp.where(incl, gram[LANES:, LANES:], 0.0)
            a0 = jnp.where(b16, a_ab, 0.0)
            e1 = jnp.where(b32 & jnp.logical_not(b16), a_ab, 0.0)
            e2 = jnp.where(b32, 0.0, a_ab)
            p2 = _mm(a0, a0)
            p4 = _mm(p2, p2)
            p8 = _mm(p4, p4)
            t = eye - a0
            t = t + _mm(t, p2)
            t = t + _mm(t, p4)
            t = t + _mm(t, p8)
            t = t - _mm(_mm(t, e1), t)
            t = t - _mm(_mm(t, e2), t)
            akv = _mm(a_ak, v_st)
            w = _mm(t, jnp.concatenate([a_st, akv], axis=1))
            rbw = _mm(a_rb, w)
            rkv = _mm(a_rk, v_st)
            r_hat = fold(r_st - rbw[:, :LANES])
            y0 = fold(rkv - rbw[:, LANES:])
            ta_tv = jnp.concatenate([fold(w[:, :LANES]), fold(w[:, LANES:])], axis=1)
            bw = _mm_tn(be * e_end, ta_tv)
            kv = _mm_tn(k_c * e_end, v_c)
            m_p = jnp.where(same_head, bw[:, :LANES], 0.0)
            n_p = jnp.where(same_head, kv - bw[:, LANES:], 0.0)
            decay_col = jnp.exp(jnp.broadcast_to(tot, (LANES, LANES)).T)
            st = st_s[p]
            zz = _mm(jnp.concatenate([m_p, r_hat], axis=0), st)
            y_s[sl] = zz[LANES:] + y0
            st_s[p] = decay_col * st - zz[:LANES] + n_p
        return carry

    lax.fori_loop(0, tb // c_len, chunk_body, 0)

    y = y_s[...]
    mu = headsum(y) * (1.0 / RWKV_N)
    d = y - mu
    var = headsum(d * d) * (1.0 / RWKV_N)
    yn = d * lax.rsqrt(var + LNX_EPS) * gain_ref[...] + bias_ref[...]
    y_ref[...] = ((yn + bo_s[...]) * g_s[...]).astype(y_ref.dtype)


def _rwkv(zr, bsz, seq, mix, w_lora, w0, a0, k_k, k_a, r_k, gain, bias):
    tb = ROW_TILE
    nt = seq // tb
    row = lambda b, t: (b * nt + t, 0)
    const = lambda b, t: (0, 0)
    vec = pl.BlockSpec((1, RWKV_W), const)
    big = pltpu.VMEM((tb, RWKV_W), F32)
    return pl.pallas_call(
        _rwkv_kernel,
        grid=(bsz, nt),
        in_specs=[
            pl.BlockSpec((tb, ZR_W), row),
            pl.BlockSpec((1, ZR_W), const),
            pl.BlockSpec((LORA_PAD, 3 * RWKV_W), const),
            vec, vec, vec, vec, vec, vec, vec,
        ],
        out_specs=pl.BlockSpec((tb, RWKV_W), row),
        out_shape=jax.ShapeDtypeStruct((bsz * seq, RWKV_W), BF16),
        scratch_shapes=[
            pltpu.VMEM((8, ZR_W), F32),
            pltpu.VMEM((RWKV_W // LANES, LANES, LANES), F32),
            big, big, big, big, big, big, big, big, big,
        ],
        compiler_params=_params("parallel", "arbitrary"),
        name="rwkv7",
    )(zr, mix, w_lora, w0, a0, k_k, k_a, r_k, gain, bias)


def _rwkv2_kernel(z_ref, mix_ref, wl_ref, w0_ref, a0_ref, kk_ref, ka_ref, rk_ref, gain_ref,
                  bias_ref, y_ref, carry_s, st_s, g_s, bo_s, y_s, rt_s, dec_s,
                  ah0_s, ah1_s, bn0_s, bn1_s, kn0_s, kn1_s, vh0_s, vh1_s, be_s, ke_s,
                  mr_s, n_s):
    tb = z_ref.shape[0]
    n_pair = RWKV_W // LANES
    c_len = CHUNK
    n_chunk = tb // c_len

    @pl.when(pl.program_id(1) == 0)
    def _():
        carry_s[...] = jnp.zeros_like(carry_s)
        st_s[...] = jnp.zeros_like(st_s)

    group_chunks = 4
    group_rows = group_chunks * c_len
    n_piece = tb // group_rows
    rowq = lax.broadcasted_iota(jnp.int32, (group_rows, 1), 0)
    ri = lax.broadcasted_iota(jnp.int32, (LANES, LANES), 0)
    ci = lax.broadcasted_iota(jnp.int32, (LANES, LANES), 1)
    same_head = (ri // RWKV_N) == (ci // RWKV_N)
    ones_head = jnp.where(same_head, 1.0, 0.0).astype(BF16)
    head0 = (lax.broadcasted_iota(jnp.int32, (1, RWKV_W), 1) % LANES) < RWKV_N

    def headsum(x):
        return jnp.concatenate(
            [_split_dot(x[:, p * LANES:(p + 1) * LANES], ones_head) for p in range(n_pair)], axis=1)

    def prep(r0, top):
        rows = pl.ds(r0, group_rows)

        def shifted(c0, c1):
            zc = z_ref[rows, c0:c1]
            zp = jnp.where(rowq == 0, top[7:8, c0:c1], pltpu.roll(zc, 1, axis=0))
            return zc + (zp - zc) * mix_ref[:, c0:c1]

        zl = shifted(3 * RWKV_W, ZR_W)
        ll = lax.broadcasted_iota(jnp.int32, (1, LORA_PAD), 1)
        act = jnp.where(ll < R_DECAY, jnp.tanh(zl),
                        jnp.where(ll < R_DECAY + R_AAA, zl, _sigmoid(zl)))
        lo = jnp.dot(act.astype(BF16), wl_ref[...], preferred_element_type=F32)
        yield
        lw = -math.exp(-0.5) * _sigmoid(w0_ref[...] + lo[:, 0:RWKV_W])
        a = _sigmoid(a0_ref[...] + lo[:, RWKV_W:2 * RWKV_W])
        g_s[rows, :] = lo[:, 2 * RWKV_W:3 * RWKV_W]
        yield

        r = shifted(0, RWKV_W)
        yield
        k = shifted(RWKV_W, 2 * RWKV_W)
        yield
        v = shifted(2 * RWKV_W, 3 * RWKV_W)
        yield
        kk = k * kk_ref[...]
        ss = headsum(kk * kk)
        alpha = kk * lax.rsqrt(jnp.maximum(ss, 1e-24))
        yield
        beta = alpha * a
        k2 = k * (1.0 + (a - 1.0) * ka_ref[...])
        bo_s[rows, :] = headsum(r * k2 * rk_ref[...]) * v
        yield

        rin = rowq % c_len
        cum = lw
        for s in (1, 2, 4, 8, 16, 32):
            cum = cum + jnp.where(rin >= s, pltpu.roll(cum, s, axis=0), 0.0)
            yield
        tot = jnp.broadcast_to(cum.reshape(group_chunks, c_len, RWKV_W)[:, c_len - 1:c_len, :],
                               (group_chunks, c_len, RWKV_W)).reshape(group_rows, RWKV_W)
        e_neg = jnp.exp(-cum)
        e_end = jnp.exp(tot - cum)
        yield
        a_t = alpha * jnp.exp(cum - lw)
        r_t = r * jnp.exp(cum)
        rt_s[rows, :] = r_t
        dec_s[rows, :] = jnp.exp(tot)
        yield
        ah0_s[rows, :] = jnp.where(head0, a_t, 0.0).astype(BF16)
        ah1_s[rows, :] = jnp.where(head0, 0.0, a_t).astype(BF16)
        yield
        vh0_s[rows, :] = jnp.where(head0, v, 0.0).astype(BF16)
        vh1_s[rows, :] = jnp.where(head0, 0.0, v).astype(BF16)
        yield
        b_n = (beta * e_neg).astype(BF16)
        bn0_s[rows, :] = jnp.where(head0, b_n, jnp.zeros_like(b_n))
        bn1_s[rows, :] = jnp.where(head0, jnp.zeros_like(b_n), b_n)
        yield
        k_n = (k2 * e_neg).astype(BF16)
        kn0_s[rows, :] = jnp.where(head0, k_n, jnp.zeros_like(k_n))
        kn1_s[rows, :] = jnp.where(head0, jnp.zeros_like(k_n), k_n)
        yield
        be_s[rows, :] = (beta * e_end).astype(BF16)
        ke_s[rows, :] = (k2 * e_end).astype(BF16)

    ti = lax.broadcasted_iota(jnp.int32, (c_len, LANES), 0)
    si = lax.broadcasted_iota(jnp.int32, (c_len, LANES), 1) % c_len
    strict = si < ti
    incl = si <= ti
    b16 = (ti // 16) == (si // 16)
    b32 = (ti // 32) == (si // 32)
    eye = jnp.where(ti == si, 1.0, 0.0).astype(F32)
    lane_h0 = lax.broadcasted_iota(jnp.int32, (1, LANES), 1) < RWKV_N
    group = group_chunks * n_pair
    nb = range(group)

    def stack(x):
        xb = x.astype(BF16)
        zero = jnp.zeros_like(xb)
        return jnp.concatenate([jnp.where(lane_h0, xb, zero), jnp.where(lane_h0, zero, xb)], axis=0)

    def mmb(xs, ys):
        return [jnp.dot(x.astype(BF16), stack(y), preferred_element_type=F32) for x, y in zip(xs, ys)]

    def solve(it):
        idx = []
        for u in range(group):
            c = it * group_chunks + u // n_pair
            p = u % n_pair
            idx.append((c, c * n_pair + p,
                        (pl.ds(pl.multiple_of(c * c_len, c_len), c_len), slice(p * LANES, (p + 1) * LANES))))
        sls = [sl for _, _, sl in idx]
        a_st = [jnp.concatenate([ah0_s[sl], ah1_s[sl]], axis=0) for sl in sls]
        lhs = [jnp.concatenate([x[:c_len] + x[c_len:], rt_s[sl].astype(BF16)], axis=0)
               for x, sl in zip(a_st, sls)]
        rhs = [jnp.concatenate([bn0_s[sl], bn1_s[sl], kn0_s[sl], kn1_s[sl]], axis=0) for sl in sls]
        gram = [lax.dot_general(x, y, (((1,), (1,)), ((), ())), preferred_element_type=F32)
                for x, y in zip(lhs, rhs)]
        a_ab = [jnp.where(strict, gm[:c_len, :LANES], 0.0) for gm in gram]
        a_kr = [jnp.concatenate([jnp.where(strict, gm[:c_len, LANES:], 0.0),
                                 jnp.where(incl, gm[c_len:, LANES:], 0.0)], axis=0).astype(BF16)
                for gm in gram]
        a_rb = [jnp.where(incl, gm[c_len:, :LANES], 0.0) for gm in gram]
        yield
        a0 = [jnp.where(b16, x, 0.0).astype(BF16) for x in a_ab]
        e1 = [jnp.where(b32 & jnp.logical_not(b16), x, 0.0).astype(BF16) for x in a_ab]
        e2 = [jnp.where(b32, 0.0, x).astype(BF16) for x in a_ab]
        p2 = mmb(a0, a0)
        yield
        p4 = mmb(p2, p2)
        yield
        p8 = mmb(p4, p4)
        t = [eye - x.astype(F32) for x in a0]
        t = [x + y for x, y in zip(t, mmb(t, p2))]
        yield
        t = [x + y for x, y in zip(t, mmb(t, p4))]
        yield
        t = [x + y for x, y in zip(t, mmb(t, p8))]
        yield
        te = mmb(t, e1)
        yield
        t = [x - y for x, y in zip(t, mmb(te, t))]
        yield
        te = mmb(t, e2)
        yield
        t = [x - y for x, y in zip(t, mmb(te, t))]
        yield
        v_st = [jnp.concatenate([vh0_s[sl], vh1_s[sl]], axis=0) for sl in sls]
        av = [jnp.dot(x, y, preferred_element_type=F32) for x, y in zip(a_kr, v_st)]
        yield
        w = [jnp.dot(x.astype(BF16), jnp.concatenate([y, stack(z[:c_len])], axis=1), preferred_element_type=F32)
             for x, y, z in zip(t, a_st, av)]
        yield
        rbw = [jnp.dot(x.astype(BF16), jnp.concatenate([stack(y[:, :LANES]), stack(y[:, LANES:])], axis=1),
                       preferred_element_type=F32) for x, y in zip(a_rb, w)]
        yield
        bw = [_mm_tn(be_s[sl], x) for sl, x in zip(sls, w)]
        yield
        kv = [_mm_tn(ke_s[sl], vh0_s[sl] + vh1_s[sl]) for sl in sls]
        yield
        for u in nb:
            c, j, sl = idx[u]
            r_hat = rt_s[sl] - rbw[u][:, :LANES]
            m_p = jnp.where(same_head, bw[u][:, :LANES], 0.0)
            mr_s[j] = jnp.concatenate([m_p, r_hat], axis=0).astype(BF16)
            y_s[sl] = av[u][c_len:] - rbw[u][:, LANES:]
            n_s[j] = jnp.where(same_head, kv[u] - bw[u][:, LANES:], 0.0)

    _run_interleaved(prep(0, carry_s[...]))
    carry_s[...] = z_ref[tb - 8:tb, :]

    def piece_body(it, carry):
        r0 = pl.multiple_of((it + 1) * group_rows, group_rows)
        _run_interleaved(solve(it), prep(r0, z_ref[pl.ds(r0 - 8, 8), :]))
        return carry

    lax.fori_loop(0, n_piece - 1, piece_body, 0)
    _run_interleaved(solve(n_piece - 1))

    def scan_body(c, carry):
        rows = pl.ds(pl.multiple_of(c * c_len, c_len), c_len)
        sts = [st_s[p] for p in range(n_pair)]
        zz = [jnp.dot(mr_s[c * n_pair + p], sts[p].astype(BF16), preferred_element_type=F32)
              for p in range(n_pair)]
        for p in range(n_pair):
            j = c * n_pair + p
            sl = (rows, slice(p * LANES, (p + 1) * LANES))
            y_s[sl] = y_s[sl] + zz[p][LANES:]
            d = dec_s[sl]
            decay_col = jnp.concatenate([d, d], axis=0).T
            st_s[p] = decay_col * sts[p] - zz[p][:LANES] + n_s[j]
        return carry

    lax.fori_loop(0, n_chunk, scan_body, 0)

    y = y_s[...]
    mu = headsum(y) * (1.0 / RWKV_N)
    d = y - mu
    var = headsum(d * d) * (1.0 / RWKV_N)
    yn = d * lax.rsqrt(var + LNX_EPS) * gain_ref[...] + bias_ref[...]
    y_ref[...] = ((yn + bo_s[...]) * g_s[...]).astype(y_ref.dtype)


def _rwkv2(zr, bsz, seq, mix, w_lora, w0, a0, k_k, k_a, r_k, gain, bias):
    tb = ROW_TILE
    nt = seq // tb
    n_prob = (tb // CHUNK) * (RWKV_W // LANES)
    row = lambda b, t: (b * nt + t, 0)
    const = lambda b, t: (0, 0)
    vec = pl.BlockSpec((1, RWKV_W), const)
    big = pltpu.VMEM((tb, RWKV_W), F32)
    half = pltpu.VMEM((tb, RWKV_W), BF16)
    return pl.pallas_call(
        _rwkv2_kernel,
        grid=(bsz, nt),
        in_specs=[
            pl.BlockSpec((tb, ZR_W), row),
            pl.BlockSpec((1, ZR_W), const),
            pl.BlockSpec((LORA_PAD, 3 * RWKV_W), const),
            vec, vec, vec, vec, vec, vec, vec,
        ],
        out_specs=pl.BlockSpec((tb, RWKV_W), row),
        out_shape=jax.ShapeDtypeStruct((bsz * seq, RWKV_W), BF16),
        scratch_shapes=[
            pltpu.VMEM((8, ZR_W), F32),
            pltpu.VMEM((RWKV_W // LANES, LANES, LANES), F32),
            big, big, big, big, big,
            half, half, half, half, half, half, half, half, half, half,
            pltpu.VMEM((n_prob, LANES + CHUNK, LANES), BF16),
            pltpu.VMEM((n_prob, LANES, LANES), F32),
        ],
        compiler_params=_params("parallel", "arbitrary"),
        name="rwkv7",
    )(zr, mix, w_lora, w0, a0, k_k, k_a, r_k, gain, bias)


def _attn_kernel(qi_tab, ki_tab, q_ref, k_ref, v_ref, lq1_ref, lk1_ref, lq2_ref, lk2_ref, sg_ref,
                 o_ref, q_s, m_s, acc_s, *, lambda_init, ratio):
    tq, tk = q_ref.shape[0], k_ref.shape[0]
    n_col = tk // LANES
    pidx = pl.program_id(1)
    qi = qi_tab[pidx]
    ki = ki_tab[pidx]
    nt = (((1,), (1,)), ((), ()))

    @pl.when(ki == 0)
    def _():
        map0 = lax.broadcasted_iota(jnp.int32, (1, LANES), 1) < DIFF_D
        for h in range(DIFF_H):
            q = q_ref[:, h * LANES:(h + 1) * LANES]
            zero = jnp.zeros_like(q)
            q_s[2 * h] = jnp.where(map0, q, zero)
            q_s[2 * h + 1] = jnp.where(map0, zero, q)
        m_s[...] = jnp.full_like(m_s, -jnp.inf)
        acc_s[...] = jnp.zeros_like(acc_s)

    def step(masked):
        if masked:
            rowp = qi * tq + lax.broadcasted_iota(jnp.int32, (tq, LANES), 0)
            colp = ki * tk + lax.broadcasted_iota(jnp.int32, (tq, LANES), 1)
            keep = [colp + j * LANES <= rowp for j in range(n_col)]
        ones = jnp.ones((tk, LANES), BF16)
        for h in range(DIFF_H):
            k = k_ref[:, h * LANES:(h + 1) * LANES]
            v_ext = jnp.concatenate([v_ref[:, h * LANES:(h + 1) * LANES], ones], axis=1)
            for c in range(2):
                i = 2 * h + c
                s = lax.dot_general(q_s[i], k, nt, preferred_element_type=F32)
                cols = [s[:, j * LANES:(j + 1) * LANES] for j in range(n_col)]
                if masked:
                    cols = [jnp.where(keep[j], cols[j], -jnp.inf) for j in range(n_col)]
                mx = cols[0]
                for j in range(1, n_col):
                    mx = jnp.maximum(mx, cols[j])
                m_old = m_s[i]
                m_new = jnp.maximum(m_old, jnp.max(mx, axis=-1, keepdims=True))
                corr = jnp.exp2(m_old - m_new)
                p = jnp.concatenate([jnp.exp2(cj - m_new).astype(BF16) for cj in cols], axis=1)
                pv = jnp.dot(p, v_ext, preferred_element_type=F32)
                acc_s[i] = jnp.concatenate([corr, corr], axis=1) * acc_s[i] + pv
                m_s[i] = m_new

    on_diag = ki >= qi * ratio

    @pl.when(on_diag)
    def _():
        step(True)

    @pl.when(jnp.logical_not(on_diag))
    def _():
        step(False)

    @pl.when(ki == (qi + 1) * ratio - 1)
    def _():
        lam = (jnp.exp(jnp.sum(lq1_ref[...] * lk1_ref[...], axis=-1, keepdims=True))
               - jnp.exp(jnp.sum(lq2_ref[...] * lk2_ref[...], axis=-1, keepdims=True))
               + lambda_init)
        for h in range(DIFF_H):
            a1 = acc_s[2 * h]
            a2 = acc_s[2 * h + 1]
            o = a1[:, :LANES] / a1[:, LANES:] - lam * (a2[:, :LANES] / a2[:, LANES:])
            o = _rms(o, sg_ref[...], SUBLN_EPS) * (1.0 - lambda_init)
            o_ref[:, h * LANES:(h + 1) * LANES] = o.astype(o_ref.dtype)


def _diff_attn(q, k, v, bsz, seq, lq1, lk1, lq2, lk2, sg, lambda_init):
    tq, tk = ATT_TQ, ATT_TK
    ratio = tq // tk
    nq, nk = seq // tq, seq // tk
    pairs = [(a, b) for a in range(nq) for b in range((a + 1) * ratio)]
    qi_tab = jnp.asarray(np.array([a for a, _ in pairs], np.int32))
    ki_tab = jnp.asarray(np.array([b for _, b in pairs], np.int32))
    qmap = lambda b, p, qt, kt: (b * nq + qt[p], 0)
    kmap = lambda b, p, qt, kt: (b * nk + kt[p], 0)
    const = lambda b, p, qt, kt: (0, 0)
    lam_spec = pl.BlockSpec((1, DIFF_D), const)
    grid_spec = pltpu.PrefetchScalarGridSpec(
        num_scalar_prefetch=2,
        grid=(bsz, len(pairs)),
        in_specs=[
            pl.BlockSpec((tq, DIFF_W), qmap),
            pl.BlockSpec((tk, DIFF_W), kmap),
            pl.BlockSpec((tk, DIFF_W), kmap),
            lam_spec, lam_spec, lam_spec, lam_spec,
            pl.BlockSpec((1, 2 * DIFF_D), const),
        ],
        out_specs=pl.BlockSpec((tq, DIFF_W), qmap),
        scratch_shapes=[
            pltpu.VMEM((2 * DIFF_H, tq, LANES), BF16),
            pltpu.VMEM((2 * DIFF_H, tq, LANES), F32),
            pltpu.VMEM((2 * DIFF_H, tq, 2 * LANES), F32),
        ],
    )
    return pl.pallas_call(
        functools.partial(_attn_kernel, lambda_init=lambda_init, ratio=ratio),
        grid_spec=grid_spec,
        out_shape=jax.ShapeDtypeStruct((bsz * seq, DIFF_W), BF16),
        compiler_params=_params("parallel", "arbitrary"),
        name="diff_attn",
    )(qi_tab, ki_tab, q, k, v, lq1, lk1, lq2, lk2, sg)


def _attn2_kernel(q_ref, k_ref, v_ref, lq1_ref, lk1_ref, lq2_ref, lk2_ref, sg_ref, o_ref,
                  q_s, m_s, acc_s, *, lambda_init, tk):
    tq = q_ref.shape[0]
    n_col = tk // LANES
    n_diag = tq // tk
    qi = pl.program_id(2)
    nt = (((1,), (1,)), ((), ()))

    map0 = (lax.broadcasted_iota(jnp.int32, (1, LANES), 1) % DIFF_D) < DIFF_D // 2
    q = q_ref[...]
    zero = jnp.zeros_like(q)
    q_s[0] = jnp.where(map0, q, zero)
    q_s[1] = jnp.where(map0, zero, q)
    m_s[...] = jnp.full_like(m_s, -jnp.inf)
    acc_s[...] = jnp.zeros_like(acc_s)
    ones = jnp.ones((tk, LANES), BF16)
    tri_keep = [lax.broadcasted_iota(jnp.int32, (tk, LANES), 1) + j * LANES
                <= lax.broadcasted_iota(jnp.int32, (tk, LANES), 0) for j in range(n_col)]

    def kv_step(kv_rows, r0, r1, masked):
        k = k_ref[kv_rows, :]
        v_ext = jnp.concatenate([v_ref[kv_rows, :], ones], axis=1)
        for c in range(2):
            s = lax.dot_general(q_s[c, r0:r1, :], k, nt, preferred_element_type=F32)
            cols = [s[:, j * LANES:(j + 1) * LANES] for j in range(n_col)]
            if masked:
                cols = [jnp.where(tri_keep[j], cols[j], -jnp.inf) for j in range(n_col)]
            mx = cols[0]
            for j in range(1, n_col):
                mx = jnp.maximum(mx, cols[j])
            m_old = m_s[c, r0:r1, :]
            m_new = jnp.maximum(m_old, jnp.max(mx, axis=-1, keepdims=True))
            corr = jnp.exp2(m_old - m_new)
            p = jnp.concatenate([jnp.exp2(cj - m_new).astype(BF16) for cj in cols], axis=1)
            pv = jnp.dot(p, v_ext, preferred_element_type=F32)
            acc_s[c, r0:r1, :] = jnp.concatenate([corr, corr], axis=1) * acc_s[c, r0:r1, :] + pv
            m_s[c, r0:r1, :] = m_new

    def full_body(j, carry):
        for u in range(n_diag):
            kv_rows = pl.ds(pl.multiple_of((j * n_diag + u) * tk, tk), tk)
            kv_step(kv_rows, 0, tq // 2, False)
            kv_step(kv_rows, tq // 2, tq, False)
        return carry

    lax.fori_loop(0, qi, full_body, 0)
    for d in range(n_diag):
        kv_rows = pl.ds(pl.multiple_of((qi * n_diag + d) * tk, tk), tk)
        kv_step(kv_rows, d * tk, (d + 1) * tk, True)
        if d + 1 < n_diag:
            kv_step(kv_rows, (d + 1) * tk, tq, False)

    lam = (jnp.exp(jnp.sum(lq1_ref[...] * lk1_ref[...], axis=-1, keepdims=True))
           - jnp.exp(jnp.sum(lq2_ref[...] * lk2_ref[...], axis=-1, keepdims=True))
           + lambda_init)
    a1 = acc_s[0]
    a2 = acc_s[1]
    o = a1[:, :LANES] / a1[:, LANES:] - lam * (a2[:, :LANES] / a2[:, LANES:])
    o_ref[...] = (_rms(o, sg_ref[...], SUBLN_EPS) * (1.0 - lambda_init)).astype(o_ref.dtype)


def _diff_attn2(q, k, v, bsz, seq, lq1, lk1, lq2, lk2, sg, lambda_init):
    tq = min(ATT_TQ, seq)
    tk = min(ATT_TK, tq)
    nq = seq // tq
    qmap = lambda b, h, i: (b * nq + i, h)
    kmap = lambda b, h, i: (b, h)
    const = lambda b, h, i: (0, 0)
    lam_spec = pl.BlockSpec((1, DIFF_D), const)
    return pl.pallas_call(
        functools.partial(_attn2_kernel, lambda_init=lambda_init, tk=tk),
        grid=(bsz, DIFF_H, nq),
        in_specs=[
            pl.BlockSpec((tq, LANES), qmap),
            pl.BlockSpec((seq, LANES), kmap),
            pl.BlockSpec((seq, LANES), kmap),
            lam_spec, lam_spec, lam_spec, lam_spec,
            pl.BlockSpec((1, 2 * DIFF_D), const),
        ],
        out_specs=pl.BlockSpec((tq, LANES), qmap),
        out_shape=jax.ShapeDtypeStruct((bsz * seq, DIFF_W), BF16),
        scratch_shapes=[
            pltpu.VMEM((2, tq, LANES), BF16),
            pltpu.VMEM((2, tq, LANES), F32),
            pltpu.VMEM((2, tq, 2 * LANES), F32),
        ],
        compiler_params=_params("parallel", "parallel", "arbitrary"),
        name="diff_attn",
    )(q, k, v, lq1, lk1, lq2, lk2, sg)


def _memkv_kernel(m_ref, g_ref, w_ref, o_ref):
    h = _rms(m_ref[...], g_ref[...], NORM_EPS).astype(BF16)
    for j in range(o_ref.shape[1] // 256):
        sl = slice(j * 256, (j + 1) * 256)
        o_ref[:, sl] = jnp.dot(h, w_ref[:, sl], preferred_element_type=F32).astype(o_ref.dtype)


def _memkv(mem2d, g, wkv):
    m = mem2d.shape[0]
    tm = MEM_LEN
    return pl.pallas_call(
        _memkv_kernel,
        grid=(m // tm,),
        in_specs=[
            pl.BlockSpec((tm, D_MODEL), lambda i: (i, 0)),
            pl.BlockSpec((1, D_MODEL), lambda i: (0, 0)),
            pl.BlockSpec((D_MODEL, 2 * D_MODEL), lambda i: (0, 0)),
        ],
        out_specs=pl.BlockSpec((tm, 2 * D_MODEL), lambda i: (i, 0)),
        out_shape=jax.ShapeDtypeStruct((m, 2 * D_MODEL), BF16),
        compiler_params=_params("parallel"),
        name="memkv",
    )(mem2d, g, wkv)


def _cross_kernel(x_ref, ya_ref, yb_ref, kv_ref, wout_ref, g_ref, wq_ref, wo_ref, o_ref, att_s):
    tm = x_ref.shape[0]
    halves = [slice(0, tm // 2), slice(tm // 2, tm)]
    nt = (((1,), (1,)), ((), ()))
    x1 = [x_ref[r, :] + (jnp.dot(ya_ref[r, :], wout_ref[0:RWKV_W, :], preferred_element_type=F32)
                         + jnp.dot(yb_ref[r, :], wout_ref[RWKV_W:, :], preferred_element_type=F32))
          for r in halves]
    hc = [_rms(x, g_ref[...], NORM_EPS).astype(BF16) for x in x1]
    scale = CROSS_D ** -0.5 * math.log2(math.e)
    q = [(jnp.dot(x, wq_ref[...], preferred_element_type=F32) * scale).astype(BF16) for x in hc]
    for h in range(CROSS_H):
        sl = slice(h * CROSS_D, (h + 1) * CROSS_D)
        kh = kv_ref[:, sl]
        vh = kv_ref[:, D_MODEL + h * CROSS_D:D_MODEL + (h + 1) * CROSS_D]
        s = [lax.dot_general(x[:, sl], kh, nt, preferred_element_type=F32) for x in q]
        p = [jnp.exp2(x - jnp.max(x, axis=-1, keepdims=True)) for x in s]
        o = [jnp.dot(x.astype(BF16), vh, preferred_element_type=F32) / jnp.sum(x, axis=-1, keepdims=True)
             for x in p]
        for r, x in zip(halves, o):
            att_s[r, sl] = x.astype(BF16)
    for r, x in zip(halves, x1):
        o_ref[r, :] = x + jnp.dot(att_s[r, :], wo_ref[...], preferred_element_type=F32)


def _cross(x2d, ya, yb, kv, bsz, seq, w_out, g, wq, wo):
    tm = PROJ_TILE
    nt = seq // tm
    row = lambda b, t: (b * nt + t, 0)
    const = lambda b, t: (0, 0)
    return pl.pallas_call(
        _cross_kernel,
        grid=(bsz, nt),
        in_specs=[
            pl.BlockSpec((tm, D_MODEL), row),
            pl.BlockSpec((tm, RWKV_W), row),
            pl.BlockSpec((tm, DIFF_W), row),
            pl.BlockSpec((MEM_LEN, 2 * D_MODEL), lambda b, t: (b, 0)),
            pl.BlockSpec((D_MODEL, D_MODEL), const, pipeline_mode=pl.Buffered(1)),
            pl.BlockSpec((1, D_MODEL), const),
            pl.BlockSpec((D_MODEL, D_MODEL), const, pipeline_mode=pl.Buffered(1)),
            pl.BlockSpec((D_MODEL, D_MODEL), const, pipeline_mode=pl.Buffered(1)),
        ],
        out_specs=pl.BlockSpec((tm, D_MODEL), row),
        out_shape=jax.ShapeDtypeStruct((bsz * seq, D_MODEL), F32),
        scratch_shapes=[pltpu.VMEM((tm, D_MODEL), BF16)],
        compiler_params=_params("parallel", "parallel"),
        name="outproj_cross",
    )(x2d, ya, yb, kv, w_out, g, wq, wo)


def _ffn_kernel(x_ref, g_ref, wg_ref, wv_ref, cw_ref, cb_ref, wd_ref, gf_ref, o_ref, carry_s, *, final):
    tm = x_ref.shape[0]
    n_ff = wg_ref.shape[0]

    @pl.when(pl.program_id(1) == 0)
    def _():
        carry_s[...] = jnp.zeros_like(carry_s)

    x = x_ref[...]
    h = _rms(x, g_ref[...], NORM_EPS).astype(BF16)

    def up(j):
        return (jnp.dot(h, wg_ref[j], preferred_element_type=F32),
                jnp.dot(h, wv_ref[j], preferred_element_type=F32))

    nxt = up(0)
    acc = None
    for j in range(n_ff):
        gate, val = nxt
        if j + 1 < n_ff:
            nxt = up(j + 1)
        ext = jnp.concatenate([carry_s[j], gate], axis=0)
        g1 = ext[7:tm + 7, :]
        g2 = ext[6:tm + 6, :]
        carry_s[j] = gate[tm - 8:tm, :]
        cwh = 0.5 * cw_ref[j]
        hc = cwh[0:1, :] * g2 + cwh[1:2, :] * g1 + cwh[2:3, :] * gate + 0.5 * cb_ref[j]
        act = (hc + hc * jnp.tanh(hc)) * val
        down = jnp.dot(act.astype(BF16), wd_ref[j], preferred_element_type=F32)
        acc = down if acc is None else acc + down
    out = x + acc
    o_ref[...] = _rms(out, gf_ref[...], NORM_EPS) if final else out


def _ffn(x2d, bsz, seq, g, wg, wv, cw, cb, wd, gf, final):
    tm = PROJ_TILE
    nt = seq // tm
    n_ff, _, tf = wg.shape
    row = lambda b, t: (b * nt + t, 0)
    c2 = lambda b, t: (0, 0)
    c3 = lambda b, t: (0, 0, 0)
    once = pl.Buffered(1)
    return pl.pallas_call(
        functools.partial(_ffn_kernel, final=final),
        grid=(bsz, nt),
        in_specs=[
            pl.BlockSpec((tm, D_MODEL), row),
            pl.BlockSpec((1, D_MODEL), c2),
            pl.BlockSpec((n_ff, D_MODEL, tf), c3, pipeline_mode=once),
            pl.BlockSpec((n_ff, D_MODEL, tf), c3, pipeline_mode=once),
            pl.BlockSpec((n_ff, 3, tf), c3),
            pl.BlockSpec((n_ff, 1, tf), c3),
            pl.BlockSpec((n_ff, tf, D_MODEL), c3, pipeline_mode=once),
            pl.BlockSpec((1, D_MODEL), c2),
        ],
        out_specs=pl.BlockSpec((tm, D_MODEL), row),
        out_shape=jax.ShapeDtypeStruct((bsz * seq, D_MODEL), F32),
        scratch_shapes=[pltpu.VMEM((n_ff, 8, tf), F32)],
        compiler_params=_params("parallel", "arbitrary"),
        name="conv_ffn",
    )(x2d, g, wg, wv, cw, cb, wd, gf)


def _rope_perm():
    idx = np.empty((DIFF_W,), np.int32)
    half = DIFF_D // 2
    for h in range(DIFF_H):
        for c in range(2):
            for d in range(DIFF_D):
                idx[h * LANES + (d // half) * DIFF_D + c * half + d % half] = h * LANES + c * DIFF_D + d
    return idx


def _layer(x2d, pos2d, mem2d, bsz, seq, lambda_init, p):
    w_in = p["w_in"]
    pad = jnp.zeros((D_MODEL, LORA_PAD - (N_SHIFT - 3 * RWKV_W)), F32)
    w_r = jnp.concatenate([w_in[:, :N_SHIFT], pad], axis=1).astype(BF16)
    perm = _rope_perm()
    w_d = jnp.concatenate([w_in[:, N_SHIFT:N_SHIFT + DIFF_W][:, perm],
                           w_in[:, N_SHIFT + DIFF_W:N_SHIFT + 2 * DIFF_W][:, perm],
                           w_in[:, N_SHIFT + 2 * DIFF_W:]], axis=1).astype(BF16)
    mix = jnp.concatenate([p["shift_mix"], jnp.zeros((LORA_PAD - (N_SHIFT - 3 * RWKV_W),), F32)])[None, :]
    w_lora = jnp.zeros((LORA_PAD, 3 * RWKV_W), F32)
    w_lora = w_lora.at[0:R_DECAY, 0:RWKV_W].set(p["w_lora_up"])
    w_lora = w_lora.at[R_DECAY:R_DECAY + R_AAA, RWKV_W:2 * RWKV_W].set(p["a_lora_up"])
    w_lora = w_lora.at[R_DECAY + R_AAA:R_DECAY + R_AAA + R_GATE, 2 * RWKV_W:].set(p["g_lora_up"])
    w_lora = w_lora.astype(BF16)
    v512 = lambda a: a.reshape(1, RWKV_W)

    zr, q, k, v = _inproj(x2d, pos2d, p["norm_mix"][None, :], w_r, w_d)
    ya = _rwkv2(zr, bsz, seq, mix, w_lora, v512(p["w0"]), v512(p["a0"]), v512(p["k_k"]),
               v512(p["k_a"]), v512(p["r_k"]), v512(p["lnx_gain"]), v512(p["lnx_bias"]))
    yb = _diff_attn2(q, k, v, bsz, seq, p["lam_q1"][None, :], p["lam_k1"][None, :],
                    p["lam_q2"][None, :], p["lam_k2"][None, :], p["subln_gain"][None, :], lambda_init)
    kv = _memkv(mem2d, p["norm_mem"][None, :], p["wkv_c"].astype(BF16))
    x2 = _cross(x2d, ya, yb, kv, bsz, seq, p["w_out"].astype(BF16), p["norm_cross"][None, :],
                p["wq_c"].astype(BF16), p["wo_c"].astype(BF16))
    n_ff = D_FF // FF_TILE
    w_up = p["w_up"]
    wg = w_up[:, :D_FF].reshape(D_MODEL, n_ff, FF_TILE).transpose(1, 0, 2).astype(BF16)
    wv = w_up[:, D_FF:].reshape(D_MODEL, n_ff, FF_TILE).transpose(1, 0, 2).astype(BF16)
    cw = p["conv_w"].reshape(3, n_ff, FF_TILE).transpose(1, 0, 2)
    cb = p["conv_b"].reshape(n_ff, 1, FF_TILE)
    wd = p["w_down"].reshape(n_ff, FF_TILE, D_MODEL).astype(BF16)
    return x2, (p["norm_ffn"][None, :], wg, wv, cw, cb, wd)


def kernel(x, mem, positions, norm_mix, w_in, shift_mix, w0, w_lora_up, a0, a_lora_up, g_lora_up, k_k, k_a, r_k, lnx_gain, lnx_bias, lam_q1, lam_k1, lam_q2, lam_k2, subln_gain, w_out, norm_cross, norm_mem, wq_c, wkv_c, wo_c, norm_ffn, w_up, conv_w, conv_b, w_down, norm_final):
    bsz, seq, _ = x.shape
    depth = norm_mix.shape[0]
    x2d = x.reshape(bsz * seq, D_MODEL)
    pos2d = positions.reshape(bsz * seq, 1)
    mem2d = mem.reshape(bsz * mem.shape[1], D_MODEL)
    stacked = dict(norm_mix=norm_mix, w_in=w_in, shift_mix=shift_mix, w0=w0, w_lora_up=w_lora_up,
                   a0=a0, a_lora_up=a_lora_up, g_lora_up=g_lora_up, k_k=k_k, k_a=k_a,
                   r_k=r_k.reshape(depth, RWKV_W), lnx_gain=lnx_gain, lnx_bias=lnx_bias,
                   lam_q1=lam_q1, lam_k1=lam_k1, lam_q2=lam_q2, lam_k2=lam_k2,
                   subln_gain=subln_gain, w_out=w_out, norm_cross=norm_cross, norm_mem=norm_mem,
                   wq_c=wq_c, wkv_c=wkv_c, wo_c=wo_c, norm_ffn=norm_ffn, w_up=w_up,
                   conv_w=conv_w, conv_b=conv_b, w_down=w_down)
    for l in range(depth):
        p = {name: a[l] for name, a in stacked.items()}
        lambda_init = 0.8 - 0.6 * math.exp(-0.3 * l)
        x2, (gn, wg, wv, cw, cb, wd) = _layer(x2d, pos2d, mem2d, bsz, seq, lambda_init, p)
        x2d = _ffn(x2, bsz, seq, gn, wg, wv, cw, cb, wd, norm_final[None, :], l == depth - 1)
    return x2d.reshape(bsz, seq, D_MODEL)
```

```python
import functools
import math

import numpy as np
import jax
import jax.numpy as jnp
from jax import lax
from jax.experimental import pallas as pl
from jax.experimental.pallas import tpu as pltpu

F32 = jnp.float32
BF16 = jnp.bfloat16

D_MODEL = 1024
RWKV_W = 512
RWKV_N = 64
R_DECAY = 32
R_AAA = 32
R_GATE = 96
DIFF_W = 512
DIFF_H = 4
DIFF_D = 64
N_SHIFT = 3 * RWKV_W + R_DECAY + R_AAA + R_GATE
MEM_LEN = 256
CROSS_H = 4
CROSS_D = D_MODEL // CROSS_H
D_FF = 2816
ROPE_THETA = 10000.0
NORM_EPS = 1e-6
LNX_EPS = 64e-5
SUBLN_EPS = 1e-5

LANES = 128
LORA_PAD = 256
ZR_W = 3 * RWKV_W + LORA_PAD
VMEM_LIMIT = 56 * 1024 * 1024

ROW_TILE = 1024
PROJ_TILE = 1024
CHUNK = 64
ATT_TQ = 2048
ATT_TK = 512
FF_TILE = 256


def _mm(a, b):
    return jnp.dot(a.astype(BF16), b.astype(BF16), preferred_element_type=F32)


def _mm_nt(a, b):
    return lax.dot_general(a.astype(BF16), b.astype(BF16), (((1,), (1,)), ((), ())),
                           preferred_element_type=F32)


def _mm_tn(a, b):
    return lax.dot_general(a.astype(BF16), b.astype(BF16), (((0,), (0,)), ((), ())),
                           preferred_element_type=F32)


def _sigmoid(x):
    return 1.0 / (1.0 + jnp.exp(-x))


def _rms(x, g, eps):
    return x * lax.rsqrt(jnp.mean(x * x, axis=-1, keepdims=True) + eps) * g


def _split_dot(x, ones_b):
    hi = x.astype(BF16)
    lo = (x - hi.astype(F32)).astype(BF16)
    return (jnp.dot(hi, ones_b, preferred_element_type=F32)
            + jnp.dot(lo, ones_b, preferred_element_type=F32))


def _run_interleaved(*stage_gens):
    live = list(stage_gens)
    while live:
        for g in list(live):
            try:
                next(g)
            except StopIteration:
                live.remove(g)


def _params(*sem):
    return pltpu.CompilerParams(dimension_semantics=sem, vmem_limit_bytes=VMEM_LIMIT)


def _inproj_kernel(x_ref, pos_ref, g_ref, wr_ref, wd_ref, zr_ref, q_ref, k_ref, v_ref):
    lane = lax.broadcasted_iota(jnp.int32, (1, LANES), 1)
    freq = (lane % (DIFF_D // 2)).astype(F32)
    inv = jnp.exp(freq * (-2.0 / DIFF_D * math.log(ROPE_THETA)))
    ang = pos_ref[...].astype(F32) * inv
    cos = jnp.cos(ang)
    sin = jnp.where(lane < LANES // 2, -jnp.sin(ang), jnp.sin(ang))
    scale = DIFF_D ** -0.5 * math.log2(math.e)
    cos_q = cos * scale
    sin_q = sin * scale

    def rope(xb, c, s):
        return xb * c + pltpu.roll(xb, LANES // 2, axis=1) * s

    h = _rms(x_ref[...], g_ref[...], NORM_EPS).astype(BF16)
    for j in range(ZR_W // 256):
        sl = slice(j * 256, (j + 1) * 256)
        zr_ref[:, sl] = jnp.dot(h, wr_ref[:, sl], preferred_element_type=F32)

    for j in range(DIFF_W // 256):
        zq = jnp.dot(h, wd_ref[:, j * 256:(j + 1) * 256], preferred_element_type=F32)
        zk = jnp.dot(h, wd_ref[:, DIFF_W + j * 256:DIFF_W + (j + 1) * 256],
                     preferred_element_type=F32)
        zv = jnp.dot(h, wd_ref[:, 2 * DIFF_W + j * 256:2 * DIFF_W + (j + 1) * 256],
                     preferred_element_type=F32)
        for u in range(2):
            c0 = j * 256 + u * LANES
            q_ref[:, c0:c0 + LANES] = rope(zq[:, u * LANES:(u + 1) * LANES], cos_q, sin_q).astype(BF16)
            k_ref[:, c0:c0 + LANES] = rope(zk[:, u * LANES:(u + 1) * LANES], cos, sin).astype(BF16)
        v_ref[:, j * 256:(j + 1) * 256] = zv.astype(BF16)


def _inproj(x2d, pos2d, g, w_r, w_d):
    m = x2d.shape[0]
    tm = PROJ_TILE
    row = lambda i: (i, 0)
    const = lambda i: (0, 0)
    return pl.pallas_call(
        _inproj_kernel,
        grid=(m // tm,),
        in_specs=[
            pl.BlockSpec((tm, D_MODEL), row),
            pl.BlockSpec((tm, 1), row),
            pl.BlockSpec((1, D_MODEL), const),
            pl.BlockSpec((D_MODEL, ZR_W), const, pipeline_mode=pl.Buffered(1)),
            pl.BlockSpec((D_MODEL, 3 * DIFF_W), const, pipeline_mode=pl.Buffered(1)),
        ],
        out_specs=[
            pl.BlockSpec((tm, ZR_W), row),
            pl.BlockSpec((tm, DIFF_W), row),
            pl.BlockSpec((tm, DIFF_W), row),
            pl.BlockSpec((tm, DIFF_W), row),
        ],
        out_shape=[
            jax.ShapeDtypeStruct((m, ZR_W), F32),
            jax.ShapeDtypeStruct((m, DIFF_W), BF16),
            jax.ShapeDtypeStruct((m, DIFF_W), BF16),
            jax.ShapeDtypeStruct((m, DIFF_W), BF16),
        ],
        compiler_params=_params("parallel"),
        name="inproj",
    )(x2d, pos2d, g, w_r, w_d)


def _rwkv_kernel(z_ref, mix_ref, wl_ref, w0_ref, a0_ref, kk_ref, ka_ref, rk_ref, gain_ref,
                 bias_ref, y_ref, carry_s, st_s, r_s, lw_s, k_s, v_s, al_s, be_s, g_s, bo_s, y_s):
    tb = z_ref.shape[0]
    n_pair = RWKV_W // LANES

    @pl.when(pl.program_id(1) == 0)
    def _():
        carry_s[...] = jnp.zeros_like(carry_s)
        st_s[...] = jnp.zeros_like(st_s)

    row = lax.broadcasted_iota(jnp.int32, (tb, 1), 0)

    def shifted(c0, c1):
        zc = z_ref[:, c0:c1]
        zp = jnp.where(row == 0, carry_s[7:8, c0:c1], pltpu.roll(zc, 1, axis=0))
        return zc + (zp - zc) * mix_ref[:, c0:c1]

    ri = lax.broadcasted_iota(jnp.int32, (LANES, LANES), 0)
    ci = lax.broadcasted_iota(jnp.int32, (LANES, LANES), 1)
    same_head = (ri // RWKV_N) == (ci // RWKV_N)
    ones_head = jnp.where(same_head, 1.0, 0.0).astype(BF16)

    def headsum(x):
        return jnp.concatenate(
            [_split_dot(x[:, p * LANES:(p + 1) * LANES], ones_head) for p in range(n_pair)], axis=1)

    zl = shifted(3 * RWKV_W, ZR_W)
    ll = lax.broadcasted_iota(jnp.int32, (1, LORA_PAD), 1)
    act = jnp.where(ll < R_DECAY, jnp.tanh(zl),
                    jnp.where(ll < R_DECAY + R_AAA, zl, _sigmoid(zl)))
    lo = jnp.dot(act.astype(BF16), wl_ref[...], preferred_element_type=F32)
    lw_s[...] = -math.exp(-0.5) * _sigmoid(w0_ref[...] + lo[:, 0:RWKV_W])
    a = _sigmoid(a0_ref[...] + lo[:, RWKV_W:2 * RWKV_W])
    g_s[...] = lo[:, 2 * RWKV_W:3 * RWKV_W]

    r = shifted(0, RWKV_W)
    k = shifted(RWKV_W, 2 * RWKV_W)
    v = shifted(2 * RWKV_W, 3 * RWKV_W)
    carry_s[...] = z_ref[tb - 8:tb, :]
    r_s[...] = r
    v_s[...] = v
    kk = k * kk_ref[...]
    ss = headsum(kk * kk)
    alpha = kk * lax.rsqrt(jnp.maximum(ss, 1e-24))
    al_s[...] = alpha
    be_s[...] = alpha * a
    k2 = k * (1.0 + (a - 1.0) * ka_ref[...])
    k_s[...] = k2
    bo_s[...] = headsum(r * k2 * rk_ref[...]) * v

    c_len = CHUNK
    tri = jnp.where(lax.broadcasted_iota(jnp.int32, (c_len, c_len), 0)
                    >= lax.broadcasted_iota(jnp.int32, (c_len, c_len), 1), 1.0, 0.0).astype(BF16)
    lane = lax.broadcasted_iota(jnp.int32, (1, LANES), 1)
    head0 = lane < RWKV_N
    strict = same_head & (ri > ci)
    incl = same_head & (ri >= ci)
    b16 = (ri // 16) == (ci // 16)
    b32 = (ri // 32) == (ci // 32)
    eye = jnp.where(ri == ci, 1.0, 0.0).astype(F32)

    def stack2(x):
        return jnp.concatenate([jnp.where(head0, x, 0.0), jnp.where(head0, 0.0, x)], axis=0)

    def fold(x):
        return x[:c_len] + x[c_len:]

    def chunk_body(c, carry):
        r0 = pl.multiple_of(c * c_len, c_len)
        for p in range(n_pair):
            sl = (pl.ds(r0, c_len), slice(p * LANES, (p + 1) * LANES))
            r_c, lw, k_c, v_c, al, be = r_s[sl], lw_s[sl], k_s[sl], v_s[sl], al_s[sl], be_s[sl]
            hi = lw.astype(BF16)
            r1 = lw - hi.astype(F32)
            mid = r1.astype(BF16)
            low = (r1 - mid.astype(F32)).astype(BF16)
            cum = (jnp.dot(tri, hi, preferred_element_type=F32)
                   + jnp.dot(tri, mid, preferred_element_type=F32)
                   + jnp.dot(tri, low, preferred_element_type=F32))
            tot = cum[c_len - 1:c_len, :]
            e_neg = jnp.exp(-cum)
            e_end = jnp.exp(tot - cum)
            a_t = al * jnp.exp(cum - lw)
            r_t = r_c * jnp.exp(cum)
            a_st, r_st, v_st = stack2(a_t), stack2(r_t), stack2(v_c)
            b_n, k_n = be * e_neg, k_c * e_neg
            gram = _mm_nt(jnp.concatenate([a_st, r_st], axis=0),
                          jnp.concatenate([b_n, b_n, k_n, k_n], axis=0))
            a_ab = jnp.where(strict, gram[:LANES, :LANES], 0.0)
            a_ak = jnp.where(strict, gram[:LANES, LANES:], 0.0)
            a_rb = jnp.where(incl, gram[LANES:, :LANES], 0.0)
            a_rk = jnp.where(incl, gram[LANES:, LANES:], 0.0)
            a0 = jnp.where(b16, a_ab, 0.0)
            e1 = jnp.where(b32 & jnp.logical_not(b16), a_ab, 0.0)
            e2 = jnp.where(b32, 0.0, a_ab)
            p2 = _mm(a0, a0)
            p4 = _mm(p2, p2)
            p8 = _mm(p4, p4)
            t = eye - a0
            t = t + _mm(t, p2)
            t = t + _mm(t, p4)
            t = t + _mm(t, p8)
            t = t - _mm(_mm(t, e1), t)
            t = t - _mm(_mm(t, e2), t)
            akv = _mm(a_ak, v_st)
            w = _mm(t, jnp.concatenate([a_st, akv], axis=1))
            rbw = _mm(a_rb, w)
            rkv = _mm(a_rk, v_st)
            r_hat = fold(r_st - rbw[:, :LANES])
            y0 = fold(rkv - rbw[:, LANES:])
            ta_tv = jnp.concatenate([fold(w[:, :LANES]), fold(w[:, LANES:])], axis=1)
            bw = _mm_tn(be * e_end, ta_tv)
            kv = _mm_tn(k_c * e_end, v_c)
            m_p = jnp.where(same_head, bw[:, :LANES], 0.0)
            n_p = jnp.where(same_head, kv - bw[:, LANES:], 0.0)
            decay_col = jnp.exp(jnp.broadcast_to(tot, (LANES, LANES)).T)
            st = st_s[p]
            zz = _mm(jnp.concatenate([m_p, r_hat], axis=0), st)
            y_s[sl] = zz[LANES:] + y0
            st_s[p] = decay_col * st - zz[:LANES] + n_p
        return carry

    lax.fori_loop(0, tb // c_len, chunk_body, 0)

    y = y_s[...]
    mu = headsum(y) * (1.0 / RWKV_N)
    d = y - mu
    var = headsum(d * d) * (1.0 / RWKV_N)
    yn = d * lax.rsqrt(var + LNX_EPS) * gain_ref[...] + bias_ref[...]
    y_ref[...] = ((yn + bo_s[...]) * g_s[...]).astype(y_ref.dtype)


def _rwkv(zr, bsz, seq, mix, w_lora, w0, a0, k_k, k_a, r_k, gain, bias):
    tb = ROW_TILE
    nt = seq // tb
    row = lambda b, t: (b * nt + t, 0)
    const = lambda b, t: (0, 0)
    vec = pl.BlockSpec((1, RWKV_W), const)
    big = pltpu.VMEM((tb, RWKV_W), F32)
    return pl.pallas_call(
        _rwkv_kernel,
        grid=(bsz, nt),
        in_specs=[
            pl.BlockSpec((tb, ZR_W), row),
            pl.BlockSpec((1, ZR_W), const),
            pl.BlockSpec((LORA_PAD, 3 * RWKV_W), const),
            vec, vec, vec, vec, vec, vec, vec,
        ],
        out_specs=pl.BlockSpec((tb, RWKV_W), row),
        out_shape=jax.ShapeDtypeStruct((bsz * seq, RWKV_W), BF16),
        scratch_shapes=[
            pltpu.VMEM((8, ZR_W), F32),
            pltpu.VMEM((RWKV_W // LANES, LANES, LANES), F32),
            big, big, big, big, big, big, big, big, big,
        ],
        compiler_params=_params("parallel", "arbitrary"),
        name="rwkv7",
    )(zr, mix, w_lora, w0, a0, k_k, k_a, r_k, gain, bias)


def _rwkv2_kernel(z_ref, mix_ref, wl_ref, w0_ref, a0_ref, kk_ref, ka_ref, rk_ref, gain_ref,
                  bias_ref, y_ref, carry_s, st_s, g_s, bo_s, y_s, rt_s, dec_s,
                  ah0_s, ah1_s, bn0_s, bn1_s, kn0_s, kn1_s, vh0_s, vh1_s, be_s, ke_s,
                  mr_s, n_s):
    tb = z_ref.shape[0]
    n_pair = RWKV_W // LANES
    c_len = CHUNK
    n_chunk = tb // c_len

    @pl.when(pl.program_id(1) == 0)
    def _():
        carry_s[...] = jnp.zeros_like(carry_s)
        st_s[...] = jnp.zeros_like(st_s)

    group_chunks = 4
    group_rows = group_chunks * c_len
    n_piece = tb // group_rows
    rowq = lax.broadcasted_iota(jnp.int32, (group_rows, 1), 0)
    ri = lax.broadcasted_iota(jnp.int32, (LANES, LANES), 0)
    ci = lax.broadcasted_iota(jnp.int32, (LANES, LANES), 1)
    same_head = (ri // RWKV_N) == (ci // RWKV_N)
    ones_head = jnp.where(same_head, 1.0, 0.0).astype(BF16)
    head0 = (lax.broadcasted_iota(jnp.int32, (1, RWKV_W), 1) % LANES) < RWKV_N

    def headsum(x):
        return jnp.concatenate(
            [_split_dot(x[:, p * LANES:(p + 1) * LANES], ones_head) for p in range(n_pair)], axis=1)

    def prep(r0, top):
        rows = pl.ds(r0, group_rows)

        def shifted(c0, c1):
            zc = z_ref[rows, c0:c1]
            zp = jnp.where(rowq == 0, top[7:8, c0:c1], pltpu.roll(zc, 1, axis=0))
            return zc + (zp - zc) * mix_ref[:, c0:c1]

        zl = shifted(3 * RWKV_W, ZR_W)
        ll = lax.broadcasted_iota(jnp.int32, (1, LORA_PAD), 1)
        act = jnp.where(ll < R_DECAY, jnp.tanh(zl),
                        jnp.where(ll < R_DECAY + R_AAA, zl, _sigmoid(zl)))
        lo = jnp.dot(act.astype(BF16), wl_ref[...], preferred_element_type=F32)
        yield
        lw = -math.exp(-0.5) * _sigmoid(w0_ref[...] + lo[:, 0:RWKV_W])
        a = _sigmoid(a0_ref[...] + lo[:, RWKV_W:2 * RWKV_W])
        g_s[rows, :] = lo[:, 2 * RWKV_W:3 * RWKV_W]
        yield

        r = shifted(0, RWKV_W)
        yield
        k = shifted(RWKV_W, 2 * RWKV_W)
        yield
        v = shifted(2 * RWKV_W, 3 * RWKV_W)
        yield
        kk = k * kk_ref[...]
        ss = headsum(kk * kk)
        alpha = kk * lax.rsqrt(jnp.maximum(ss, 1e-24))
        yield
        beta = alpha * a
        k2 = k * (1.0 + (a - 1.0) * ka_ref[...])
        bo_s[rows, :] = headsum(r * k2 * rk_ref[...]) * v
        yield

        rin = rowq % c_len
        cum = lw
        for s in (1, 2, 4, 8, 16, 32):
            cum = cum + jnp.where(rin >= s, pltpu.roll(cum, s, axis=0), 0.0)
            yield
        tot = jnp.broadcast_to(cum.reshape(group_chunks, c_len, RWKV_W)[:, c_len - 1:c_len, :],
                               (group_chunks, c_len, RWKV_W)).reshape(group_rows, RWKV_W)
        e_neg = jnp.exp(-cum)
        e_end = jnp.exp(tot - cum)
        yield
        a_t = alpha * jnp.exp(cum - lw)
        r_t = r * jnp.exp(cum)
        rt_s[rows, :] = r_t
        dec_s[rows, :] = jnp.exp(tot)
        yield
        ah0_s[rows, :] = jnp.where(head0, a_t, 0.0).astype(BF16)
        ah1_s[rows, :] = jnp.where(head0, 0.0, a_t).astype(BF16)
        yield
        vh0_s[rows, :] = jnp.where(head0, v, 0.0).astype(BF16)
        vh1_s[rows, :] = jnp.where(head0, 0.0, v).astype(BF16)
        yield
        b_n = (beta * e_neg).astype(BF16)
        bn0_s[rows, :] = jnp.where(head0, b_n, jnp.zeros_like(b_n))
        bn1_s[rows, :] = jnp.where(head0, jnp.zeros_like(b_n), b_n)
        yield
        k_n = (k2 * e_neg).astype(BF16)
        kn0_s[rows, :] = jnp.where(head0, k_n, jnp.zeros_like(k_n))
        kn1_s[rows, :] = jnp.where(head0, jnp.zeros_like(k_n), k_n)
        yield
        be_s[rows, :] = (beta * e_end).astype(BF16)
        ke_s[rows, :] = (k2 * e_end).astype(BF16)

    ti = lax.broadcasted_iota(jnp.int32, (c_len, LANES), 0)
    si = lax.broadcasted_iota(jnp.int32, (c_len, LANES), 1) % c_len
    strict = si < ti
    incl = si <= ti
    b16 = (ti // 16) == (si // 16)
    b32 = (ti // 32) == (si // 32)
    eye = jnp.where(ti == si, 1.0, 0.0).astype(F32)
    lane_h0 = lax.broadcasted_iota(jnp.int32, (1, LANES), 1) < RWKV_N
    group = group_chunks * n_pair
    nb = range(group)

    def stack(x):
        xb = x.astype(BF16)
        zero = jnp.zeros_like(xb)
        return jnp.concatenate([jnp.where(lane_h0, xb, zero), jnp.where(lane_h0, zero, xb)], axis=0)

    def mmb(xs, ys):
        return [jnp.dot(x.astype(BF16), stack(y), preferred_element_type=F32) for x, y in zip(xs, ys)]

    def solve(it):
        idx = []
        for u in range(group):
            c = it * group_chunks + u // n_pair
            p = u % n_pair
            idx.append((c, c * n_pair + p,
                        (pl.ds(pl.multiple_of(c * c_len, c_len), c_len), slice(p * LANES, (p + 1) * LANES))))
        sls = [sl for _, _, sl in idx]
        a_st = [jnp.concatenate([ah0_s[sl], ah1_s[sl]], axis=0) for sl in sls]
        lhs = [jnp.concatenate([x[:c_len] + x[c_len:], rt_s[sl].astype(BF16)], axis=0)
               for x, sl in zip(a_st, sls)]
        rhs = [jnp.concatenate([bn0_s[sl], bn1_s[sl], kn0_s[sl], kn1_s[sl]], axis=0) for sl in sls]
        gram = [lax.dot_general(x, y, (((1,), (1,)), ((), ())), preferred_element_type=F32)
                for x, y in zip(lhs, rhs)]
        a_ab = [jnp.where(strict, gm[:c_len, :LANES], 0.0) for gm in gram]
        a_kr = [jnp.concatenate([jnp.where(strict, gm[:c_len, LANES:], 0.0),
                                 jnp.where(incl, gm[c_len:, LANES:], 0.0)], axis=0).astype(BF16)
                for gm in gram]
        a_rb = [jnp.where(incl, gm[c_len:, :LANES], 0.0) for gm in gram]
        yield
        a0 = [jnp.where(b16, x, 0.0).astype(BF16) for x in a_ab]
        e1 = [jnp.where(b32 & jnp.logical_not(b16), x, 0.0).astype(BF16) for x in a_ab]
        e2 = [jnp.where(b32, 0.0, x).astype(BF16) for x in a_ab]
        p2 = mmb(a0, a0)
        yield
        p4 = mmb(p2, p2)
        yield
        p8 = mmb(p4, p4)
        t = [eye - x.astype(F32) for x in a0]
        t = [x + y for x, y in zip(t, mmb(t, p2))]
        yield
        t = [x + y for x, y in zip(t, mmb(t, p4))]
        yield
        t = [x + y for x, y in zip(t, mmb(t, p8))]
        yield
        te = mmb(t, e1)
        yield
        t = [x - y for x, y in zip(t, mmb(te, t))]
        yield
        te = mmb(t, e2)
        yield
        t = [x - y for x, y in zip(t, mmb(te, t))]
        yield
        v_st = [jnp.concatenate([vh0_s[sl], vh1_s[sl]], axis=0) for sl in sls]
        av = [jnp.dot(x, y, preferred_element_type=F32) for x, y in zip(a_kr, v_st)]
        yield
        w = [jnp.dot(x.astype(BF16), jnp.concatenate([y, stack(z[:c_len])], axis=1), preferred_element_type=F32)
             for x, y, z in zip(t, a_st, av)]
        yield
        rbw = [jnp.dot(x.astype(BF16), jnp.concatenate([stack(y[:, :LANES]), stack(y[:, LANES:])], axis=1),
                       preferred_element_type=F32) for x, y in zip(a_rb, w)]
        yield
        bw = [_mm_tn(be_s[sl], x) for sl, x in zip(sls, w)]
        yield
        kv = [_mm_tn(ke_s[sl], vh0_s[sl] + vh1_s[sl]) for sl in sls]
        yield
        for u in nb:
            c, j, sl = idx[u]
            r_hat = rt_s[sl] - rbw[u][:, :LANES]
            m_p = jnp.where(same_head, bw[u][:, :LANES], 0.0)
            mr_s[j] = jnp.concatenate([m_p, r_hat], axis=0).astype(BF16)
            y_s[sl] = av[u][c_len:] - rbw[u][:, LANES:]
            n_s[j] = jnp.where(same_head, kv[u] - bw[u][:, LANES:], 0.0)

    _run_interleaved(prep(0, carry_s[...]))
    carry_s[...] = z_ref[tb - 8:tb, :]

    def piece_body(it, carry):
        r0 = pl.multiple_of((it + 1) * group_rows, group_rows)
        _run_interleaved(solve(it), prep(r0, z_ref[pl.ds(r0 - 8, 8), :]))
        return carry

    lax.fori_loop(0, n_piece - 1, piece_body, 0)
    _run_interleaved(solve(n_piece - 1))

    def scan_body(c, carry):
        rows = pl.ds(pl.multiple_of(c * c_len, c_len), c_len)
        sts = [st_s[p] for p in range(n_pair)]
        zz = [jnp.dot(mr_s[c * n_pair + p], sts[p].astype(BF16), preferred_element_type=F32)
              for p in range(n_pair)]
        for p in range(n_pair):
            j = c * n_pair + p
            sl = (rows, slice(p * LANES, (p + 1) * LANES))
            y_s[sl] = y_s[sl] + zz[p][LANES:]
            d = dec_s[sl]
            decay_col = jnp.concatenate([d, d], axis=0).T
            st_s[p] = decay_col * sts[p] - zz[p][:LANES] + n_s[j]
        return carry

    lax.fori_loop(0, n_chunk, scan_body, 0)

    y = y_s[...]
    mu = headsum(y) * (1.0 / RWKV_N)
    d = y - mu
    var = headsum(d * d) * (1.0 / RWKV_N)
    yn = d * lax.rsqrt(var + LNX_EPS) * gain_ref[...] + bias_ref[...]
    y_ref[...] = ((yn + bo_s[...]) * g_s[...]).astype(y_ref.dtype)


def _rwkv2(zr, bsz, seq, mix, w_lora, w0, a0, k_k, k_a, r_k, gain, bias):
    tb = ROW_TILE
    nt = seq // tb
    n_prob = (tb // CHUNK) * (RWKV_W // LANES)
    row = lambda b, t: (b * nt + t, 0)
    const = lambda b, t: (0, 0)
    vec = pl.BlockSpec((1, RWKV_W), const)
    big = pltpu.VMEM((tb, RWKV_W), F32)
    half = pltpu.VMEM((tb, RWKV_W), BF16)
    return pl.pallas_call(
        _rwkv2_kernel,
        grid=(bsz, nt),
        in_specs=[
            pl.BlockSpec((tb, ZR_W), row),
            pl.BlockSpec((1, ZR_W), const),
            pl.BlockSpec((LORA_PAD, 3 * RWKV_W), const),
            vec, vec, vec, vec, vec, vec, vec,
        ],
        out_specs=pl.BlockSpec((tb, RWKV_W), row),
        out_shape=jax.ShapeDtypeStruct((bsz * seq, RWKV_W), BF16),
        scratch_shapes=[
            pltpu.VMEM((8, ZR_W), F32),
            pltpu.VMEM((RWKV_W // LANES, LANES, LANES), F32),
            big, big, big, big, big,
            half, half, half, half, half, half, half, half, half, half,
            pltpu.VMEM((n_prob, LANES + CHUNK, LANES), BF16),
            pltpu.VMEM((n_prob, LANES, LANES), F32),
        ],
        compiler_params=_params("parallel", "arbitrary"),
        name="rwkv7",
    )(zr, mix, w_lora, w0, a0, k_k, k_a, r_k, gain, bias)


def _attn_kernel(qi_tab, ki_tab, q_ref, k_ref, v_ref, lq1_ref, lk1_ref, lq2_ref, lk2_ref, sg_ref,
                 o_ref, q_s, m_s, acc_s, *, lambda_init, ratio):
    tq, tk = q_ref.shape[0], k_ref.shape[0]
    n_col = tk // LANES
    pidx = pl.program_id(1)
    qi = qi_tab[pidx]
    ki = ki_tab[pidx]
    nt = (((1,), (1,)), ((), ()))

    @pl.when(ki == 0)
    def _():
        map0 = lax.broadcasted_iota(jnp.int32, (1, LANES), 1) < DIFF_D
        for h in range(DIFF_H):
            q = q_ref[:, h * LANES:(h + 1) * LANES]
            zero = jnp.zeros_like(q)
            q_s[2 * h] = jnp.where(map0, q, zero)
            q_s[2 * h + 1] = jnp.where(map0, zero, q)
        m_s[...] = jnp.full_like(m_s, -jnp.inf)
        acc_s[...] = jnp.zeros_like(acc_s)

    def step(masked):
        if masked:
            rowp = qi * tq + lax.broadcasted_iota(jnp.int32, (tq, LANES), 0)
            colp = ki * tk + lax.broadcasted_iota(jnp.int32, (tq, LANES), 1)
            keep = [colp + j * LANES <= rowp for j in range(n_col)]
        ones = jnp.ones((tk, LANES), BF16)
        for h in range(DIFF_H):
            k = k_ref[:, h * LANES:(h + 1) * LANES]
            v_ext = jnp.concatenate([v_ref[:, h * LANES:(h + 1) * LANES], ones], axis=1)
            for c in range(2):
                i = 2 * h + c
                s = lax.dot_general(q_s[i], k, nt, preferred_element_type=F32)
                cols = [s[:, j * LANES:(j + 1) * LANES] for j in range(n_col)]
                if masked:
                    cols = [jnp.where(keep[j], cols[j], -jnp.inf) for j in range(n_col)]
                mx = cols[0]
                for j in range(1, n_col):
                    mx = jnp.maximum(mx, cols[j])
                m_old = m_s[i]
                m_new = jnp.maximum(m_old, jnp.max(mx, axis=-1, keepdims=True))
                corr = jnp.exp2(m_old - m_new)
                p = jnp.concatenate([jnp.exp2(cj - m_new).astype(BF16) for cj in cols], axis=1)
                pv = jnp.dot(p, v_ext, preferred_element_type=F32)
                acc_s[i] = jnp.concatenate([corr, corr], axis=1) * acc_s[i] + pv
                m_s[i] = m_new

    on_diag = ki >= qi * ratio

    @pl.when(on_diag)
    def _():
        step(True)

    @pl.when(jnp.logical_not(on_diag))
    def _():
        step(False)

    @pl.when(ki == (qi + 1) * ratio - 1)
    def _():
        lam = (jnp.exp(jnp.sum(lq1_ref[...] * lk1_ref[...], axis=-1, keepdims=True))
               - jnp.exp(jnp.sum(lq2_ref[...] * lk2_ref[...], axis=-1, keepdims=True))
               + lambda_init)
        for h in range(DIFF_H):
            a1 = acc_s[2 * h]
            a2 = acc_s[2 * h + 1]
            o = a1[:, :LANES] / a1[:, LANES:] - lam * (a2[:, :LANES] / a2[:, LANES:])
            o = _rms(o, sg_ref[...], SUBLN_EPS) * (1.0 - lambda_init)
            o_ref[:, h * LANES:(h + 1) * LANES] = o.astype(o_ref.dtype)


def _diff_attn(q, k, v, bsz, seq, lq1, lk1, lq2, lk2, sg, lambda_init):
    tq, tk = ATT_TQ, ATT_TK
    ratio = tq // tk
    nq, nk = seq // tq, seq // tk
    pairs = [(a, b) for a in range(nq) for b in range((a + 1) * ratio)]
    qi_tab = jnp.asarray(np.array([a for a, _ in pairs], np.int32))
    ki_tab = jnp.asarray(np.array([b for _, b in pairs], np.int32))
    qmap = lambda b, p, qt, kt: (b * nq + qt[p], 0)
    kmap = lambda b, p, qt, kt: (b * nk + kt[p], 0)
    const = lambda b, p, qt, kt: (0, 0)
    lam_spec = pl.BlockSpec((1, DIFF_D), const)
    grid_spec = pltpu.PrefetchScalarGridSpec(
        num_scalar_prefetch=2,
        grid=(bsz, len(pairs)),
        in_specs=[
            pl.BlockSpec((tq, DIFF_W), qmap),
            pl.BlockSpec((tk, DIFF_W), kmap),
            pl.BlockSpec((tk, DIFF_W), kmap),
            lam_spec, lam_spec, lam_spec, lam_spec,
            pl.BlockSpec((1, 2 * DIFF_D), const),
        ],
        out_specs=pl.BlockSpec((tq, DIFF_W), qmap),
        scratch_shapes=[
            pltpu.VMEM((2 * DIFF_H, tq, LANES), BF16),
            pltpu.VMEM((2 * DIFF_H, tq, LANES), F32),
            pltpu.VMEM((2 * DIFF_H, tq, 2 * LANES), F32),
        ],
    )
    return pl.pallas_call(
        functools.partial(_attn_kernel, lambda_init=lambda_init, ratio=ratio),
        grid_spec=grid_spec,
        out_shape=jax.ShapeDtypeStruct((bsz * seq, DIFF_W), BF16),
        compiler_params=_params("parallel", "arbitrary"),
        name="diff_attn",
    )(qi_tab, ki_tab, q, k, v, lq1, lk1, lq2, lk2, sg)


def _attn2_kernel(q_ref, k_ref, v_ref, lq1_ref, lk1_ref, lq2_ref, lk2_ref, sg_ref, o_ref,
                  q_s, m_s, acc_s, *, lambda_init, tk):
    tq = q_ref.shape[0]
    n_col = tk // LANES
    n_diag = tq // tk
    qi = pl.program_id(2)
    nt = (((1,), (1,)), ((), ()))

    map0 = (lax.broadcasted_iota(jnp.int32, (1, LANES), 1) % DIFF_D) < DIFF_D // 2
    q = q_ref[...]
    zero = jnp.zeros_like(q)
    q_s[0] = jnp.where(map0, q, zero)
    q_s[1] = jnp.where(map0, zero, q)
    m_s[...] = jnp.full_like(m_s, -jnp.inf)
    acc_s[...] = jnp.zeros_like(acc_s)
    ones = jnp.ones((tk, LANES), BF16)
    tri_keep = [lax.broadcasted_iota(jnp.int32, (tk, LANES), 1) + j * LANES
                <= lax.broadcasted_iota(jnp.int32, (tk, LANES), 0) for j in range(n_col)]

    def kv_step(kv_rows, r0, r1, masked):
        k = k_ref[kv_rows, :]
        v_ext = jnp.concatenate([v_ref[kv_rows, :], ones], axis=1)
        for c in range(2):
            s = lax.dot_general(q_s[c, r0:r1, :], k, nt, preferred_element_type=F32)
            cols = [s[:, j * LANES:(j + 1) * LANES] for j in range(n_col)]
            if masked:
                cols = [jnp.where(tri_keep[j], cols[j], -jnp.inf) for j in range(n_col)]
            mx = cols[0]
            for j in range(1, n_col):
                mx = jnp.maximum(mx, cols[j])
            m_old = m_s[c, r0:r1, :]
            m_new = jnp.maximum(m_old, jnp.max(mx, axis=-1, keepdims=True))
            corr = jnp.exp2(m_old - m_new)
            p = jnp.concatenate([jnp.exp2(cj - m_new).astype(BF16) for cj in cols], axis=1)
            pv = jnp.dot(p, v_ext, preferred_element_type=F32)
            acc_s[c, r0:r1, :] = jnp.concatenate([corr, corr], axis=1) * acc_s[c, r0:r1, :] + pv
            m_s[c, r0:r1, :] = m_new

    def full_body(j, carry):
        for u in range(n_diag):
            kv_rows = pl.ds(pl.multiple_of((j * n_diag + u) * tk, tk), tk)
            kv_step(kv_rows, 0, tq // 2, False)
            kv_step(kv_rows, tq // 2, tq, False)
        return carry

    lax.fori_loop(0, qi, full_body, 0)
    for d in range(n_diag):
        kv_rows = pl.ds(pl.multiple_of((qi * n_diag + d) * tk, tk), tk)
        kv_step(kv_rows, d * tk, (d + 1) * tk, True)
        if d + 1 < n_diag:
            kv_step(kv_rows, (d + 1) * tk, tq, False)

    lam = (jnp.exp(jnp.sum(lq1_ref[...] * lk1_ref[...], axis=-1, keepdims=True))
           - jnp.exp(jnp.sum(lq2_ref[...] * lk2_ref[...], axis=-1, keepdims=True))
           + lambda_init)
    a1 = acc_s[0]
    a2 = acc_s[1]
    o = a1[:, :LANES] / a1[:, LANES:] - lam * (a2[:, :LANES] / a2[:, LANES:])
    o_ref[...] = (_rms(o, sg_ref[...], SUBLN_EPS) * (1.0 - lambda_init)).astype(o_ref.dtype)


def _diff_attn2(q, k, v, bsz, seq, lq1, lk1, lq2, lk2, sg, lambda_init):
    tq = min(ATT_TQ, seq)
    tk = min(ATT_TK, tq)
    nq = seq // tq
    qmap = lambda b, h, i: (b * nq + i, h)
    kmap = lambda b, h, i: (b, h)
    const = lambda b, h, i: (0, 0)
    lam_spec = pl.BlockSpec((1, DIFF_D), const)
    return pl.pallas_call(
        functools.partial(_attn2_kernel, lambda_init=lambda_init, tk=tk),
        grid=(bsz, DIFF_H, nq),
        in_specs=[
            pl.BlockSpec((tq, LANES), qmap),
            pl.BlockSpec((seq, LANES), kmap),
            pl.BlockSpec((seq, LANES), kmap),
            lam_spec, lam_spec, lam_spec, lam_spec,
            pl.BlockSpec((1, 2 * DIFF_D), const),
        ],
        out_specs=pl.BlockSpec((tq, LANES), qmap),
        out_shape=jax.ShapeDtypeStruct((bsz * seq, DIFF_W), BF16),
        scratch_shapes=[
            pltpu.VMEM((2, tq, LANES), BF16),
            pltpu.VMEM((2, tq, LANES), F32),
            pltpu.VMEM((2, tq, 2 * LANES), F32),
        ],
        compiler_params=_params("parallel", "parallel", "arbitrary"),
        name="diff_attn",
    )(q, k, v, lq1, lk1, lq2, lk2, sg)


def _memkv_kernel(m_ref, g_ref, w_ref, o_ref):
    h = _rms(m_ref[...], g_ref[...], NORM_EPS).astype(BF16)
    for j in range(o_ref.shape[1] // 256):
        sl = slice(j * 256, (j + 1) * 256)
        o_ref[:, sl] = jnp.dot(h, w_ref[:, sl], preferred_element_type=F32).astype(o_ref.dtype)


def _memkv(mem2d, g, wkv):
    m = mem2d.shape[0]
    tm = MEM_LEN
    return pl.pallas_call(
        _memkv_kernel,
        grid=(m // tm,),
        in_specs=[
            pl.BlockSpec((tm, D_MODEL), lambda i: (i, 0)),
            pl.BlockSpec((1, D_MODEL), lambda i: (0, 0)),
            pl.BlockSpec((D_MODEL, 2 * D_MODEL), lambda i: (0, 0)),
        ],
        out_specs=pl.BlockSpec((tm, 2 * D_MODEL), lambda i: (i, 0)),
        out_shape=jax.ShapeDtypeStruct((m, 2 * D_MODEL), BF16),
        compiler_params=_params("parallel"),
        name="memkv",
    )(mem2d, g, wkv)


def _cross_kernel(x_ref, ya_ref, yb_ref, kv_ref, wout_ref, g_ref, wq_ref, wo_ref, o_ref, att_s):
    tm = x_ref.shape[0]
    halves = [slice(0, tm // 2), slice(tm // 2, tm)]
    nt = (((1,), (1,)), ((), ()))
    x1 = [x_ref[r, :] + (jnp.dot(ya_ref[r, :], wout_ref[0:RWKV_W, :], preferred_element_type=F32)
                         + jnp.dot(yb_ref[r, :], wout_ref[RWKV_W:, :], preferred_element_type=F32))
          for r in halves]
    hc = [_rms(x, g_ref[...], NORM_EPS).astype(BF16) for x in x1]
    scale = CROSS_D ** -0.5 * math.log2(math.e)
    q = [(jnp.dot(x, wq_ref[...], preferred_element_type=F32) * scale).astype(BF16) for x in hc]
    for h in range(CROSS_H):
        sl = slice(h * CROSS_D, (h + 1) * CROSS_D)
        kh = kv_ref[:, sl]
        vh = kv_ref[:, D_MODEL + h * CROSS_D:D_MODEL + (h + 1) * CROSS_D]
        s = [lax.dot_general(x[:, sl], kh, nt, preferred_element_type=F32) for x in q]
        p = [jnp.exp2(x - jnp.max(x, axis=-1, keepdims=True)) for x in s]
        o = [jnp.dot(x.astype(BF16), vh, preferred_element_type=F32) / jnp.sum(x, axis=-1, keepdims=True)
             for x in p]
        for r, x in zip(halves, o):
            att_s[r, sl] = x.astype(BF16)
    for r, x in zip(halves, x1):
        o_ref[r, :] = x + jnp.dot(att_s[r, :], wo_ref[...], preferred_element_type=F32)


def _cross(x2d, ya, yb, kv, bsz, seq, w_out, g, wq, wo):
    tm = PROJ_TILE
    nt = seq // tm
    row = lambda b, t: (b * nt + t, 0)
    const = lambda b, t: (0, 0)
    return pl.pallas_call(
        _cross_kernel,
        grid=(bsz, nt),
        in_specs=[
            pl.BlockSpec((tm, D_MODEL), row),
            pl.BlockSpec((tm, RWKV_W), row),
            pl.BlockSpec((tm, DIFF_W), row),
            pl.BlockSpec((MEM_LEN, 2 * D_MODEL), lambda b, t: (b, 0)),
            pl.BlockSpec((D_MODEL, D_MODEL), const, pipeline_mode=pl.Buffered(1)),
            pl.BlockSpec((1, D_MODEL), const),
            pl.BlockSpec((D_MODEL, D_MODEL), const, pipeline_mode=pl.Buffered(1)),
            pl.BlockSpec((D_MODEL, D_MODEL), const, pipeline_mode=pl.Buffered(1)),
        ],
        out_specs=pl.BlockSpec((tm, D_MODEL), row),
        out_shape=jax.ShapeDtypeStruct((bsz * seq, D_MODEL), F32),
        scratch_shapes=[pltpu.VMEM((tm, D_MODEL), BF16)],
        compiler_params=_params("parallel", "parallel"),
        name="outproj_cross",
    )(x2d, ya, yb, kv, w_out, g, wq, wo)


def _ffn_kernel(x_ref, g_ref, wg_ref, wv_ref, cw_ref, cb_ref, wd_ref, gf_ref, o_ref, carry_s, act_s, *, final):
    tm = x_ref.shape[0]
    n_ff = wg_ref.shape[0]

    @pl.when(pl.program_id(1) == 0)
    def _():
        carry_s[...] = jnp.zeros_like(carry_s)

    x = x_ref[...]
    h = _rms(x, g_ref[...], NORM_EPS).astype(BF16)

    def up(j):
        return (jnp.dot(h, wg_ref[j], preferred_element_type=F32),
                jnp.dot(h, wv_ref[j], preferred_element_type=F32))

    tf = wg_ref.shape[2]
    split = (n_ff + 1) // 2 * tf
    nxt = up(0)
    down = []
    for j in range(n_ff):
        gate, val = nxt
        if j + 1 < n_ff:
            nxt = up(j + 1)
        ext = jnp.concatenate([carry_s[j], gate], axis=0)
        g1 = ext[7:tm + 7, :]
        g2 = ext[6:tm + 6, :]
        carry_s[j] = gate[tm - 8:tm, :]
        cwh = 0.5 * cw_ref[j]
        hc = cwh[0:1, :] * g2 + cwh[1:2, :] * g1 + cwh[2:3, :] * gate + 0.5 * cb_ref[j]
        act_s[:, j * tf:(j + 1) * tf] = ((hc + hc * jnp.tanh(hc)) * val).astype(BF16)
        if (j + 1) * tf == split:
            down.append(jnp.dot(act_s[:, :split], wd_ref[:split, :], preferred_element_type=F32))
    down.append(jnp.dot(act_s[:, split:], wd_ref[split:, :], preferred_element_type=F32))
    out = x + (down[0] + down[1])
    o_ref[...] = _rms(out, gf_ref[...], NORM_EPS) if final else out


def _ffn(x2d, bsz, seq, g, wg, wv, cw, cb, wd, gf, final):
    tm = PROJ_TILE
    nt = seq // tm
    n_ff, _, tf = wg.shape
    row = lambda b, t: (b * nt + t, 0)
    c2 = lambda b, t: (0, 0)
    c3 = lambda b, t: (0, 0, 0)
    once = pl.Buffered(1)
    return pl.pallas_call(
        functools.partial(_ffn_kernel, final=final),
        grid=(bsz, nt),
        in_specs=[
            pl.BlockSpec((tm, D_MODEL), row),
            pl.BlockSpec((1, D_MODEL), c2),
            pl.BlockSpec((n_ff, D_MODEL, tf), c3, pipeline_mode=once),
            pl.BlockSpec((n_ff, D_MODEL, tf), c3, pipeline_mode=once),
            pl.BlockSpec((n_ff, 3, tf), c3),
            pl.BlockSpec((n_ff, 1, tf), c3),
            pl.BlockSpec((n_ff * tf, D_MODEL), c2, pipeline_mode=once),
            pl.BlockSpec((1, D_MODEL), c2),
        ],
        out_specs=pl.BlockSpec((tm, D_MODEL), row),
        out_shape=jax.ShapeDtypeStruct((bsz * seq, D_MODEL), F32),
        scratch_shapes=[pltpu.VMEM((n_ff, 8, tf), F32), pltpu.VMEM((tm, n_ff * tf), BF16)],
        compiler_params=_params("parallel", "arbitrary"),
        name="conv_ffn",
    )(x2d, g, wg, wv, cw, cb, wd, gf)


def _rope_perm():
    idx = np.empty((DIFF_W,), np.int32)
    half = DIFF_D // 2
    for h in range(DIFF_H):
        for c in range(2):
            for d in range(DIFF_D):
                idx[h * LANES + (d // half) * DIFF_D + c * half + d % half] = h * LANES + c * DIFF_D + d
    return idx


def _layer(x2d, pos2d, mem2d, bsz, seq, lambda_init, p):
    w_in = p["w_in"]
    pad = jnp.zeros((D_MODEL, LORA_PAD - (N_SHIFT - 3 * RWKV_W)), F32)
    w_r = jnp.concatenate([w_in[:, :N_SHIFT], pad], axis=1).astype(BF16)
    perm = _rope_perm()
    w_d = jnp.concatenate([w_in[:, N_SHIFT:N_SHIFT + DIFF_W][:, perm],
                           w_in[:, N_SHIFT + DIFF_W:N_SHIFT + 2 * DIFF_W][:, perm],
                           w_in[:, N_SHIFT + 2 * DIFF_W:]], axis=1).astype(BF16)
    mix = jnp.concatenate([p["shift_mix"], jnp.zeros((LORA_PAD - (N_SHIFT - 3 * RWKV_W),), F32)])[None, :]
    w_lora = jnp.zeros((LORA_PAD, 3 * RWKV_W), F32)
    w_lora = w_lora.at[0:R_DECAY, 0:RWKV_W].set(p["w_lora_up"])
    w_lora = w_lora.at[R_DECAY:R_DECAY + R_AAA, RWKV_W:2 * RWKV_W].set(p["a_lora_up"])
    w_lora = w_lora.at[R_DECAY + R_AAA:R_DECAY + R_AAA + R_GATE, 2 * RWKV_W:].set(p["g_lora_up"])
    w_lora = w_lora.astype(BF16)
    v512 = lambda a: a.reshape(1, RWKV_W)

    zr, q, k, v = _inproj(x2d, pos2d, p["norm_mix"][None, :], w_r, w_d)
    ya = _rwkv2(zr, bsz, seq, mix, w_lora, v512(p["w0"]), v512(p["a0"]), v512(p["k_k"]),
               v512(p["k_a"]), v512(p["r_k"]), v512(p["lnx_gain"]), v512(p["lnx_bias"]))
    yb = _diff_attn2(q, k, v, bsz, seq, p["lam_q1"][None, :], p["lam_k1"][None, :],
                    p["lam_q2"][None, :], p["lam_k2"][None, :], p["subln_gain"][None, :], lambda_init)
    kv = _memkv(mem2d, p["norm_mem"][None, :], p["wkv_c"].astype(BF16))
    x2 = _cross(x2d, ya, yb, kv, bsz, seq, p["w_out"].astype(BF16), p["norm_cross"][None, :],
                p["wq_c"].astype(BF16), p["wo_c"].astype(BF16))
    n_ff = D_FF // FF_TILE
    w_up = p["w_up"]
    wg = w_up[:, :D_FF].reshape(D_MODEL, n_ff, FF_TILE).transpose(1, 0, 2).astype(BF16)
    wv = w_up[:, D_FF:].reshape(D_MODEL, n_ff, FF_TILE).transpose(1, 0, 2).astype(BF16)
    cw = p["conv_w"].reshape(3, n_ff, FF_TILE).transpose(1, 0, 2)
    cb = p["conv_b"].reshape(n_ff, 1, FF_TILE)
    wd = p["w_down"].astype(BF16)
    return x2, (p["norm_ffn"][None, :], wg, wv, cw, cb, wd)


def kernel(x, mem, positions, norm_mix, w_in, shift_mix, w0, w_lora_up, a0, a_lora_up, g_lora_up, k_k, k_a, r_k, lnx_gain, lnx_bias, lam_q1, lam_k1, lam_q2, lam_k2, subln_gain, w_out, norm_cross, norm_mem, wq_c, wkv_c, wo_c, norm_ffn, w_up, conv_w, conv_b, w_down, norm_final):
    bsz, seq, _ = x.shape
    depth = norm_mix.shape[0]
    x2d = x.reshape(bsz * seq, D_MODEL)
    pos2d = positions.reshape(bsz * seq, 1)
    mem2d = mem.reshape(bsz * mem.shape[1], D_MODEL)
    stacked = dict(norm_mix=norm_mix, w_in=w_in, shift_mix=shift_mix, w0=w0, w_lora_up=w_lora_up,
                   a0=a0, a_lora_up=a_lora_up, g_lora_up=g_lora_up, k_k=k_k, k_a=k_a,
                   r_k=r_k.reshape(depth, RWKV_W), lnx_gain=lnx_gain, lnx_bias=lnx_bias,
                   lam_q1=lam_q1, lam_k1=lam_k1, lam_q2=lam_q2, lam_k2=lam_k2,
                   subln_gain=subln_gain, w_out=w_out, norm_cross=norm_cross, norm_mem=norm_mem,
                   wq_c=wq_c, wkv_c=wkv_c, wo_c=wo_c, norm_ffn=norm_ffn, w_up=w_up,
                   conv_w=conv_w, conv_b=conv_b, w_down=w_down)
    for l in range(depth):
        p = {name: a[l] for name, a in stacked.items()}
        lambda_init = 0.8 - 0.6 * math.exp(-0.3 * l)
        x2, (gn, wg, wv, cw, cb, wd) = _layer(x2d, pos2d, mem2d, bsz, seq, lambda_init, p)
        x2d = _ffn(x2, bsz, seq, gn, wg, wv, cw, cb, wd, norm_final[None, :], l == depth - 1)
    return x2d.reshape(bsz, seq, D_MODEL)
```

```python
import functools
import math

import numpy as np
import jax
import jax.numpy as jnp
from jax import lax
from jax.experimental import pallas as pl
from jax.experimental.pallas import tpu as pltpu

F32 = jnp.float32
BF16 = jnp.bfloat16

D_MODEL = 1024
RWKV_W = 512
RWKV_N = 64
R_DECAY = 32
R_AAA = 32
R_GATE = 96
DIFF_W = 512
DIFF_H = 4
DIFF_D = 64
N_SHIFT = 3 * RWKV_W + R_DECAY + R_AAA + R_GATE
MEM_LEN = 256
CROSS_H = 4
CROSS_D = D_MODEL // CROSS_H
D_FF = 2816
ROPE_THETA = 10000.0
NORM_EPS = 1e-6
LNX_EPS = 64e-5
SUBLN_EPS = 1e-5

LANES = 128
LORA_PAD = 256
ZR_W = 3 * RWKV_W + LORA_PAD
VMEM_LIMIT = 56 * 1024 * 1024

ROW_TILE = 1024
PROJ_TILE = 1024
CHUNK = 64
ATT_TQ = 2048
ATT_TK = 512
FF_TILE = 256


def _mm(a, b):
    return jnp.dot(a.astype(BF16), b.astype(BF16), preferred_element_type=F32)


def _mm_nt(a, b):
    return lax.dot_general(a.astype(BF16), b.astype(BF16), (((1,), (1,)), ((), ())),
                           preferred_element_type=F32)


def _mm_tn(a, b):
    return lax.dot_general(a.astype(BF16), b.astype(BF16), (((0,), (0,)), ((), ())),
                           preferred_element_type=F32)


def _sigmoid(x):
    return 1.0 / (1.0 + jnp.exp(-x))


def _rms(x, g, eps):
    return x * lax.rsqrt(jnp.mean(x * x, axis=-1, keepdims=True) + eps) * g


def _split_dot(x, ones_b):
    hi = x.astype(BF16)
    lo = (x - hi.astype(F32)).astype(BF16)
    return (jnp.dot(hi, ones_b, preferred_element_type=F32)
            + jnp.dot(lo, ones_b, preferred_element_type=F32))


def _run_interleaved(*stage_gens):
    live = list(stage_gens)
    while live:
        for g in list(live):
            try:
                next(g)
            except StopIteration:
                live.remove(g)


def _params(*sem):
    return pltpu.CompilerParams(dimension_semantics=sem, vmem_limit_bytes=VMEM_LIMIT)


def _inproj_kernel(x_ref, pos_ref, g_ref, wr_ref, wd_ref, zr_ref, q_ref, k_ref, v_ref):
    lane = lax.broadcasted_iota(jnp.int32, (1, LANES), 1)
    freq = (lane % (DIFF_D // 2)).astype(F32)
    inv = jnp.exp(freq * (-2.0 / DIFF_D * math.log(ROPE_THETA)))
    ang = pos_ref[...].astype(F32) * inv
    cos = jnp.cos(ang)
    sin = jnp.where(lane < LANES // 2, -jnp.sin(ang), jnp.sin(ang))
    scale = DIFF_D ** -0.5 * math.log2(math.e)
    cos_q = cos * scale
    sin_q = sin * scale

    def rope(xb, c, s):
        return xb * c + pltpu.roll(xb, LANES // 2, axis=1) * s

    h = _rms(x_ref[...], g_ref[...], NORM_EPS).astype(BF16)
    for j in range(ZR_W // 256):
        sl = slice(j * 256, (j + 1) * 256)
        zr_ref[:, sl] = jnp.dot(h, wr_ref[:, sl], preferred_element_type=F32)

    for j in range(DIFF_W // 256):
        zq = jnp.dot(h, wd_ref[:, j * 256:(j + 1) * 256], preferred_element_type=F32)
        zk = jnp.dot(h, wd_ref[:, DIFF_W + j * 256:DIFF_W + (j + 1) * 256],
                     preferred_element_type=F32)
        zv = jnp.dot(h, wd_ref[:, 2 * DIFF_W + j * 256:2 * DIFF_W + (j + 1) * 256],
                     preferred_element_type=F32)
        for u in range(2):
            c0 = j * 256 + u * LANES
            q_ref[:, c0:c0 + LANES] = rope(zq[:, u * LANES:(u + 1) * LANES], cos_q, sin_q).astype(BF16)
            k_ref[:, c0:c0 + LANES] = rope(zk[:, u * LANES:(u + 1) * LANES], cos, sin).astype(BF16)
        v_ref[:, j * 256:(j + 1) * 256] = zv.astype(BF16)


def _inproj(x2d, pos2d, g, w_r, w_d):
    m = x2d.shape[0]
    tm = PROJ_TILE
    row = lambda i: (i, 0)
    const = lambda i: (0, 0)
    return pl.pallas_call(
        _inproj_kernel,
        grid=(m // tm,),
        in_specs=[
            pl.BlockSpec((tm, D_MODEL), row),
            pl.BlockSpec((tm, 1), row),
            pl.BlockSpec((1, D_MODEL), const),
            pl.BlockSpec((D_MODEL, ZR_W), const, pipeline_mode=pl.Buffered(1)),
            pl.BlockSpec((D_MODEL, 3 * DIFF_W), const, pipeline_mode=pl.Buffered(1)),
        ],
        out_specs=[
            pl.BlockSpec((tm, ZR_W), row),
            pl.BlockSpec((tm, DIFF_W), row),
            pl.BlockSpec((tm, DIFF_W), row),
            pl.BlockSpec((tm, DIFF_W), row),
        ],
        out_shape=[
            jax.ShapeDtypeStruct((m, ZR_W), F32),
            jax.ShapeDtypeStruct((m, DIFF_W), BF16),
            jax.ShapeDtypeStruct((m, DIFF_W), BF16),
            jax.ShapeDtypeStruct((m, DIFF_W), BF16),
        ],
        compiler_params=_params("parallel"),
        name="inproj",
    )(x2d, pos2d, g, w_r, w_d)


def _rwkv_kernel(z_ref, mix_ref, wl_ref, w0_ref, a0_ref, kk_ref, ka_ref, rk_ref, gain_ref,
                 bias_ref, y_ref, carry_s, st_s, r_s, lw_s, k_s, v_s, al_s, be_s, g_s, bo_s, y_s):
    tb = z_ref.shape[0]
    n_pair = RWKV_W // LANES

    @pl.when(pl.program_id(1) == 0)
    def _():
        carry_s[...] = jnp.zeros_like(carry_s)
        st_s[...] = jnp.zeros_like(st_s)

    row = lax.broadcasted_iota(jnp.int32, (tb, 1), 0)

    def shifted(c0, c1):
        zc = z_ref[:, c0:c1]
        zp = jnp.where(row == 0, carry_s[7:8, c0:c1], pltpu.roll(zc, 1, axis=0))
        return zc + (zp - zc) * mix_ref[:, c0:c1]

    ri = lax.broadcasted_iota(jnp.int32, (LANES, LANES), 0)
    ci = lax.broadcasted_iota(jnp.int32, (LANES, LANES), 1)
    same_head = (ri // RWKV_N) == (ci // RWKV_N)
    ones_head = jnp.where(same_head, 1.0, 0.0).astype(BF16)

    def headsum(x):
        return jnp.concatenate(
            [_split_dot(x[:, p * LANES:(p + 1) * LANES], ones_head) for p in range(n_pair)], axis=1)

    zl = shifted(3 * RWKV_W, ZR_W)
    ll = lax.broadcasted_iota(jnp.int32, (1, LORA_PAD), 1)
    act = jnp.where(ll < R_DECAY, jnp.tanh(zl),
                    jnp.where(ll < R_DECAY + R_AAA, zl, _sigmoid(zl)))
    lo = jnp.dot(act.astype(BF16), wl_ref[...], preferred_element_type=F32)
    lw_s[...] = -math.exp(-0.5) * _sigmoid(w0_ref[...] + lo[:, 0:RWKV_W])
    a = _sigmoid(a0_ref[...] + lo[:, RWKV_W:2 * RWKV_W])
    g_s[...] = lo[:, 2 * RWKV_W:3 * RWKV_W]

    r = shifted(0, RWKV_W)
    k = shifted(RWKV_W, 2 * RWKV_W)
    v = shifted(2 * RWKV_W, 3 * RWKV_W)
    carry_s[...] = z_ref[tb - 8:tb, :]
    r_s[...] = r
    v_s[...] = v
    kk = k * kk_ref[...]
    ss = headsum(kk * kk)
    alpha = kk * lax.rsqrt(jnp.maximum(ss, 1e-24))
    al_s[...] = alpha
    be_s[...] = alpha * a
    k2 = k * (1.0 + (a - 1.0) * ka_ref[...])
    k_s[...] = k2
    bo_s[...] = headsum(r * k2 * rk_ref[...]) * v

    c_len = CHUNK
    tri = jnp.where(lax.broadcasted_iota(jnp.int32, (c_len, c_len), 0)
                    >= lax.broadcasted_iota(jnp.int32, (c_len, c_len), 1), 1.0, 0.0).astype(BF16)
    lane = lax.broadcasted_iota(jnp.int32, (1, LANES), 1)
    head0 = lane < RWKV_N
    strict = same_head & (ri > ci)
    incl = same_head & (ri >= ci)
    b16 = (ri // 16) == (ci // 16)
    b32 = (ri // 32) == (ci // 32)
    eye = jnp.where(ri == ci, 1.0, 0.0).astype(F32)

    def stack2(x):
        return jnp.concatenate([jnp.where(head0, x, 0.0), jnp.where(head0, 0.0, x)], axis=0)

    def fold(x):
        return x[:c_len] + x[c_len:]

    def chunk_body(c, carry):
        r0 = pl.multiple_of(c * c_len, c_len)
        for p in range(n_pair):
            sl = (pl.ds(r0, c_len), slice(p * LANES, (p + 1) * LANES))
            r_c, lw, k_c, v_c, al, be = r_s[sl], lw_s[sl], k_s[sl], v_s[sl], al_s[sl], be_s[sl]
            hi = lw.astype(BF16)
            r1 = lw - hi.astype(F32)
            mid = r1.astype(BF16)
            low = (r1 - mid.astype(F32)).astype(BF16)
            cum = (jnp.dot(tri, hi, preferred_element_type=F32)
                   + jnp.dot(tri, mid, preferred_element_type=F32)
                   + jnp.dot(tri, low, preferred_element_type=F32))
            tot = cum[c_len - 1:c_len, :]
            e_neg = jnp.exp(-cum)
            e_end = jnp.exp(tot - cum)
            a_t = al * jnp.exp(cum - lw)
            r_t = r_c * jnp.exp(cum)
            a_st, r_st, v_st = stack2(a_t), stack2(r_t), stack2(v_c)
            b_n, k_n = be * e_neg, k_c * e_neg
            gram = _mm_nt(jnp.concatenate([a_st, r_st], axis=0),
                          jnp.concatenate([b_n, b_n, k_n, k_n], axis=0))
            a_ab = jnp.where(strict, gram[:LANES, :LANES], 0.0)
            a_ak = jnp.where(strict, gram[:LANES, LANES:], 0.0)
            a_rb = jnp.where(incl, gram[LANES:, :LANES], 0.0)
            a_rk = jnp.where(incl, gram[LANES:, LANES:], 0.0)
            a0 = jnp.where(b16, a_ab, 0.0)
            e1 = jnp.where(b32 & jnp.logical_not(b16), a_ab, 0.0)
            e2 = jnp.where(b32, 0.0, a_ab)
            p2 = _mm(a0, a0)
            p4 = _mm(p2, p2)
            p8 = _mm(p4, p4)
            t = eye - a0
            t = t + _mm(t, p2)
            t = t + _mm(t, p4)
            t = t + _mm(t, p8)
            t = t - _mm(_mm(t, e1), t)
            t = t - _mm(_mm(t, e2), t)
            akv = _mm(a_ak, v_st)
            w = _mm(t, jnp.concatenate([a_st, akv], axis=1))
            rbw = _mm(a_rb, w)
            rkv = _mm(a_rk, v_st)
            r_hat = fold(r_st - rbw[:, :LANES])
            y0 = fold(rkv - rbw[:, LANES:])
            ta_tv = jnp.concatenate([fold(w[:, :LANES]), fold(w[:, LANES:])], axis=1)
            bw = _mm_tn(be * e_end, ta_tv)
            kv = _mm_tn(k_c * e_end, v_c)
            m_p = jnp.where(same_head, bw[:, :LANES], 0.0)
            n_p = jnp.where(same_head, kv - bw[:, LANES:], 0.0)
            decay_col = jnp.exp(jnp.broadcast_to(tot, (LANES, LANES)).T)
            st = st_s[p]
            zz = _mm(jnp.concatenate([m_p, r_hat], axis=0), st)
            y_s[sl] = zz[LANES:] + y0
            st_s[p] = decay_col * st - zz[:LANES] + n_p
        return carry

    lax.fori_loop(0, tb // c_len, chunk_body, 0)

    y = y_s[...]
    mu = headsum(y) * (1.0 / RWKV_N)
    d = y - mu
    var = headsum(d * d) * (1.0 / RWKV_N)
    yn = d * lax.rsqrt(var + LNX_EPS) * gain_ref[...] + bias_ref[...]
    y_ref[...] = ((yn + bo_s[...].astype(F32)) * g_s[...].astype(F32)).astype(y_ref.dtype)


def _rwkv(zr, bsz, seq, mix, w_lora, w0, a0, k_k, k_a, r_k, gain, bias):
    tb = ROW_TILE
    nt = seq // tb
    row = lambda b, t: (b * nt + t, 0)
    const = lambda b, t: (0, 0)
    vec = pl.BlockSpec((1, RWKV_W), const)
    big = pltpu.VMEM((tb, RWKV_W), F32)
    return pl.pallas_call(
        _rwkv_kernel,
        grid=(bsz, nt),
        in_specs=[
            pl.BlockSpec((tb, ZR_W), row),
            pl.BlockSpec((1, ZR_W), const),
            pl.BlockSpec((LORA_PAD, 3 * RWKV_W), const),
            vec, vec, vec, vec, vec, vec, vec,
        ],
        out_specs=pl.BlockSpec((tb, RWKV_W), row),
        out_shape=jax.ShapeDtypeStruct((bsz * seq, RWKV_W), BF16),
        scratch_shapes=[
            pltpu.VMEM((8, ZR_W), F32),
            pltpu.VMEM((RWKV_W // LANES, LANES, LANES), F32),
            big, big, big, big, big, big, big, big, big,
        ],
        compiler_params=_params("parallel", "arbitrary"),
        name="rwkv7",
    )(zr, mix, w_lora, w0, a0, k_k, k_a, r_k, gain, bias)


def _rwkv2_kernel(z_ref, mix_ref, wl_ref, w0_ref, a0_ref, kk_ref, ka_ref, rk_ref, gain_ref,
                  bias_ref, y_ref, carry_s, st_s, g_s, bo_s, y_s, rt_s, dec_s,
                  ah0_s, ah1_s, bn0_s, bn1_s, kn0_s, kn1_s, vh0_s, vh1_s, be_s, ke_s,
                  mr_s, n_s):
    tb = z_ref.shape[0]
    n_pair = RWKV_W // LANES
    c_len = CHUNK
    n_chunk = tb // c_len

    @pl.when(pl.program_id(1) == 0)
    def _():
        carry_s[...] = jnp.zeros_like(carry_s)
        st_s[...] = jnp.zeros_like(st_s)

    group_chunks = 4
    group_rows = group_chunks * c_len
    n_piece = tb // group_rows
    rowq = lax.broadcasted_iota(jnp.int32, (group_rows, 1), 0)
    ri = lax.broadcasted_iota(jnp.int32, (LANES, LANES), 0)
    ci = lax.broadcasted_iota(jnp.int32, (LANES, LANES), 1)
    same_head = (ri // RWKV_N) == (ci // RWKV_N)
    ones_head = jnp.where(same_head, 1.0, 0.0).astype(BF16)
    head0 = (lax.broadcasted_iota(jnp.int32, (1, RWKV_W), 1) % LANES) < RWKV_N

    def headsum(x):
        return jnp.concatenate(
            [_split_dot(x[:, p * LANES:(p + 1) * LANES], ones_head) for p in range(n_pair)], axis=1)

    def prep(r0, top):
        rows = pl.ds(r0, group_rows)

        def shifted(c0, c1):
            zc = z_ref[rows, c0:c1]
            zp = jnp.where(rowq == 0, top[7:8, c0:c1], pltpu.roll(zc, 1, axis=0))
            return zc + (zp - zc) * mix_ref[:, c0:c1]

        zl = shifted(3 * RWKV_W, ZR_W)
        ll = lax.broadcasted_iota(jnp.int32, (1, LORA_PAD), 1)
        act = jnp.where(ll < R_DECAY, jnp.tanh(zl),
                        jnp.where(ll < R_DECAY + R_AAA, zl, _sigmoid(zl)))
        lo = jnp.dot(act.astype(BF16), wl_ref[...], preferred_element_type=F32)
        yield
        lw = -math.exp(-0.5) * _sigmoid(w0_ref[...] + lo[:, 0:RWKV_W])
        a = _sigmoid(a0_ref[...] + lo[:, RWKV_W:2 * RWKV_W])
        g_s[rows, :] = lo[:, 2 * RWKV_W:3 * RWKV_W].astype(BF16)
        yield

        r = shifted(0, RWKV_W)
        yield
        k = shifted(RWKV_W, 2 * RWKV_W)
        yield
        v = shifted(2 * RWKV_W, 3 * RWKV_W)
        yield
        kk = k * kk_ref[...]
        ss = headsum(kk * kk)
        alpha = kk * lax.rsqrt(jnp.maximum(ss, 1e-24))
        yield
        beta = alpha * a
        k2 = k * (1.0 + (a - 1.0) * ka_ref[...])
        bo_s[rows, :] = (headsum(r * k2 * rk_ref[...]) * v).astype(BF16)
        yield

        rin = rowq % c_len
        cum = lw
        for s in (1, 2, 4, 8, 16, 32):
            cum = cum + jnp.where(rin >= s, pltpu.roll(cum, s, axis=0), 0.0)
            yield
        tot = jnp.broadcast_to(cum.reshape(group_chunks, c_len, RWKV_W)[:, c_len - 1:c_len, :],
                               (group_chunks, c_len, RWKV_W)).reshape(group_rows, RWKV_W)
        e_neg = jnp.exp(-cum)
        e_end = jnp.exp(tot - cum)
        yield
        a_t = alpha * jnp.exp(cum - lw)
        r_t = r * jnp.exp(cum)
        rt_s[rows, :] = r_t
        slab0 = r0 // c_len * 8
        if not isinstance(slab0, int):
            slab0 = pl.multiple_of(slab0, group_chunks * 8)
        dec_s[pl.ds(slab0, group_chunks * 8), :] = jnp.exp(
            tot.reshape(group_chunks, c_len, RWKV_W)[:, :8, :].reshape(group_chunks * 8, RWKV_W))
        yield
        ah0_s[rows, :] = jnp.where(head0, a_t, 0.0).astype(BF16)
        ah1_s[rows, :] = jnp.where(head0, 0.0, a_t).astype(BF16)
        yield
        vh0_s[rows, :] = jnp.where(head0, v, 0.0).astype(BF16)
        vh1_s[rows, :] = jnp.where(head0, 0.0, v).astype(BF16)
        yield
        b_n = (beta * e_neg).astype(BF16)
        bn0_s[rows, :] = jnp.where(head0, b_n, jnp.zeros_like(b_n))
        bn1_s[rows, :] = jnp.where(head0, jnp.zeros_like(b_n), b_n)
        yield
        k_n = (k2 * e_neg).astype(BF16)
        kn0_s[rows, :] = jnp.where(head0, k_n, jnp.zeros_like(k_n))
        kn1_s[rows, :] = jnp.where(head0, jnp.zeros_like(k_n), k_n)
        yield
        be_s[rows, :] = (beta * e_end).astype(BF16)
        ke_s[rows, :] = (k2 * e_end).astype(BF16)

    ti = lax.broadcasted_iota(jnp.int32, (c_len, LANES), 0)
    si = lax.broadcasted_iota(jnp.int32, (c_len, LANES), 1) % c_len
    strict = si < ti
    incl = si <= ti
    b16 = (ti // 16) == (si // 16)
    b32 = (ti // 32) == (si // 32)
    eye = jnp.where(ti == si, 1.0, 0.0).astype(F32)
    lane_h0 = lax.broadcasted_iota(jnp.int32, (1, LANES), 1) < RWKV_N
    group = group_chunks * n_pair
    nb = range(group)

    def stack(x):
        xb = x.astype(BF16)
        zero = jnp.zeros_like(xb)
        return jnp.concatenate([jnp.where(lane_h0, xb, zero), jnp.where(lane_h0, zero, xb)], axis=0)

    def mmb(xs, ys):
        return [jnp.dot(x.astype(BF16), stack(y), preferred_element_type=F32) for x, y in zip(xs, ys)]

    def solve(it):
        idx = []
        for u in range(group):
            c = it * group_chunks + u // n_pair
            p = u % n_pair
            idx.append((c, c * n_pair + p,
                        (pl.ds(pl.multiple_of(c * c_len, c_len), c_len), slice(p * LANES, (p + 1) * LANES))))
        sls = [sl for _, _, sl in idx]
        a_st = [jnp.concatenate([ah0_s[sl], ah1_s[sl]], axis=0) for sl in sls]
        lhs = [jnp.concatenate([x[:c_len] + x[c_len:], rt_s[sl].astype(BF16)], axis=0)
               for x, sl in zip(a_st, sls)]
        rhs = [jnp.concatenate([bn0_s[sl], bn1_s[sl], kn0_s[sl], kn1_s[sl]], axis=0) for sl in sls]
        gram = [lax.dot_general(x, y, (((1,), (1,)), ((), ())), preferred_element_type=F32)
                for x, y in zip(lhs, rhs)]
        a_ab = [jnp.where(strict, gm[:c_len, :LANES], 0.0) for gm in gram]
        a_kr = [jnp.concatenate([jnp.where(strict, gm[:c_len, LANES:], 0.0),
                                 jnp.where(incl, gm[c_len:, LANES:], 0.0)], axis=0).astype(BF16)
                for gm in gram]
        a_rb = [jnp.where(incl, gm[c_len:, :LANES], 0.0) for gm in gram]
        yield
        a0 = [jnp.where(b16, x, 0.0).astype(BF16) for x in a_ab]
        e1 = [jnp.where(b32 & jnp.logical_not(b16), x, 0.0).astype(BF16) for x in a_ab]
        e2 = [jnp.where(b32, 0.0, x).astype(BF16) for x in a_ab]
        p2 = mmb(a0, a0)
        yield
        p4 = mmb(p2, p2)
        yield
        p8 = mmb(p4, p4)
        t = [eye - x.astype(F32) for x in a0]
        t = [x + y for x, y in zip(t, mmb(t, p2))]
        yield
        t = [x + y for x, y in zip(t, mmb(t, p4))]
        yield
        t = [x + y for x, y in zip(t, mmb(t, p8))]
        yield
        te = mmb(t, e1)
        yield
        t = [x - y for x, y in zip(t, mmb(te, t))]
        yield
        te = mmb(t, e2)
        yield
        t = [x - y for x, y in zip(t, mmb(te, t))]
        yield
        v_st = [jnp.concatenate([vh0_s[sl], vh1_s[sl]], axis=0) for sl in sls]
        av = [jnp.dot(x, y, preferred_element_type=F32) for x, y in zip(a_kr, v_st)]
        yield
        w = [jnp.dot(x.astype(BF16), jnp.concatenate([y, stack(z[:c_len])], axis=1), preferred_element_type=F32)
             for x, y, z in zip(t, a_st, av)]
        yield
        rbw = [jnp.dot(x.astype(BF16), jnp.concatenate([stack(y[:, :LANES]), stack(y[:, LANES:])], axis=1),
                       preferred_element_type=F32) for x, y in zip(a_rb, w)]
        yield
        bw = [_mm_tn(be_s[sl], x) for sl, x in zip(sls, w)]
        yield
        kv = [_mm_tn(ke_s[sl], vh0_s[sl] + vh1_s[sl]) for sl in sls]
        yield
        for u in nb:
            c, j, sl = idx[u]
            r_hat = rt_s[sl] - rbw[u][:, :LANES]
            m_p = jnp.where(same_head, bw[u][:, :LANES], 0.0)
            mr_s[j] = jnp.concatenate([m_p, r_hat], axis=0).astype(BF16)
            y_s[sl] = av[u][c_len:] - rbw[u][:, LANES:]
            n_s[j] = jnp.where(same_head, kv[u] - bw[u][:, LANES:], 0.0).astype(BF16)

    def scan(it):
        for u in range(group_chunks):
            c = it * group_chunks + u
            rows = pl.ds(pl.multiple_of(c * c_len, c_len), c_len)
            sts = [st_s[p] for p in range(n_pair)]
            zz = [jnp.dot(mr_s[c * n_pair + p], sts[p].astype(BF16), preferred_element_type=F32)
                  for p in range(n_pair)]
            yield
            for p in range(n_pair):
                j = c * n_pair + p
                sl = (rows, slice(p * LANES, (p + 1) * LANES))
                y_s[sl] = y_s[sl] + zz[p][LANES:]
                d = dec_s[pl.ds(pl.multiple_of(c * 8, 8), 8), p * LANES:(p + 1) * LANES]
                decay_col = jnp.broadcast_to(d[0:1, :], (LANES, LANES)).T
                st_s[p] = decay_col * sts[p] - zz[p][:LANES] + n_s[j].astype(F32)
            yield

    def below(it):
        r0 = pl.multiple_of((it + 1) * group_rows, group_rows)
        return prep(r0, z_ref[pl.ds(r0 - 8, 8), :])

    _run_interleaved(prep(0, carry_s[...]))
    carry_s[...] = z_ref[tb - 8:tb, :]
    _run_interleaved(solve(0), below(0))

    def piece_body(it, carry):
        _run_interleaved(solve(it), below(it), scan(it - 1))
        return carry

    lax.fori_loop(1, n_piece - 1, piece_body, 0)
    _run_interleaved(solve(n_piece - 1), scan(n_piece - 2))
    _run_interleaved(scan(n_piece - 1))

    y = y_s[...]
    mu = headsum(y) * (1.0 / RWKV_N)
    d = y - mu
    var = headsum(d * d) * (1.0 / RWKV_N)
    yn = d * lax.rsqrt(var + LNX_EPS) * gain_ref[...] + bias_ref[...]
    y_ref[...] = ((yn + bo_s[...].astype(F32)) * g_s[...].astype(F32)).astype(y_ref.dtype)


def _rwkv2(zr, bsz, seq, mix, w_lora, w0, a0, k_k, k_a, r_k, gain, bias):
    tb = ROW_TILE
    nt = seq // tb
    n_prob = (tb // CHUNK) * (RWKV_W // LANES)
    row = lambda b, t: (b * nt + t, 0)
    const = lambda b, t: (0, 0)
    vec = pl.BlockSpec((1, RWKV_W), const)
    big = pltpu.VMEM((tb, RWKV_W), F32)
    half = pltpu.VMEM((tb, RWKV_W), BF16)
    return pl.pallas_call(
        _rwkv2_kernel,
        grid=(bsz, nt),
        in_specs=[
            pl.BlockSpec((tb, ZR_W), row),
            pl.BlockSpec((1, ZR_W), const),
            pl.BlockSpec((LORA_PAD, 3 * RWKV_W), const),
            vec, vec, vec, vec, vec, vec, vec,
        ],
        out_specs=pl.BlockSpec((tb, RWKV_W), row),
        out_shape=jax.ShapeDtypeStruct((bsz * seq, RWKV_W), BF16),
        scratch_shapes=[
            pltpu.VMEM((8, ZR_W), F32),
            pltpu.VMEM((RWKV_W // LANES, LANES, LANES), F32),
            half, half, big, big,
            pltpu.VMEM((tb // CHUNK * 8, RWKV_W), F32),
            half, half, half, half, half, half, half, half, half, half,
            pltpu.VMEM((n_prob, LANES + CHUNK, LANES), BF16),
            pltpu.VMEM((n_prob, LANES, LANES), BF16),
        ],
        compiler_params=_params("parallel", "arbitrary"),
        name="rwkv7",
    )(zr, mix, w_lora, w0, a0, k_k, k_a, r_k, gain, bias)


def _attn_kernel(qi_tab, ki_tab, q_ref, k_ref, v_ref, lq1_ref, lk1_ref, lq2_ref, lk2_ref, sg_ref,
                 o_ref, q_s, m_s, acc_s, *, lambda_init, ratio):
    tq, tk = q_ref.shape[0], k_ref.shape[0]
    n_col = tk // LANES
    pidx = pl.program_id(1)
    qi = qi_tab[pidx]
    ki = ki_tab[pidx]
    nt = (((1,), (1,)), ((), ()))

    @pl.when(ki == 0)
    def _():
        map0 = lax.broadcasted_iota(jnp.int32, (1, LANES), 1) < DIFF_D
        for h in range(DIFF_H):
            q = q_ref[:, h * LANES:(h + 1) * LANES]
            zero = jnp.zeros_like(q)
            q_s[2 * h] = jnp.where(map0, q, zero)
            q_s[2 * h + 1] = jnp.where(map0, zero, q)
        m_s[...] = jnp.full_like(m_s, -jnp.inf)
        acc_s[...] = jnp.zeros_like(acc_s)

    def step(masked):
        if masked:
            rowp = qi * tq + lax.broadcasted_iota(jnp.int32, (tq, LANES), 0)
            colp = ki * tk + lax.broadcasted_iota(jnp.int32, (tq, LANES), 1)
            keep = [colp + j * LANES <= rowp for j in range(n_col)]
        ones = jnp.ones((tk, LANES), BF16)
        for h in range(DIFF_H):
            k = k_ref[:, h * LANES:(h + 1) * LANES]
            v_ext = jnp.concatenate([v_ref[:, h * LANES:(h + 1) * LANES], ones], axis=1)
            for c in range(2):
                i = 2 * h + c
                s = lax.dot_general(q_s[i], k, nt, preferred_element_type=F32)
                cols = [s[:, j * LANES:(j + 1) * LANES] for j in range(n_col)]
                if masked:
                    cols = [jnp.where(keep[j], cols[j], -jnp.inf) for j in range(n_col)]
                mx = cols[0]
                for j in range(1, n_col):
                    mx = jnp.maximum(mx, cols[j])
                m_old = m_s[i]
                m_new = jnp.maximum(m_old, jnp.max(mx, axis=-1, keepdims=True))
                corr = jnp.exp2(m_old - m_new)
                p = jnp.concatenate([jnp.exp2(cj - m_new).astype(BF16) for cj in cols], axis=1)
                pv = jnp.dot(p, v_ext, preferred_element_type=F32)
                acc_s[i] = jnp.concatenate([corr, corr], axis=1) * acc_s[i] + pv
                m_s[i] = m_new

    on_diag = ki >= qi * ratio

    @pl.when(on_diag)
    def _():
        step(True)

    @pl.when(jnp.logical_not(on_diag))
    def _():
        step(False)

    @pl.when(ki == (qi + 1) * ratio - 1)
    def _():
        lam = (jnp.exp(jnp.sum(lq1_ref[...] * lk1_ref[...], axis=-1, keepdims=True))
               - jnp.exp(jnp.sum(lq2_ref[...] * lk2_ref[...], axis=-1, keepdims=True))
               + lambda_init)
        for h in range(DIFF_H):
            a1 = acc_s[2 * h]
            a2 = acc_s[2 * h + 1]
            o = a1[:, :LANES] / a1[:, LANES:] - lam * (a2[:, :LANES] / a2[:, LANES:])
            o = _rms(o, sg_ref[...], SUBLN_EPS) * (1.0 - lambda_init)
            o_ref[:, h * LANES:(h + 1) * LANES] = o.astype(o_ref.dtype)


def _diff_attn(q, k, v, bsz, seq, lq1, lk1, lq2, lk2, sg, lambda_init):
    tq, tk = ATT_TQ, ATT_TK
    ratio = tq // tk
    nq, nk = seq // tq, seq // tk
    pairs = [(a, b) for a in range(nq) for b in range((a + 1) * ratio)]
    qi_tab = jnp.asarray(np.array([a for a, _ in pairs], np.int32))
    ki_tab = jnp.asarray(np.array([b for _, b in pairs], np.int32))
    qmap = lambda b, p, qt, kt: (b * nq + qt[p], 0)
    kmap = lambda b, p, qt, kt: (b * nk + kt[p], 0)
    const = lambda b, p, qt, kt: (0, 0)
    lam_spec = pl.BlockSpec((1, DIFF_D), const)
    grid_spec = pltpu.PrefetchScalarGridSpec(
        num_scalar_prefetch=2,
        grid=(bsz, len(pairs)),
        in_specs=[
            pl.BlockSpec((tq, DIFF_W), qmap),
            pl.BlockSpec((tk, DIFF_W), kmap),
            pl.BlockSpec((tk, DIFF_W), kmap),
            lam_spec, lam_spec, lam_spec, lam_spec,
            pl.BlockSpec((1, 2 * DIFF_D), const),
        ],
        out_specs=pl.BlockSpec((tq, DIFF_W), qmap),
        scratch_shapes=[
            pltpu.VMEM((2 * DIFF_H, tq, LANES), BF16),
            pltpu.VMEM((2 * DIFF_H, tq, LANES), F32),
            pltpu.VMEM((2 * DIFF_H, tq, 2 * LANES), F32),
        ],
    )
    return pl.pallas_call(
        functools.partial(_attn_kernel, lambda_init=lambda_init, ratio=ratio),
        grid_spec=grid_spec,
        out_shape=jax.ShapeDtypeStruct((bsz * seq, DIFF_W), BF16),
        compiler_params=_params("parallel", "arbitrary"),
        name="diff_attn",
    )(qi_tab, ki_tab, q, k, v, lq1, lk1, lq2, lk2, sg)


def _attn2_kernel(q_ref, k_ref, v_ref, lq1_ref, lk1_ref, lq2_ref, lk2_ref, sg_ref, o_ref,
                  q_s, m_s, acc_s, *, lambda_init, tk):
    tq = q_ref.shape[0]
    n_col = tk // LANES
    n_diag = tq // tk
    qi = pl.program_id(2)
    nt = (((1,), (1,)), ((), ()))

    map0 = (lax.broadcasted_iota(jnp.int32, (1, LANES), 1) % DIFF_D) < DIFF_D // 2
    q = q_ref[...]
    zero = jnp.zeros_like(q)
    q_s[0] = jnp.where(map0, q, zero)
    q_s[1] = jnp.where(map0, zero, q)
    ones = jnp.ones((tk, LANES), BF16)
    tri_keep = [lax.broadcasted_iota(jnp.int32, (tk, LANES), 1) + j * LANES
                <= lax.broadcasted_iota(jnp.int32, (tk, LANES), 0) for j in range(n_col)]

    def kv_step(kv_rows, r0, r1, masked, first=False):
        k = k_ref[kv_rows, :]
        v_ext = jnp.concatenate([v_ref[kv_rows, :], ones], axis=1)
        for c in range(2):
            s = lax.dot_general(q_s[c, r0:r1, :], k, nt, preferred_element_type=F32)
            cols = [s[:, j * LANES:(j + 1) * LANES] for j in range(n_col)]
            if masked:
                cols = [jnp.where(tri_keep[j], cols[j], -jnp.inf) for j in range(n_col)]
            mx = cols[0]
            for j in range(1, n_col):
                mx = jnp.maximum(mx, cols[j])
            m_new = jnp.broadcast_to(jnp.max(mx, axis=-1, keepdims=True), mx.shape)
            if not first:
                m_old = m_s[c, r0:r1, :]
                m_new = jnp.maximum(m_old, m_new)
            p = jnp.concatenate([jnp.exp2(cj - m_new).astype(BF16) for cj in cols], axis=1)
            pv = jnp.dot(p, v_ext, preferred_element_type=F32)
            if first:
                acc_s[c, r0:r1, :] = pv
            else:
                corr = jnp.exp2(m_old - m_new)
                acc_s[c, r0:r1, :] = jnp.concatenate([corr, corr], axis=1) * acc_s[c, r0:r1, :] + pv
            m_s[c, r0:r1, :] = m_new

    def full_body(j, carry):
        for u in range(n_diag):
            kv_rows = pl.ds(pl.multiple_of((j * n_diag + u) * tk, tk), tk)
            kv_step(kv_rows, 0, tq // 2, False)
            kv_step(kv_rows, tq // 2, tq, False)
        return carry

    for d in range(n_diag):
        kv_rows = pl.ds(pl.multiple_of((qi * n_diag + d) * tk, tk), tk)
        kv_step(kv_rows, d * tk, (d + 1) * tk, True, first=d == 0)
        if d + 1 < n_diag:
            kv_step(kv_rows, (d + 1) * tk, tq, False, first=d == 0)
    lax.fori_loop(0, qi, full_body, 0)

    lam = (jnp.exp(jnp.sum(lq1_ref[...] * lk1_ref[...], axis=-1, keepdims=True))
           - jnp.exp(jnp.sum(lq2_ref[...] * lk2_ref[...], axis=-1, keepdims=True))
           + lambda_init)
    a1 = acc_s[0]
    a2 = acc_s[1]
    o = a1[:, :LANES] / a1[:, LANES:] - lam * (a2[:, :LANES] / a2[:, LANES:])
    o_ref[...] = (_rms(o, sg_ref[...], SUBLN_EPS) * (1.0 - lambda_init)).astype(o_ref.dtype)


def _diff_attn2(q, k, v, bsz, seq, lq1, lk1, lq2, lk2, sg, lambda_init):
    tq = min(ATT_TQ, seq)
    tk = min(ATT_TK, tq)
    nq = seq // tq
    qmap = lambda b, h, i: (b * nq + i, h)
    kmap = lambda b, h, i: (b, h)
    const = lambda b, h, i: (0, 0)
    lam_spec = pl.BlockSpec((1, DIFF_D), const)
    return pl.pallas_call(
        functools.partial(_attn2_kernel, lambda_init=lambda_init, tk=tk),
        grid=(bsz, DIFF_H, nq),
        in_specs=[
            pl.BlockSpec((tq, LANES), qmap),
            pl.BlockSpec((seq, LANES), kmap),
            pl.BlockSpec((seq, LANES), kmap),
            lam_spec, lam_spec, lam_spec, lam_spec,
            pl.BlockSpec((1, 2 * DIFF_D), const),
        ],
        out_specs=pl.BlockSpec((tq, LANES), qmap),
        out_shape=jax.ShapeDtypeStruct((bsz * seq, DIFF_W), BF16),
        scratch_shapes=[
            pltpu.VMEM((2, tq, LANES), BF16),
            pltpu.VMEM((2, tq, LANES), F32),
            pltpu.VMEM((2, tq, 2 * LANES), F32),
        ],
        compiler_params=_params("parallel", "parallel", "arbitrary"),
        name="diff_attn",
    )(q, k, v, lq1, lk1, lq2, lk2, sg)


def _memkv_kernel(m_ref, g_ref, w_ref, o_ref):
    h = _rms(m_ref[...], g_ref[...], NORM_EPS).astype(BF16)
    for j in range(o_ref.shape[1] // 256):
        sl = slice(j * 256, (j + 1) * 256)
        o_ref[:, sl] = jnp.dot(h, w_ref[:, sl], preferred_element_type=F32).astype(o_ref.dtype)


def _memkv(mem2d, g, wkv):
    m = mem2d.shape[0]
    tm = MEM_LEN
    return pl.pallas_call(
        _memkv_kernel,
        grid=(m // tm,),
        in_specs=[
            pl.BlockSpec((tm, D_MODEL), lambda i: (i, 0)),
            pl.BlockSpec((1, D_MODEL), lambda i: (0, 0)),
            pl.BlockSpec((D_MODEL, 2 * D_MODEL), lambda i: (0, 0)),
        ],
        out_specs=pl.BlockSpec((tm, 2 * D_MODEL), lambda i: (i, 0)),
        out_shape=jax.ShapeDtypeStruct((m, 2 * D_MODEL), BF16),
        compiler_params=_params("parallel"),
        name="memkv",
    )(mem2d, g, wkv)


def _cross_kernel(x_ref, ya_ref, yb_ref, kv_ref, wout_ref, g_ref, wq_ref, wo_ref, o_ref, att_s):
    tm = x_ref.shape[0]
    halves = [slice(0, tm // 2), slice(tm // 2, tm)]
    nt = (((1,), (1,)), ((), ()))
    x1 = [x_ref[r, :] + (jnp.dot(ya_ref[r, :], wout_ref[0:RWKV_W, :], preferred_element_type=F32)
                         + jnp.dot(yb_ref[r, :], wout_ref[RWKV_W:, :], preferred_element_type=F32))
          for r in halves]
    hc = [_rms(x, g_ref[...], NORM_EPS).astype(BF16) for x in x1]
    scale = CROSS_D ** -0.5 * math.log2(math.e)
    q = [(jnp.dot(x, wq_ref[...], preferred_element_type=F32) * scale).astype(BF16) for x in hc]
    for h in range(CROSS_H):
        sl = slice(h * CROSS_D, (h + 1) * CROSS_D)
        kh = kv_ref[:, sl]
        vh = kv_ref[:, D_MODEL + h * CROSS_D:D_MODEL + (h + 1) * CROSS_D]
        s = [lax.dot_general(x[:, sl], kh, nt, preferred_element_type=F32) for x in q]
        p = [jnp.exp2(x - jnp.max(x, axis=-1, keepdims=True)) for x in s]
        o = [jnp.dot(x.astype(BF16), vh, preferred_element_type=F32) / jnp.sum(x, axis=-1, keepdims=True)
             for x in p]
        for r, x in zip(halves, o):
            att_s[r, sl] = x.astype(BF16)
    for r, x in zip(halves, x1):
        o_ref[r, :] = x + jnp.dot(att_s[r, :], wo_ref[...], preferred_element_type=F32)


def _cross(x2d, ya, yb, kv, bsz, seq, w_out, g, wq, wo):
    tm = PROJ_TILE
    nt = seq // tm
    row = lambda b, t: (b * nt + t, 0)
    const = lambda b, t: (0, 0)
    return pl.pallas_call(
        _cross_kernel,
        grid=(bsz, nt),
        in_specs=[
            pl.BlockSpec((tm, D_MODEL), row),
            pl.BlockSpec((tm, RWKV_W), row),
            pl.BlockSpec((tm, DIFF_W), row),
            pl.BlockSpec((MEM_LEN, 2 * D_MODEL), lambda b, t: (b, 0)),
            pl.BlockSpec((D_MODEL, D_MODEL), const, pipeline_mode=pl.Buffered(1)),
            pl.BlockSpec((1, D_MODEL), const),
            pl.BlockSpec((D_MODEL, D_MODEL), const, pipeline_mode=pl.Buffered(1)),
            pl.BlockSpec((D_MODEL, D_MODEL), const, pipeline_mode=pl.Buffered(1)),
        ],
        out_specs=pl.BlockSpec((tm, D_MODEL), row),
        out_shape=jax.ShapeDtypeStruct((bsz * seq, D_MODEL), F32),
        scratch_shapes=[pltpu.VMEM((tm, D_MODEL), BF16)],
        compiler_params=_params("parallel", "parallel"),
        name="outproj_cross",
    )(x2d, ya, yb, kv, w_out, g, wq, wo)


def _ffn_kernel(x_ref, g_ref, wg_ref, wv_ref, cw_ref, cb_ref, wd_ref, gf_ref, o_ref, carry_s, act_s, *, final):
    tm = x_ref.shape[0]
    n_ff = wg_ref.shape[0]

    @pl.when(pl.program_id(1) == 0)
    def _():
        carry_s[...] = jnp.zeros_like(carry_s)

    x = x_ref[...]
    h = _rms(x, g_ref[...], NORM_EPS).astype(BF16)

    def up(j):
        return (jnp.dot(h, wg_ref[j], preferred_element_type=F32),
                jnp.dot(h, wv_ref[j], preferred_element_type=F32))

    tf = wg_ref.shape[2]
    split = (n_ff + 1) // 2 * tf
    nxt = up(0)
    down = []
    for j in range(n_ff):
        gate, val = nxt
        if j + 1 < n_ff:
            nxt = up(j + 1)
        ext = jnp.concatenate([carry_s[j], gate], axis=0)
        g1 = ext[7:tm + 7, :]
        g2 = ext[6:tm + 6, :]
        carry_s[j] = gate[tm - 8:tm, :]
        cwh = 0.5 * cw_ref[j]
        hc = cwh[0:1, :] * g2 + cwh[1:2, :] * g1 + cwh[2:3, :] * gate + 0.5 * cb_ref[j]
        act_s[:, j * tf:(j + 1) * tf] = ((hc + hc * jnp.tanh(hc)) * val).astype(BF16)
        if (j + 1) * tf == split:
            down.append(jnp.dot(act_s[:, :split], wd_ref[:split, :], preferred_element_type=F32))
    down.append(jnp.dot(act_s[:, split:], wd_ref[split:, :], preferred_element_type=F32))
    out = x + (down[0] + down[1])
    o_ref[...] = _rms(out, gf_ref[...], NORM_EPS) if final else out


def _ffn(x2d, bsz, seq, g, wg, wv, cw, cb, wd, gf, final):
    tm = PROJ_TILE
    nt = seq // tm
    n_ff, _, tf = wg.shape
    row = lambda b, t: (b * nt + t, 0)
    c2 = lambda b, t: (0, 0)
    c3 = lambda b, t: (0, 0, 0)
    once = pl.Buffered(1)
    return pl.pallas_call(
        functools.partial(_ffn_kernel, final=final),
        grid=(bsz, nt),
        in_specs=[
            pl.BlockSpec((tm, D_MODEL), row),
            pl.BlockSpec((1, D_MODEL), c2),
            pl.BlockSpec((n_ff, D_MODEL, tf), c3, pipeline_mode=once),
            pl.BlockSpec((n_ff, D_MODEL, tf), c3, pipeline_mode=once),
            pl.BlockSpec((n_ff, 3, tf), c3),
            pl.BlockSpec((n_ff, 1, tf), c3),
            pl.BlockSpec((n_ff * tf, D_MODEL), c2, pipeline_mode=once),
            pl.BlockSpec((1, D_MODEL), c2),
        ],
        out_specs=pl.BlockSpec((tm, D_MODEL), row),
        out_shape=jax.ShapeDtypeStruct((bsz * seq, D_MODEL), F32),
        scratch_shapes=[pltpu.VMEM((n_ff, 8, tf), F32), pltpu.VMEM((tm, n_ff * tf), BF16)],
        compiler_params=_params("parallel", "arbitrary"),
        name="conv_ffn",
    )(x2d, g, wg, wv, cw, cb, wd, gf)


def _rope_perm():
    idx = np.empty((DIFF_W,), np.int32)
    half = DIFF_D // 2
    for h in range(DIFF_H):
        for c in range(2):
            for d in range(DIFF_D):
                idx[h * LANES + (d // half) * DIFF_D + c * half + d % half] = h * LANES + c * DIFF_D + d
    return idx


def _layer(x2d, pos2d, mem2d, bsz, seq, lambda_init, p):
    w_in = p["w_in"]
    pad = jnp.zeros((D_MODEL, LORA_PAD - (N_SHIFT - 3 * RWKV_W)), F32)
    w_r = jnp.concatenate([w_in[:, :N_SHIFT], pad], axis=1).astype(BF16)
    perm = _rope_perm()
    w_d = jnp.concatenate([w_in[:, N_SHIFT:N_SHIFT + DIFF_W][:, perm],
                           w_in[:, N_SHIFT + DIFF_W:N_SHIFT + 2 * DIFF_W][:, perm],
                           w_in[:, N_SHIFT + 2 * DIFF_W:]], axis=1).astype(BF16)
    mix = jnp.concatenate([p["shift_mix"], jnp.zeros((LORA_PAD - (N_SHIFT - 3 * RWKV_W),), F32)])[None, :]
    w_lora = jnp.zeros((LORA_PAD, 3 * RWKV_W), F32)
    w_lora = w_lora.at[0:R_DECAY, 0:RWKV_W].set(p["w_lora_up"])
    w_lora = w_lora.at[R_DECAY:R_DECAY + R_AAA, RWKV_W:2 * RWKV_W].set(p["a_lora_up"])
    w_lora = w_lora.at[R_DECAY + R_AAA:R_DECAY + R_AAA + R_GATE, 2 * RWKV_W:].set(p["g_lora_up"])
    w_lora = w_lora.astype(BF16)
    v512 = lambda a: a.reshape(1, RWKV_W)

    zr, q, k, v = _inproj(x2d, pos2d, p["norm_mix"][None, :], w_r, w_d)
    ya = _rwkv2(zr, bsz, seq, mix, w_lora, v512(p["w0"]), v512(p["a0"]), v512(p["k_k"]),
               v512(p["k_a"]), v512(p["r_k"]), v512(p["lnx_gain"]), v512(p["lnx_bias"]))
    yb = _diff_attn2(q, k, v, bsz, seq, p["lam_q1"][None, :], p["lam_k1"][None, :],
                    p["lam_q2"][None, :], p["lam_k2"][None, :], p["subln_gain"][None, :], lambda_init)
    kv = _memkv(mem2d, p["norm_mem"][None, :], p["wkv_c"].astype(BF16))
    x2 = _cross(x2d, ya, yb, kv, bsz, seq, p["w_out"].astype(BF16), p["norm_cross"][None, :],
                p["wq_c"].astype(BF16), p["wo_c"].astype(BF16))
    n_ff = D_FF // FF_TILE
    w_up = p["w_up"]
    wg = w_up[:, :D_FF].reshape(D_MODEL, n_ff, FF_TILE).transpose(1, 0, 2).astype(BF16)
    wv = w_up[:, D_FF:].reshape(D_MODEL, n_ff, FF_TILE).transpose(1, 0, 2).astype(BF16)
    cw = p["conv_w"].reshape(3, n_ff, FF_TILE).transpose(1, 0, 2)
    cb = p["conv_b"].reshape(n_ff, 1, FF_TILE)
    wd = p["w_down"].astype(BF16)
    return x2, (p["norm_ffn"][None, :], wg, wv, cw, cb, wd)


def kernel(x, mem, positions, norm_mix, w_in, shift_mix, w0, w_lora_up, a0, a_lora_up, g_lora_up, k_k, k_a, r_k, lnx_gain, lnx_bias, lam_q1, lam_k1, lam_q2, lam_k2, subln_gain, w_out, norm_cross, norm_mem, wq_c, wkv_c, wo_c, norm_ffn, w_up, conv_w, conv_b, w_down, norm_final):
    bsz, seq, _ = x.shape
    depth = norm_mix.shape[0]
    x2d = x.reshape(bsz * seq, D_MODEL)
    pos2d = positions.reshape(bsz * seq, 1)
    mem2d = mem.reshape(bsz * mem.shape[1], D_MODEL)
    stacked = dict(norm_mix=norm_mix, w_in=w_in, shift_mix=shift_mix, w0=w0, w_lora_up=w_lora_up,
                   a0=a0, a_lora_up=a_lora_up, g_lora_up=g_lora_up, k_k=k_k, k_a=k_a,
                   r_k=r_k.reshape(depth, RWKV_W), lnx_gain=lnx_gain, lnx_bias=lnx_bias,
                   lam_q1=lam_q1, lam_k1=lam_k1, lam_q2=lam_q2, lam_k2=lam_k2,
                   subln_gain=subln_gain, w_out=w_out, norm_cross=norm_cross, norm_mem=norm_mem,
                   wq_c=wq_c, wkv_c=wkv_c, wo_c=wo_c, norm_ffn=norm_ffn, w_up=w_up,
                   conv_w=conv_w, conv_b=conv_b, w_down=w_down)
    for l in range(depth):
        p = {name: a[l] for name, a in stacked.items()}
        lambda_init = 0.8 - 0.6 * math.exp(-0.3 * l)
        x2, (gn, wg, wv, cw, cb, wd) = _layer(x2d, pos2d, mem2d, bsz, seq, lambda_init, p)
        x2d = _ffn(x2, bsz, seq, gn, wg, wv, cw, cb, wd, norm_final[None, :], l == depth - 1)
    return x2d.reshape(bsz, seq, D_MODEL)
```

```python
import functools
import math

import numpy as np
import jax
import jax.numpy as jnp
from jax import lax
from jax.experimental import pallas as pl
from jax.experimental.pallas import tpu as pltpu

F32 = jnp.float32
BF16 = jnp.bfloat16

D_MODEL = 1024
RWKV_W = 512
RWKV_N = 64
R_DECAY = 32
R_AAA = 32
R_GATE = 96
DIFF_W = 512
DIFF_H = 4
DIFF_D = 64
N_SHIFT = 3 * RWKV_W + R_DECAY + R_AAA + R_GATE
MEM_LEN = 256
CROSS_H = 4
CROSS_D = D_MODEL // CROSS_H
D_FF = 2816
ROPE_THETA = 10000.0
NORM_EPS = 1e-6
LNX_EPS = 64e-5
SUBLN_EPS = 1e-5

LANES = 128
LORA_PAD = 256
ZR_W = 3 * RWKV_W + LORA_PAD
VMEM_LIMIT = 56 * 1024 * 1024

ROW_TILE = 1024
PROJ_TILE = 1024
CHUNK = 64
ATT_TQ = 2048
ATT_TK = 512
FF_TILE = 256
TRIG_BLOCKS = 4


def _mm(a, b):
    return jnp.dot(a.astype(BF16), b.astype(BF16), preferred_element_type=F32)


def _mm_nt(a, b):
    return lax.dot_general(a.astype(BF16), b.astype(BF16), (((1,), (1,)), ((), ())),
                           preferred_element_type=F32)


def _mm_tn(a, b):
    return lax.dot_general(a.astype(BF16), b.astype(BF16), (((0,), (0,)), ((), ())),
                           preferred_element_type=F32)


def _sigmoid(x):
    return 1.0 / (1.0 + jnp.exp(-x))


def _rms(x, g, eps):
    return x * lax.rsqrt(jnp.mean(x * x, axis=-1, keepdims=True) + eps) * g


def _split_dot(x, ones_b):
    hi = x.astype(BF16)
    lo = (x - hi.astype(F32)).astype(BF16)
    return (jnp.dot(hi, ones_b, preferred_element_type=F32)
            + jnp.dot(lo, ones_b, preferred_element_type=F32))


def _run_interleaved(*stage_gens):
    live = list(stage_gens)
    while live:
        for g in list(live):
            try:
                next(g)
            except StopIteration:
                live.remove(g)


def _params(*sem):
    return pltpu.CompilerParams(dimension_semantics=sem, vmem_limit_bytes=VMEM_LIMIT)


def _inproj_kernel(x_ref, pos_ref, g_ref, wr_ref, wd_ref, zr_ref, q_ref, k_ref, v_ref, trig_s):
    tm = x_ref.shape[0]
    lane = lax.broadcasted_iota(jnp.int32, (1, LANES), 1)
    freq = (lane % (DIFF_D // 2)).astype(F32)
    inv = jnp.exp(freq * (-2.0 / DIFF_D * math.log(ROPE_THETA)))
    scale = DIFF_D ** -0.5 * math.log2(math.e)

    def rope_tables():
        n_blk = TRIG_BLOCKS
        rb = tm // n_blk
        for i in range(n_blk):
            rows = slice(i * rb, (i + 1) * rb)
            ang = pos_ref[rows, :].astype(F32) * inv
            cos = jnp.cos(ang)
            sin = jnp.where(lane < LANES // 2, -jnp.sin(ang), jnp.sin(ang))
            trig_s[0, rows, :] = cos
            trig_s[1, rows, :] = sin
            trig_s[2, rows, :] = cos * scale
            trig_s[3, rows, :] = sin * scale
            yield

    h = _rms(x_ref[...], g_ref[...], NORM_EPS).astype(BF16)

    def rwkv_slab():
        n_blk = ZR_W // 256
        for j in range(n_blk):
            sl = slice(j * 256, (j + 1) * 256)
            lhs = h
            t = j - (n_blk - TRIG_BLOCKS)
            if t >= 0:
                r = (t + 1) * (tm // TRIG_BLOCKS) - 8
                lhs = jnp.where(trig_s[0, r:r + 8, 0:1][0:1, :] < 2.0, h, jnp.zeros_like(h))
            zr_ref[:, sl] = jnp.dot(lhs, wr_ref[:, sl], preferred_element_type=F32)
            yield

    _run_interleaved(rope_tables(), rwkv_slab())
    cos, sin, cos_q, sin_q = trig_s[0], trig_s[1], trig_s[2], trig_s[3]

    def rope(xb, c, s):
        return xb * c + pltpu.roll(xb, LANES // 2, axis=1) * s

    for j in range(DIFF_W // 256):
        zq = jnp.dot(h, wd_ref[:, j * 256:(j + 1) * 256], preferred_element_type=F32)
        zk = jnp.dot(h, wd_ref[:, DIFF_W + j * 256:DIFF_W + (j + 1) * 256],
                     preferred_element_type=F32)
        zv = jnp.dot(h, wd_ref[:, 2 * DIFF_W + j * 256:2 * DIFF_W + (j + 1) * 256],
                     preferred_element_type=F32)
        for u in range(2):
            c0 = j * 256 + u * LANES
            q_ref[:, c0:c0 + LANES] = rope(zq[:, u * LANES:(u + 1) * LANES], cos_q, sin_q).astype(BF16)
            k_ref[:, c0:c0 + LANES] = rope(zk[:, u * LANES:(u + 1) * LANES], cos, sin).astype(BF16)
        v_ref[:, j * 256:(j + 1) * 256] = zv.astype(BF16)


def _inproj(x2d, pos2d, g, w_r, w_d):
    m = x2d.shape[0]
    tm = PROJ_TILE
    row = lambda i: (i, 0)
    const = lambda i: (0, 0)
    return pl.pallas_call(
        _inproj_kernel,
        grid=(m // tm,),
        in_specs=[
            pl.BlockSpec((tm, D_MODEL), row),
            pl.BlockSpec((tm, 1), row),
            pl.BlockSpec((1, D_MODEL), const),
            pl.BlockSpec((D_MODEL, ZR_W), const, pipeline_mode=pl.Buffered(1)),
            pl.BlockSpec((D_MODEL, 3 * DIFF_W), const, pipeline_mode=pl.Buffered(1)),
        ],
        out_specs=[
            pl.BlockSpec((tm, ZR_W), row),
            pl.BlockSpec((tm, DIFF_W), row),
            pl.BlockSpec((tm, DIFF_W), row),
            pl.BlockSpec((tm, DIFF_W), row),
        ],
        out_shape=[
            jax.ShapeDtypeStruct((m, ZR_W), F32),
            jax.ShapeDtypeStruct((m, DIFF_W), BF16),
            jax.ShapeDtypeStruct((m, DIFF_W), BF16),
            jax.ShapeDtypeStruct((m, DIFF_W), BF16),
        ],
        scratch_shapes=[pltpu.VMEM((4, tm, LANES), F32)],
        compiler_params=_params("parallel"),
        name="inproj",
    )(x2d, pos2d, g, w_r, w_d)


def _rwkv_kernel(z_ref, mix_ref, wl_ref, w0_ref, a0_ref, kk_ref, ka_ref, rk_ref, gain_ref,
                 bias_ref, y_ref, carry_s, st_s, r_s, lw_s, k_s, v_s, al_s, be_s, g_s, bo_s, y_s):
    tb = z_ref.shape[0]
    n_pair = RWKV_W // LANES

    @pl.when(pl.program_id(1) == 0)
    def _():
        carry_s[...] = jnp.zeros_like(carry_s)
        st_s[...] = jnp.zeros_like(st_s)

    row = lax.broadcasted_iota(jnp.int32, (tb, 1), 0)

    def shifted(c0, c1):
        zc = z_ref[:, c0:c1]
        zp = jnp.where(row == 0, carry_s[7:8, c0:c1], pltpu.roll(zc, 1, axis=0))
        return zc + (zp - zc) * mix_ref[:, c0:c1]

    ri = lax.broadcasted_iota(jnp.int32, (LANES, LANES), 0)
    ci = lax.broadcasted_iota(jnp.int32, (LANES, LANES), 1)
    same_head = (ri // RWKV_N) == (ci // RWKV_N)
    ones_head = jnp.where(same_head, 1.0, 0.0).astype(BF16)

    def headsum(x):
        return jnp.concatenate(
            [_split_dot(x[:, p * LANES:(p + 1) * LANES], ones_head) for p in range(n_pair)], axis=1)

    zl = shifted(3 * RWKV_W, ZR_W)
    ll = lax.broadcasted_iota(jnp.int32, (1, LORA_PAD), 1)
    act = jnp.where(ll < R_DECAY, jnp.tanh(zl),
                    jnp.where(ll < R_DECAY + R_AAA, zl, _sigmoid(zl)))
    lo = jnp.dot(act.astype(BF16), wl_ref[...], preferred_element_type=F32)
    lw_s[...] = -math.exp(-0.5) * _sigmoid(w0_ref[...] + lo[:, 0:RWKV_W])
    a = _sigmoid(a0_ref[...] + lo[:, RWKV_W:2 * RWKV_W])
    g_s[...] = lo[:, 2 * RWKV_W:3 * RWKV_W]

    r = shifted(0, RWKV_W)
    k = shifted(RWKV_W, 2 * RWKV_W)
    v = shifted(2 * RWKV_W, 3 * RWKV_W)
    carry_s[...] = z_ref[tb - 8:tb, :]
    r_s[...] = r
    v_s[...] = v
    kk = k * kk_ref[...]
    ss = headsum(kk * kk)
    alpha = kk * lax.rsqrt(jnp.maximum(ss, 1e-24))
    al_s[...] = alpha
    be_s[...] = alpha * a
    k2 = k * (1.0 + (a - 1.0) * ka_ref[...])
    k_s[...] = k2
    bo_s[...] = headsum(r * k2 * rk_ref[...]) * v

    c_len = CHUNK
    tri = jnp.where(lax.broadcasted_iota(jnp.int32, (c_len, c_len), 0)
                    >= lax.broadcasted_iota(jnp.int32, (c_len, c_len), 1), 1.0, 0.0).astype(BF16)
    lane = lax.broadcasted_iota(jnp.int32, (1, LANES), 1)
    head0 = lane < RWKV_N
    strict = same_head & (ri > ci)
    incl = same_head & (ri >= ci)
    b16 = (ri // 16) == (ci // 16)
    b32 = (ri // 32) == (ci // 32)
    eye = jnp.where(ri == ci, 1.0, 0.0).astype(F32)

    def stack2(x):
        return jnp.concatenate([jnp.where(head0, x, 0.0), jnp.where(head0, 0.0, x)], axis=0)

    def fold(x):
        return x[:c_len] + x[c_len:]

    def chunk_body(c, carry):
        r0 = pl.multiple_of(c * c_len, c_len)
        for p in range(n_pair):
            sl = (pl.ds(r0, c_len), slice(p * LANES, (p + 1) * LANES))
            r_c, lw, k_c, v_c, al, be = r_s[sl], lw_s[sl], k_s[sl], v_s[sl], al_s[sl], be_s[sl]
            hi = lw.astype(BF16)
            r1 = lw - hi.astype(F32)
            mid = r1.astype(BF16)
            low = (r1 - mid.astype(F32)).astype(BF16)
            cum = (jnp.dot(tri, hi, preferred_element_type=F32)
                   + jnp.dot(tri, mid, preferred_element_type=F32)
                   + jnp.dot(tri, low, preferred_element_type=F32))
            tot = cum[c_len - 1:c_len, :]
            e_neg = jnp.exp(-cum)
            e_end = jnp.exp(tot - cum)
            a_t = al * jnp.exp(cum - lw)
            r_t = r_c * jnp.exp(cum)
            a_st, r_st, v_st = stack2(a_t), stack2(r_t), stack2(v_c)
            b_n, k_n = be * e_neg, k_c * e_neg
            gram = _mm_nt(jnp.concatenate([a_st, r_st], axis=0),
                          jnp.concatenate([b_n, b_n, k_n, k_n], axis=0))
            a_ab = jnp.where(strict, gram[:LANES, :LANES], 0.0)
            a_ak = jnp.where(strict, gram[:LANES, LANES:], 0.0)
            a_rb = jnp.where(incl, gram[LANES:, :LANES], 0.0)
            a_rk = jnp.where(incl, gram[LANES:, LANES:], 0.0)
            a0 = jnp.where(b16, a_ab, 0.0)
            e1 = jnp.where(b32 & jnp.logical_not(b16), a_ab, 0.0)
            e2 = jnp.where(b32, 0.0, a_ab)
            p2 = _mm(a0, a0)
            p4 = _mm(p2, p2)
            p8 = _mm(p4, p4)
            t = eye - a0
            t = t + _mm(t, p2)
            t = t + _mm(t, p4)
            t = t + _mm(t, p8)
            t = t - _mm(_mm(t, e1), t)
            t = t - _mm(_mm(t, e2), t)
            akv = _mm(a_ak, v_st)
            w = _mm(t, jnp.concatenate([a_st, akv], axis=1))
            rbw = _mm(a_rb, w)
            rkv = _mm(a_rk, v_st)
            r_hat = fold(r_st - rbw[:, :LANES])
            y0 = fold(rkv - rbw[:, LANES:])
            ta_tv = jnp.concatenate([fold(w[:, :LANES]), fold(w[:, LANES:])], axis=1)
            bw = _mm_tn(be * e_end, ta_tv)
            kv = _mm_tn(k_c * e_end, v_c)
            m_p = jnp.where(same_head, bw[:, :LANES], 0.0)
            n_p = jnp.where(same_head, kv - bw[:, LANES:], 0.0)
            decay_col = jnp.exp(jnp.broadcast_to(tot, (LANES, LANES)).T)
            st = st_s[p]
            zz = _mm(jnp.concatenate([m_p, r_hat], axis=0), st)
            y_s[sl] = zz[LANES:] + y0
            st_s[p] = decay_col * st - zz[:LANES] + n_p
        return carry

    lax.fori_loop(0, tb // c_len, chunk_body, 0)

    y = y_s[...]
    mu = headsum(y) * (1.0 / RWKV_N)
    d = y - mu
    var = headsum(d * d) * (1.0 / RWKV_N)
    yn = d * lax.rsqrt(var + LNX_EPS) * gain_ref[...] + bias_ref[...]
    y_ref[...] = ((yn + bo_s[...].astype(F32)) * g_s[...].astype(F32)).astype(y_ref.dtype)


def _rwkv(zr, bsz, seq, mix, w_lora, w0, a0, k_k, k_a, r_k, gain, bias):
    tb = ROW_TILE
    nt = seq // tb
    row = lambda b, t: (b * nt + t, 0)
    const = lambda b, t: (0, 0)
    vec = pl.BlockSpec((1, RWKV_W), const)
    big = pltpu.VMEM((tb, RWKV_W), F32)
    return pl.pallas_call(
        _rwkv_kernel,
        grid=(bsz, nt),
        in_specs=[
            pl.BlockSpec((tb, ZR_W), row),
            pl.BlockSpec((1, ZR_W), const),
            pl.BlockSpec((LORA_PAD, 3 * RWKV_W), const),
            vec, vec, vec, vec, vec, vec, vec,
        ],
        out_specs=pl.BlockSpec((tb, RWKV_W), row),
        out_shape=jax.ShapeDtypeStruct((bsz * seq, RWKV_W), BF16),
        scratch_shapes=[
            pltpu.VMEM((8, ZR_W), F32),
            pltpu.VMEM((RWKV_W // LANES, LANES, LANES), F32),
            big, big, big, big, big, big, big, big, big,
        ],
        compiler_params=_params("parallel", "arbitrary"),
        name="rwkv7",
    )(zr, mix, w_lora, w0, a0, k_k, k_a, r_k, gain, bias)


def _rwkv2_kernel(z_ref, mix_ref, wl_ref, w0_ref, a0_ref, kk_ref, ka_ref, rk_ref, gain_ref,
                  bias_ref, y_ref, carry_s, st_s, g_s, bo_s, y_s, rt_s, dec_s,
                  ah0_s, ah1_s, bn0_s, bn1_s, kn0_s, kn1_s, vh0_s, vh1_s, be_s, ke_s,
                  mr_s, n_s):
    tb = z_ref.shape[0]
    n_pair = RWKV_W // LANES
    c_len = CHUNK
    n_chunk = tb // c_len

    @pl.when(pl.program_id(1) == 0)
    def _():
        carry_s[...] = jnp.zeros_like(carry_s)
        st_s[...] = jnp.zeros_like(st_s)

    group_chunks = 4
    group_rows = group_chunks * c_len
    n_piece = tb // group_rows
    rowq = lax.broadcasted_iota(jnp.int32, (group_rows, 1), 0)
    ri = lax.broadcasted_iota(jnp.int32, (LANES, LANES), 0)
    ci = lax.broadcasted_iota(jnp.int32, (LANES, LANES), 1)
    same_head = (ri // RWKV_N) == (ci // RWKV_N)
    ones_head = jnp.where(same_head, 1.0, 0.0).astype(BF16)
    head0 = (lax.broadcasted_iota(jnp.int32, (1, RWKV_W), 1) % LANES) < RWKV_N

    def headsum(x):
        return jnp.concatenate(
            [_split_dot(x[:, p * LANES:(p + 1) * LANES], ones_head) for p in range(n_pair)], axis=1)

    def prep(r0, top):
        rows = pl.ds(r0, group_rows)

        def shifted(c0, c1):
            zc = z_ref[rows, c0:c1]
            zp = jnp.where(rowq == 0, top[7:8, c0:c1], pltpu.roll(zc, 1, axis=0))
            return zc + (zp - zc) * mix_ref[:, c0:c1]

        zl = shifted(3 * RWKV_W, ZR_W)
        ll = lax.broadcasted_iota(jnp.int32, (1, LORA_PAD), 1)
        act = jnp.where(ll < R_DECAY, jnp.tanh(zl),
                        jnp.where(ll < R_DECAY + R_AAA, zl, _sigmoid(zl)))
        lo = jnp.dot(act.astype(BF16), wl_ref[...], preferred_element_type=F32)
        yield
        lw = -math.exp(-0.5) * _sigmoid(w0_ref[...] + lo[:, 0:RWKV_W])
        a = _sigmoid(a0_ref[...] + lo[:, RWKV_W:2 * RWKV_W])
        g_s[rows, :] = lo[:, 2 * RWKV_W:3 * RWKV_W].astype(BF16)
        yield

        r = shifted(0, RWKV_W)
        yield
        k = shifted(RWKV_W, 2 * RWKV_W)
        yield
        v = shifted(2 * RWKV_W, 3 * RWKV_W)
        yield
        kk = k * kk_ref[...]
        ss = headsum(kk * kk)
        alpha = kk * lax.rsqrt(jnp.maximum(ss, 1e-24))
        yield
        beta = alpha * a
        k2 = k * (1.0 + (a - 1.0) * ka_ref[...])
        bo_s[rows, :] = (headsum(r * k2 * rk_ref[...]) * v).astype(BF16)
        yield

        rin = rowq % c_len
        cum = lw
        for s in (1, 2, 4, 8, 16, 32):
            cum = cum + jnp.where(rin >= s, pltpu.roll(cum, s, axis=0), 0.0)
            yield
        tot = jnp.broadcast_to(cum.reshape(group_chunks, c_len, RWKV_W)[:, c_len - 1:c_len, :],
                               (group_chunks, c_len, RWKV_W)).reshape(group_rows, RWKV_W)
        e_neg = jnp.exp(-cum)
        e_end = jnp.exp(tot - cum)
        yield
        a_t = alpha * jnp.exp(cum - lw)
        r_t = r * jnp.exp(cum)
        rt_s[rows, :] = r_t
        slab0 = r0 // c_len * 8
        if not isinstance(slab0, int):
            slab0 = pl.multiple_of(slab0, group_chunks * 8)
        dec_s[pl.ds(slab0, group_chunks * 8), :] = jnp.exp(
            tot.reshape(group_chunks, c_len, RWKV_W)[:, :8, :].reshape(group_chunks * 8, RWKV_W))
        yield
        ah0_s[rows, :] = jnp.where(head0, a_t, 0.0).astype(BF16)
        ah1_s[rows, :] = jnp.where(head0, 0.0, a_t).astype(BF16)
        yield
        vh0_s[rows, :] = jnp.where(head0, v, 0.0).astype(BF16)
        vh1_s[rows, :] = jnp.where(head0, 0.0, v).astype(BF16)
        yield
        b_n = (beta * e_neg).astype(BF16)
        bn0_s[rows, :] = jnp.where(head0, b_n, jnp.zeros_like(b_n))
        bn1_s[rows, :] = jnp.where(head0, jnp.zeros_like(b_n), b_n)
        yield
        k_n = (k2 * e_neg).astype(BF16)
        kn0_s[rows, :] = jnp.where(head0, k_n, jnp.zeros_like(k_n))
        kn1_s[rows, :] = jnp.where(head0, jnp.zeros_like(k_n), k_n)
        yield
        be_s[rows, :] = (beta * e_end).astype(BF16)
        ke_s[rows, :] = (k2 * e_end).astype(BF16)

    ti = lax.broadcasted_iota(jnp.int32, (c_len, LANES), 0)
    si = lax.broadcasted_iota(jnp.int32, (c_len, LANES), 1) % c_len
    strict = si < ti
    incl = si <= ti
    b16 = (ti // 16) == (si // 16)
    b32 = (ti // 32) == (si // 32)
    eye = jnp.where(ti == si, 1.0, 0.0).astype(F32)
    lane_h0 = lax.broadcasted_iota(jnp.int32, (1, LANES), 1) < RWKV_N
    group = group_chunks * n_pair
    nb = range(group)

    def stack(x):
        xb = x.astype(BF16)
        zero = jnp.zeros_like(xb)
        return jnp.concatenate([jnp.where(lane_h0, xb, zero), jnp.where(lane_h0, zero, xb)], axis=0)

    def mmb(xs, ys):
        return [jnp.dot(x.astype(BF16), stack(y), preferred_element_type=F32) for x, y in zip(xs, ys)]

    def solve(it):
        idx = []
        for u in range(group):
            c = it * group_chunks + u // n_pair
            p = u % n_pair
            idx.append((c, c * n_pair + p,
                        (pl.ds(pl.multiple_of(c * c_len, c_len), c_len), slice(p * LANES, (p + 1) * LANES))))
        sls = [sl for _, _, sl in idx]
        a_st = [jnp.concatenate([ah0_s[sl], ah1_s[sl]], axis=0) for sl in sls]
        lhs = [jnp.concatenate([x[:c_len] + x[c_len:], rt_s[sl].astype(BF16)], axis=0)
               for x, sl in zip(a_st, sls)]
        rhs = [jnp.concatenate([bn0_s[sl], bn1_s[sl], kn0_s[sl], kn1_s[sl]], axis=0) for sl in sls]
        gram = [lax.dot_general(x, y, (((1,), (1,)), ((), ())), preferred_element_type=F32)
                for x, y in zip(lhs, rhs)]
        a_ab = [jnp.where(strict, gm[:c_len, :LANES], 0.0) for gm in gram]
        a_kr = [jnp.concatenate([jnp.where(strict, gm[:c_len, LANES:], 0.0),
                                 jnp.where(incl, gm[c_len:, LANES:], 0.0)], axis=0).astype(BF16)
                for gm in gram]
        a_rb = [jnp.where(incl, gm[c_len:, :LANES], 0.0) for gm in gram]
        yield
        a0 = [jnp.where(b16, x, 0.0).astype(BF16) for x in a_ab]
        e1 = [jnp.where(b32 & jnp.logical_not(b16), x, 0.0).astype(BF16) for x in a_ab]
        e2 = [jnp.where(b32, 0.0, x).astype(BF16) for x in a_ab]
        p2 = mmb(a0, a0)
        yield
        p4 = mmb(p2, p2)
        yield
        p8 = mmb(p4, p4)
        t = [eye - x.astype(F32) for x in a0]
        t = [x + y for x, y in zip(t, mmb(t, p2))]
        yield
        t = [x + y for x, y in zip(t, mmb(t, p4))]
        yield
        t = [x + y for x, y in zip(t, mmb(t, p8))]
        yield
        te = mmb(t, e1)
        yield
        t = [x - y for x, y in zip(t, mmb(te, t))]
        yield
        te = mmb(t, e2)
        yield
        t = [x - y for x, y in zip(t, mmb(te, t))]
        yield
        v_st = [jnp.concatenate([vh0_s[sl], vh1_s[sl]], axis=0) for sl in sls]
        av = [jnp.dot(x, y, preferred_element_type=F32) for x, y in zip(a_kr, v_st)]
        yield
        w = [jnp.dot(x.astype(BF16), jnp.concatenate([y, stack(z[:c_len])], axis=1), preferred_element_type=F32)
             for x, y, z in zip(t, a_st, av)]
        yield
        rbw = [jnp.dot(x.astype(BF16), jnp.concatenate([stack(y[:, :LANES]), stack(y[:, LANES:])], axis=1),
                       preferred_element_type=F32) for x, y in zip(a_rb, w)]
        yield
        bw = [_mm_tn(be_s[sl], x) for sl, x in zip(sls, w)]
        yield
        kv = [_mm_tn(ke_s[sl], vh0_s[sl] + vh1_s[sl]) for sl in sls]
        yield
        for u in nb:
            c, j, sl = idx[u]
            r_hat = rt_s[sl] - rbw[u][:, :LANES]
            m_p = jnp.where(same_head, bw[u][:, :LANES], 0.0)
            mr_s[j] = jnp.concatenate([m_p, r_hat], axis=0).astype(BF16)
            y_s[sl] = av[u][c_len:] - rbw[u][:, LANES:]
            n_s[j] = jnp.where(same_head, kv[u] - bw[u][:, LANES:], 0.0).astype(BF16)

    def scan(it):
        for u in range(group_chunks):
            c = it * group_chunks + u
            rows = pl.ds(pl.multiple_of(c * c_len, c_len), c_len)
            sts = [st_s[p] for p in range(n_pair)]
            zz = [jnp.dot(mr_s[c * n_pair + p], sts[p].astype(BF16), preferred_element_type=F32)
                  for p in range(n_pair)]
            yield
            for p in range(n_pair):
                j = c * n_pair + p
                sl = (rows, slice(p * LANES, (p + 1) * LANES))
                y_s[sl] = y_s[sl] + zz[p][LANES:]
                d = dec_s[pl.ds(pl.multiple_of(c * 8, 8), 8), p * LANES:(p + 1) * LANES]
                decay_col = jnp.broadcast_to(d[0:1, :], (LANES, LANES)).T
                st_s[p] = decay_col * sts[p] - zz[p][:LANES] + n_s[j].astype(F32)
            yield

    def below(it):
        r0 = pl.multiple_of((it + 1) * group_rows, group_rows)
        return prep(r0, z_ref[pl.ds(r0 - 8, 8), :])

    _run_interleaved(prep(0, carry_s[...]))
    carry_s[...] = z_ref[tb - 8:tb, :]
    _run_interleaved(solve(0), below(0))

    def piece_body(it, carry):
        _run_interleaved(solve(it), below(it), scan(it - 1))
        return carry

    lax.fori_loop(1, n_piece - 1, piece_body, 0)
    _run_interleaved(solve(n_piece - 1), scan(n_piece - 2))
    _run_interleaved(scan(n_piece - 1))

    y = y_s[...]
    mu = headsum(y) * (1.0 / RWKV_N)
    d = y - mu
    var = headsum(d * d) * (1.0 / RWKV_N)
    yn = d * lax.rsqrt(var + LNX_EPS) * gain_ref[...] + bias_ref[...]
    y_ref[...] = ((yn + bo_s[...].astype(F32)) * g_s[...].astype(F32)).astype(y_ref.dtype)


def _rwkv2(zr, bsz, seq, mix, w_lora, w0, a0, k_k, k_a, r_k, gain, bias):
    tb = ROW_TILE
    nt = seq // tb
    n_prob = (tb // CHUNK) * (RWKV_W // LANES)
    row = lambda b, t: (b * nt + t, 0)
    const = lambda b, t: (0, 0)
    vec = pl.BlockSpec((1, RWKV_W), const)
    big = pltpu.VMEM((tb, RWKV_W), F32)
    half = pltpu.VMEM((tb, RWKV_W), BF16)
    return pl.pallas_call(
        _rwkv2_kernel,
        grid=(bsz, nt),
        in_specs=[
            pl.BlockSpec((tb, ZR_W), row),
            pl.BlockSpec((1, ZR_W), const),
            pl.BlockSpec((LORA_PAD, 3 * RWKV_W), const),
            vec, vec, vec, vec, vec, vec, vec,
        ],
        out_specs=pl.BlockSpec((tb, RWKV_W), row),
        out_shape=jax.ShapeDtypeStruct((bsz * seq, RWKV_W), BF16),
        scratch_shapes=[
            pltpu.VMEM((8, ZR_W), F32),
            pltpu.VMEM((RWKV_W // LANES, LANES, LANES), F32),
            half, half, big, big,
            pltpu.VMEM((tb // CHUNK * 8, RWKV_W), F32),
            half, half, half, half, half, half, half, half, half, half,
            pltpu.VMEM((n_prob, LANES + CHUNK, LANES), BF16),
            pltpu.VMEM((n_prob, LANES, LANES), BF16),
        ],
        compiler_params=_params("parallel", "arbitrary"),
        name="rwkv7",
    )(zr, mix, w_lora, w0, a0, k_k, k_a, r_k, gain, bias)


def _attn_kernel(qi_tab, ki_tab, q_ref, k_ref, v_ref, lq1_ref, lk1_ref, lq2_ref, lk2_ref, sg_ref,
                 o_ref, q_s, m_s, acc_s, *, lambda_init, ratio):
    tq, tk = q_ref.shape[0], k_ref.shape[0]
    n_col = tk // LANES
    pidx = pl.program_id(1)
    qi = qi_tab[pidx]
    ki = ki_tab[pidx]
    nt = (((1,), (1,)), ((), ()))

    @pl.when(ki == 0)
    def _():
        map0 = lax.broadcasted_iota(jnp.int32, (1, LANES), 1) < DIFF_D
        for h in range(DIFF_H):
            q = q_ref[:, h * LANES:(h + 1) * LANES]
            zero = jnp.zeros_like(q)
            q_s[2 * h] = jnp.where(map0, q, zero)
            q_s[2 * h + 1] = jnp.where(map0, zero, q)
        m_s[...] = jnp.full_like(m_s, -jnp.inf)
        acc_s[...] = jnp.zeros_like(acc_s)

    def step(masked):
        if masked:
            rowp = qi * tq + lax.broadcasted_iota(jnp.int32, (tq, LANES), 0)
            colp = ki * tk + lax.broadcasted_iota(jnp.int32, (tq, LANES), 1)
            keep = [colp + j * LANES <= rowp for j in range(n_col)]
        ones = jnp.ones((tk, LANES), BF16)
        for h in range(DIFF_H):
            k = k_ref[:, h * LANES:(h + 1) * LANES]
            v_ext = jnp.concatenate([v_ref[:, h * LANES:(h + 1) * LANES], ones], axis=1)
            for c in range(2):
                i = 2 * h + c
                s = lax.dot_general(q_s[i], k, nt, preferred_element_type=F32)
                cols = [s[:, j * LANES:(j + 1) * LANES] for j in range(n_col)]
                if masked:
                    cols = [jnp.where(keep[j], cols[j], -jnp.inf) for j in range(n_col)]
                mx = cols[0]
                for j in range(1, n_col):
                    mx = jnp.maximum(mx, cols[j])
                m_old = m_s[i]
                m_new = jnp.maximum(m_old, jnp.max(mx, axis=-1, keepdims=True))
                corr = jnp.exp2(m_old - m_new)
                p = jnp.concatenate([jnp.exp2(cj - m_new).astype(BF16) for cj in cols], axis=1)
                pv = jnp.dot(p, v_ext, preferred_element_type=F32)
                acc_s[i] = jnp.concatenate([corr, corr], axis=1) * acc_s[i] + pv
                m_s[i] = m_new

    on_diag = ki >= qi * ratio

    @pl.when(on_diag)
    def _():
        step(True)

    @pl.when(jnp.logical_not(on_diag))
    def _():
        step(False)

    @pl.when(ki == (qi + 1) * ratio - 1)
    def _():
        lam = (jnp.exp(jnp.sum(lq1_ref[...] * lk1_ref[...], axis=-1, keepdims=True))
               - jnp.exp(jnp.sum(lq2_ref[...] * lk2_ref[...], axis=-1, keepdims=True))
               + lambda_init)
        for h in range(DIFF_H):
            a1 = acc_s[2 * h]
            a2 = acc_s[2 * h + 1]
            o = a1[:, :LANES] / a1[:, LANES:] - lam * (a2[:, :LANES] / a2[:, LANES:])
            o = _rms(o, sg_ref[...], SUBLN_EPS) * (1.0 - lambda_init)
            o_ref[:, h * LANES:(h + 1) * LANES] = o.astype(o_ref.dtype)


def _diff_attn(q, k, v, bsz, seq, lq1, lk1, lq2, lk2, sg, lambda_init):
    tq, tk = ATT_TQ, ATT_TK
    ratio = tq // tk
    nq, nk = seq // tq, seq // tk
    pairs = [(a, b) for a in range(nq) for b in range((a + 1) * ratio)]
    qi_tab = jnp.asarray(np.array([a for a, _ in pairs], np.int32))
    ki_tab = jnp.asarray(np.array([b for _, b in pairs], np.int32))
    qmap = lambda b, p, qt, kt: (b * nq + qt[p], 0)
    kmap = lambda b, p, qt, kt: (b * nk + kt[p], 0)
    const = lambda b, p, qt, kt: (0, 0)
    lam_spec = pl.BlockSpec((1, DIFF_D), const)
    grid_spec = pltpu.PrefetchScalarGridSpec(
        num_scalar_prefetch=2,
        grid=(bsz, len(pairs)),
        in_specs=[
            pl.BlockSpec((tq, DIFF_W), qmap),
            pl.BlockSpec((tk, DIFF_W), kmap),
            pl.BlockSpec((tk, DIFF_W), kmap),
            lam_spec, lam_spec, lam_spec, lam_spec,
            pl.BlockSpec((1, 2 * DIFF_D), const),
        ],
        out_specs=pl.BlockSpec((tq, DIFF_W), qmap),
        scratch_shapes=[
            pltpu.VMEM((2 * DIFF_H, tq, LANES), BF16),
            pltpu.VMEM((2 * DIFF_H, tq, LANES), F32),
            pltpu.VMEM((2 * DIFF_H, tq, 2 * LANES), F32),
        ],
    )
    return pl.pallas_call(
        functools.partial(_attn_kernel, lambda_init=lambda_init, ratio=ratio),
        grid_spec=grid_spec,
        out_shape=jax.ShapeDtypeStruct((bsz * seq, DIFF_W), BF16),
        compiler_params=_params("parallel", "arbitrary"),
        name="diff_attn",
    )(qi_tab, ki_tab, q, k, v, lq1, lk1, lq2, lk2, sg)


def _attn2_kernel(q_ref, k_ref, v_ref, lq1_ref, lk1_ref, lq2_ref, lk2_ref, sg_ref, o_ref,
                  q_s, m_s, acc_s, *, lambda_init, tk):
    tq = q_ref.shape[0]
    n_col = tk // LANES
    n_diag = tq // tk
    qi = pl.program_id(2)
    nt = (((1,), (1,)), ((), ()))

    map0 = (lax.broadcasted_iota(jnp.int32, (1, LANES), 1) % DIFF_D) < DIFF_D // 2
    q = q_ref[...]
    zero = jnp.zeros_like(q)
    q_s[0] = jnp.where(map0, q, zero)
    q_s[1] = jnp.where(map0, zero, q)
    ones = jnp.ones((tk, LANES), BF16)
    tri_keep = [lax.broadcasted_iota(jnp.int32, (tk, LANES), 1) + j * LANES
                <= lax.broadcasted_iota(jnp.int32, (tk, LANES), 0) for j in range(n_col)]

    def kv_step(kv_rows, r0, r1, masked, first=False):
        k = k_ref[kv_rows, :]
        v_ext = jnp.concatenate([v_ref[kv_rows, :], ones], axis=1)
        for c in range(2):
            s = lax.dot_general(q_s[c, r0:r1, :], k, nt, preferred_element_type=F32)
            cols = [s[:, j * LANES:(j + 1) * LANES] for j in range(n_col)]
            if masked:
                cols = [jnp.where(tri_keep[j], cols[j], -jnp.inf) for j in range(n_col)]
            mx = cols[0]
            for j in range(1, n_col):
                mx = jnp.maximum(mx, cols[j])
            m_new = jnp.broadcast_to(jnp.max(mx, axis=-1, keepdims=True), mx.shape)
            if not first:
                m_old = m_s[c, r0:r1, :]
                m_new = jnp.maximum(m_old, m_new)
            p = jnp.concatenate([jnp.exp2(cj - m_new).astype(BF16) for cj in cols], axis=1)
            pv = jnp.dot(p, v_ext, preferred_element_type=F32)
            if first:
                acc_s[c, r0:r1, :] = pv
            else:
                corr = jnp.exp2(m_old - m_new)
                acc_s[c, r0:r1, :] = jnp.concatenate([corr, corr], axis=1) * acc_s[c, r0:r1, :] + pv
            m_s[c, r0:r1, :] = m_new

    def full_body(j, carry):
        for u in range(n_diag):
            kv_rows = pl.ds(pl.multiple_of((j * n_diag + u) * tk, tk), tk)
            kv_step(kv_rows, 0, tq // 2, False)
            kv_step(kv_rows, tq // 2, tq, False)
        return carry

    for d in range(n_diag):
        kv_rows = pl.ds(pl.multiple_of((qi * n_diag + d) * tk, tk), tk)
        kv_step(kv_rows, d * tk, (d + 1) * tk, True, first=d == 0)
        if d + 1 < n_diag:
            kv_step(kv_rows, (d + 1) * tk, tq, False, first=d == 0)
    lax.fori_loop(0, qi, full_body, 0)

    lam = (jnp.exp(jnp.sum(lq1_ref[...] * lk1_ref[...], axis=-1, keepdims=True))
           - jnp.exp(jnp.sum(lq2_ref[...] * lk2_ref[...], axis=-1, keepdims=True))
           + lambda_init)
    a1 = acc_s[0]
    a2 = acc_s[1]
    o = a1[:, :LANES] / a1[:, LANES:] - lam * (a2[:, :LANES] / a2[:, LANES:])
    o_ref[...] = (_rms(o, sg_ref[...], SUBLN_EPS) * (1.0 - lambda_init)).astype(o_ref.dtype)


def _diff_attn2(q, k, v, bsz, seq, lq1, lk1, lq2, lk2, sg, lambda_init):
    tq = min(ATT_TQ, seq)
    tk = min(ATT_TK, tq)
    nq = seq // tq
    qmap = lambda b, h, i: (b * nq + i, h)
    kmap = lambda b, h, i: (b, h)
    const = lambda b, h, i: (0, 0)
    lam_spec = pl.BlockSpec((1, DIFF_D), const)
    return pl.pallas_call(
        functools.partial(_attn2_kernel, lambda_init=lambda_init, tk=tk),
        grid=(bsz, DIFF_H, nq),
        in_specs=[
            pl.BlockSpec((tq, LANES), qmap),
            pl.BlockSpec((seq, LANES), kmap),
            pl.BlockSpec((seq, LANES), kmap),
            lam_spec, lam_spec, lam_spec, lam_spec,
            pl.BlockSpec((1, 2 * DIFF_D), const),
        ],
        out_specs=pl.BlockSpec((tq, LANES), qmap),
        out_shape=jax.ShapeDtypeStruct((bsz * seq, DIFF_W), BF16),
        scratch_shapes=[
            pltpu.VMEM((2, tq, LANES), BF16),
            pltpu.VMEM((2, tq, LANES), F32),
            pltpu.VMEM((2, tq, 2 * LANES), F32),
        ],
        compiler_params=_params("parallel", "parallel", "arbitrary"),
        name="diff_attn",
    )(q, k, v, lq1, lk1, lq2, lk2, sg)


def _memkv_kernel(m_ref, g_ref, w_ref, o_ref):
    h = _rms(m_ref[...], g_ref[...], NORM_EPS).astype(BF16)
    for j in range(o_ref.shape[1] // 256):
        sl = slice(j * 256, (j + 1) * 256)
        o_ref[:, sl] = jnp.dot(h, w_ref[:, sl], preferred_element_type=F32).astype(o_ref.dtype)


def _memkv(mem2d, g, wkv):
    m = mem2d.shape[0]
    tm = MEM_LEN
    return pl.pallas_call(
        _memkv_kernel,
        grid=(m // tm,),
        in_specs=[
            pl.BlockSpec((tm, D_MODEL), lambda i: (i, 0)),
            pl.BlockSpec((1, D_MODEL), lambda i: (0, 0)),
            pl.BlockSpec((D_MODEL, 2 * D_MODEL), lambda i: (0, 0)),
        ],
        out_specs=pl.BlockSpec((tm, 2 * D_MODEL), lambda i: (i, 0)),
        out_shape=jax.ShapeDtypeStruct((m, 2 * D_MODEL), BF16),
        compiler_params=_params("parallel"),
        name="memkv",
    )(mem2d, g, wkv)


def _cross_kernel(x_ref, ya_ref, yb_ref, kv_ref, wout_ref, g_ref, wq_ref, wo_ref, o_ref, att_s):
    tm = x_ref.shape[0]
    halves = [slice(0, tm // 2), slice(tm // 2, tm)]
    nt = (((1,), (1,)), ((), ()))
    x1 = [x_ref[r, :] + (jnp.dot(ya_ref[r, :], wout_ref[0:RWKV_W, :], preferred_element_type=F32)
                         + jnp.dot(yb_ref[r, :], wout_ref[RWKV_W:, :], preferred_element_type=F32))
          for r in halves]
    hc = [_rms(x, g_ref[...], NORM_EPS).astype(BF16) for x in x1]
    scale = CROSS_D ** -0.5 * math.log2(math.e)
    q = [(jnp.dot(x, wq_ref[...], preferred_element_type=F32) * scale).astype(BF16) for x in hc]
    for h in range(CROSS_H):
        sl = slice(h * CROSS_D, (h + 1) * CROSS_D)
        kh = kv_ref[:, sl]
        vh = kv_ref[:, D_MODEL + h * CROSS_D:D_MODEL + (h + 1) * CROSS_D]
        s = [lax.dot_general(x[:, sl], kh, nt, preferred_element_type=F32) for x in q]
        p = [jnp.exp2(x - jnp.max(x, axis=-1, keepdims=True)) for x in s]
        o = [jnp.dot(x.astype(BF16), vh, preferred_element_type=F32) / jnp.sum(x, axis=-1, keepdims=True)
             for x in p]
        for r, x in zip(halves, o):
            att_s[r, sl] = x.astype(BF16)
    for r, x in zip(halves, x1):
        o_ref[r, :] = x + jnp.dot(att_s[r, :], wo_ref[...], preferred_element_type=F32)


def _cross(x2d, ya, yb, kv, bsz, seq, w_out, g, wq, wo):
    tm = PROJ_TILE
    nt = seq // tm
    row = lambda b, t: (b * nt + t, 0)
    const = lambda b, t: (0, 0)
    return pl.pallas_call(
        _cross_kernel,
        grid=(bsz, nt),
        in_specs=[
            pl.BlockSpec((tm, D_MODEL), row),
            pl.BlockSpec((tm, RWKV_W), row),
            pl.BlockSpec((tm, DIFF_W), row),
            pl.BlockSpec((MEM_LEN, 2 * D_MODEL), lambda b, t: (b, 0)),
            pl.BlockSpec((D_MODEL, D_MODEL), const, pipeline_mode=pl.Buffered(1)),
            pl.BlockSpec((1, D_MODEL), const),
            pl.BlockSpec((D_MODEL, D_MODEL), const, pipeline_mode=pl.Buffered(1)),
            pl.BlockSpec((D_MODEL, D_MODEL), const, pipeline_mode=pl.Buffered(1)),
        ],
        out_specs=pl.BlockSpec((tm, D_MODEL), row),
        out_shape=jax.ShapeDtypeStruct((bsz * seq, D_MODEL), F32),
        scratch_shapes=[pltpu.VMEM((tm, D_MODEL), BF16)],
        compiler_params=_params("parallel", "parallel"),
        name="outproj_cross",
    )(x2d, ya, yb, kv, w_out, g, wq, wo)


def _ffn_kernel(x_ref, g_ref, wg_ref, wv_ref, cw_ref, cb_ref, wd_ref, gf_ref, o_ref, carry_s, act_s, *, final):
    tm = x_ref.shape[0]
    n_ff = wg_ref.shape[0]

    @pl.when(pl.program_id(1) == 0)
    def _():
        carry_s[...] = jnp.zeros_like(carry_s)

    x = x_ref[...]
    h = _rms(x, g_ref[...], NORM_EPS).astype(BF16)

    def up(j):
        return (jnp.dot(h, wg_ref[j], preferred_element_type=F32),
                jnp.dot(h, wv_ref[j], preferred_element_type=F32))

    tf = wg_ref.shape[2]
    split = (n_ff + 1) // 2 * tf
    nxt = up(0)
    down = []
    for j in range(n_ff):
        gate, val = nxt
        if j + 1 < n_ff:
            nxt = up(j + 1)
        ext = jnp.concatenate([carry_s[j], gate], axis=0)
        g1 = ext[7:tm + 7, :]
        g2 = ext[6:tm + 6, :]
        carry_s[j] = gate[tm - 8:tm, :]
        cwh = 0.5 * cw_ref[j]
        hc = cwh[0:1, :] * g2 + cwh[1:2, :] * g1 + cwh[2:3, :] * gate + 0.5 * cb_ref[j]
        act_s[:, j * tf:(j + 1) * tf] = ((hc + hc * jnp.tanh(hc)) * val).astype(BF16)
        if (j + 1) * tf == split:
            down.append(jnp.dot(act_s[:, :split], wd_ref[:split, :], preferred_element_type=F32))
    down.append(jnp.dot(act_s[:, split:], wd_ref[split:, :], preferred_element_type=F32))
    out = x + (down[0] + down[1])
    o_ref[...] = _rms(out, gf_ref[...], NORM_EPS) if final else out


def _ffn(x2d, bsz, seq, g, wg, wv, cw, cb, wd, gf, final):
    tm = PROJ_TILE
    nt = seq // tm
    n_ff, _, tf = wg.shape
    row = lambda b, t: (b * nt + t, 0)
    c2 = lambda b, t: (0, 0)
    c3 = lambda b, t: (0, 0, 0)
    once = pl.Buffered(1)
    return pl.pallas_call(
        functools.partial(_ffn_kernel, final=final),
        grid=(bsz, nt),
        in_specs=[
            pl.BlockSpec((tm, D_MODEL), row),
            pl.BlockSpec((1, D_MODEL), c2),
            pl.BlockSpec((n_ff, D_MODEL, tf), c3, pipeline_mode=once),
            pl.BlockSpec((n_ff, D_MODEL, tf), c3, pipeline_mode=once),
            pl.BlockSpec((n_ff, 3, tf), c3),
            pl.BlockSpec((n_ff, 1, tf), c3),
            pl.BlockSpec((n_ff * tf, D_MODEL), c2, pipeline_mode=once),
            pl.BlockSpec((1, D_MODEL), c2),
        ],
        out_specs=pl.BlockSpec((tm, D_MODEL), row),
        out_shape=jax.ShapeDtypeStruct((bsz * seq, D_MODEL), F32),
        scratch_shapes=[pltpu.VMEM((n_ff, 8, tf), F32), pltpu.VMEM((tm, n_ff * tf), BF16)],
        compiler_params=_params("parallel", "arbitrary"),
        name="conv_ffn",
    )(x2d, g, wg, wv, cw, cb, wd, gf)


def _rope_perm():
    idx = np.empty((DIFF_W,), np.int32)
    half = DIFF_D // 2
    for h in range(DIFF_H):
        for c in range(2):
            for d in range(DIFF_D):
                idx[h * LANES + (d // half) * DIFF_D + c * half + d % half] = h * LANES + c * DIFF_D + d
    return idx


def _layer(x2d, pos2d, mem2d, bsz, seq, lambda_init, p):
    w_in = p["w_in"]
    pad = jnp.zeros((D_MODEL, LORA_PAD - (N_SHIFT - 3 * RWKV_W)), F32)
    w_r = jnp.concatenate([w_in[:, :N_SHIFT], pad], axis=1).astype(BF16)
    perm = _rope_perm()
    w_d = jnp.concatenate([w_in[:, N_SHIFT:N_SHIFT + DIFF_W][:, perm],
                           w_in[:, N_SHIFT + DIFF_W:N_SHIFT + 2 * DIFF_W][:, perm],
                           w_in[:, N_SHIFT + 2 * DIFF_W:]], axis=1).astype(BF16)
    mix = jnp.concatenate([p["shift_mix"], jnp.zeros((LORA_PAD - (N_SHIFT - 3 * RWKV_W),), F32)])[None, :]
    w_lora = jnp.zeros((LORA_PAD, 3 * RWKV_W), F32)
    w_lora = w_lora.at[0:R_DECAY, 0:RWKV_W].set(p["w_lora_up"])
    w_lora = w_lora.at[R_DECAY:R_DECAY + R_AAA, RWKV_W:2 * RWKV_W].set(p["a_lora_up"])
    w_lora = w_lora.at[R_DECAY + R_AAA:R_DECAY + R_AAA + R_GATE, 2 * RWKV_W:].set(p["g_lora_up"])
    w_lora = w_lora.astype(BF16)
    v512 = lambda a: a.reshape(1, RWKV_W)

    zr, q, k, v = _inproj(x2d, pos2d, p["norm_mix"][None, :], w_r, w_d)
    ya = _rwkv2(zr, bsz, seq, mix, w_lora, v512(p["w0"]), v512(p["a0"]), v512(p["k_k"]),
               v512(p["k_a"]), v512(p["r_k"]), v512(p["lnx_gain"]), v512(p["lnx_bias"]))
    yb = _diff_attn2(q, k, v, bsz, seq, p["lam_q1"][None, :], p["lam_k1"][None, :],
                    p["lam_q2"][None, :], p["lam_k2"][None, :], p["subln_gain"][None, :], lambda_init)
    kv = _memkv(mem2d, p["norm_mem"][None, :], p["wkv_c"].astype(BF16))
    x2 = _cross(x2d, ya, yb, kv, bsz, seq, p["w_out"].astype(BF16), p["norm_cross"][None, :],
                p["wq_c"].astype(BF16), p["wo_c"].astype(BF16))
    n_ff = D_FF // FF_TILE
    w_up = p["w_up"]
    wg = w_up[:, :D_FF].reshape(D_MODEL, n_ff, FF_TILE).transpose(1, 0, 2).astype(BF16)
    wv = w_up[:, D_FF:].reshape(D_MODEL, n_ff, FF_TILE).transpose(1, 0, 2).astype(BF16)
    cw = p["conv_w"].reshape(3, n_ff, FF_TILE).transpose(1, 0, 2)
    cb = p["conv_b"].reshape(n_ff, 1, FF_TILE)
    wd = p["w_down"].astype(BF16)
    return x2, (p["norm_ffn"][None, :], wg, wv, cw, cb, wd)


def kernel(x, mem, positions, norm_mix, w_in, shift_mix, w0, w_lora_up, a0, a_lora_up, g_lora_up, k_k, k_a, r_k, lnx_gain, lnx_bias, lam_q1, lam_k1, lam_q2, lam_k2, subln_gain, w_out, norm_cross, norm_mem, wq_c, wkv_c, wo_c, norm_ffn, w_up, conv_w, conv_b, w_down, norm_final):
    bsz, seq, _ = x.shape
    depth = norm_mix.shape[0]
    x2d = x.reshape(bsz * seq, D_MODEL)
    pos2d = positions.reshape(bsz * seq, 1)
    mem2d = mem.reshape(bsz * mem.shape[1], D_MODEL)
    stacked = dict(norm_mix=norm_mix, w_in=w_in, shift_mix=shift_mix, w0=w0, w_lora_up=w_lora_up,
                   a0=a0, a_lora_up=a_lora_up, g_lora_up=g_lora_up, k_k=k_k, k_a=k_a,
                   r_k=r_k.reshape(depth, RWKV_W), lnx_gain=lnx_gain, lnx_bias=lnx_bias,
                   lam_q1=lam_q1, lam_k1=lam_k1, lam_q2=lam_q2, lam_k2=lam_k2,
                   subln_gain=subln_gain, w_out=w_out, norm_cross=norm_cross, norm_mem=norm_mem,
                   wq_c=wq_c, wkv_c=wkv_c, wo_c=wo_c, norm_ffn=norm_ffn, w_up=w_up,
                   conv_w=conv_w, conv_b=conv_b, w_down=w_down)
    for l in range(depth):
        p = {name: a[l] for name, a in stacked.items()}
        lambda_init = 0.8 - 0.6 * math.exp(-0.3 * l)
        x2, (gn, wg, wv, cw, cb, wd) = _layer(x2d, pos2d, mem2d, bsz, seq, lambda_init, p)
        x2d = _ffn(x2, bsz, seq, gn, wg, wv, cw, cb, wd, norm_final[None, :], l == depth - 1)
    return x2d.reshape(bsz, seq, D_MODEL)
```

```python
import functools
import math

import numpy as np
import jax
import jax.numpy as jnp
from jax import lax
from jax.experimental import pallas as pl
from jax.experimental.pallas import tpu as pltpu

F32 = jnp.float32
BF16 = jnp.bfloat16

D_MODEL = 1024
RWKV_W = 512
RWKV_N = 64
R_DECAY = 32
R_AAA = 32
R_GATE = 96
DIFF_W = 512
DIFF_H = 4
DIFF_D = 64
N_SHIFT = 3 * RWKV_W + R_DECAY + R_AAA + R_GATE
MEM_LEN = 256
CROSS_H = 4
CROSS_D = D_MODEL // CROSS_H
D_FF = 2816
ROPE_THETA = 10000.0
NORM_EPS = 1e-6
LNX_EPS = 64e-5
SUBLN_EPS = 1e-5

LANES = 128
LORA_PAD = 256
ZR_W = 3 * RWKV_W + LORA_PAD
VMEM_LIMIT = 56 * 1024 * 1024

ROW_TILE = 1024
PROJ_TILE = 1024
CHUNK = 64
ATT_TQ = 2048
ATT_TK = 512
FF_TILE = 256
TRIG_BLOCKS = 4


def _mm_tn(a, b):
    return lax.dot_general(a.astype(BF16), b.astype(BF16), (((0,), (0,)), ((), ())),
                           preferred_element_type=F32)


def _sigmoid(x):
    return 1.0 / (1.0 + jnp.exp(-x))


def _rms(x, g, eps):
    return x * lax.rsqrt(jnp.mean(x * x, axis=-1, keepdims=True) + eps) * g


def _split_dot(x, ones_b):
    hi = x.astype(BF16)
    lo = (x - hi.astype(F32)).astype(BF16)
    return (jnp.dot(hi, ones_b, preferred_element_type=F32)
            + jnp.dot(lo, ones_b, preferred_element_type=F32))


def _run_interleaved(*stage_gens):
    live = list(stage_gens)
    while live:
        for g in list(live):
            try:
                next(g)
            except StopIteration:
                live.remove(g)


def _params(*sem):
    return pltpu.CompilerParams(dimension_semantics=sem, vmem_limit_bytes=VMEM_LIMIT)


def _inproj_kernel(x_ref, pos_ref, g_ref, wr_ref, wd_ref, zr_ref, q_ref, k_ref, v_ref, trig_s):
    tm = x_ref.shape[0]
    lane = lax.broadcasted_iota(jnp.int32, (1, LANES), 1)
    freq = (lane % (DIFF_D // 2)).astype(F32)
    inv = jnp.exp(freq * (-2.0 / DIFF_D * math.log(ROPE_THETA)))
    scale = DIFF_D ** -0.5 * math.log2(math.e)

    def rope_tables():
        n_blk = TRIG_BLOCKS
        rb = tm // n_blk
        for i in range(n_blk):
            rows = slice(i * rb, (i + 1) * rb)
            ang = pos_ref[rows, :].astype(F32) * inv
            cos = jnp.cos(ang)
            sin = jnp.where(lane < LANES // 2, -jnp.sin(ang), jnp.sin(ang))
            trig_s[0, rows, :] = cos
            trig_s[1, rows, :] = sin
            trig_s[2, rows, :] = cos * scale
            trig_s[3, rows, :] = sin * scale
            yield

    h = _rms(x_ref[...], g_ref[...], NORM_EPS).astype(BF16)

    def rwkv_slab():
        n_blk = ZR_W // 256
        for j in range(n_blk):
            sl = slice(j * 256, (j + 1) * 256)
            lhs = h
            t = j - (n_blk - TRIG_BLOCKS)
            if t >= 0:
                r = (t + 1) * (tm // TRIG_BLOCKS) - 8
                lhs = jnp.where(trig_s[0, r:r + 8, 0:1][0:1, :] < 2.0, h, jnp.zeros_like(h))
            zr_ref[:, sl] = jnp.dot(lhs, wr_ref[:, sl], preferred_element_type=F32)
            yield

    _run_interleaved(rope_tables(), rwkv_slab())
    cos, sin, cos_q, sin_q = trig_s[0], trig_s[1], trig_s[2], trig_s[3]

    def rope(xb, c, s):
        return xb * c + pltpu.roll(xb, LANES // 2, axis=1) * s

    for j in range(DIFF_W // 256):
        zq = jnp.dot(h, wd_ref[:, j * 256:(j + 1) * 256], preferred_element_type=F32)
        zk = jnp.dot(h, wd_ref[:, DIFF_W + j * 256:DIFF_W + (j + 1) * 256],
                     preferred_element_type=F32)
        zv = jnp.dot(h, wd_ref[:, 2 * DIFF_W + j * 256:2 * DIFF_W + (j + 1) * 256],
                     preferred_element_type=F32)
        for u in range(2):
            c0 = j * 256 + u * LANES
            q_ref[:, c0:c0 + LANES] = rope(zq[:, u * LANES:(u + 1) * LANES], cos_q, sin_q).astype(BF16)
            k_ref[:, c0:c0 + LANES] = rope(zk[:, u * LANES:(u + 1) * LANES], cos, sin).astype(BF16)
        v_ref[:, j * 256:(j + 1) * 256] = zv.astype(BF16)


def _inproj(x2d, pos2d, g, w_r, w_d):
    m = x2d.shape[0]
    tm = PROJ_TILE
    row = lambda i: (i, 0)
    const = lambda i: (0, 0)
    return pl.pallas_call(
        _inproj_kernel,
        grid=(m // tm,),
        in_specs=[
            pl.BlockSpec((tm, D_MODEL), row),
            pl.BlockSpec((tm, 1), row),
            pl.BlockSpec((1, D_MODEL), const),
            pl.BlockSpec((D_MODEL, ZR_W), const, pipeline_mode=pl.Buffered(1)),
            pl.BlockSpec((D_MODEL, 3 * DIFF_W), const, pipeline_mode=pl.Buffered(1)),
        ],
        out_specs=[
            pl.BlockSpec((tm, ZR_W), row),
            pl.BlockSpec((tm, DIFF_W), row),
            pl.BlockSpec((tm, DIFF_W), row),
            pl.BlockSpec((tm, DIFF_W), row),
        ],
        out_shape=[
            jax.ShapeDtypeStruct((m, ZR_W), F32),
            jax.ShapeDtypeStruct((m, DIFF_W), BF16),
            jax.ShapeDtypeStruct((m, DIFF_W), BF16),
            jax.ShapeDtypeStruct((m, DIFF_W), BF16),
        ],
        scratch_shapes=[pltpu.VMEM((4, tm, LANES), F32)],
        compiler_params=_params("parallel"),
        name="inproj",
    )(x2d, pos2d, g, w_r, w_d)


def _rwkv_kernel(z_ref, mix_ref, wl_ref, w0_ref, a0_ref, kk_ref, ka_ref, rk_ref, gain_ref,
                  bias_ref, y_ref, carry_s, st_s, g_s, bo_s, y_s, rt_s, dec_s,
                  ah0_s, ah1_s, bn0_s, bn1_s, kn0_s, kn1_s, vh0_s, vh1_s, be_s, ke_s,
                  mr_s, n_s):
    tb = z_ref.shape[0]
    n_pair = RWKV_W // LANES
    c_len = CHUNK
    n_chunk = tb // c_len

    @pl.when(pl.program_id(1) == 0)
    def _():
        carry_s[...] = jnp.zeros_like(carry_s)
        st_s[...] = jnp.zeros_like(st_s)

    group_chunks = 4
    group_rows = group_chunks * c_len
    n_piece = tb // group_rows
    rowq = lax.broadcasted_iota(jnp.int32, (group_rows, 1), 0)
    ri = lax.broadcasted_iota(jnp.int32, (LANES, LANES), 0)
    ci = lax.broadcasted_iota(jnp.int32, (LANES, LANES), 1)
    same_head = (ri // RWKV_N) == (ci // RWKV_N)
    ones_head = jnp.where(same_head, 1.0, 0.0).astype(BF16)
    head0 = (lax.broadcasted_iota(jnp.int32, (1, RWKV_W), 1) % LANES) < RWKV_N

    def headsum(x):
        return jnp.concatenate(
            [_split_dot(x[:, p * LANES:(p + 1) * LANES], ones_head) for p in range(n_pair)], axis=1)

    def prep(r0, top):
        rows = pl.ds(r0, group_rows)

        def shifted(c0, c1):
            zc = z_ref[rows, c0:c1]
            zp = jnp.where(rowq == 0, top[7:8, c0:c1], pltpu.roll(zc, 1, axis=0))
            return zc + (zp - zc) * mix_ref[:, c0:c1]

        zl = shifted(3 * RWKV_W, ZR_W)
        ll = lax.broadcasted_iota(jnp.int32, (1, LORA_PAD), 1)
        act = jnp.where(ll < R_DECAY, jnp.tanh(zl),
                        jnp.where(ll < R_DECAY + R_AAA, zl, _sigmoid(zl)))
        lo = jnp.dot(act.astype(BF16), wl_ref[...], preferred_element_type=F32)
        yield
        lw = -math.exp(-0.5) * _sigmoid(w0_ref[...] + lo[:, 0:RWKV_W])
        a = _sigmoid(a0_ref[...] + lo[:, RWKV_W:2 * RWKV_W])
        g_s[rows, :] = lo[:, 2 * RWKV_W:3 * RWKV_W].astype(BF16)
        yield

        r = shifted(0, RWKV_W)
        yield
        k = shifted(RWKV_W, 2 * RWKV_W)
        yield
        v = shifted(2 * RWKV_W, 3 * RWKV_W)
        yield
        kk = k * kk_ref[...]
        ss = headsum(kk * kk)
        alpha = kk * lax.rsqrt(jnp.maximum(ss, 1e-24))
        yield
        beta = alpha * a
        k2 = k * (1.0 + (a - 1.0) * ka_ref[...])
        bo_s[rows, :] = (headsum(r * k2 * rk_ref[...]) * v).astype(BF16)
        yield

        rin = rowq % c_len
        cum = lw
        for s in (1, 2, 4, 8, 16, 32):
            cum = cum + jnp.where(rin >= s, pltpu.roll(cum, s, axis=0), 0.0)
            yield
        tot = jnp.broadcast_to(cum.reshape(group_chunks, c_len, RWKV_W)[:, c_len - 1:c_len, :],
                               (group_chunks, c_len, RWKV_W)).reshape(group_rows, RWKV_W)
        e_neg = jnp.exp(-cum)
        e_end = jnp.exp(tot - cum)
        yield
        a_t = alpha * jnp.exp(cum - lw)
        r_t = r * jnp.exp(cum)
        rt_s[rows, :] = r_t
        slab0 = r0 // c_len * 8
        if not isinstance(slab0, int):
            slab0 = pl.multiple_of(slab0, group_chunks * 8)
        dec_s[pl.ds(slab0, group_chunks * 8), :] = jnp.exp(
            tot.reshape(group_chunks, c_len, RWKV_W)[:, :8, :].reshape(group_chunks * 8, RWKV_W))
        yield
        ah0_s[rows, :] = jnp.where(head0, a_t, 0.0).astype(BF16)
        ah1_s[rows, :] = jnp.where(head0, 0.0, a_t).astype(BF16)
        yield
        vh0_s[rows, :] = jnp.where(head0, v, 0.0).astype(BF16)
        vh1_s[rows, :] = jnp.where(head0, 0.0, v).astype(BF16)
        yield
        b_n = (beta * e_neg).astype(BF16)
        bn0_s[rows, :] = jnp.where(head0, b_n, jnp.zeros_like(b_n))
        bn1_s[rows, :] = jnp.where(head0, jnp.zeros_like(b_n), b_n)
        yield
        k_n = (k2 * e_neg).astype(BF16)
        kn0_s[rows, :] = jnp.where(head0, k_n, jnp.zeros_like(k_n))
        kn1_s[rows, :] = jnp.where(head0, jnp.zeros_like(k_n), k_n)
        yield
        be_s[rows, :] = (beta * e_end).astype(BF16)
        ke_s[rows, :] = (k2 * e_end).astype(BF16)

    ti = lax.broadcasted_iota(jnp.int32, (c_len, LANES), 0)
    si = lax.broadcasted_iota(jnp.int32, (c_len, LANES), 1) % c_len
    strict = si < ti
    incl = si <= ti
    b16 = (ti // 16) == (si // 16)
    b32 = (ti // 32) == (si // 32)
    eye = jnp.where(ti == si, 1.0, 0.0).astype(F32)
    lane_h0 = lax.broadcasted_iota(jnp.int32, (1, LANES), 1) < RWKV_N
    group = group_chunks * n_pair
    nb = range(group)

    def stack(x):
        xb = x.astype(BF16)
        zero = jnp.zeros_like(xb)
        return jnp.concatenate([jnp.where(lane_h0, xb, zero), jnp.where(lane_h0, zero, xb)], axis=0)

    def mmb(xs, ys):
        return [jnp.dot(x.astype(BF16), stack(y), preferred_element_type=F32) for x, y in zip(xs, ys)]

    def solve(it):
        idx = []
        for u in range(group):
            c = it * group_chunks + u // n_pair
            p = u % n_pair
            idx.append((c, c * n_pair + p,
                        (pl.ds(pl.multiple_of(c * c_len, c_len), c_len), slice(p * LANES, (p + 1) * LANES))))
        sls = [sl for _, _, sl in idx]
        a_st = [jnp.concatenate([ah0_s[sl], ah1_s[sl]], axis=0) for sl in sls]
        lhs = [jnp.concatenate([x[:c_len] + x[c_len:], rt_s[sl].astype(BF16)], axis=0)
               for x, sl in zip(a_st, sls)]
        rhs = [jnp.concatenate([bn0_s[sl], bn1_s[sl], kn0_s[sl], kn1_s[sl]], axis=0) for sl in sls]
        gram = [lax.dot_general(x, y, (((1,), (1,)), ((), ())), preferred_element_type=F32)
                for x, y in zip(lhs, rhs)]
        a_ab = [jnp.where(strict, gm[:c_len, :LANES], 0.0) for gm in gram]
        a_kr = [jnp.concatenate([jnp.where(strict, gm[:c_len, LANES:], 0.0),
                                 jnp.where(incl, gm[c_len:, LANES:], 0.0)], axis=0).astype(BF16)
                for gm in gram]
        a_rb = [jnp.where(incl, gm[c_len:, :LANES], 0.0) for gm in gram]
        yield
        a0 = [jnp.where(b16, x, 0.0).astype(BF16) for x in a_ab]
        e1 = [jnp.where(b32 & jnp.logical_not(b16), x, 0.0).astype(BF16) for x in a_ab]
        e2 = [jnp.where(b32, 0.0, x).astype(BF16) for x in a_ab]
        p2 = mmb(a0, a0)
        yield
        p4 = mmb(p2, p2)
        yield
        p8 = mmb(p4, p4)
        t = [eye - x.astype(F32) for x in a0]
        t = [x + y for x, y in zip(t, mmb(t, p2))]
        yield
        t = [x + y for x, y in zip(t, mmb(t, p4))]
        yield
        t = [x + y for x, y in zip(t, mmb(t, p8))]
        yield
        te = mmb(t, e1)
        yield
        t = [x - y for x, y in zip(t, mmb(te, t))]
        yield
        te = mmb(t, e2)
        yield
        t = [x - y for x, y in zip(t, mmb(te, t))]
        yield
        v_st = [jnp.concatenate([vh0_s[sl], vh1_s[sl]], axis=0) for sl in sls]
        av = [jnp.dot(x, y, preferred_element_type=F32) for x, y in zip(a_kr, v_st)]
        yield
        w = [jnp.dot(x.astype(BF16), jnp.concatenate([y, stack(z[:c_len])], axis=1), preferred_element_type=F32)
             for x, y, z in zip(t, a_st, av)]
        yield
        rbw = [jnp.dot(x.astype(BF16), jnp.concatenate([stack(y[:, :LANES]), stack(y[:, LANES:])], axis=1),
                       preferred_element_type=F32) for x, y in zip(a_rb, w)]
        yield
        bw = [_mm_tn(be_s[sl], x) for sl, x in zip(sls, w)]
        yield
        kv = [_mm_tn(ke_s[sl], vh0_s[sl] + vh1_s[sl]) for sl in sls]
        yield
        for u in nb:
            c, j, sl = idx[u]
            r_hat = rt_s[sl] - rbw[u][:, :LANES]
            m_p = jnp.where(same_head, bw[u][:, :LANES], 0.0)
            mr_s[j] = jnp.concatenate([m_p, r_hat], axis=0).astype(BF16)
            y_s[sl] = av[u][c_len:] - rbw[u][:, LANES:]
            n_s[j] = jnp.where(same_head, kv[u] - bw[u][:, LANES:], 0.0).astype(BF16)

    def scan(it):
        for u in range(group_chunks):
            c = it * group_chunks + u
            rows = pl.ds(pl.multiple_of(c * c_len, c_len), c_len)
            sts = [st_s[p] for p in range(n_pair)]
            zz = [jnp.dot(mr_s[c * n_pair + p], sts[p].astype(BF16), preferred_element_type=F32)
                  for p in range(n_pair)]
            yield
            for p in range(n_pair):
                j = c * n_pair + p
                sl = (rows, slice(p * LANES, (p + 1) * LANES))
                y_s[sl] = y_s[sl] + zz[p][LANES:]
                d = dec_s[pl.ds(pl.multiple_of(c * 8, 8), 8), p * LANES:(p + 1) * LANES]
                decay_col = jnp.broadcast_to(d[0:1, :], (LANES, LANES)).T
                st_s[p] = decay_col * sts[p] - zz[p][:LANES] + n_s[j].astype(F32)
            yield

    def below(it):
        r0 = pl.multiple_of((it + 1) * group_rows, group_rows)
        return prep(r0, z_ref[pl.ds(r0 - 8, 8), :])

    _run_interleaved(prep(0, carry_s[...]))
    carry_s[...] = z_ref[tb - 8:tb, :]
    _run_interleaved(solve(0), below(0))

    def piece_body(it, carry):
        _run_interleaved(solve(it), below(it), scan(it - 1))
        return carry

    lax.fori_loop(1, n_piece - 1, piece_body, 0)
    _run_interleaved(solve(n_piece - 1), scan(n_piece - 2))
    _run_interleaved(scan(n_piece - 1))

    y = y_s[...]
    mu = headsum(y) * (1.0 / RWKV_N)
    d = y - mu
    var = headsum(d * d) * (1.0 / RWKV_N)
    yn = d * lax.rsqrt(var + LNX_EPS) * gain_ref[...] + bias_ref[...]
    y_ref[...] = ((yn + bo_s[...].astype(F32)) * g_s[...].astype(F32)).astype(y_ref.dtype)


def _rwkv(zr, bsz, seq, mix, w_lora, w0, a0, k_k, k_a, r_k, gain, bias):
    tb = ROW_TILE
    nt = seq // tb
    n_prob = (tb // CHUNK) * (RWKV_W // LANES)
    row = lambda b, t: (b * nt + t, 0)
    const = lambda b, t: (0, 0)
    vec = pl.BlockSpec((1, RWKV_W), const)
    big = pltpu.VMEM((tb, RWKV_W), F32)
    half = pltpu.VMEM((tb, RWKV_W), BF16)
    return pl.pallas_call(
        _rwkv_kernel,
        grid=(bsz, nt),
        in_specs=[
            pl.BlockSpec((tb, ZR_W), row),
            pl.BlockSpec((1, ZR_W), const),
            pl.BlockSpec((LORA_PAD, 3 * RWKV_W), const),
            vec, vec, vec, vec, vec, vec, vec,
        ],
        out_specs=pl.BlockSpec((tb, RWKV_W), row),
        out_shape=jax.ShapeDtypeStruct((bsz * seq, RWKV_W), BF16),
        scratch_shapes=[
            pltpu.VMEM((8, ZR_W), F32),
            pltpu.VMEM((RWKV_W // LANES, LANES, LANES), F32),
            half, half, big, big,
            pltpu.VMEM((tb // CHUNK * 8, RWKV_W), F32),
            half, half, half, half, half, half, half, half, half, half,
            pltpu.VMEM((n_prob, LANES + CHUNK, LANES), BF16),
            pltpu.VMEM((n_prob, LANES, LANES), BF16),
        ],
        compiler_params=_params("parallel", "arbitrary"),
        name="rwkv7",
    )(zr, mix, w_lora, w0, a0, k_k, k_a, r_k, gain, bias)


def _attn_kernel(q_ref, k_ref, v_ref, lq1_ref, lk1_ref, lq2_ref, lk2_ref, sg_ref, o_ref,
                  q_s, m_s, acc_s, *, lambda_init, tk):
    tq = q_ref.shape[0]
    n_col = tk // LANES
    n_diag = tq // tk
    qi = pl.program_id(2)
    nt = (((1,), (1,)), ((), ()))

    map0 = (lax.broadcasted_iota(jnp.int32, (1, LANES), 1) % DIFF_D) < DIFF_D // 2
    q = q_ref[...]
    zero = jnp.zeros_like(q)
    q_s[0] = jnp.where(map0, q, zero)
    q_s[1] = jnp.where(map0, zero, q)
    ones = jnp.ones((tk, LANES), BF16)
    tri_keep = [lax.broadcasted_iota(jnp.int32, (tk, LANES), 1) + j * LANES
                <= lax.broadcasted_iota(jnp.int32, (tk, LANES), 0) for j in range(n_col)]

    def kv_step(kv_rows, r0, r1, masked, first=False):
        k = k_ref[kv_rows, :]
        v_ext = jnp.concatenate([v_ref[kv_rows, :], ones], axis=1)
        for c in range(2):
            s = lax.dot_general(q_s[c, r0:r1, :], k, nt, preferred_element_type=F32)
            cols = [s[:, j * LANES:(j + 1) * LANES] for j in range(n_col)]
            if masked:
                cols = [jnp.where(tri_keep[j], cols[j], -jnp.inf) for j in range(n_col)]
            mx = cols[0]
            for j in range(1, n_col):
                mx = jnp.maximum(mx, cols[j])
            m_new = jnp.broadcast_to(jnp.max(mx, axis=-1, keepdims=True), mx.shape)
            if not first:
                m_old = m_s[c, r0:r1, :]
                m_new = jnp.maximum(m_old, m_new)
            p = jnp.concatenate([jnp.exp2(cj - m_new).astype(BF16) for cj in cols], axis=1)
            pv = jnp.dot(p, v_ext, preferred_element_type=F32)
            if first:
                acc_s[c, r0:r1, :] = pv
            else:
                corr = jnp.exp2(m_old - m_new)
                acc_s[c, r0:r1, :] = jnp.concatenate([corr, corr], axis=1) * acc_s[c, r0:r1, :] + pv
            m_s[c, r0:r1, :] = m_new

    def full_body(j, carry):
        for u in range(n_diag):
            kv_rows = pl.ds(pl.multiple_of((j * n_diag + u) * tk, tk), tk)
            kv_step(kv_rows, 0, tq // 2, False)
            kv_step(kv_rows, tq // 2, tq, False)
        return carry

    for d in range(n_diag):
        kv_rows = pl.ds(pl.multiple_of((qi * n_diag + d) * tk, tk), tk)
        kv_step(kv_rows, d * tk, (d + 1) * tk, True, first=d == 0)
        if d + 1 < n_diag:
            kv_step(kv_rows, (d + 1) * tk, tq, False, first=d == 0)
    lax.fori_loop(0, qi, full_body, 0)

    lam = (jnp.exp(jnp.sum(lq1_ref[...] * lk1_ref[...], axis=-1, keepdims=True))
           - jnp.exp(jnp.sum(lq2_ref[...] * lk2_ref[...], axis=-1, keepdims=True))
           + lambda_init)
    a1 = acc_s[0]
    a2 = acc_s[1]
    o = a1[:, :LANES] / a1[:, LANES:] - lam * (a2[:, :LANES] / a2[:, LANES:])
    o_ref[...] = (_rms(o, sg_ref[...], SUBLN_EPS) * (1.0 - lambda_init)).astype(o_ref.dtype)


def _diff_attn(q, k, v, bsz, seq, lq1, lk1, lq2, lk2, sg, lambda_init):
    tq = min(ATT_TQ, seq)
    tk = min(ATT_TK, tq)
    nq = seq // tq
    qmap = lambda b, h, i: (b * nq + i, h)
    kmap = lambda b, h, i: (b, h)
    const = lambda b, h, i: (0, 0)
    lam_spec = pl.BlockSpec((1, DIFF_D), const)
    return pl.pallas_call(
        functools.partial(_attn_kernel, lambda_init=lambda_init, tk=tk),
        grid=(bsz, DIFF_H, nq),
        in_specs=[
            pl.BlockSpec((tq, LANES), qmap),
            pl.BlockSpec((seq, LANES), kmap),
            pl.BlockSpec((seq, LANES), kmap),
            lam_spec, lam_spec, lam_spec, lam_spec,
            pl.BlockSpec((1, 2 * DIFF_D), const),
        ],
        out_specs=pl.BlockSpec((tq, LANES), qmap),
        out_shape=jax.ShapeDtypeStruct((bsz * seq, DIFF_W), BF16),
        scratch_shapes=[
            pltpu.VMEM((2, tq, LANES), BF16),
            pltpu.VMEM((2, tq, LANES), F32),
            pltpu.VMEM((2, tq, 2 * LANES), F32),
        ],
        compiler_params=_params("parallel", "parallel", "arbitrary"),
        name="diff_attn",
    )(q, k, v, lq1, lk1, lq2, lk2, sg)


def _memkv_kernel(m_ref, g_ref, w_ref, o_ref):
    h = _rms(m_ref[...], g_ref[...], NORM_EPS).astype(BF16)
    for j in range(o_ref.shape[1] // 256):
        sl = slice(j * 256, (j + 1) * 256)
        o_ref[:, sl] = jnp.dot(h, w_ref[:, sl], preferred_element_type=F32).astype(o_ref.dtype)


def _memkv(mem2d, g, wkv):
    m = mem2d.shape[0]
    tm = MEM_LEN
    return pl.pallas_call(
        _memkv_kernel,
        grid=(m // tm,),
        in_specs=[
            pl.BlockSpec((tm, D_MODEL), lambda i: (i, 0)),
            pl.BlockSpec((1, D_MODEL), lambda i: (0, 0)),
            pl.BlockSpec((D_MODEL, 2 * D_MODEL), lambda i: (0, 0)),
        ],
        out_specs=pl.BlockSpec((tm, 2 * D_MODEL), lambda i: (i, 0)),
        out_shape=jax.ShapeDtypeStruct((m, 2 * D_MODEL), BF16),
        compiler_params=_params("parallel"),
        name="memkv",
    )(mem2d, g, wkv)


def _cross_kernel(x_ref, ya_ref, yb_ref, kv_ref, wout_ref, g_ref, wq_ref, wo_ref, o_ref, att_s):
    tm = x_ref.shape[0]
    halves = [slice(0, tm // 2), slice(tm // 2, tm)]
    nt = (((1,), (1,)), ((), ()))
    x1 = [x_ref[r, :] + (jnp.dot(ya_ref[r, :], wout_ref[0:RWKV_W, :], preferred_element_type=F32)
                         + jnp.dot(yb_ref[r, :], wout_ref[RWKV_W:, :], preferred_element_type=F32))
          for r in halves]
    hc = [_rms(x, g_ref[...], NORM_EPS).astype(BF16) for x in x1]
    scale = CROSS_D ** -0.5 * math.log2(math.e)
    q = [(jnp.dot(x, wq_ref[...], preferred_element_type=F32) * scale).astype(BF16) for x in hc]
    for h in range(CROSS_H):
        sl = slice(h * CROSS_D, (h + 1) * CROSS_D)
        kh = kv_ref[:, sl]
        vh = kv_ref[:, D_MODEL + h * CROSS_D:D_MODEL + (h + 1) * CROSS_D]
        s = [lax.dot_general(x[:, sl], kh, nt, preferred_element_type=F32) for x in q]
        p = [jnp.exp2(x - jnp.max(x, axis=-1, keepdims=True)) for x in s]
        o = [jnp.dot(x.astype(BF16), vh, preferred_element_type=F32) / jnp.sum(x, axis=-1, keepdims=True)
             for x in p]
        for r, x in zip(halves, o):
            att_s[r, sl] = x.astype(BF16)
    for r, x in zip(halves, x1):
        o_ref[r, :] = x + jnp.dot(att_s[r, :], wo_ref[...], preferred_element_type=F32)


def _cross(x2d, ya, yb, kv, bsz, seq, w_out, g, wq, wo):
    tm = PROJ_TILE
    nt = seq // tm
    row = lambda b, t: (b * nt + t, 0)
    const = lambda b, t: (0, 0)
    return pl.pallas_call(
        _cross_kernel,
        grid=(bsz, nt),
        in_specs=[
            pl.BlockSpec((tm, D_MODEL), row),
            pl.BlockSpec((tm, RWKV_W), row),
            pl.BlockSpec((tm, DIFF_W), row),
            pl.BlockSpec((MEM_LEN, 2 * D_MODEL), lambda b, t: (b, 0)),
            pl.BlockSpec((D_MODEL, D_MODEL), const, pipeline_mode=pl.Buffered(1)),
            pl.BlockSpec((1, D_MODEL), const),
            pl.BlockSpec((D_MODEL, D_MODEL), const, pipeline_mode=pl.Buffered(1)),
            pl.BlockSpec((D_MODEL, D_MODEL), const, pipeline_mode=pl.Buffered(1)),
        ],
        out_specs=pl.BlockSpec((tm, D_MODEL), row),
        out_shape=jax.ShapeDtypeStruct((bsz * seq, D_MODEL), F32),
        scratch_shapes=[pltpu.VMEM((tm, D_MODEL), BF16)],
        compiler_params=_params("parallel", "parallel"),
        name="outproj_cross",
    )(x2d, ya, yb, kv, w_out, g, wq, wo)


def _ffn_kernel(x_ref, g_ref, wg_ref, wv_ref, cw_ref, cb_ref, wd_ref, gf_ref, o_ref, carry_s, act_s, *, final):
    tm = x_ref.shape[0]
    n_ff, _, tf = carry_s.shape

    @pl.when(pl.program_id(1) == 0)
    def _():
        carry_s[...] = jnp.zeros_like(carry_s)

    x = x_ref[...]
    h = _rms(x, g_ref[...], NORM_EPS).astype(BF16)

    def up(j):
        cols = slice(j * tf, (j + 1) * tf)
        return (jnp.dot(h, wg_ref[:, cols], preferred_element_type=F32),
                jnp.dot(h, wv_ref[:, cols], preferred_element_type=F32))

    split = (n_ff + 1) // 2 * tf
    nxt = up(0)
    down = []
    for j in range(n_ff):
        gate, val = nxt
        if j + 1 < n_ff:
            nxt = up(j + 1)
        ext = jnp.concatenate([carry_s[j], gate], axis=0)
        g1 = ext[7:tm + 7, :]
        g2 = ext[6:tm + 6, :]
        carry_s[j] = gate[tm - 8:tm, :]
        cwh = 0.5 * cw_ref[:, j * tf:(j + 1) * tf]
        hc = cwh[0:1, :] * g2 + cwh[1:2, :] * g1 + cwh[2:3, :] * gate + 0.5 * cb_ref[:, j * tf:(j + 1) * tf]
        act_s[:, j * tf:(j + 1) * tf] = ((hc + hc * jnp.tanh(hc)) * val).astype(BF16)
        if (j + 1) * tf == split:
            down.append(jnp.dot(act_s[:, :split], wd_ref[:split, :], preferred_element_type=F32))
    down.append(jnp.dot(act_s[:, split:], wd_ref[split:, :], preferred_element_type=F32))
    out = x + (down[0] + down[1])
    o_ref[...] = _rms(out, gf_ref[...], NORM_EPS) if final else out


def _ffn(x2d, bsz, seq, g, wg, wv, cw, cb, wd, gf, final):
    tm = PROJ_TILE
    nt = seq // tm
    tf = FF_TILE
    n_ff = D_FF // tf
    row = lambda b, t: (b * nt + t, 0)
    c2 = lambda b, t: (0, 0)
    once = pl.Buffered(1)
    return pl.pallas_call(
        functools.partial(_ffn_kernel, final=final),
        grid=(bsz, nt),
        in_specs=[
            pl.BlockSpec((tm, D_MODEL), row),
            pl.BlockSpec((1, D_MODEL), c2),
            pl.BlockSpec((D_MODEL, D_FF), c2, pipeline_mode=once),
            pl.BlockSpec((D_MODEL, D_FF), c2, pipeline_mode=once),
            pl.BlockSpec((3, D_FF), c2),
            pl.BlockSpec((1, D_FF), c2),
            pl.BlockSpec((D_FF, D_MODEL), c2, pipeline_mode=once),
            pl.BlockSpec((1, D_MODEL), c2),
        ],
        out_specs=pl.BlockSpec((tm, D_MODEL), row),
        out_shape=jax.ShapeDtypeStruct((bsz * seq, D_MODEL), F32),
        scratch_shapes=[pltpu.VMEM((n_ff, 8, tf), F32), pltpu.VMEM((tm, n_ff * tf), BF16)],
        compiler_params=_params("parallel", "arbitrary"),
        name="conv_ffn",
    )(x2d, g, wg, wv, cw, cb, wd, gf)


def _rope_perm():
    idx = np.empty((DIFF_W,), np.int32)
    half = DIFF_D // 2
    for h in range(DIFF_H):
        for c in range(2):
            for d in range(DIFF_D):
                idx[h * LANES + (d // half) * DIFF_D + c * half + d % half] = h * LANES + c * DIFF_D + d
    return idx


def _layer(x2d, pos2d, mem2d, bsz, seq, lambda_init, p):
    w_in = p["w_in"]
    pad = jnp.zeros((D_MODEL, LORA_PAD - (N_SHIFT - 3 * RWKV_W)), F32)
    w_r = jnp.concatenate([w_in[:, :N_SHIFT], pad], axis=1).astype(BF16)
    perm = _rope_perm()
    w_d = jnp.concatenate([w_in[:, N_SHIFT:N_SHIFT + DIFF_W][:, perm],
                           w_in[:, N_SHIFT + DIFF_W:N_SHIFT + 2 * DIFF_W][:, perm],
                           w_in[:, N_SHIFT + 2 * DIFF_W:]], axis=1).astype(BF16)
    mix = jnp.concatenate([p["shift_mix"], jnp.zeros((LORA_PAD - (N_SHIFT - 3 * RWKV_W),), F32)])[None, :]
    w_lora = jnp.zeros((LORA_PAD, 3 * RWKV_W), F32)
    w_lora = w_lora.at[0:R_DECAY, 0:RWKV_W].set(p["w_lora_up"])
    w_lora = w_lora.at[R_DECAY:R_DECAY + R_AAA, RWKV_W:2 * RWKV_W].set(p["a_lora_up"])
    w_lora = w_lora.at[R_DECAY + R_AAA:R_DECAY + R_AAA + R_GATE, 2 * RWKV_W:].set(p["g_lora_up"])
    w_lora = w_lora.astype(BF16)
    v512 = lambda a: a.reshape(1, RWKV_W)

    zr, q, k, v = _inproj(x2d, pos2d, p["norm_mix"][None, :], w_r, w_d)
    ya = _rwkv(zr, bsz, seq, mix, w_lora, v512(p["w0"]), v512(p["a0"]), v512(p["k_k"]),
               v512(p["k_a"]), v512(p["r_k"]), v512(p["lnx_gain"]), v512(p["lnx_bias"]))
    yb = _diff_attn(q, k, v, bsz, seq, p["lam_q1"][None, :], p["lam_k1"][None, :],
                    p["lam_q2"][None, :], p["lam_k2"][None, :], p["subln_gain"][None, :], lambda_init)
    kv = _memkv(mem2d, p["norm_mem"][None, :], p["wkv_c"].astype(BF16))
    x2 = _cross(x2d, ya, yb, kv, bsz, seq, p["w_out"].astype(BF16), p["norm_cross"][None, :],
                p["wq_c"].astype(BF16), p["wo_c"].astype(BF16))
    w_up = p["w_up"]
    wg = w_up[:, :D_FF].astype(BF16)
    wv = w_up[:, D_FF:].astype(BF16)
    cw = p["conv_w"]
    cb = p["conv_b"][None, :]
    wd = p["w_down"].astype(BF16)
    return x2, (p["norm_ffn"][None, :], wg, wv, cw, cb, wd)


def kernel(x, mem, positions, norm_mix, w_in, shift_mix, w0, w_lora_up, a0, a_lora_up, g_lora_up, k_k, k_a, r_k, lnx_gain, lnx_bias, lam_q1, lam_k1, lam_q2, lam_k2, subln_gain, w_out, norm_cross, norm_mem, wq_c, wkv_c, wo_c, norm_ffn, w_up, conv_w, conv_b, w_down, norm_final):
    bsz, seq, _ = x.shape
    depth = norm_mix.shape[0]
    x2d = x.reshape(bsz * seq, D_MODEL)
    pos2d = positions.reshape(bsz * seq, 1)
    mem2d = mem.reshape(bsz * mem.shape[1], D_MODEL)
    stacked = dict(norm_mix=norm_mix, w_in=w_in, shift_mix=shift_mix, w0=w0, w_lora_up=w_lora_up,
                   a0=a0, a_lora_up=a_lora_up, g_lora_up=g_lora_up, k_k=k_k, k_a=k_a,
                   r_k=r_k.reshape(depth, RWKV_W), lnx_gain=lnx_gain, lnx_bias=lnx_bias,
                   lam_q1=lam_q1, lam_k1=lam_k1, lam_q2=lam_q2, lam_k2=lam_k2,
                   subln_gain=subln_gain, w_out=w_out, norm_cross=norm_cross, norm_mem=norm_mem,
                   wq_c=wq_c, wkv_c=wkv_c, wo_c=wo_c, norm_ffn=norm_ffn, w_up=w_up,
                   conv_w=conv_w, conv_b=conv_b, w_down=w_down)
    for l in range(depth):
        p = {name: a[l] for name, a in stacked.items()}
        lambda_init = 0.8 - 0.6 * math.exp(-0.3 * l)
        x2, (gn, wg, wv, cw, cb, wd) = _layer(x2d, pos2d, mem2d, bsz, seq, lambda_init, p)
        x2d = _ffn(x2, bsz, seq, gn, wg, wv, cw, cb, wd, norm_final[None, :], l == depth - 1)
    return x2d.reshape(bsz, seq, D_MODEL)
```

```python
import functools
import math

import numpy as np
import jax
import jax.numpy as jnp
from jax import lax
from jax.experimental import pallas as pl
from jax.experimental.pallas import tpu as pltpu

F32 = jnp.float32
BF16 = jnp.bfloat16

D_MODEL = 1024
RWKV_W = 512
RWKV_N = 64
R_DECAY = 32
R_AAA = 32
R_GATE = 96
DIFF_W = 512
DIFF_H = 4
DIFF_D = 64
N_SHIFT = 3 * RWKV_W + R_DECAY + R_AAA + R_GATE
MEM_LEN = 256
CROSS_H = 4
CROSS_D = D_MODEL // CROSS_H
D_FF = 2816
ROPE_THETA = 10000.0
NORM_EPS = 1e-6
LNX_EPS = 64e-5
SUBLN_EPS = 1e-5

LANES = 128
LORA_PAD = 256
ZR_W = 3 * RWKV_W + LORA_PAD
VMEM_LIMIT = 56 * 1024 * 1024

ROW_TILE = 1024
PROJ_TILE = 1024
CHUNK = 64
ATT_TQ = 2048
ATT_TK = 512
FF_TILE = 256
TRIG_BLOCKS = 4


def _mm_tn(a, b):
    return lax.dot_general(a.astype(BF16), b.astype(BF16), (((0,), (0,)), ((), ())),
                           preferred_element_type=F32)


def _sigmoid(x):
    return 0.5 * jnp.tanh(0.5 * x) + 0.5


def _rms(x, g, eps):
    return x * lax.rsqrt(jnp.mean(x * x, axis=-1, keepdims=True) + eps) * g


def _run_interleaved(*stage_gens):
    live = list(stage_gens)
    while live:
        for g in list(live):
            try:
                next(g)
            except StopIteration:
                live.remove(g)


def _params(*sem):
    return pltpu.CompilerParams(dimension_semantics=sem, vmem_limit_bytes=VMEM_LIMIT)


def _inproj_kernel(x_ref, pos_ref, g_ref, wr_ref, wd_ref, zr_ref, q_ref, k_ref, v_ref, trig_s):
    tm = x_ref.shape[0]
    lane = lax.broadcasted_iota(jnp.int32, (1, LANES), 1)
    freq = (lane % (DIFF_D // 2)).astype(F32)
    inv = jnp.exp(freq * (-2.0 / DIFF_D * math.log(ROPE_THETA)))
    scale = DIFF_D ** -0.5 * math.log2(math.e)

    def rope_tables():
        n_blk = TRIG_BLOCKS
        rb = tm // n_blk
        for i in range(n_blk):
            rows = slice(i * rb, (i + 1) * rb)
            ang = pos_ref[rows, :].astype(F32) * inv
            cos = jnp.cos(ang)
            sin = jnp.where(lane < LANES // 2, -jnp.sin(ang), jnp.sin(ang))
            trig_s[0, rows, :] = cos
            trig_s[1, rows, :] = sin
            trig_s[2, rows, :] = cos * scale
            trig_s[3, rows, :] = sin * scale
            yield

    h = _rms(x_ref[...], g_ref[...], NORM_EPS).astype(BF16)

    def rwkv_slab():
        n_blk = ZR_W // 256
        for j in range(n_blk):
            sl = slice(j * 256, (j + 1) * 256)
            lhs = h
            t = j - (n_blk - TRIG_BLOCKS)
            if t >= 0:
                r = (t + 1) * (tm // TRIG_BLOCKS) - 8
                lhs = jnp.where(trig_s[0, r:r + 8, 0:1][0:1, :] < 2.0, h, jnp.zeros_like(h))
            zr_ref[:, sl] = jnp.dot(lhs, wr_ref[:, sl], preferred_element_type=F32)
            yield

    _run_interleaved(rope_tables(), rwkv_slab())
    cos, sin, cos_q, sin_q = trig_s[0], trig_s[1], trig_s[2], trig_s[3]

    def rope(xb, c, s):
        return xb * c + pltpu.roll(xb, LANES // 2, axis=1) * s

    for j in range(DIFF_W // 256):
        zq = jnp.dot(h, wd_ref[:, j * 256:(j + 1) * 256], preferred_element_type=F32)
        zk = jnp.dot(h, wd_ref[:, DIFF_W + j * 256:DIFF_W + (j + 1) * 256],
                     preferred_element_type=F32)
        zv = jnp.dot(h, wd_ref[:, 2 * DIFF_W + j * 256:2 * DIFF_W + (j + 1) * 256],
                     preferred_element_type=F32)
        for u in range(2):
            c0 = j * 256 + u * LANES
            q_ref[:, c0:c0 + LANES] = rope(zq[:, u * LANES:(u + 1) * LANES], cos_q, sin_q).astype(BF16)
            k_ref[:, c0:c0 + LANES] = rope(zk[:, u * LANES:(u + 1) * LANES], cos, sin).astype(BF16)
        v_ref[:, j * 256:(j + 1) * 256] = zv.astype(BF16)


def _inproj(x2d, pos2d, g, w_r, w_d):
    m = x2d.shape[0]
    tm = PROJ_TILE
    row = lambda i: (i, 0)
    const = lambda i: (0, 0)
    return pl.pallas_call(
        _inproj_kernel,
        grid=(m // tm,),
        in_specs=[
            pl.BlockSpec((tm, D_MODEL), row),
            pl.BlockSpec((tm, 1), row),
            pl.BlockSpec((1, D_MODEL), const),
            pl.BlockSpec((D_MODEL, ZR_W), const, pipeline_mode=pl.Buffered(1)),
            pl.BlockSpec((D_MODEL, 3 * DIFF_W), const, pipeline_mode=pl.Buffered(1)),
        ],
        out_specs=[
            pl.BlockSpec((tm, ZR_W), row),
            pl.BlockSpec((tm, DIFF_W), row),
            pl.BlockSpec((tm, DIFF_W), row),
            pl.BlockSpec((tm, DIFF_W), row),
        ],
        out_shape=[
            jax.ShapeDtypeStruct((m, ZR_W), F32),
            jax.ShapeDtypeStruct((m, DIFF_W), BF16),
            jax.ShapeDtypeStruct((m, DIFF_W), BF16),
            jax.ShapeDtypeStruct((m, DIFF_W), BF16),
        ],
        scratch_shapes=[pltpu.VMEM((4, tm, LANES), F32)],
        compiler_params=_params("parallel"),
        name="inproj",
    )(x2d, pos2d, g, w_r, w_d)


def _rwkv_kernel(z_ref, mix_ref, wl_ref, w0_ref, a0_ref, kk_ref, ka_ref, rk_ref, gain_ref,
                  bias_ref, y_ref, carry_s, st_s, g_s, bo_s, y_s, rt_s, dec_s,
                  ah0_s, ah1_s, bn0_s, bn1_s, kn0_s, kn1_s, vh0_s, vh1_s, be_s, ke_s,
                  mr_s, n_s):
    tb = z_ref.shape[0]
    n_pair = RWKV_W // LANES
    c_len = CHUNK
    n_chunk = tb // c_len

    @pl.when(pl.program_id(1) == 0)
    def _():
        carry_s[...] = jnp.zeros_like(carry_s)
        st_s[...] = jnp.zeros_like(st_s)

    group_chunks = 4
    group_rows = group_chunks * c_len
    n_piece = tb // group_rows
    rowq = lax.broadcasted_iota(jnp.int32, (group_rows, 1), 0)
    ri = lax.broadcasted_iota(jnp.int32, (LANES, LANES), 0)
    ci = lax.broadcasted_iota(jnp.int32, (LANES, LANES), 1)
    same_head = (ri // RWKV_N) == (ci // RWKV_N)
    ones_head = jnp.where(same_head, 1.0, 0.0).astype(BF16)
    head0 = (lax.broadcasted_iota(jnp.int32, (1, RWKV_W), 1) % LANES) < RWKV_N

    def headsum(x):
        return jnp.concatenate(
            [jnp.dot(x[:, p * LANES:(p + 1) * LANES].astype(BF16), ones_head, preferred_element_type=F32)
             for p in range(n_pair)], axis=1)

    def prep(r0, top):
        rows = pl.ds(r0, group_rows)

        def shifted(c0, c1):
            zc = z_ref[rows, c0:c1]
            zp = jnp.where(rowq == 0, top[7:8, c0:c1], pltpu.roll(zc, 1, axis=0))
            return zc + (zp - zc) * mix_ref[:, c0:c1]

        zl = shifted(3 * RWKV_W, ZR_W)
        ll = lax.broadcasted_iota(jnp.int32, (1, LORA_PAD), 1)
        act = jnp.where(ll < R_DECAY, jnp.tanh(zl),
                        jnp.where(ll < R_DECAY + R_AAA, zl, _sigmoid(zl)))
        lo = jnp.dot(act.astype(BF16), wl_ref[...], preferred_element_type=F32)
        yield
        lw = -math.exp(-0.5) * _sigmoid(w0_ref[...] + lo[:, 0:RWKV_W])
        a = _sigmoid(a0_ref[...] + lo[:, RWKV_W:2 * RWKV_W])
        g_s[rows, :] = lo[:, 2 * RWKV_W:3 * RWKV_W].astype(BF16)
        yield

        r = shifted(0, RWKV_W)
        yield
        k = shifted(RWKV_W, 2 * RWKV_W)
        yield
        v = shifted(2 * RWKV_W, 3 * RWKV_W)
        yield
        kk = k * kk_ref[...]
        ss = headsum(kk * kk)
        alpha = kk * lax.rsqrt(jnp.maximum(ss, 1e-24))
        yield
        beta = alpha * a
        k2 = k * (1.0 + (a - 1.0) * ka_ref[...])
        bo_s[rows, :] = (headsum(r * k2 * rk_ref[...]) * v).astype(BF16)
        yield

        rin = rowq % c_len
        cum = lw
        for s in (1, 2, 4, 8, 16, 32):
            cum = cum + jnp.where(rin >= s, pltpu.roll(cum, s, axis=0), 0.0)
            yield
        tot = jnp.broadcast_to(cum.reshape(group_chunks, c_len, RWKV_W)[:, c_len - 1:c_len, :],
                               (group_chunks, c_len, RWKV_W)).reshape(group_rows, RWKV_W)
        e_neg = jnp.exp(-cum)
        e_end = jnp.exp(tot - cum)
        yield
        a_t = alpha * jnp.exp(cum - lw)
        r_t = r * jnp.exp(cum)
        rt_s[rows, :] = r_t
        slab0 = r0 // c_len * 8
        if not isinstance(slab0, int):
            slab0 = pl.multiple_of(slab0, group_chunks * 8)
        dec_s[pl.ds(slab0, group_chunks * 8), :] = jnp.exp(
            tot.reshape(group_chunks, c_len, RWKV_W)[:, :8, :].reshape(group_chunks * 8, RWKV_W))
        yield
        ah0_s[rows, :] = jnp.where(head0, a_t, 0.0).astype(BF16)
        ah1_s[rows, :] = jnp.where(head0, 0.0, a_t).astype(BF16)
        yield
        vh0_s[rows, :] = jnp.where(head0, v, 0.0).astype(BF16)
        vh1_s[rows, :] = jnp.where(head0, 0.0, v).astype(BF16)
        yield
        b_n = (beta * e_neg).astype(BF16)
        bn0_s[rows, :] = jnp.where(head0, b_n, jnp.zeros_like(b_n))
        bn1_s[rows, :] = jnp.where(head0, jnp.zeros_like(b_n), b_n)
        yield
        k_n = (k2 * e_neg).astype(BF16)
        kn0_s[rows, :] = jnp.where(head0, k_n, jnp.zeros_like(k_n))
        kn1_s[rows, :] = jnp.where(head0, jnp.zeros_like(k_n), k_n)
        yield
        be_s[rows, :] = (beta * e_end).astype(BF16)
        ke_s[rows, :] = (k2 * e_end).astype(BF16)

    ti = lax.broadcasted_iota(jnp.int32, (c_len, LANES), 0)
    si = lax.broadcasted_iota(jnp.int32, (c_len, LANES), 1) % c_len
    strict = si < ti
    incl = si <= ti
    b16 = (ti // 16) == (si // 16)
    b32 = (ti // 32) == (si // 32)
    eye = jnp.where(ti == si, 1.0, 0.0).astype(F32)
    lane_h0 = lax.broadcasted_iota(jnp.int32, (1, LANES), 1) < RWKV_N
    group = group_chunks * n_pair
    nb = range(group)

    def stack(x):
        xb = x.astype(BF16)
        zero = jnp.zeros_like(xb)
        return jnp.concatenate([jnp.where(lane_h0, xb, zero), jnp.where(lane_h0, zero, xb)], axis=0)

    def mmb(xs, ys):
        return [jnp.dot(x.astype(BF16), stack(y), preferred_element_type=F32) for x, y in zip(xs, ys)]

    def solve(it):
        idx = []
        for u in range(group):
            c = it * group_chunks + u // n_pair
            p = u % n_pair
            idx.append((c, c * n_pair + p,
                        (pl.ds(pl.multiple_of(c * c_len, c_len), c_len), slice(p * LANES, (p + 1) * LANES))))
        sls = [sl for _, _, sl in idx]
        a_st = [jnp.concatenate([ah0_s[sl], ah1_s[sl]], axis=0) for sl in sls]
        lhs = [jnp.concatenate([x[:c_len] + x[c_len:], rt_s[sl].astype(BF16)], axis=0)
               for x, sl in zip(a_st, sls)]
        rhs = [jnp.concatenate([bn0_s[sl], bn1_s[sl], kn0_s[sl], kn1_s[sl]], axis=0) for sl in sls]
        gram = [lax.dot_general(x, y, (((1,), (1,)), ((), ())), preferred_element_type=F32)
                for x, y in zip(lhs, rhs)]
        a_ab = [jnp.where(strict, gm[:c_len, :LANES], 0.0) for gm in gram]
        a_kr = [jnp.concatenate([jnp.where(strict, gm[:c_len, LANES:], 0.0),
                                 jnp.where(incl, gm[c_len:, LANES:], 0.0)], axis=0).astype(BF16)
                for gm in gram]
        a_rb = [jnp.where(incl, gm[c_len:, :LANES], 0.0) for gm in gram]
        yield
        a0 = [jnp.where(b16, x, 0.0).astype(BF16) for x in a_ab]
        e1 = [jnp.where(b32 & jnp.logical_not(b16), x, 0.0).astype(BF16) for x in a_ab]
        e2 = [jnp.where(b32, 0.0, x).astype(BF16) for x in a_ab]
        p2 = mmb(a0, a0)
        yield
        p4 = mmb(p2, p2)
        yield
        p8 = mmb(p4, p4)
        t = [eye - x.astype(F32) for x in a0]
        t = [x + y for x, y in zip(t, mmb(t, p2))]
        yield
        t = [x + y for x, y in zip(t, mmb(t, p4))]
        yield
        t = [x + y for x, y in zip(t, mmb(t, p8))]
        yield
        te = mmb(t, e1)
        yield
        t = [x - y for x, y in zip(t, mmb(te, t))]
        yield
        te = mmb(t, e2)
        yield
        t = [x - y for x, y in zip(t, mmb(te, t))]
        yield
        v_st = [jnp.concatenate([vh0_s[sl], vh1_s[sl]], axis=0) for sl in sls]
        av = [jnp.dot(x, y, preferred_element_type=F32) for x, y in zip(a_kr, v_st)]
        yield
        w = [jnp.dot(x.astype(BF16), jnp.concatenate([y, stack(z[:c_len])], axis=1), preferred_element_type=F32)
             for x, y, z in zip(t, a_st, av)]
        yield
        rbw = [jnp.dot(x.astype(BF16), jnp.concatenate([stack(y[:, :LANES]), stack(y[:, LANES:])], axis=1),
                       preferred_element_type=F32) for x, y in zip(a_rb, w)]
        yield
        bw = [_mm_tn(be_s[sl], x) for sl, x in zip(sls, w)]
        yield
        kv = [_mm_tn(ke_s[sl], vh0_s[sl] + vh1_s[sl]) for sl in sls]
        yield
        for u in nb:
            c, j, sl = idx[u]
            r_hat = rt_s[sl] - rbw[u][:, :LANES]
            m_p = jnp.where(same_head, bw[u][:, :LANES], 0.0)
            mr_s[j] = jnp.concatenate([m_p, r_hat], axis=0).astype(BF16)
            y_s[sl] = av[u][c_len:] - rbw[u][:, LANES:]
            n_s[j] = jnp.where(same_head, kv[u] - bw[u][:, LANES:], 0.0).astype(BF16)

    def scan(it):
        for u in range(group_chunks):
            c = it * group_chunks + u
            rows = pl.ds(pl.multiple_of(c * c_len, c_len), c_len)
            sts = [st_s[p] for p in range(n_pair)]
            zz = [jnp.dot(mr_s[c * n_pair + p], sts[p].astype(BF16), preferred_element_type=F32)
                  for p in range(n_pair)]
            yield
            for p in range(n_pair):
                j = c * n_pair + p
                sl = (rows, slice(p * LANES, (p + 1) * LANES))
                y_s[sl] = y_s[sl] + zz[p][LANES:]
                d = dec_s[pl.ds(pl.multiple_of(c * 8, 8), 8), p * LANES:(p + 1) * LANES]
                decay_col = jnp.broadcast_to(d[0:1, :], (LANES, LANES)).T
                st_s[p] = decay_col * sts[p] - zz[p][:LANES] + n_s[j].astype(F32)
            yield

    def below(it):
        r0 = pl.multiple_of((it + 1) * group_rows, group_rows)
        return prep(r0, z_ref[pl.ds(r0 - 8, 8), :])

    _run_interleaved(prep(0, carry_s[...]))
    carry_s[...] = z_ref[tb - 8:tb, :]
    _run_interleaved(solve(0), below(0))

    def piece_body(it, carry):
        _run_interleaved(solve(it), below(it), scan(it - 1))
        return carry

    lax.fori_loop(1, n_piece - 1, piece_body, 0)
    _run_interleaved(solve(n_piece - 1), scan(n_piece - 2))
    _run_interleaved(scan(n_piece - 1))

    y = y_s[...]
    mu = headsum(y) * (1.0 / RWKV_N)
    d = y - mu
    var = headsum(d * d) * (1.0 / RWKV_N)
    yn = d * lax.rsqrt(var + LNX_EPS) * gain_ref[...] + bias_ref[...]
    y_ref[...] = ((yn + bo_s[...].astype(F32)) * g_s[...].astype(F32)).astype(y_ref.dtype)


def _rwkv(zr, bsz, seq, mix, w_lora, w0, a0, k_k, k_a, r_k, gain, bias):
    tb = ROW_TILE
    nt = seq // tb
    n_prob = (tb // CHUNK) * (RWKV_W // LANES)
    row = lambda b, t: (b * nt + t, 0)
    const = lambda b, t: (0, 0)
    vec = pl.BlockSpec((1, RWKV_W), const)
    big = pltpu.VMEM((tb, RWKV_W), F32)
    half = pltpu.VMEM((tb, RWKV_W), BF16)
    return pl.pallas_call(
        _rwkv_kernel,
        grid=(bsz, nt),
        in_specs=[
            pl.BlockSpec((tb, ZR_W), row),
            pl.BlockSpec((1, ZR_W), const),
            pl.BlockSpec((LORA_PAD, 3 * RWKV_W), const),
            vec, vec, vec, vec, vec, vec, vec,
        ],
        out_specs=pl.BlockSpec((tb, RWKV_W), row),
        out_shape=jax.ShapeDtypeStruct((bsz * seq, RWKV_W), BF16),
        scratch_shapes=[
            pltpu.VMEM((8, ZR_W), F32),
            pltpu.VMEM((RWKV_W // LANES, LANES, LANES), F32),
            half, half, big, big,
            pltpu.VMEM((tb // CHUNK * 8, RWKV_W), F32),
            half, half, half, half, half, half, half, half, half, half,
            pltpu.VMEM((n_prob, LANES + CHUNK, LANES), BF16),
            pltpu.VMEM((n_prob, LANES, LANES), BF16),
        ],
        compiler_params=_params("parallel", "arbitrary"),
        name="rwkv7",
    )(zr, mix, w_lora, w0, a0, k_k, k_a, r_k, gain, bias)


def _attn_kernel(q_ref, k_ref, v_ref, lq1_ref, lk1_ref, lq2_ref, lk2_ref, sg_ref, o_ref,
                  q_s, m_s, acc_s, *, lambda_init, tk):
    tq = q_ref.shape[0]
    n_col = tk // LANES
    n_diag = tq // tk
    qi = pl.program_id(2)
    nt = (((1,), (1,)), ((), ()))

    map0 = (lax.broadcasted_iota(jnp.int32, (1, LANES), 1) % DIFF_D) < DIFF_D // 2
    q = q_ref[...]
    zero = jnp.zeros_like(q)
    q_s[0] = jnp.where(map0, q, zero)
    q_s[1] = jnp.where(map0, zero, q)
    ones = jnp.ones((tk, LANES), BF16)
    tri_keep = [lax.broadcasted_iota(jnp.int32, (tk, LANES), 1) + j * LANES
                <= lax.broadcasted_iota(jnp.int32, (tk, LANES), 0) for j in range(n_col)]

    def kv_step(kv_rows, r0, r1, masked, first=False):
        k = k_ref[kv_rows, :]
        v_ext = jnp.concatenate([v_ref[kv_rows, :], ones], axis=1)
        for c in range(2):
            s = lax.dot_general(q_s[c, r0:r1, :], k, nt, preferred_element_type=F32)
            cols = [s[:, j * LANES:(j + 1) * LANES] for j in range(n_col)]
            if masked:
                cols = [jnp.where(tri_keep[j], cols[j], -jnp.inf) for j in range(n_col)]
            mx = cols[0]
            for j in range(1, n_col):
                mx = jnp.maximum(mx, cols[j])
            m_new = jnp.broadcast_to(jnp.max(mx, axis=-1, keepdims=True), mx.shape)
            if not first:
                m_old = m_s[c, r0:r1, :]
                m_new = jnp.maximum(m_old, m_new)
            p = jnp.concatenate([jnp.exp2(cj - m_new).astype(BF16) for cj in cols], axis=1)
            pv = jnp.dot(p, v_ext, preferred_element_type=F32)
            if first:
                acc_s[c, r0:r1, :] = pv
            else:
                corr = jnp.exp2(m_old - m_new)
                acc_s[c, r0:r1, :] = jnp.concatenate([corr, corr], axis=1) * acc_s[c, r0:r1, :] + pv
            m_s[c, r0:r1, :] = m_new

    def full_body(j, carry):
        for u in range(n_diag):
            kv_rows = pl.ds(pl.multiple_of((j * n_diag + u) * tk, tk), tk)
            kv_step(kv_rows, 0, tq // 2, False)
            kv_step(kv_rows, tq // 2, tq, False)
        return carry

    for d in range(n_diag):
        kv_rows = pl.ds(pl.multiple_of((qi * n_diag + d) * tk, tk), tk)
        kv_step(kv_rows, d * tk, (d + 1) * tk, True, first=d == 0)
        if d + 1 < n_diag:
            kv_step(kv_rows, (d + 1) * tk, tq, False, first=d == 0)
    lax.fori_loop(0, qi, full_body, 0)

    lam = (jnp.exp(jnp.sum(lq1_ref[...] * lk1_ref[...], axis=-1, keepdims=True))
           - jnp.exp(jnp.sum(lq2_ref[...] * lk2_ref[...], axis=-1, keepdims=True))
           + lambda_init)
    a1 = acc_s[0]
    a2 = acc_s[1]
    o = a1[:, :LANES] / a1[:, LANES:] - lam * (a2[:, :LANES] / a2[:, LANES:])
    o_ref[...] = (_rms(o, sg_ref[...], SUBLN_EPS) * (1.0 - lambda_init)).astype(o_ref.dtype)


def _diff_attn(q, k, v, bsz, seq, lq1, lk1, lq2, lk2, sg, lambda_init):
    tq = min(ATT_TQ, seq)
    tk = min(ATT_TK, tq)
    nq = seq // tq
    qmap = lambda b, h, i: (b * nq + i, h)
    kmap = lambda b, h, i: (b, h)
    const = lambda b, h, i: (0, 0)
    lam_spec = pl.BlockSpec((1, DIFF_D), const)
    return pl.pallas_call(
        functools.partial(_attn_kernel, lambda_init=lambda_init, tk=tk),
        grid=(bsz, DIFF_H, nq),
        in_specs=[
            pl.BlockSpec((tq, LANES), qmap),
            pl.BlockSpec((seq, LANES), kmap),
            pl.BlockSpec((seq, LANES), kmap),
            lam_spec, lam_spec, lam_spec, lam_spec,
            pl.BlockSpec((1, 2 * DIFF_D), const),
        ],
        out_specs=pl.BlockSpec((tq, LANES), qmap),
        out_shape=jax.ShapeDtypeStruct((bsz * seq, DIFF_W), BF16),
        scratch_shapes=[
            pltpu.VMEM((2, tq, LANES), BF16),
            pltpu.VMEM((2, tq, LANES), F32),
            pltpu.VMEM((2, tq, 2 * LANES), F32),
        ],
        compiler_params=_params("parallel", "parallel", "arbitrary"),
        name="diff_attn",
    )(q, k, v, lq1, lk1, lq2, lk2, sg)


def _memkv_kernel(m_ref, g_ref, w_ref, o_ref):
    h = _rms(m_ref[...], g_ref[...], NORM_EPS).astype(BF16)
    for j in range(o_ref.shape[1] // 256):
        sl = slice(j * 256, (j + 1) * 256)
        o_ref[:, sl] = jnp.dot(h, w_ref[:, sl], preferred_element_type=F32).astype(o_ref.dtype)


def _memkv(mem2d, g, wkv):
    m = mem2d.shape[0]
    tm = MEM_LEN
    return pl.pallas_call(
        _memkv_kernel,
        grid=(m // tm,),
        in_specs=[
            pl.BlockSpec((tm, D_MODEL), lambda i: (i, 0)),
            pl.BlockSpec((1, D_MODEL), lambda i: (0, 0)),
            pl.BlockSpec((D_MODEL, 2 * D_MODEL), lambda i: (0, 0)),
        ],
        out_specs=pl.BlockSpec((tm, 2 * D_MODEL), lambda i: (i, 0)),
        out_shape=jax.ShapeDtypeStruct((m, 2 * D_MODEL), BF16),
        compiler_params=_params("parallel"),
        name="memkv",
    )(mem2d, g, wkv)


def _cross_kernel(x_ref, ya_ref, yb_ref, kv_ref, wout_ref, g_ref, wq_ref, wo_ref, o_ref, att_s):
    tm = x_ref.shape[0]
    halves = [slice(0, tm // 2), slice(tm // 2, tm)]
    nt = (((1,), (1,)), ((), ()))
    x1 = [x_ref[r, :] + (jnp.dot(ya_ref[r, :], wout_ref[0:RWKV_W, :], preferred_element_type=F32)
                         + jnp.dot(yb_ref[r, :], wout_ref[RWKV_W:, :], preferred_element_type=F32))
          for r in halves]
    hc = [_rms(x, g_ref[...], NORM_EPS).astype(BF16) for x in x1]
    scale = CROSS_D ** -0.5 * math.log2(math.e)
    q = [(jnp.dot(x, wq_ref[...], preferred_element_type=F32) * scale).astype(BF16) for x in hc]
    for h in range(CROSS_H):
        sl = slice(h * CROSS_D, (h + 1) * CROSS_D)
        kh = kv_ref[:, sl]
        vh = kv_ref[:, D_MODEL + h * CROSS_D:D_MODEL + (h + 1) * CROSS_D]
        s = [lax.dot_general(x[:, sl], kh, nt, preferred_element_type=F32) for x in q]
        p = [jnp.exp2(x - jnp.max(x, axis=-1, keepdims=True)) for x in s]
        o = [jnp.dot(x.astype(BF16), vh, preferred_element_type=F32) / jnp.sum(x, axis=-1, keepdims=True)
             for x in p]
        for r, x in zip(halves, o):
            att_s[r, sl] = x.astype(BF16)
    for r, x in zip(halves, x1):
        o_ref[r, :] = x + jnp.dot(att_s[r, :], wo_ref[...], preferred_element_type=F32)


def _cross(x2d, ya, yb, kv, bsz, seq, w_out, g, wq, wo):
    tm = PROJ_TILE
    nt = seq // tm
    row = lambda b, t: (b * nt + t, 0)
    const = lambda b, t: (0, 0)
    return pl.pallas_call(
        _cross_kernel,
        grid=(bsz, nt),
        in_specs=[
            pl.BlockSpec((tm, D_MODEL), row),
            pl.BlockSpec((tm, RWKV_W), row),
            pl.BlockSpec((tm, DIFF_W), row),
            pl.BlockSpec((MEM_LEN, 2 * D_MODEL), lambda b, t: (b, 0)),
            pl.BlockSpec((D_MODEL, D_MODEL), const, pipeline_mode=pl.Buffered(1)),
            pl.BlockSpec((1, D_MODEL), const),
            pl.BlockSpec((D_MODEL, D_MODEL), const, pipeline_mode=pl.Buffered(1)),
            pl.BlockSpec((D_MODEL, D_MODEL), const, pipeline_mode=pl.Buffered(1)),
        ],
        out_specs=pl.BlockSpec((tm, D_MODEL), row),
        out_shape=jax.ShapeDtypeStruct((bsz * seq, D_MODEL), F32),
        scratch_shapes=[pltpu.VMEM((tm, D_MODEL), BF16)],
        compiler_params=_params("parallel", "parallel"),
        name="outproj_cross",
    )(x2d, ya, yb, kv, w_out, g, wq, wo)


def _ffn_kernel(x_ref, g_ref, wg_ref, wv_ref, cw_ref, cb_ref, wd_ref, gf_ref, o_ref, carry_s, act_s, *, final):
    tm = x_ref.shape[0]
    n_ff, _, tf = carry_s.shape

    @pl.when(pl.program_id(1) == 0)
    def _():
        carry_s[...] = jnp.zeros_like(carry_s)

    x = x_ref[...]
    h = _rms(x, g_ref[...], NORM_EPS).astype(BF16)

    def up(j):
        cols = slice(j * tf, (j + 1) * tf)
        return (jnp.dot(h, wg_ref[:, cols], preferred_element_type=F32),
                jnp.dot(h, wv_ref[:, cols], preferred_element_type=F32))

    split = (n_ff + 1) // 2 * tf
    nxt = up(0)
    down = []
    for j in range(n_ff):
        gate, val = nxt
        if j + 1 < n_ff:
            nxt = up(j + 1)
        ext = jnp.concatenate([carry_s[j], gate], axis=0)
        g1 = ext[7:tm + 7, :]
        g2 = ext[6:tm + 6, :]
        carry_s[j] = gate[tm - 8:tm, :]
        cwh = 0.5 * cw_ref[:, j * tf:(j + 1) * tf]
        hc = cwh[0:1, :] * g2 + cwh[1:2, :] * g1 + cwh[2:3, :] * gate + 0.5 * cb_ref[:, j * tf:(j + 1) * tf]
        act_s[:, j * tf:(j + 1) * tf] = ((hc + hc * jnp.tanh(hc)) * val).astype(BF16)
        if (j + 1) * tf == split:
            down.append(jnp.dot(act_s[:, :split], wd_ref[:split, :], preferred_element_type=F32))
    down.append(jnp.dot(act_s[:, split:], wd_ref[split:, :], preferred_element_type=F32))
    out = x + (down[0] + down[1])
    o_ref[...] = _rms(out, gf_ref[...], NORM_EPS) if final else out


def _ffn(x2d, bsz, seq, g, wg, wv, cw, cb, wd, gf, final):
    tm = PROJ_TILE
    nt = seq // tm
    tf = FF_TILE
    n_ff = D_FF // tf
    row = lambda b, t: (b * nt + t, 0)
    c2 = lambda b, t: (0, 0)
    once = pl.Buffered(1)
    return pl.pallas_call(
        functools.partial(_ffn_kernel, final=final),
        grid=(bsz, nt),
        in_specs=[
            pl.BlockSpec((tm, D_MODEL), row),
            pl.BlockSpec((1, D_MODEL), c2),
            pl.BlockSpec((D_MODEL, D_FF), c2, pipeline_mode=once),
            pl.BlockSpec((D_MODEL, D_FF), c2, pipeline_mode=once),
            pl.BlockSpec((3, D_FF), c2),
            pl.BlockSpec((1, D_FF), c2),
            pl.BlockSpec((D_FF, D_MODEL), c2, pipeline_mode=once),
            pl.BlockSpec((1, D_MODEL), c2),
        ],
        out_specs=pl.BlockSpec((tm, D_MODEL), row),
        out_shape=jax.ShapeDtypeStruct((bsz * seq, D_MODEL), F32),
        scratch_shapes=[pltpu.VMEM((n_ff, 8, tf), F32), pltpu.VMEM((tm, n_ff * tf), BF16)],
        compiler_params=_params("parallel", "arbitrary"),
        name="conv_ffn",
    )(x2d, g, wg, wv, cw, cb, wd, gf)


def _rope_perm():
    idx = np.empty((DIFF_W,), np.int32)
    half = DIFF_D // 2
    for h in range(DIFF_H):
        for c in range(2):
            for d in range(DIFF_D):
                idx[h * LANES + (d // half) * DIFF_D + c * half + d % half] = h * LANES + c * DIFF_D + d
    return idx


def _layer(x2d, pos2d, mem2d, bsz, seq, lambda_init, p):
    w_in = p["w_in"]
    pad = jnp.zeros((D_MODEL, LORA_PAD - (N_SHIFT - 3 * RWKV_W)), F32)
    w_r = jnp.concatenate([w_in[:, :N_SHIFT], pad], axis=1).astype(BF16)
    perm = _rope_perm()
    w_d = jnp.concatenate([w_in[:, N_SHIFT:N_SHIFT + DIFF_W][:, perm],
                           w_in[:, N_SHIFT + DIFF_W:N_SHIFT + 2 * DIFF_W][:, perm],
                           w_in[:, N_SHIFT + 2 * DIFF_W:]], axis=1).astype(BF16)
    mix = jnp.concatenate([p["shift_mix"], jnp.zeros((LORA_PAD - (N_SHIFT - 3 * RWKV_W),), F32)])[None, :]
    w_lora = jnp.zeros((LORA_PAD, 3 * RWKV_W), F32)
    w_lora = w_lora.at[0:R_DECAY, 0:RWKV_W].set(p["w_lora_up"])
    w_lora = w_lora.at[R_DECAY:R_DECAY + R_AAA, RWKV_W:2 * RWKV_W].set(p["a_lora_up"])
    w_lora = w_lora.at[R_DECAY + R_AAA:R_DECAY + R_AAA + R_GATE, 2 * RWKV_W:].set(p["g_lora_up"])
    w_lora = w_lora.astype(BF16)
    v512 = lambda a: a.reshape(1, RWKV_W)

    zr, q, k, v = _inproj(x2d, pos2d, p["norm_mix"][None, :], w_r, w_d)
    ya = _rwkv(zr, bsz, seq, mix, w_lora, v512(p["w0"]), v512(p["a0"]), v512(p["k_k"]),
               v512(p["k_a"]), v512(p["r_k"]), v512(p["lnx_gain"]), v512(p["lnx_bias"]))
    yb = _diff_attn(q, k, v, bsz, seq, p["lam_q1"][None, :], p["lam_k1"][None, :],
                    p["lam_q2"][None, :], p["lam_k2"][None, :], p["subln_gain"][None, :], lambda_init)
    kv = _memkv(mem2d, p["norm_mem"][None, :], p["wkv_c"].astype(BF16))
    x2 = _cross(x2d, ya, yb, kv, bsz, seq, p["w_out"].astype(BF16), p["norm_cross"][None, :],
                p["wq_c"].astype(BF16), p["wo_c"].astype(BF16))
    w_up = p["w_up"]
    wg = w_up[:, :D_FF].astype(BF16)
    wv = w_up[:, D_FF:].astype(BF16)
    cw = p["conv_w"]
    cb = p["conv_b"][None, :]
    wd = p["w_down"].astype(BF16)
    return x2, (p["norm_ffn"][None, :], wg, wv, cw, cb, wd)


def kernel(x, mem, positions, norm_mix, w_in, shift_mix, w0, w_lora_up, a0, a_lora_up, g_lora_up, k_k, k_a, r_k, lnx_gain, lnx_bias, lam_q1, lam_k1, lam_q2, lam_k2, subln_gain, w_out, norm_cross, norm_mem, wq_c, wkv_c, wo_c, norm_ffn, w_up, conv_w, conv_b, w_down, norm_final):
    bsz, seq, _ = x.shape
    depth = norm_mix.shape[0]
    x2d = x.reshape(bsz * seq, D_MODEL)
    pos2d = positions.reshape(bsz * seq, 1)
    mem2d = mem.reshape(bsz * mem.shape[1], D_MODEL)
    stacked = dict(norm_mix=norm_mix, w_in=w_in, shift_mix=shift_mix, w0=w0, w_lora_up=w_lora_up,
                   a0=a0, a_lora_up=a_lora_up, g_lora_up=g_lora_up, k_k=k_k, k_a=k_a,
                   r_k=r_k.reshape(depth, RWKV_W), lnx_gain=lnx_gain, lnx_bias=lnx_bias,
                   lam_q1=lam_q1, lam_k1=lam_k1, lam_q2=lam_q2, lam_k2=lam_k2,
                   subln_gain=subln_gain, w_out=w_out, norm_cross=norm_cross, norm_mem=norm_mem,
                   wq_c=wq_c, wkv_c=wkv_c, wo_c=wo_c, norm_ffn=norm_ffn, w_up=w_up,
                   conv_w=conv_w, conv_b=conv_b, w_down=w_down)
    for l in range(depth):
        p = {name: a[l] for name, a in stacked.items()}
        lambda_init = 0.8 - 0.6 * math.exp(-0.3 * l)
        x2, (gn, wg, wv, cw, cb, wd) = _layer(x2d, pos2d, mem2d, bsz, seq, lambda_init, p)
        x2d = _ffn(x2, bsz, seq, gn, wg, wv, cw, cb, wd, norm_final[None, :], l == depth - 1)
    return x2d.reshape(bsz, seq, D_MODEL)
```

```python
import functools
import math

import numpy as np
import jax
import jax.numpy as jnp
from jax import lax
from jax.experimental import pallas as pl
from jax.experimental.pallas import tpu as pltpu

F32 = jnp.float32
BF16 = jnp.bfloat16

D_MODEL = 1024
RWKV_W = 512
RWKV_N = 64
R_DECAY = 32
R_AAA = 32
R_GATE = 96
DIFF_W = 512
DIFF_H = 4
DIFF_D = 64
N_SHIFT = 3 * RWKV_W + R_DECAY + R_AAA + R_GATE
MEM_LEN = 256
CROSS_H = 4
CROSS_D = D_MODEL // CROSS_H
D_FF = 2816
ROPE_THETA = 10000.0
NORM_EPS = 1e-6
LNX_EPS = 64e-5
SUBLN_EPS = 1e-5

LANES = 128
LORA_PAD = 256
ZR_W = 3 * RWKV_W + LORA_PAD
VMEM_LIMIT = 56 * 1024 * 1024

ROW_TILE = 1024
PROJ_TILE = 1024
CHUNK = 64
ATT_TQ = 2048
ATT_TK = 512
FF_TILE = 256
TRIG_BLOCKS = 4


def _mm_tn(a, b):
    return lax.dot_general(a.astype(BF16), b.astype(BF16), (((0,), (0,)), ((), ())),
                           preferred_element_type=F32)


def _sigmoid(x):
    return 0.5 * jnp.tanh(0.5 * x) + 0.5


def _rms(x, g, eps):
    return x * lax.rsqrt(jnp.mean(x * x, axis=-1, keepdims=True) + eps) * g


def _run_interleaved(*stage_gens):
    live = list(stage_gens)
    while live:
        for g in list(live):
            try:
                next(g)
            except StopIteration:
                live.remove(g)


def _params(*sem):
    return pltpu.CompilerParams(dimension_semantics=sem, vmem_limit_bytes=VMEM_LIMIT)


def _inproj_kernel(x_ref, pos_ref, g_ref, wr_ref, wd_ref, zr_ref, q_ref, k_ref, v_ref, trig_s):
    tm = x_ref.shape[0]
    lane = lax.broadcasted_iota(jnp.int32, (1, LANES), 1)
    freq = (lane % (DIFF_D // 2)).astype(F32)
    inv = jnp.exp(freq * (-2.0 / DIFF_D * math.log(ROPE_THETA)))
    scale = DIFF_D ** -0.5 * math.log2(math.e)

    def rope_tables():
        n_blk = TRIG_BLOCKS
        rb = tm // n_blk
        for i in range(n_blk):
            rows = slice(i * rb, (i + 1) * rb)
            ang = pos_ref[rows, :].astype(F32) * inv
            cos = jnp.cos(ang)
            sin = jnp.where(lane < LANES // 2, -jnp.sin(ang), jnp.sin(ang))
            trig_s[0, rows, :] = cos
            trig_s[1, rows, :] = sin
            trig_s[2, rows, :] = cos * scale
            trig_s[3, rows, :] = sin * scale
            yield

    h = _rms(x_ref[...], g_ref[...], NORM_EPS).astype(BF16)

    def rwkv_slab():
        n_blk = ZR_W // 256
        for j in range(n_blk):
            sl = slice(j * 256, (j + 1) * 256)
            lhs = h
            t = j - (n_blk - TRIG_BLOCKS)
            if t >= 0:
                r = (t + 1) * (tm // TRIG_BLOCKS) - 8
                lhs = jnp.where(trig_s[0, r:r + 8, 0:1][0:1, :] < 2.0, h, jnp.zeros_like(h))
            zr_ref[:, sl] = jnp.dot(lhs, wr_ref[:, sl], preferred_element_type=F32)
            yield

    _run_interleaved(rope_tables(), rwkv_slab())
    cos, sin, cos_q, sin_q = trig_s[0], trig_s[1], trig_s[2], trig_s[3]

    def rope(xb, c, s):
        return xb * c + pltpu.roll(xb, LANES // 2, axis=1) * s

    for j in range(DIFF_W // 256):
        zq = jnp.dot(h, wd_ref[:, j * 256:(j + 1) * 256], preferred_element_type=F32)
        zk = jnp.dot(h, wd_ref[:, DIFF_W + j * 256:DIFF_W + (j + 1) * 256],
                     preferred_element_type=F32)
        zv = jnp.dot(h, wd_ref[:, 2 * DIFF_W + j * 256:2 * DIFF_W + (j + 1) * 256],
                     preferred_element_type=F32)
        for u in range(2):
            c0 = j * 256 + u * LANES
            q_ref[:, c0:c0 + LANES] = rope(zq[:, u * LANES:(u + 1) * LANES], cos_q, sin_q).astype(BF16)
            k_ref[:, c0:c0 + LANES] = rope(zk[:, u * LANES:(u + 1) * LANES], cos, sin).astype(BF16)
        v_ref[:, j * 256:(j + 1) * 256] = zv.astype(BF16)


def _inproj(x2d, pos2d, g, w_r, w_d):
    m = x2d.shape[0]
    tm = PROJ_TILE
    row = lambda i: (i, 0)
    const = lambda i: (0, 0)
    return pl.pallas_call(
        _inproj_kernel,
        grid=(m // tm,),
        in_specs=[
            pl.BlockSpec((tm, D_MODEL), row),
            pl.BlockSpec((tm, 1), row),
            pl.BlockSpec((1, D_MODEL), const),
            pl.BlockSpec((D_MODEL, ZR_W), const, pipeline_mode=pl.Buffered(1)),
            pl.BlockSpec((D_MODEL, 3 * DIFF_W), const, pipeline_mode=pl.Buffered(1)),
        ],
        out_specs=[
            pl.BlockSpec((tm, ZR_W), row),
            pl.BlockSpec((tm, DIFF_W), row),
            pl.BlockSpec((tm, DIFF_W), row),
            pl.BlockSpec((tm, DIFF_W), row),
        ],
        out_shape=[
            jax.ShapeDtypeStruct((m, ZR_W), F32),
            jax.ShapeDtypeStruct((m, DIFF_W), BF16),
            jax.ShapeDtypeStruct((m, DIFF_W), BF16),
            jax.ShapeDtypeStruct((m, DIFF_W), BF16),
        ],
        scratch_shapes=[pltpu.VMEM((4, tm, LANES), F32)],
        compiler_params=_params("parallel"),
        name="inproj",
    )(x2d, pos2d, g, w_r, w_d)


def _rwkv_kernel(z_ref, mix_ref, wl_ref, w0_ref, a0_ref, kk_ref, ka_ref, rk_ref, gain_ref,
                  bias_ref, y_ref, carry_s, st_s, g_s, bo_s, y_s, rt_s, dec_s,
                  ah0_s, ah1_s, bn0_s, bn1_s, kn0_s, kn1_s, vh0_s, vh1_s, be_s, ke_s,
                  mr_s, n_s):
    tb = z_ref.shape[0]
    n_pair = RWKV_W // LANES
    c_len = CHUNK
    n_chunk = tb // c_len

    @pl.when(pl.program_id(1) == 0)
    def _():
        carry_s[...] = jnp.zeros_like(carry_s)
        st_s[...] = jnp.zeros_like(st_s)

    group_chunks = 4
    group_rows = group_chunks * c_len
    n_piece = tb // group_rows
    rowq = lax.broadcasted_iota(jnp.int32, (group_rows, 1), 0)
    ri = lax.broadcasted_iota(jnp.int32, (LANES, LANES), 0)
    ci = lax.broadcasted_iota(jnp.int32, (LANES, LANES), 1)
    same_head = (ri // RWKV_N) == (ci // RWKV_N)
    ones_head = jnp.where(same_head, 1.0, 0.0).astype(BF16)
    head0 = (lax.broadcasted_iota(jnp.int32, (1, RWKV_W), 1) % LANES) < RWKV_N

    def headsum(x):
        return jnp.concatenate(
            [jnp.dot(x[:, p * LANES:(p + 1) * LANES].astype(BF16), ones_head, preferred_element_type=F32)
             for p in range(n_pair)], axis=1)

    def prep(r0, top):
        rows = pl.ds(r0, group_rows)

        def shifted(c0, c1):
            zc = z_ref[rows, c0:c1]
            zp = jnp.where(rowq == 0, top[7:8, c0:c1], pltpu.roll(zc, 1, axis=0))
            return zc + (zp - zc) * mix_ref[:, c0:c1]

        zl = shifted(3 * RWKV_W, ZR_W)
        ll = lax.broadcasted_iota(jnp.int32, (1, LORA_PAD), 1)
        act = jnp.where(ll < R_DECAY, jnp.tanh(zl),
                        jnp.where(ll < R_DECAY + R_AAA, zl, _sigmoid(zl)))
        lo = jnp.dot(act.astype(BF16), wl_ref[...], preferred_element_type=F32)
        yield
        lw = -math.exp(-0.5) * _sigmoid(w0_ref[...] + lo[:, 0:RWKV_W])
        a = _sigmoid(a0_ref[...] + lo[:, RWKV_W:2 * RWKV_W])
        g_s[rows, :] = lo[:, 2 * RWKV_W:3 * RWKV_W].astype(BF16)
        yield

        r = shifted(0, RWKV_W)
        yield
        k = shifted(RWKV_W, 2 * RWKV_W)
        yield
        v = shifted(2 * RWKV_W, 3 * RWKV_W)
        yield
        kk = k * kk_ref[...]
        ss = headsum(kk * kk)
        alpha = kk * lax.rsqrt(jnp.maximum(ss, 1e-24))
        yield
        beta = alpha * a
        k2 = k * (1.0 + (a - 1.0) * ka_ref[...])
        bo_s[rows, :] = (headsum(r * k2 * rk_ref[...]) * v).astype(BF16)
        yield

        rin = rowq % c_len
        cum = lw
        for s in (1, 2, 4, 8, 16, 32):
            cum = cum + jnp.where(rin >= s, pltpu.roll(cum, s, axis=0), 0.0)
            yield
        tot = jnp.broadcast_to(cum.reshape(group_chunks, c_len, RWKV_W)[:, c_len - 1:c_len, :],
                               (group_chunks, c_len, RWKV_W)).reshape(group_rows, RWKV_W)
        e_neg = jnp.exp(-cum)
        e_end = jnp.exp(tot - cum)
        yield
        a_t = alpha * jnp.exp(cum - lw)
        r_t = r * jnp.exp(cum)
        rt_s[rows, :] = r_t
        slab0 = r0 // c_len * 8
        if not isinstance(slab0, int):
            slab0 = pl.multiple_of(slab0, group_chunks * 8)
        dec_s[pl.ds(slab0, group_chunks * 8), :] = jnp.exp(
            tot.reshape(group_chunks, c_len, RWKV_W)[:, :8, :].reshape(group_chunks * 8, RWKV_W))
        yield
        ah0_s[rows, :] = jnp.where(head0, a_t, 0.0).astype(BF16)
        ah1_s[rows, :] = jnp.where(head0, 0.0, a_t).astype(BF16)
        yield
        vh0_s[rows, :] = jnp.where(head0, v, 0.0).astype(BF16)
        vh1_s[rows, :] = jnp.where(head0, 0.0, v).astype(BF16)
        yield
        b_n = (beta * e_neg).astype(BF16)
        bn0_s[rows, :] = jnp.where(head0, b_n, jnp.zeros_like(b_n))
        bn1_s[rows, :] = jnp.where(head0, jnp.zeros_like(b_n), b_n)
        yield
        k_n = (k2 * e_neg).astype(BF16)
        kn0_s[rows, :] = jnp.where(head0, k_n, jnp.zeros_like(k_n))
        kn1_s[rows, :] = jnp.where(head0, jnp.zeros_like(k_n), k_n)
        yield
        be_s[rows, :] = (beta * e_end).astype(BF16)
        ke_s[rows, :] = (k2 * e_end).astype(BF16)

    ti = lax.broadcasted_iota(jnp.int32, (c_len, LANES), 0)
    si = lax.broadcasted_iota(jnp.int32, (c_len, LANES), 1) % c_len
    strict = si < ti
    incl = si <= ti
    b16 = (ti // 16) == (si // 16)
    b32 = (ti // 32) == (si // 32)
    eye = jnp.where(ti == si, 1.0, 0.0).astype(F32)
    lane_h0 = lax.broadcasted_iota(jnp.int32, (1, LANES), 1) < RWKV_N
    group = group_chunks * n_pair
    nb = range(group)

    def stack(x):
        xb = x.astype(BF16)
        zero = jnp.zeros_like(xb)
        return jnp.concatenate([jnp.where(lane_h0, xb, zero), jnp.where(lane_h0, zero, xb)], axis=0)

    def mmb(xs, ys):
        return [jnp.dot(x.astype(BF16), stack(y), preferred_element_type=F32) for x, y in zip(xs, ys)]

    def solve(it):
        idx = []
        for u in range(group):
            c = it * group_chunks + u // n_pair
            p = u % n_pair
            idx.append((c, c * n_pair + p,
                        (pl.ds(pl.multiple_of(c * c_len, c_len), c_len), slice(p * LANES, (p + 1) * LANES))))
        sls = [sl for _, _, sl in idx]
        a_st = [jnp.concatenate([ah0_s[sl], ah1_s[sl]], axis=0) for sl in sls]
        lhs = [jnp.concatenate([x[:c_len] + x[c_len:], rt_s[sl].astype(BF16)], axis=0)
               for x, sl in zip(a_st, sls)]
        rhs = [jnp.concatenate([bn0_s[sl], bn1_s[sl], kn0_s[sl], kn1_s[sl]], axis=0) for sl in sls]
        gram = [lax.dot_general(x, y, (((1,), (1,)), ((), ())), preferred_element_type=F32)
                for x, y in zip(lhs, rhs)]
        a_ab = [jnp.where(strict, gm[:c_len, :LANES], 0.0) for gm in gram]
        a_kr = [jnp.concatenate([jnp.where(strict, gm[:c_len, LANES:], 0.0),
                                 jnp.where(incl, gm[c_len:, LANES:], 0.0)], axis=0).astype(BF16)
                for gm in gram]
        a_rb = [jnp.where(incl, gm[c_len:, :LANES], 0.0) for gm in gram]
        yield
        a0 = [jnp.where(b16, x, 0.0).astype(BF16) for x in a_ab]
        e1 = [jnp.where(b32 & jnp.logical_not(b16), x, 0.0).astype(BF16) for x in a_ab]
        e2 = [jnp.where(b32, 0.0, x).astype(BF16) for x in a_ab]
        p2 = mmb(a0, a0)
        yield
        p4 = mmb(p2, p2)
        yield
        p8 = mmb(p4, p4)
        t = [eye - x.astype(F32) for x in a0]
        t = [x + y for x, y in zip(t, mmb(t, p2))]
        yield
        t = [x + y for x, y in zip(t, mmb(t, p4))]
        yield
        t = [x + y for x, y in zip(t, mmb(t, p8))]
        yield
        te = mmb(t, e1)
        yield
        t = [x - y for x, y in zip(t, mmb(te, t))]
        yield
        te = mmb(t, e2)
        yield
        t = [x - y for x, y in zip(t, mmb(te, t))]
        yield
        v_st = [jnp.concatenate([vh0_s[sl], vh1_s[sl]], axis=0) for sl in sls]
        av = [jnp.dot(x, y, preferred_element_type=F32) for x, y in zip(a_kr, v_st)]
        yield
        w = [jnp.dot(x.astype(BF16), jnp.concatenate([y, stack(z[:c_len])], axis=1), preferred_element_type=F32)
             for x, y, z in zip(t, a_st, av)]
        yield
        rbw = [jnp.dot(x.astype(BF16), jnp.concatenate([stack(y[:, :LANES]), stack(y[:, LANES:])], axis=1),
                       preferred_element_type=F32) for x, y in zip(a_rb, w)]
        yield
        bw = [_mm_tn(be_s[sl], x) for sl, x in zip(sls, w)]
        yield
        kv = [_mm_tn(ke_s[sl], vh0_s[sl] + vh1_s[sl]) for sl in sls]
        yield
        for u in nb:
            c, j, sl = idx[u]
            r_hat = rt_s[sl] - rbw[u][:, :LANES]
            m_p = jnp.where(same_head, bw[u][:, :LANES], 0.0)
            mr_s[j] = jnp.concatenate([m_p, r_hat], axis=0).astype(BF16)
            y_s[sl] = av[u][c_len:] - rbw[u][:, LANES:]
            n_s[j] = jnp.where(same_head, kv[u] - bw[u][:, LANES:], 0.0).astype(BF16)

    def scan(it):
        for u in range(group_chunks):
            c = it * group_chunks + u
            rows = pl.ds(pl.multiple_of(c * c_len, c_len), c_len)
            sts = [st_s[p] for p in range(n_pair)]
            zz = [jnp.dot(mr_s[c * n_pair + p], sts[p].astype(BF16), preferred_element_type=F32)
                  for p in range(n_pair)]
            yield
            for p in range(n_pair):
                j = c * n_pair + p
                sl = (rows, slice(p * LANES, (p + 1) * LANES))
                y_s[sl] = y_s[sl] + zz[p][LANES:]
                d = dec_s[pl.ds(pl.multiple_of(c * 8, 8), 8), p * LANES:(p + 1) * LANES]
                decay_col = jnp.broadcast_to(d[0:1, :], (LANES, LANES)).T
                st_s[p] = decay_col * sts[p] - zz[p][:LANES] + n_s[j].astype(F32)
            yield

    def below(it):
        r0 = pl.multiple_of((it + 1) * group_rows, group_rows)
        return prep(r0, z_ref[pl.ds(r0 - 8, 8), :])

    _run_interleaved(prep(0, carry_s[...]))
    carry_s[...] = z_ref[tb - 8:tb, :]
    _run_interleaved(solve(0), below(0))

    def piece_body(it, carry):
        _run_interleaved(solve(it), below(it), scan(it - 1))
        return carry

    lax.fori_loop(1, n_piece - 1, piece_body, 0)
    _run_interleaved(solve(n_piece - 1), scan(n_piece - 2))
    _run_interleaved(scan(n_piece - 1))

    y = y_s[...]
    mu = headsum(y) * (1.0 / RWKV_N)
    d = y - mu
    var = headsum(d * d) * (1.0 / RWKV_N)
    yn = d * lax.rsqrt(var + LNX_EPS) * gain_ref[...] + bias_ref[...]
    y_ref[...] = ((yn + bo_s[...].astype(F32)) * g_s[...].astype(F32)).astype(y_ref.dtype)


def _rwkv(zr, bsz, seq, mix, w_lora, w0, a0, k_k, k_a, r_k, gain, bias):
    tb = ROW_TILE
    nt = seq // tb
    n_prob = (tb // CHUNK) * (RWKV_W // LANES)
    row = lambda b, t: (b * nt + t, 0)
    const = lambda b, t: (0, 0)
    vec = pl.BlockSpec((1, RWKV_W), const)
    big = pltpu.VMEM((tb, RWKV_W), F32)
    half = pltpu.VMEM((tb, RWKV_W), BF16)
    return pl.pallas_call(
        _rwkv_kernel,
        grid=(bsz, nt),
        in_specs=[
            pl.BlockSpec((tb, ZR_W), row),
            pl.BlockSpec((1, ZR_W), const),
            pl.BlockSpec((LORA_PAD, 3 * RWKV_W), const),
            vec, vec, vec, vec, vec, vec, vec,
        ],
        out_specs=pl.BlockSpec((tb, RWKV_W), row),
        out_shape=jax.ShapeDtypeStruct((bsz * seq, RWKV_W), BF16),
        scratch_shapes=[
            pltpu.VMEM((8, ZR_W), F32),
            pltpu.VMEM((RWKV_W // LANES, LANES, LANES), F32),
            half, half, big, big,
            pltpu.VMEM((tb // CHUNK * 8, RWKV_W), F32),
            half, half, half, half, half, half, half, half, half, half,
            pltpu.VMEM((n_prob, LANES + CHUNK, LANES), BF16),
            pltpu.VMEM((n_prob, LANES, LANES), BF16),
        ],
        compiler_params=_params("parallel", "arbitrary"),
        name="rwkv7",
    )(zr, mix, w_lora, w0, a0, k_k, k_a, r_k, gain, bias)


def _attn_kernel(q_ref, k_ref, v_ref, lq1_ref, lk1_ref, lq2_ref, lk2_ref, sg_ref, o_ref,
                  q_s, m_s, acc_s, *, lambda_init, tk):
    tq = q_ref.shape[0]
    n_col = tk // LANES
    n_diag = tq // tk
    qi = pl.program_id(2)
    nt = (((1,), (1,)), ((), ()))

    map0 = (lax.broadcasted_iota(jnp.int32, (1, LANES), 1) % DIFF_D) < DIFF_D // 2
    q = q_ref[...]
    zero = jnp.zeros_like(q)
    q_s[0] = jnp.where(map0, q, zero)
    q_s[1] = jnp.where(map0, zero, q)
    ones = jnp.ones((tk, LANES), BF16)
    tri_keep = [lax.broadcasted_iota(jnp.int32, (tk, LANES), 1) + j * LANES
                <= lax.broadcasted_iota(jnp.int32, (tk, LANES), 0) for j in range(n_col)]

    def kv_step(kv_rows, r0, r1, masked, first=False):
        k = k_ref[kv_rows, :]
        v_ext = jnp.concatenate([v_ref[kv_rows, :], ones], axis=1)
        for c in range(2):
            s = lax.dot_general(q_s[c, r0:r1, :], k, nt, preferred_element_type=F32)
            cols = [s[:, j * LANES:(j + 1) * LANES] for j in range(n_col)]
            if masked:
                cols = [jnp.where(tri_keep[j], cols[j], -jnp.inf) for j in range(n_col)]
            mx = cols[0]
            for j in range(1, n_col):
                mx = jnp.maximum(mx, cols[j])
            m_new = jnp.broadcast_to(jnp.max(mx, axis=-1, keepdims=True), mx.shape)
            if not first:
                m_old = m_s[c, r0:r1, :]
                m_new = jnp.maximum(m_old, m_new)
            p = jnp.concatenate([jnp.exp2(cj - m_new).astype(BF16) for cj in cols], axis=1)
            pv = jnp.dot(p, v_ext, preferred_element_type=F32)
            if first:
                acc_s[c, r0:r1, :] = pv
            else:
                corr = jnp.exp2(m_old - m_new)
                acc_s[c, r0:r1, :] = jnp.concatenate([corr, corr], axis=1) * acc_s[c, r0:r1, :] + pv
            m_s[c, r0:r1, :] = m_new

    def full_body(j, carry):
        for u in range(n_diag):
            kv_rows = pl.ds(pl.multiple_of((j * n_diag + u) * tk, tk), tk)
            kv_step(kv_rows, 0, tq // 2, False)
            kv_step(kv_rows, tq // 2, tq, False)
        return carry

    for d in range(n_diag):
        kv_rows = pl.ds(pl.multiple_of((qi * n_diag + d) * tk, tk), tk)
        kv_step(kv_rows, d * tk, (d + 1) * tk, True, first=d == 0)
        if d + 1 < n_diag:
            kv_step(kv_rows, (d + 1) * tk, tq, False, first=d == 0)
    lax.fori_loop(0, qi, full_body, 0)

    lam = (jnp.exp(jnp.sum(lq1_ref[...] * lk1_ref[...], axis=-1, keepdims=True))
           - jnp.exp(jnp.sum(lq2_ref[...] * lk2_ref[...], axis=-1, keepdims=True))
           + lambda_init)
    a1 = acc_s[0]
    a2 = acc_s[1]
    o = a1[:, :LANES] / a1[:, LANES:] - lam * (a2[:, :LANES] / a2[:, LANES:])
    o_ref[...] = (_rms(o, sg_ref[...], SUBLN_EPS) * (1.0 - lambda_init)).astype(o_ref.dtype)


def _diff_attn(q, k, v, bsz, seq, lq1, lk1, lq2, lk2, sg, lambda_init):
    tq = min(ATT_TQ, seq)
    tk = min(ATT_TK, tq)
    nq = seq // tq
    qmap = lambda b, h, i: (b * nq + i, h)
    kmap = lambda b, h, i: (b, h)
    const = lambda b, h, i: (0, 0)
    lam_spec = pl.BlockSpec((1, DIFF_D), const)
    return pl.pallas_call(
        functools.partial(_attn_kernel, lambda_init=lambda_init, tk=tk),
        grid=(bsz, DIFF_H, nq),
        in_specs=[
            pl.BlockSpec((tq, LANES), qmap),
            pl.BlockSpec((seq, LANES), kmap),
            pl.BlockSpec((seq, LANES), kmap),
            lam_spec, lam_spec, lam_spec, lam_spec,
            pl.BlockSpec((1, 2 * DIFF_D), const),
        ],
        out_specs=pl.BlockSpec((tq, LANES), qmap),
        out_shape=jax.ShapeDtypeStruct((bsz * seq, DIFF_W), BF16),
        scratch_shapes=[
            pltpu.VMEM((2, tq, LANES), BF16),
            pltpu.VMEM((2, tq, LANES), F32),
            pltpu.VMEM((2, tq, 2 * LANES), F32),
        ],
        compiler_params=_params("parallel", "parallel", "arbitrary"),
        name="diff_attn",
    )(q, k, v, lq1, lk1, lq2, lk2, sg)


def _memkv_kernel(m_ref, g_ref, w_ref, o_ref):
    h = _rms(m_ref[...], g_ref[...], NORM_EPS).astype(BF16)
    for j in range(o_ref.shape[1] // 256):
        sl = slice(j * 256, (j + 1) * 256)
        o_ref[:, sl] = jnp.dot(h, w_ref[:, sl], preferred_element_type=F32).astype(o_ref.dtype)


def _memkv(mem2d, g, wkv):
    m = mem2d.shape[0]
    tm = MEM_LEN
    return pl.pallas_call(
        _memkv_kernel,
        grid=(m // tm,),
        in_specs=[
            pl.BlockSpec((tm, D_MODEL), lambda i: (i, 0)),
            pl.BlockSpec((1, D_MODEL), lambda i: (0, 0)),
            pl.BlockSpec((D_MODEL, 2 * D_MODEL), lambda i: (0, 0)),
        ],
        out_specs=pl.BlockSpec((tm, 2 * D_MODEL), lambda i: (i, 0)),
        out_shape=jax.ShapeDtypeStruct((m, 2 * D_MODEL), BF16),
        compiler_params=_params("parallel"),
        name="memkv",
    )(mem2d, g, wkv)


def _cross_kernel(x_ref, ya_ref, yb_ref, kv_ref, wout_ref, g_ref, wq_ref, wo_ref, o_ref, att_s):
    tm = x_ref.shape[0]
    halves = [slice(0, tm // 2), slice(tm // 2, tm)]
    nt = (((1,), (1,)), ((), ()))
    x1 = [x_ref[r, :] + (jnp.dot(ya_ref[r, :], wout_ref[0:RWKV_W, :], preferred_element_type=F32)
                         + jnp.dot(yb_ref[r, :], wout_ref[RWKV_W:, :], preferred_element_type=F32))
          for r in halves]
    hc = [_rms(x, g_ref[...], NORM_EPS).astype(BF16) for x in x1]
    scale = CROSS_D ** -0.5 * math.log2(math.e)
    q = [(jnp.dot(x, wq_ref[...], preferred_element_type=F32) * scale).astype(BF16) for x in hc]
    for h in range(CROSS_H):
        sl = slice(h * CROSS_D, (h + 1) * CROSS_D)
        kh = kv_ref[:, sl]
        vh = kv_ref[:, D_MODEL + h * CROSS_D:D_MODEL + (h + 1) * CROSS_D]
        s = [lax.dot_general(x[:, sl], kh, nt, preferred_element_type=F32) for x in q]
        p = [jnp.exp2(x - jnp.max(x, axis=-1, keepdims=True)) for x in s]
        o = [jnp.dot(x.astype(BF16), vh, preferred_element_type=F32) / jnp.sum(x, axis=-1, keepdims=True)
             for x in p]
        for r, x in zip(halves, o):
            att_s[r, sl] = x.astype(BF16)
    for r, x in zip(halves, x1):
        o_ref[r, :] = x + jnp.dot(att_s[r, :], wo_ref[...], preferred_element_type=F32)


def _cross(x2d, ya, yb, kv, bsz, seq, w_out, g, wq, wo):
    tm = PROJ_TILE
    nt = seq // tm
    row = lambda b, t: (b * nt + t, 0)
    const = lambda b, t: (0, 0)
    return pl.pallas_call(
        _cross_kernel,
        grid=(bsz, nt),
        in_specs=[
            pl.BlockSpec((tm, D_MODEL), row),
            pl.BlockSpec((tm, RWKV_W), row),
            pl.BlockSpec((tm, DIFF_W), row),
            pl.BlockSpec((MEM_LEN, 2 * D_MODEL), lambda b, t: (b, 0)),
            pl.BlockSpec((D_MODEL, D_MODEL), const, pipeline_mode=pl.Buffered(1)),
            pl.BlockSpec((1, D_MODEL), const),
            pl.BlockSpec((D_MODEL, D_MODEL), const, pipeline_mode=pl.Buffered(1)),
            pl.BlockSpec((D_MODEL, D_MODEL), const, pipeline_mode=pl.Buffered(1)),
        ],
        out_specs=pl.BlockSpec((tm, D_MODEL), row),
        out_shape=jax.ShapeDtypeStruct((bsz * seq, D_MODEL), F32),
        scratch_shapes=[pltpu.VMEM((tm, D_MODEL), BF16)],
        compiler_params=_params("parallel", "parallel"),
        name="outproj_cross",
    )(x2d, ya, yb, kv, w_out, g, wq, wo)


def _ffn_kernel(x_ref, g_ref, wg_ref, wv_ref, cw_ref, cb_ref, wd_ref, gf_ref, o_ref, carry_s, act_s, *, final):
    tm = x_ref.shape[0]
    n_ff, _, tf = carry_s.shape

    @pl.when(pl.program_id(1) == 0)
    def _():
        carry_s[...] = jnp.zeros_like(carry_s)

    x = x_ref[...]
    h = _rms(x, g_ref[...], NORM_EPS).astype(BF16)

    def up(j):
        cols = slice(j * tf, (j + 1) * tf)
        return (jnp.dot(h, wg_ref[:, cols], preferred_element_type=F32),
                jnp.dot(h, wv_ref[:, cols], preferred_element_type=F32))

    split = (n_ff + 1) // 2 * tf
    row = lax.broadcasted_iota(jnp.int32, (tm, 1), 0)
    nxt = up(0)
    down = []
    for j in range(n_ff):
        gate, val = nxt
        if j + 1 < n_ff:
            nxt = up(j + 1)
        prev = carry_s[j]
        g1 = jnp.where(row == 0, prev[7:8, :], pltpu.roll(gate, 1, axis=0))
        g2 = jnp.where(row == 0, prev[6:7, :], pltpu.roll(g1, 1, axis=0))
        carry_s[j] = gate[tm - 8:tm, :]
        cwh = 0.5 * cw_ref[:, j * tf:(j + 1) * tf]
        hc = cwh[0:1, :] * g2 + cwh[1:2, :] * g1 + cwh[2:3, :] * gate + 0.5 * cb_ref[:, j * tf:(j + 1) * tf]
        act_s[:, j * tf:(j + 1) * tf] = ((hc + hc * jnp.tanh(hc)) * val).astype(BF16)
        if (j + 1) * tf == split:
            down.append(jnp.dot(act_s[:, :split], wd_ref[:split, :], preferred_element_type=F32))
    down.append(jnp.dot(act_s[:, split:], wd_ref[split:, :], preferred_element_type=F32))
    out = x + (down[0] + down[1])
    o_ref[...] = _rms(out, gf_ref[...], NORM_EPS) if final else out


def _ffn(x2d, bsz, seq, g, wg, wv, cw, cb, wd, gf, final):
    tm = PROJ_TILE
    nt = seq // tm
    tf = FF_TILE
    n_ff = D_FF // tf
    row = lambda b, t: (b * nt + t, 0)
    c2 = lambda b, t: (0, 0)
    once = pl.Buffered(1)
    return pl.pallas_call(
        functools.partial(_ffn_kernel, final=final),
        grid=(bsz, nt),
        in_specs=[
            pl.BlockSpec((tm, D_MODEL), row),
            pl.BlockSpec((1, D_MODEL), c2),
            pl.BlockSpec((D_MODEL, D_FF), c2, pipeline_mode=once),
            pl.BlockSpec((D_MODEL, D_FF), c2, pipeline_mode=once),
            pl.BlockSpec((3, D_FF), c2),
            pl.BlockSpec((1, D_FF), c2),
            pl.BlockSpec((D_FF, D_MODEL), c2, pipeline_mode=once),
            pl.BlockSpec((1, D_MODEL), c2),
        ],
        out_specs=pl.BlockSpec((tm, D_MODEL), row),
        out_shape=jax.ShapeDtypeStruct((bsz * seq, D_MODEL), F32),
        scratch_shapes=[pltpu.VMEM((n_ff, 8, tf), F32), pltpu.VMEM((tm, n_ff * tf), BF16)],
        compiler_params=_params("parallel", "arbitrary"),
        name="conv_ffn",
    )(x2d, g, wg, wv, cw, cb, wd, gf)


def _rope_perm():
    idx = np.empty((DIFF_W,), np.int32)
    half = DIFF_D // 2
    for h in range(DIFF_H):
        for c in range(2):
            for d in range(DIFF_D):
                idx[h * LANES + (d // half) * DIFF_D + c * half + d % half] = h * LANES + c * DIFF_D + d
    return idx


def _layer(x2d, pos2d, mem2d, bsz, seq, lambda_init, p):
    w_in = p["w_in"]
    pad = jnp.zeros((D_MODEL, LORA_PAD - (N_SHIFT - 3 * RWKV_W)), F32)
    w_r = jnp.concatenate([w_in[:, :N_SHIFT], pad], axis=1).astype(BF16)
    perm = _rope_perm()
    w_d = jnp.concatenate([w_in[:, N_SHIFT:N_SHIFT + DIFF_W][:, perm],
                           w_in[:, N_SHIFT + DIFF_W:N_SHIFT + 2 * DIFF_W][:, perm],
                           w_in[:, N_SHIFT + 2 * DIFF_W:]], axis=1).astype(BF16)
    mix = jnp.concatenate([p["shift_mix"], jnp.zeros((LORA_PAD - (N_SHIFT - 3 * RWKV_W),), F32)])[None, :]
    w_lora = jnp.zeros((LORA_PAD, 3 * RWKV_W), F32)
    w_lora = w_lora.at[0:R_DECAY, 0:RWKV_W].set(p["w_lora_up"])
    w_lora = w_lora.at[R_DECAY:R_DECAY + R_AAA, RWKV_W:2 * RWKV_W].set(p["a_lora_up"])
    w_lora = w_lora.at[R_DECAY + R_AAA:R_DECAY + R_AAA + R_GATE, 2 * RWKV_W:].set(p["g_lora_up"])
    w_lora = w_lora.astype(BF16)
    v512 = lambda a: a.reshape(1, RWKV_W)

    zr, q, k, v = _inproj(x2d, pos2d, p["norm_mix"][None, :], w_r, w_d)
    ya = _rwkv(zr, bsz, seq, mix, w_lora, v512(p["w0"]), v512(p["a0"]), v512(p["k_k"]),
               v512(p["k_a"]), v512(p["r_k"]), v512(p["lnx_gain"]), v512(p["lnx_bias"]))
    yb = _diff_attn(q, k, v, bsz, seq, p["lam_q1"][None, :], p["lam_k1"][None, :],
                    p["lam_q2"][None, :], p["lam_k2"][None, :], p["subln_gain"][None, :], lambda_init)
    kv = _memkv(mem2d, p["norm_mem"][None, :], p["wkv_c"].astype(BF16))
    x2 = _cross(x2d, ya, yb, kv, bsz, seq, p["w_out"].astype(BF16), p["norm_cross"][None, :],
                p["wq_c"].astype(BF16), p["wo_c"].astype(BF16))
    w_up = p["w_up"]
    wg = w_up[:, :D_FF].astype(BF16)
    wv = w_up[:, D_FF:].astype(BF16)
    cw = p["conv_w"]
    cb = p["conv_b"][None, :]
    wd = p["w_down"].astype(BF16)
    return x2, (p["norm_ffn"][None, :], wg, wv, cw, cb, wd)


def kernel(x, mem, positions, norm_mix, w_in, shift_mix, w0, w_lora_up, a0, a_lora_up, g_lora_up, k_k, k_a, r_k, lnx_gain, lnx_bias, lam_q1, lam_k1, lam_q2, lam_k2, subln_gain, w_out, norm_cross, norm_mem, wq_c, wkv_c, wo_c, norm_ffn, w_up, conv_w, conv_b, w_down, norm_final):
    bsz, seq, _ = x.shape
    depth = norm_mix.shape[0]
    x2d = x.reshape(bsz * seq, D_MODEL)
    pos2d = positions.reshape(bsz * seq, 1)
    mem2d = mem.reshape(bsz * mem.shape[1], D_MODEL)
    stacked = dict(norm_mix=norm_mix, w_in=w_in, shift_mix=shift_mix, w0=w0, w_lora_up=w_lora_up,
                   a0=a0, a_lora_up=a_lora_up, g_lora_up=g_lora_up, k_k=k_k, k_a=k_a,
                   r_k=r_k.reshape(depth, RWKV_W), lnx_gain=lnx_gain, lnx_bias=lnx_bias,
                   lam_q1=lam_q1, lam_k1=lam_k1, lam_q2=lam_q2, lam_k2=lam_k2,
                   subln_gain=subln_gain, w_out=w_out, norm_cross=norm_cross, norm_mem=norm_mem,
                   wq_c=wq_c, wkv_c=wkv_c, wo_c=wo_c, norm_ffn=norm_ffn, w_up=w_up,
                   conv_w=conv_w, conv_b=conv_b, w_down=w_down)
    for l in range(depth):
        p = {name: a[l] for name, a in stacked.items()}
        lambda_init = 0.8 - 0.6 * math.exp(-0.3 * l)
        x2, (gn, wg, wv, cw, cb, wd) = _layer(x2d, pos2d, mem2d, bsz, seq, lambda_init, p)
        x2d = _ffn(x2, bsz, seq, gn, wg, wv, cw, cb, wd, norm_final[None, :], l == depth - 1)
    return x2d.reshape(bsz, seq, D_MODEL)
```

```python
import functools
import math

import numpy as np
import jax
import jax.numpy as jnp
from jax import lax
from jax.experimental import pallas as pl
from jax.experimental.pallas import tpu as pltpu

F32 = jnp.float32
BF16 = jnp.bfloat16

D_MODEL = 1024
RWKV_W = 512
RWKV_N = 64
R_DECAY = 32
R_AAA = 32
R_GATE = 96
DIFF_W = 512
DIFF_H = 4
DIFF_D = 64
N_SHIFT = 3 * RWKV_W + R_DECAY + R_AAA + R_GATE
MEM_LEN = 256
CROSS_H = 4
CROSS_D = D_MODEL // CROSS_H
D_FF = 2816
ROPE_THETA = 10000.0
NORM_EPS = 1e-6
LNX_EPS = 64e-5
SUBLN_EPS = 1e-5

LANES = 128
MXU_COLS = 256
LORA_PAD = 256
ZR_W = 3 * RWKV_W + LORA_PAD
VMEM_LIMIT = 56 * 1024 * 1024

ROW_TILE = 1024
PROJ_TILE = 1024
CHUNK = 64
ATT_TQ = 2048
ATT_TK = 512
FF_TILE = 256
TRIG_BLOCKS = 4


def _mm_tn(a, b):
    return lax.dot_general(a.astype(BF16), b.astype(BF16), (((0,), (0,)), ((), ())),
                           preferred_element_type=F32)


def _sigmoid(x):
    return 0.5 * jnp.tanh(0.5 * x) + 0.5


def _rms(x, g, eps):
    return x * lax.rsqrt(jnp.mean(x * x, axis=-1, keepdims=True) + eps) * g


def _run_interleaved(*stage_gens):
    live = list(stage_gens)
    while live:
        for g in list(live):
            try:
                next(g)
            except StopIteration:
                live.remove(g)


def _params(*sem):
    return pltpu.CompilerParams(dimension_semantics=sem, vmem_limit_bytes=VMEM_LIMIT)


def _inproj_kernel(x_ref, pos_ref, g_ref, wr_ref, wd_ref, zr_ref, q_ref, k_ref, v_ref, trig_s):
    tm = x_ref.shape[0]
    lane = lax.broadcasted_iota(jnp.int32, (1, LANES), 1)
    freq = (lane % (DIFF_D // 2)).astype(F32)
    inv = jnp.exp(freq * (-2.0 / DIFF_D * math.log(ROPE_THETA)))
    scale = DIFF_D ** -0.5 * math.log2(math.e)

    def rope_tables():
        n_blk = TRIG_BLOCKS
        rb = tm // n_blk
        for i in range(n_blk):
            rows = slice(i * rb, (i + 1) * rb)
            ang = pos_ref[rows, :].astype(F32) * inv
            cos = jnp.cos(ang)
            sin = jnp.where(lane < LANES // 2, -jnp.sin(ang), jnp.sin(ang))
            trig_s[0, rows, :] = cos
            trig_s[1, rows, :] = sin
            trig_s[2, rows, :] = cos * scale
            trig_s[3, rows, :] = sin * scale
            yield

    h = _rms(x_ref[...], g_ref[...], NORM_EPS).astype(BF16)

    def rwkv_slab():
        n_blk = ZR_W // MXU_COLS
        for j in range(n_blk):
            sl = slice(j * MXU_COLS, (j + 1) * MXU_COLS)
            lhs = h
            t = j - (n_blk - TRIG_BLOCKS)
            if t >= 0:
                r = (t + 1) * (tm // TRIG_BLOCKS) - 8
                lhs = jnp.where(trig_s[0, r:r + 8, 0:1][0:1, :] < 2.0, h, jnp.zeros_like(h))
            zr_ref[:, sl] = jnp.dot(lhs, wr_ref[:, sl], preferred_element_type=F32)
            yield

    _run_interleaved(rope_tables(), rwkv_slab())
    cos, sin, cos_q, sin_q = trig_s[0], trig_s[1], trig_s[2], trig_s[3]

    def rope(xb, c, s):
        return xb * c + pltpu.roll(xb, LANES // 2, axis=1) * s

    for j in range(DIFF_W // MXU_COLS):
        c_lo, c_hi = j * MXU_COLS, (j + 1) * MXU_COLS
        zq = jnp.dot(h, wd_ref[:, c_lo:c_hi], preferred_element_type=F32)
        zk = jnp.dot(h, wd_ref[:, DIFF_W + c_lo:DIFF_W + c_hi], preferred_element_type=F32)
        zv = jnp.dot(h, wd_ref[:, 2 * DIFF_W + c_lo:2 * DIFF_W + c_hi], preferred_element_type=F32)
        for u in range(MXU_COLS // LANES):
            c0 = c_lo + u * LANES
            q_ref[:, c0:c0 + LANES] = rope(zq[:, u * LANES:(u + 1) * LANES], cos_q, sin_q).astype(BF16)
            k_ref[:, c0:c0 + LANES] = rope(zk[:, u * LANES:(u + 1) * LANES], cos, sin).astype(BF16)
        v_ref[:, c_lo:c_hi] = zv.astype(BF16)


def _inproj(x2d, pos2d, g, w_r, w_d):
    m = x2d.shape[0]
    tm = PROJ_TILE
    row = lambda i: (i, 0)
    const = lambda i: (0, 0)
    return pl.pallas_call(
        _inproj_kernel,
        grid=(m // tm,),
        in_specs=[
            pl.BlockSpec((tm, D_MODEL), row),
            pl.BlockSpec((tm, 1), row),
            pl.BlockSpec((1, D_MODEL), const),
            pl.BlockSpec((D_MODEL, ZR_W), const, pipeline_mode=pl.Buffered(1)),
            pl.BlockSpec((D_MODEL, 3 * DIFF_W), const, pipeline_mode=pl.Buffered(1)),
        ],
        out_specs=[
            pl.BlockSpec((tm, ZR_W), row),
            pl.BlockSpec((tm, DIFF_W), row),
            pl.BlockSpec((tm, DIFF_W), row),
            pl.BlockSpec((tm, DIFF_W), row),
        ],
        out_shape=[
            jax.ShapeDtypeStruct((m, ZR_W), F32),
            jax.ShapeDtypeStruct((m, DIFF_W), BF16),
            jax.ShapeDtypeStruct((m, DIFF_W), BF16),
            jax.ShapeDtypeStruct((m, DIFF_W), BF16),
        ],
        scratch_shapes=[pltpu.VMEM((4, tm, LANES), F32)],
        compiler_params=_params("parallel"),
        name="inproj",
    )(x2d, pos2d, g, w_r, w_d)


def _rwkv_kernel(z_ref, mix_ref, wl_ref, w0_ref, a0_ref, kk_ref, ka_ref, rk_ref, gain_ref,
                  bias_ref, y_ref, carry_s, st_s, g_s, bo_s, y_s, rt_s, dec_s,
                  ah0_s, ah1_s, bn0_s, bn1_s, kn0_s, kn1_s, vh0_s, vh1_s, be_s, ke_s,
                  mr_s, n_s):
    tb = z_ref.shape[0]
    n_pair = RWKV_W // LANES
    c_len = CHUNK
    n_chunk = tb // c_len

    @pl.when(pl.program_id(1) == 0)
    def _():
        carry_s[...] = jnp.zeros_like(carry_s)
        st_s[...] = jnp.zeros_like(st_s)

    group_chunks = 4
    group_rows = group_chunks * c_len
    n_piece = tb // group_rows
    rowq = lax.broadcasted_iota(jnp.int32, (group_rows, 1), 0)
    ri = lax.broadcasted_iota(jnp.int32, (LANES, LANES), 0)
    ci = lax.broadcasted_iota(jnp.int32, (LANES, LANES), 1)
    same_head = (ri // RWKV_N) == (ci // RWKV_N)
    ones_head = jnp.where(same_head, 1.0, 0.0).astype(BF16)
    head0 = (lax.broadcasted_iota(jnp.int32, (1, RWKV_W), 1) % LANES) < RWKV_N

    def headsum(x):
        return jnp.concatenate(
            [jnp.dot(x[:, p * LANES:(p + 1) * LANES].astype(BF16), ones_head, preferred_element_type=F32)
             for p in range(n_pair)], axis=1)

    def prep(r0, top):
        rows = pl.ds(r0, group_rows)

        def shifted(c0, c1):
            zc = z_ref[rows, c0:c1]
            zp = jnp.where(rowq == 0, top[7:8, c0:c1], pltpu.roll(zc, 1, axis=0))
            return zc + (zp - zc) * mix_ref[:, c0:c1]

        zl = shifted(3 * RWKV_W, ZR_W)
        ll = lax.broadcasted_iota(jnp.int32, (1, LORA_PAD), 1)
        act = jnp.where(ll < R_DECAY, jnp.tanh(zl),
                        jnp.where(ll < R_DECAY + R_AAA, zl, _sigmoid(zl)))
        lo = jnp.dot(act.astype(BF16), wl_ref[...], preferred_element_type=F32)
        yield
        lw = -math.exp(-0.5) * _sigmoid(w0_ref[...] + lo[:, 0:RWKV_W])
        a = _sigmoid(a0_ref[...] + lo[:, RWKV_W:2 * RWKV_W])
        g_s[rows, :] = lo[:, 2 * RWKV_W:3 * RWKV_W].astype(BF16)
        yield

        r = shifted(0, RWKV_W)
        yield
        k = shifted(RWKV_W, 2 * RWKV_W)
        yield
        v = shifted(2 * RWKV_W, 3 * RWKV_W)
        yield
        kk = k * kk_ref[...]
        ss = headsum(kk * kk)
        alpha = kk * lax.rsqrt(jnp.maximum(ss, 1e-24))
        yield
        beta = alpha * a
        k2 = k * (1.0 + (a - 1.0) * ka_ref[...])
        bo_s[rows, :] = (headsum(r * k2 * rk_ref[...]) * v).astype(BF16)
        yield

        rin = rowq % c_len
        cum = lw
        for s in (1, 2, 4, 8, 16, 32):
            cum = cum + jnp.where(rin >= s, pltpu.roll(cum, s, axis=0), 0.0)
            yield
        tot = jnp.broadcast_to(cum.reshape(group_chunks, c_len, RWKV_W)[:, c_len - 1:c_len, :],
                               (group_chunks, c_len, RWKV_W)).reshape(group_rows, RWKV_W)
        e_neg = jnp.exp(-cum)
        e_end = jnp.exp(tot - cum)
        yield
        a_t = alpha * jnp.exp(cum - lw)
        r_t = r * jnp.exp(cum)
        rt_s[rows, :] = r_t
        slab0 = r0 // c_len * 8
        if not isinstance(slab0, int):
            slab0 = pl.multiple_of(slab0, group_chunks * 8)
        dec_s[pl.ds(slab0, group_chunks * 8), :] = jnp.exp(
            tot.reshape(group_chunks, c_len, RWKV_W)[:, :8, :].reshape(group_chunks * 8, RWKV_W))
        yield
        ah0_s[rows, :] = jnp.where(head0, a_t, 0.0).astype(BF16)
        ah1_s[rows, :] = jnp.where(head0, 0.0, a_t).astype(BF16)
        yield
        vh0_s[rows, :] = jnp.where(head0, v, 0.0).astype(BF16)
        vh1_s[rows, :] = jnp.where(head0, 0.0, v).astype(BF16)
        yield
        b_n = (beta * e_neg).astype(BF16)
        bn0_s[rows, :] = jnp.where(head0, b_n, jnp.zeros_like(b_n))
        bn1_s[rows, :] = jnp.where(head0, jnp.zeros_like(b_n), b_n)
        yield
        k_n = (k2 * e_neg).astype(BF16)
        kn0_s[rows, :] = jnp.where(head0, k_n, jnp.zeros_like(k_n))
        kn1_s[rows, :] = jnp.where(head0, jnp.zeros_like(k_n), k_n)
        yield
        be_s[rows, :] = (beta * e_end).astype(BF16)
        ke_s[rows, :] = (k2 * e_end).astype(BF16)

    ti = lax.broadcasted_iota(jnp.int32, (c_len, LANES), 0)
    si = lax.broadcasted_iota(jnp.int32, (c_len, LANES), 1) % c_len
    strict = si < ti
    incl = si <= ti
    b16 = (ti // 16) == (si // 16)
    b32 = (ti // 32) == (si // 32)
    eye = jnp.where(ti == si, 1.0, 0.0).astype(F32)
    lane_h0 = lax.broadcasted_iota(jnp.int32, (1, LANES), 1) < RWKV_N
    group = group_chunks * n_pair
    nb = range(group)

    def stack(x):
        xb = x.astype(BF16)
        zero = jnp.zeros_like(xb)
        return jnp.concatenate([jnp.where(lane_h0, xb, zero), jnp.where(lane_h0, zero, xb)], axis=0)

    def mmb(xs, ys):
        return [jnp.dot(x.astype(BF16), stack(y), preferred_element_type=F32) for x, y in zip(xs, ys)]

    def solve(it):
        idx = []
        for u in range(group):
            c = it * group_chunks + u // n_pair
            p = u % n_pair
            idx.append((c, c * n_pair + p,
                        (pl.ds(pl.multiple_of(c * c_len, c_len), c_len), slice(p * LANES, (p + 1) * LANES))))
        sls = [sl for _, _, sl in idx]
        a_st = [jnp.concatenate([ah0_s[sl], ah1_s[sl]], axis=0) for sl in sls]
        lhs = [jnp.concatenate([x[:c_len] + x[c_len:], rt_s[sl].astype(BF16)], axis=0)
               for x, sl in zip(a_st, sls)]
        rhs = [jnp.concatenate([bn0_s[sl], bn1_s[sl], kn0_s[sl], kn1_s[sl]], axis=0) for sl in sls]
        gram = [lax.dot_general(x, y, (((1,), (1,)), ((), ())), preferred_element_type=F32)
                for x, y in zip(lhs, rhs)]
        a_ab = [jnp.where(strict, gm[:c_len, :LANES], 0.0) for gm in gram]
        a_kr = [jnp.concatenate([jnp.where(strict, gm[:c_len, LANES:], 0.0),
                                 jnp.where(incl, gm[c_len:, LANES:], 0.0)], axis=0).astype(BF16)
                for gm in gram]
        a_rb = [jnp.where(incl, gm[c_len:, :LANES], 0.0) for gm in gram]
        yield
        a0 = [jnp.where(b16, x, 0.0).astype(BF16) for x in a_ab]
        e1 = [jnp.where(b32 & jnp.logical_not(b16), x, 0.0).astype(BF16) for x in a_ab]
        e2 = [jnp.where(b32, 0.0, x).astype(BF16) for x in a_ab]
        p2 = mmb(a0, a0)
        yield
        p4 = mmb(p2, p2)
        yield
        p8 = mmb(p4, p4)
        t = [eye - x.astype(F32) for x in a0]
        t = [x + y for x, y in zip(t, mmb(t, p2))]
        yield
        t = [x + y for x, y in zip(t, mmb(t, p4))]
        yield
        t = [x + y for x, y in zip(t, mmb(t, p8))]
        yield
        te = mmb(t, e1)
        yield
        t = [x - y for x, y in zip(t, mmb(te, t))]
        yield
        te = mmb(t, e2)
        yield
        t = [x - y for x, y in zip(t, mmb(te, t))]
        yield
        v_st = [jnp.concatenate([vh0_s[sl], vh1_s[sl]], axis=0) for sl in sls]
        av = [jnp.dot(x, y, preferred_element_type=F32) for x, y in zip(a_kr, v_st)]
        yield
        w = [jnp.dot(x.astype(BF16), jnp.concatenate([y, stack(z[:c_len])], axis=1), preferred_element_type=F32)
             for x, y, z in zip(t, a_st, av)]
        yield
        rbw = [jnp.dot(x.astype(BF16), jnp.concatenate([stack(y[:, :LANES]), stack(y[:, LANES:])], axis=1),
                       preferred_element_type=F32) for x, y in zip(a_rb, w)]
        yield
        bw = [_mm_tn(be_s[sl], x) for sl, x in zip(sls, w)]
        yield
        kv = [_mm_tn(ke_s[sl], vh0_s[sl] + vh1_s[sl]) for sl in sls]
        yield
        for u in nb:
            c, j, sl = idx[u]
            r_hat = rt_s[sl] - rbw[u][:, :LANES]
            m_p = jnp.where(same_head, bw[u][:, :LANES], 0.0)
            mr_s[j] = jnp.concatenate([m_p, r_hat], axis=0).astype(BF16)
            y_s[sl] = av[u][c_len:] - rbw[u][:, LANES:]
            n_s[j] = jnp.where(same_head, kv[u] - bw[u][:, LANES:], 0.0).astype(BF16)

    def scan(it):
        for u in range(group_chunks):
            c = it * group_chunks + u
            rows = pl.ds(pl.multiple_of(c * c_len, c_len), c_len)
            sts = [st_s[p] for p in range(n_pair)]
            zz = [jnp.dot(mr_s[c * n_pair + p], sts[p].astype(BF16), preferred_element_type=F32)
                  for p in range(n_pair)]
            yield
            for p in range(n_pair):
                j = c * n_pair + p
                sl = (rows, slice(p * LANES, (p + 1) * LANES))
                y_s[sl] = y_s[sl] + zz[p][LANES:]
                d = dec_s[pl.ds(pl.multiple_of(c * 8, 8), 8), p * LANES:(p + 1) * LANES]
                decay_col = jnp.broadcast_to(d[0:1, :], (LANES, LANES)).T
                st_s[p] = decay_col * sts[p] - zz[p][:LANES] + n_s[j].astype(F32)
            yield

    def below(it):
        r0 = pl.multiple_of((it + 1) * group_rows, group_rows)
        return prep(r0, z_ref[pl.ds(r0 - 8, 8), :])

    _run_interleaved(prep(0, carry_s[...]))
    carry_s[...] = z_ref[tb - 8:tb, :]
    _run_interleaved(solve(0), below(0))

    def piece_body(it, carry):
        _run_interleaved(solve(it), below(it), scan(it - 1))
        return carry

    lax.fori_loop(1, n_piece - 1, piece_body, 0)
    _run_interleaved(solve(n_piece - 1), scan(n_piece - 2))
    _run_interleaved(scan(n_piece - 1))

    y = y_s[...]
    mu = headsum(y) * (1.0 / RWKV_N)
    d = y - mu
    var = headsum(d * d) * (1.0 / RWKV_N)
    yn = d * lax.rsqrt(var + LNX_EPS) * gain_ref[...] + bias_ref[...]
    y_ref[...] = ((yn + bo_s[...].astype(F32)) * g_s[...].astype(F32)).astype(y_ref.dtype)


def _rwkv(zr, bsz, seq, mix, w_lora, w0, a0, k_k, k_a, r_k, gain, bias):
    tb = ROW_TILE
    nt = seq // tb
    n_prob = (tb // CHUNK) * (RWKV_W // LANES)
    row = lambda b, t: (b * nt + t, 0)
    const = lambda b, t: (0, 0)
    vec = pl.BlockSpec((1, RWKV_W), const)
    big = pltpu.VMEM((tb, RWKV_W), F32)
    half = pltpu.VMEM((tb, RWKV_W), BF16)
    return pl.pallas_call(
        _rwkv_kernel,
        grid=(bsz, nt),
        in_specs=[
            pl.BlockSpec((tb, ZR_W), row),
            pl.BlockSpec((1, ZR_W), const),
            pl.BlockSpec((LORA_PAD, 3 * RWKV_W), const),
            vec, vec, vec, vec, vec, vec, vec,
        ],
        out_specs=pl.BlockSpec((tb, RWKV_W), row),
        out_shape=jax.ShapeDtypeStruct((bsz * seq, RWKV_W), BF16),
        scratch_shapes=[
            pltpu.VMEM((8, ZR_W), F32),
            pltpu.VMEM((RWKV_W // LANES, LANES, LANES), F32),
            half, half, big, big,
            pltpu.VMEM((tb // CHUNK * 8, RWKV_W), F32),
            half, half, half, half, half, half, half, half, half, half,
            pltpu.VMEM((n_prob, LANES + CHUNK, LANES), BF16),
            pltpu.VMEM((n_prob, LANES, LANES), BF16),
        ],
        compiler_params=_params("parallel", "arbitrary"),
        name="rwkv7",
    )(zr, mix, w_lora, w0, a0, k_k, k_a, r_k, gain, bias)


def _attn_kernel(q_ref, k_ref, v_ref, lq1_ref, lk1_ref, lq2_ref, lk2_ref, sg_ref, o_ref,
                  q_s, m_s, acc_s, *, lambda_init, tk):
    tq = q_ref.shape[0]
    n_col = tk // LANES
    n_diag = tq // tk
    qi = pl.program_id(2)
    nt = (((1,), (1,)), ((), ()))

    map0 = (lax.broadcasted_iota(jnp.int32, (1, LANES), 1) % DIFF_D) < DIFF_D // 2
    q = q_ref[...]
    zero = jnp.zeros_like(q)
    q_s[0] = jnp.where(map0, q, zero)
    q_s[1] = jnp.where(map0, zero, q)
    ones = jnp.ones((tk, LANES), BF16)
    tri_keep = [lax.broadcasted_iota(jnp.int32, (tk, LANES), 1) + j * LANES
                <= lax.broadcasted_iota(jnp.int32, (tk, LANES), 0) for j in range(n_col)]

    def kv_step(kv_rows, r0, r1, masked, first=False):
        k = k_ref[kv_rows, :]
        v_ext = jnp.concatenate([v_ref[kv_rows, :], ones], axis=1)
        for c in range(2):
            s = lax.dot_general(q_s[c, r0:r1, :], k, nt, preferred_element_type=F32)
            cols = [s[:, j * LANES:(j + 1) * LANES] for j in range(n_col)]
            if masked:
                cols = [jnp.where(tri_keep[j], cols[j], -jnp.inf) for j in range(n_col)]
            mx = cols[0]
            for j in range(1, n_col):
                mx = jnp.maximum(mx, cols[j])
            m_new = jnp.broadcast_to(jnp.max(mx, axis=-1, keepdims=True), mx.shape)
            if not first:
                m_old = m_s[c, r0:r1, :]
                m_new = jnp.maximum(m_old, m_new)
            p = jnp.concatenate([jnp.exp2(cj - m_new).astype(BF16) for cj in cols], axis=1)
            pv = jnp.dot(p, v_ext, preferred_element_type=F32)
            if first:
                acc_s[c, r0:r1, :] = pv
            else:
                corr = jnp.exp2(m_old - m_new)
                acc_s[c, r0:r1, :] = jnp.concatenate([corr, corr], axis=1) * acc_s[c, r0:r1, :] + pv
            m_s[c, r0:r1, :] = m_new

    def full_body(j, carry):
        for u in range(n_diag):
            kv_rows = pl.ds(pl.multiple_of((j * n_diag + u) * tk, tk), tk)
            kv_step(kv_rows, 0, tq // 2, False)
            kv_step(kv_rows, tq // 2, tq, False)
        return carry

    for d in range(n_diag):
        kv_rows = pl.ds(pl.multiple_of((qi * n_diag + d) * tk, tk), tk)
        kv_step(kv_rows, d * tk, (d + 1) * tk, True, first=d == 0)
        if d + 1 < n_diag:
            kv_step(kv_rows, (d + 1) * tk, tq, False, first=d == 0)
    lax.fori_loop(0, qi, full_body, 0)

    lam = (jnp.exp(jnp.sum(lq1_ref[...] * lk1_ref[...], axis=-1, keepdims=True))
           - jnp.exp(jnp.sum(lq2_ref[...] * lk2_ref[...], axis=-1, keepdims=True))
           + lambda_init)
    a1 = acc_s[0]
    a2 = acc_s[1]
    o = a1[:, :LANES] / a1[:, LANES:] - lam * (a2[:, :LANES] / a2[:, LANES:])
    o_ref[...] = (_rms(o, sg_ref[...], SUBLN_EPS) * (1.0 - lambda_init)).astype(o_ref.dtype)


def _diff_attn(q, k, v, bsz, seq, lq1, lk1, lq2, lk2, sg, lambda_init):
    tq = min(ATT_TQ, seq)
    tk = min(ATT_TK, tq)
    nq = seq // tq
    qmap = lambda b, h, i: (b * nq + i, h)
    kmap = lambda b, h, i: (b, h)
    const = lambda b, h, i: (0, 0)
    lam_spec = pl.BlockSpec((1, DIFF_D), const)
    return pl.pallas_call(
        functools.partial(_attn_kernel, lambda_init=lambda_init, tk=tk),
        grid=(bsz, DIFF_H, nq),
        in_specs=[
            pl.BlockSpec((tq, LANES), qmap),
            pl.BlockSpec((seq, LANES), kmap),
            pl.BlockSpec((seq, LANES), kmap),
            lam_spec, lam_spec, lam_spec, lam_spec,
            pl.BlockSpec((1, 2 * DIFF_D), const),
        ],
        out_specs=pl.BlockSpec((tq, LANES), qmap),
        out_shape=jax.ShapeDtypeStruct((bsz * seq, DIFF_W), BF16),
        scratch_shapes=[
            pltpu.VMEM((2, tq, LANES), BF16),
            pltpu.VMEM((2, tq, LANES), F32),
            pltpu.VMEM((2, tq, 2 * LANES), F32),
        ],
        compiler_params=_params("parallel", "parallel", "arbitrary"),
        name="diff_attn",
    )(q, k, v, lq1, lk1, lq2, lk2, sg)


def _memkv_kernel(m_ref, g_ref, w_ref, o_ref):
    h = _rms(m_ref[...], g_ref[...], NORM_EPS).astype(BF16)
    for j in range(o_ref.shape[1] // MXU_COLS):
        sl = slice(j * MXU_COLS, (j + 1) * MXU_COLS)
        o_ref[:, sl] = jnp.dot(h, w_ref[:, sl], preferred_element_type=F32).astype(o_ref.dtype)


def _memkv(mem2d, g, wkv):
    m = mem2d.shape[0]
    tm = MEM_LEN
    return pl.pallas_call(
        _memkv_kernel,
        grid=(m // tm,),
        in_specs=[
            pl.BlockSpec((tm, D_MODEL), lambda i: (i, 0)),
            pl.BlockSpec((1, D_MODEL), lambda i: (0, 0)),
            pl.BlockSpec((D_MODEL, 2 * D_MODEL), lambda i: (0, 0)),
        ],
        out_specs=pl.BlockSpec((tm, 2 * D_MODEL), lambda i: (i, 0)),
        out_shape=jax.ShapeDtypeStruct((m, 2 * D_MODEL), BF16),
        compiler_params=_params("parallel"),
        name="memkv",
    )(mem2d, g, wkv)


def _cross_kernel(x_ref, ya_ref, yb_ref, kv_ref, wout_ref, g_ref, wq_ref, wo_ref, o_ref, att_s):
    tm = x_ref.shape[0]
    halves = [slice(0, tm // 2), slice(tm // 2, tm)]
    nt = (((1,), (1,)), ((), ()))
    x1 = [x_ref[r, :] + (jnp.dot(ya_ref[r, :], wout_ref[0:RWKV_W, :], preferred_element_type=F32)
                         + jnp.dot(yb_ref[r, :], wout_ref[RWKV_W:, :], preferred_element_type=F32))
          for r in halves]
    hc = [_rms(x, g_ref[...], NORM_EPS).astype(BF16) for x in x1]
    scale = CROSS_D ** -0.5 * math.log2(math.e)
    q = [(jnp.dot(x, wq_ref[...], preferred_element_type=F32) * scale).astype(BF16) for x in hc]
    for h in range(CROSS_H):
        sl = slice(h * CROSS_D, (h + 1) * CROSS_D)
        kh = kv_ref[:, sl]
        vh = kv_ref[:, D_MODEL + h * CROSS_D:D_MODEL + (h + 1) * CROSS_D]
        s = [lax.dot_general(x[:, sl], kh, nt, preferred_element_type=F32) for x in q]
        p = [jnp.exp2(x - jnp.max(x, axis=-1, keepdims=True)) for x in s]
        o = [jnp.dot(x.astype(BF16), vh, preferred_element_type=F32) / jnp.sum(x, axis=-1, keepdims=True)
             for x in p]
        for r, x in zip(halves, o):
            att_s[r, sl] = x.astype(BF16)
    for r, x in zip(halves, x1):
        o_ref[r, :] = x + jnp.dot(att_s[r, :], wo_ref[...], preferred_element_type=F32)


def _cross(x2d, ya, yb, kv, bsz, seq, w_out, g, wq, wo):
    tm = PROJ_TILE
    nt = seq // tm
    row = lambda b, t: (b * nt + t, 0)
    const = lambda b, t: (0, 0)
    return pl.pallas_call(
        _cross_kernel,
        grid=(bsz, nt),
        in_specs=[
            pl.BlockSpec((tm, D_MODEL), row),
            pl.BlockSpec((tm, RWKV_W), row),
            pl.BlockSpec((tm, DIFF_W), row),
            pl.BlockSpec((MEM_LEN, 2 * D_MODEL), lambda b, t: (b, 0)),
            pl.BlockSpec((D_MODEL, D_MODEL), const, pipeline_mode=pl.Buffered(1)),
            pl.BlockSpec((1, D_MODEL), const),
            pl.BlockSpec((D_MODEL, D_MODEL), const, pipeline_mode=pl.Buffered(1)),
            pl.BlockSpec((D_MODEL, D_MODEL), const, pipeline_mode=pl.Buffered(1)),
        ],
        out_specs=pl.BlockSpec((tm, D_MODEL), row),
        out_shape=jax.ShapeDtypeStruct((bsz * seq, D_MODEL), F32),
        scratch_shapes=[pltpu.VMEM((tm, D_MODEL), BF16)],
        compiler_params=_params("parallel", "parallel"),
        name="outproj_cross",
    )(x2d, ya, yb, kv, w_out, g, wq, wo)


def _ffn_kernel(x_ref, g_ref, wg_ref, wv_ref, cw_ref, cb_ref, wd_ref, gf_ref, o_ref, carry_s, act_s, *, final):
    tm = x_ref.shape[0]
    n_ff, _, tf = carry_s.shape

    @pl.when(pl.program_id(1) == 0)
    def _():
        carry_s[...] = jnp.zeros_like(carry_s)

    x = x_ref[...]
    h = _rms(x, g_ref[...], NORM_EPS).astype(BF16)

    def up(j):
        cols = slice(j * tf, (j + 1) * tf)
        return (jnp.dot(h, wg_ref[:, cols], preferred_element_type=F32),
                jnp.dot(h, wv_ref[:, cols], preferred_element_type=F32))

    split = (n_ff + 1) // 2 * tf
    row = lax.broadcasted_iota(jnp.int32, (tm, 1), 0)
    nxt = up(0)
    down = []
    for j in range(n_ff):
        gate, val = nxt
        if j + 1 < n_ff:
            nxt = up(j + 1)
        prev = carry_s[j]
        g1 = jnp.where(row == 0, prev[7:8, :], pltpu.roll(gate, 1, axis=0))
        g2 = jnp.where(row == 0, prev[6:7, :], pltpu.roll(g1, 1, axis=0))
        carry_s[j] = gate[tm - 8:tm, :]
        cwh = 0.5 * cw_ref[:, j * tf:(j + 1) * tf]
        hc = cwh[0:1, :] * g2 + cwh[1:2, :] * g1 + cwh[2:3, :] * gate + 0.5 * cb_ref[:, j * tf:(j + 1) * tf]
        act_s[:, j * tf:(j + 1) * tf] = ((hc + hc * jnp.tanh(hc)) * val).astype(BF16)
        if (j + 1) * tf == split:
            down.append(jnp.dot(act_s[:, :split], wd_ref[:split, :], preferred_element_type=F32))
    down.append(jnp.dot(act_s[:, split:], wd_ref[split:, :], preferred_element_type=F32))
    out = x + (down[0] + down[1])
    o_ref[...] = _rms(out, gf_ref[...], NORM_EPS) if final else out


def _ffn(x2d, bsz, seq, g, wg, wv, cw, cb, wd, gf, final):
    tm = PROJ_TILE
    nt = seq // tm
    tf = FF_TILE
    n_ff = D_FF // tf
    row = lambda b, t: (b * nt + t, 0)
    c2 = lambda b, t: (0, 0)
    once = pl.Buffered(1)
    return pl.pallas_call(
        functools.partial(_ffn_kernel, final=final),
        grid=(bsz, nt),
        in_specs=[
            pl.BlockSpec((tm, D_MODEL), row),
            pl.BlockSpec((1, D_MODEL), c2),
            pl.BlockSpec((D_MODEL, D_FF), c2, pipeline_mode=once),
            pl.BlockSpec((D_MODEL, D_FF), c2, pipeline_mode=once),
            pl.BlockSpec((3, D_FF), c2),
            pl.BlockSpec((1, D_FF), c2),
            pl.BlockSpec((D_FF, D_MODEL), c2, pipeline_mode=once),
            pl.BlockSpec((1, D_MODEL), c2),
        ],
        out_specs=pl.BlockSpec((tm, D_MODEL), row),
        out_shape=jax.ShapeDtypeStruct((bsz * seq, D_MODEL), F32),
        scratch_shapes=[pltpu.VMEM((n_ff, 8, tf), F32), pltpu.VMEM((tm, n_ff * tf), BF16)],
        compiler_params=_params("parallel", "arbitrary"),
        name="conv_ffn",
    )(x2d, g, wg, wv, cw, cb, wd, gf)


def _rope_perm():
    idx = np.empty((DIFF_W,), np.int32)
    half = DIFF_D // 2
    for h in range(DIFF_H):
        for c in range(2):
            for d in range(DIFF_D):
                idx[h * LANES + (d // half) * DIFF_D + c * half + d % half] = h * LANES + c * DIFF_D + d
    return idx


def _layer(x2d, pos2d, mem2d, bsz, seq, lambda_init, p):
    w_in = p["w_in"]
    pad = jnp.zeros((D_MODEL, LORA_PAD - (N_SHIFT - 3 * RWKV_W)), F32)
    w_r = jnp.concatenate([w_in[:, :N_SHIFT], pad], axis=1).astype(BF16)
    perm = _rope_perm()
    w_d = jnp.concatenate([w_in[:, N_SHIFT:N_SHIFT + DIFF_W][:, perm],
                           w_in[:, N_SHIFT + DIFF_W:N_SHIFT + 2 * DIFF_W][:, perm],
                           w_in[:, N_SHIFT + 2 * DIFF_W:]], axis=1).astype(BF16)
    mix = jnp.concatenate([p["shift_mix"], jnp.zeros((LORA_PAD - (N_SHIFT - 3 * RWKV_W),), F32)])[None, :]
    w_lora = jnp.zeros((LORA_PAD, 3 * RWKV_W), F32)
    w_lora = w_lora.at[0:R_DECAY, 0:RWKV_W].set(p["w_lora_up"])
    w_lora = w_lora.at[R_DECAY:R_DECAY + R_AAA, RWKV_W:2 * RWKV_W].set(p["a_lora_up"])
    w_lora = w_lora.at[R_DECAY + R_AAA:R_DECAY + R_AAA + R_GATE, 2 * RWKV_W:].set(p["g_lora_up"])
    w_lora = w_lora.astype(BF16)
    v512 = lambda a: a.reshape(1, RWKV_W)

    zr, q, k, v = _inproj(x2d, pos2d, p["norm_mix"][None, :], w_r, w_d)
    ya = _rwkv(zr, bsz, seq, mix, w_lora, v512(p["w0"]), v512(p["a0"]), v512(p["k_k"]),
               v512(p["k_a"]), v512(p["r_k"]), v512(p["lnx_gain"]), v512(p["lnx_bias"]))
    yb = _diff_attn(q, k, v, bsz, seq, p["lam_q1"][None, :], p["lam_k1"][None, :],
                    p["lam_q2"][None, :], p["lam_k2"][None, :], p["subln_gain"][None, :], lambda_init)
    kv = _memkv(mem2d, p["norm_mem"][None, :], p["wkv_c"].astype(BF16))
    x2 = _cross(x2d, ya, yb, kv, bsz, seq, p["w_out"].astype(BF16), p["norm_cross"][None, :],
                p["wq_c"].astype(BF16), p["wo_c"].astype(BF16))
    w_up = p["w_up"]
    wg = w_up[:, :D_FF].astype(BF16)
    wv = w_up[:, D_FF:].astype(BF16)
    cw = p["conv_w"]
    cb = p["conv_b"][None, :]
    wd = p["w_down"].astype(BF16)
    return x2, (p["norm_ffn"][None, :], wg, wv, cw, cb, wd)


def kernel(x, mem, positions, norm_mix, w_in, shift_mix, w0, w_lora_up, a0, a_lora_up, g_lora_up, k_k, k_a, r_k, lnx_gain, lnx_bias, lam_q1, lam_k1, lam_q2, lam_k2, subln_gain, w_out, norm_cross, norm_mem, wq_c, wkv_c, wo_c, norm_ffn, w_up, conv_w, conv_b, w_down, norm_final):
    bsz, seq, _ = x.shape
    depth = norm_mix.shape[0]
    x2d = x.reshape(bsz * seq, D_MODEL)
    pos2d = positions.reshape(bsz * seq, 1)
    mem2d = mem.reshape(bsz * mem.shape[1], D_MODEL)
    stacked = dict(norm_mix=norm_mix, w_in=w_in, shift_mix=shift_mix, w0=w0, w_lora_up=w_lora_up,
                   a0=a0, a_lora_up=a_lora_up, g_lora_up=g_lora_up, k_k=k_k, k_a=k_a,
                   r_k=r_k.reshape(depth, RWKV_W), lnx_gain=lnx_gain, lnx_bias=lnx_bias,
                   lam_q1=lam_q1, lam_k1=lam_k1, lam_q2=lam_q2, lam_k2=lam_k2,
                   subln_gain=subln_gain, w_out=w_out, norm_cross=norm_cross, norm_mem=norm_mem,
                   wq_c=wq_c, wkv_c=wkv_c, wo_c=wo_c, norm_ffn=norm_ffn, w_up=w_up,
                   conv_w=conv_w, conv_b=conv_b, w_down=w_down)
    for l in range(depth):
        p = {name: a[l] for name, a in stacked.items()}
        lambda_init = 0.8 - 0.6 * math.exp(-0.3 * l)
        x2, (gn, wg, wv, cw, cb, wd) = _layer(x2d, pos2d, mem2d, bsz, seq, lambda_init, p)
        x2d = _ffn(x2, bsz, seq, gn, wg, wv, cw, cb, wd, norm_final[None, :], l == depth - 1)
    return x2d.reshape(bsz, seq, D_MODEL)
```
